```python
import math
import jax
import jax.numpy as jnp
from jax import lax
import numpy as np

D_MODEL = 2048
BATCH = 16
SEQ = 256
DEPTH = 2
DEC_BATCH = 4
DEC_SEQ = 4096
PAST_LEN = 256

GRID_W = 64
N_BRANCH = 4
MIX_W = 512
HY_ORDER = 2
HY_SHORT = 3
HY_BANDS = 16
HY_PE = 1 + 2 * HY_BANDS
HY_FFN = 64
ML_H = 4
ML_DK = MIX_W // ML_H
ML_DV = MIX_W // ML_H
ML_CHUNK = 64
LRU_H = 4
LRU_BLK = MIX_W // LRU_H
LRU_CONV = 4
LRU_C = 8.0
DA_H = 4
DA_DV = MIX_W // DA_H
DA_DK = DA_DV // 2
Q_BLOCK = 128
ROPE_BASE = 10000.0
N_EXPERTS = 64
TOP_K = 6
N_GROUPS = 8
TOPK_GROUPS = 4
D_EXPERT = 512
D_SHARED = 512
ROUTED_SCALE = 2.5
MOE_BLOCK = 128
EPS = 1e-6
SPLIT_POINTS = (3 * MIX_W, 7 * MIX_W, 7 * MIX_W + 4 * ML_H, 9 * MIX_W + 4 * ML_H, 12 * MIX_W + 4 * ML_H)
N_IN = 12 * MIX_W + 4 * ML_H + N_BRANCH * D_MODEL
F32 = jnp.float32

kernel_name = 'hybrid_diffusion_trunk_step'


def rms_norm(x, g):
    xf = x.astype(F32)
    y = xf * lax.rsqrt(jnp.mean(xf * xf, axis=-1, keepdims=True) + EPS)
    return (y * g.astype(F32)).astype(x.dtype)


def swiglu(x, wg, wu, wd):
    return (jax.nn.silu(x @ wg) * (x @ wu)) @ wd


def depthwise_conv(x, w, left):
    k = w.shape[0]
    n = x.shape[1]
    xp = jnp.pad(x, ((0, 0), (left, k - 1 - left), (0, 0)))
    return sum(xp[:, j:j + n] * w[j] for j in range(k))


def flip_if(a, reverse):
    return jnp.flip(a, axis=1) if reverse else a


def modulation(cvec, w, b):
    m = (jax.nn.silu(cvec) @ w + b)[:, None, :]
    return jnp.split(m, 6, axis=-1)


def hyena_kernel_rfft(n, w1, b1, w2, b2, w3, freq, decay):
    t = jnp.arange(n, dtype=F32)[:, None]
    tn = t / n
    bands = jnp.linspace(1e-4, HY_BANDS - 1, HY_BANDS, dtype=F32)
    ang = (2.0 * math.pi / n) * t * bands
    pe = jnp.concatenate([tn, jnp.cos(ang), jnp.sin(ang)], axis=-1)
    fr = freq.astype(F32)
    z = jnp.sin(fr * (pe @ w1.astype(F32) + b1.astype(F32)))
    z = jnp.sin(fr * (z @ w2.astype(F32) + b2.astype(F32)))
    filt = (z @ w3.astype(F32)) * jnp.exp(-tn * jnp.abs(decay.astype(F32)))
    filt = filt.reshape(n, 2, HY_ORDER, MIX_W)
    fwd, bwd = filt[:, 0], filt[:, 1]
    k = jnp.concatenate([fwd.at[0].add(bwd[0]), jnp.zeros((1, HY_ORDER, MIX_W), F32), bwd[:0:-1]], axis=0)
    k = k / jnp.sum(jnp.abs(k), axis=0, keepdims=True)
    return jnp.fft.rfft(k, axis=0)


def hyena_mixer(u, w_short, w1, b1, w2, b2, w3, freq, decay, skip):
    n = u.shape[1]
    u = depthwise_conv(u, w_short, HY_SHORT // 2)
    v, x1, x2 = jnp.split(u.astype(F32), 3, axis=-1)
    kf = hyena_kernel_rfft(n, w1, b1, w2, b2, w3, freq, decay)
    z = v
    for o, gate in enumerate((x1, x2)):
        zf = jnp.fft.rfft(z, n=2 * n, axis=1)
        conv = jnp.fft.irfft(zf * kf[:, o], n=2 * n, axis=1)[:, :n]
        z = gate * (conv + z * skip[o].astype(F32))
    return z.astype(u.dtype)


def mlstm_direction(q, k, v, ig, lf, c0, n0, m0):
    bsz, n, nh, _ = q.shape
    nc = n // ML_CHUNK

    def chunks(a):
        return jnp.moveaxis(a.reshape((bsz, nc, ML_CHUNK) + a.shape[2:]), 1, 0)

    causal = jnp.tril(jnp.ones((ML_CHUNK, ML_CHUNK), bool))[None, :, :, None]

    def step(carry, xs):
        cm, nv, m = carry
        qc, kc, vc, ic, fc = xs
        b = jnp.cumsum(fc, axis=1)
        logw = b[:, :, None] - b[:, None] + ic[:, None]
        logw = jnp.where(causal, logw, -jnp.inf)
        g = b + m[:, None]
        mt = jnp.maximum(g, jnp.max(logw, axis=2))
        s = jnp.einsum('bthd,bshd->btsh', qc, kc) * jnp.exp(logw - mt[:, :, None])
        inter = jnp.exp(g - mt)
        num = jnp.einsum('btsh,bshv->bthv', s, vc) + inter[..., None] * jnp.einsum('bthd,bhdv->bthv', qc, cm)
        den = jnp.sum(s, axis=2) + inter * jnp.einsum('bthd,bhd->bth', qc, nv)
        h = num / jnp.maximum(jnp.abs(den), jnp.exp(-mt))[..., None]
        bl = b[:, -1]
        wlog = bl[:, None] - b + ic
        m_new = jnp.maximum(bl + m, jnp.max(wlog, axis=1))
        dec = jnp.exp(bl + m - m_new)
        ws = jnp.exp(wlog - m_new[:, None])
        c_new = dec[..., None, None] * cm + jnp.einsum('bsh,bshd,bshv->bhdv', ws, kc, vc)
        n_new = dec[..., None] * nv + jnp.einsum('bsh,bshd->bhd', ws, kc)
        return (c_new, n_new, m_new), h

    (cm, nv, m), hs = lax.scan(step, (c0, n0, m0), tuple(chunks(a) for a in (q, k, v, ig, lf)))
    h = jnp.moveaxis(hs, 0, 1).reshape(bsz, n, nh, -1)
    return h, cm, nv, m


def mlstm_mixer(qkvo, gpre, gate_bias, out_gain, c0, n0, m0):
    bsz, n, _ = qkvo.shape
    q, k, v, o = jnp.split(qkvo.astype(F32), 4, axis=-1)
    q = q.reshape(bsz, n, ML_H, ML_DK) * ML_DK ** -0.5
    k = k.reshape(bsz, n, ML_H, ML_DK)
    v = v.reshape(bsz, n, ML_H, ML_DV)
    pre = gpre.astype(F32).reshape(bsz, n, 2, 2, ML_H) + gate_bias.astype(F32)
    h_sum = 0.0
    cs, ns, ms = [], [], []
    for d in range(2):
        h, cm, nv, m = mlstm_direction(flip_if(q, d), flip_if(k, d), flip_if(v, d),
                                       flip_if(pre[:, :, d, 0], d), flip_if(jax.nn.log_sigmoid(pre[:, :, d, 1]), d),
                                       c0[:, d].astype(F32), n0[:, d].astype(F32), m0[:, d].astype(F32))
        h_sum = h_sum + flip_if(h, d)
        cs.append(cm)
        ns.append(nv)
        ms.append(m)
    h = rms_norm(h_sum, out_gain.reshape(ML_H, ML_DV)).reshape(bsz, n, MIX_W)
    y = jax.nn.sigmoid(o) * h
    return y.astype(qkvo.dtype), jnp.stack(cs, axis=1), jnp.stack(ns, axis=1), jnp.stack(ms, axis=1)


def lru_combine(left, right):
    a1, b1 = left
    a2, b2 = right
    return a1 * a2, a2 * b1 + b2


def rglru_mixer(u, conv_w, conv_b, wa, ba, wx, bx, lam, h0):
    bsz, n, _ = u.shape
    xb, yb = jnp.split(u, 2, axis=-1)
    x = (depthwise_conv(xb, conv_w, LRU_CONV // 2) + conv_b).astype(F32)
    xh = x.reshape(bsz, n, LRU_H, LRU_BLK)
    h_sum = 0.0
    finals = []
    for d in range(2):
        r = jax.nn.sigmoid(jnp.einsum('blhi,hij->blhj', xh, wa[d].astype(F32)).reshape(bsz, n, MIX_W) + ba[d])
        i = jax.nn.sigmoid(jnp.einsum('blhi,hij->blhj', xh, wx[d].astype(F32)).reshape(bsz, n, MIX_W) + bx[d])
        log_a = -LRU_C * r * jax.nn.softplus(-lam[d].astype(F32))
        a = jnp.exp(log_a)
        bterm = jnp.sqrt(jnp.maximum(-jnp.expm1(2.0 * log_a), 0.0)) * (i * x)
        a_cum, h = lax.associative_scan(lru_combine, (flip_if(a, d), flip_if(bterm, d)), axis=1)
        h = h + a_cum * h0[:, d].astype(F32)[:, None]
        finals.append(h[:, -1])
        h_sum = h_sum + flip_if(h, d)
    y = h_sum * jax.nn.gelu(yb.astype(F32))
    return y.astype(u.dtype), jnp.stack(finals, axis=1)


def rotate(x, ang):
    x1, x2 = jnp.split(x, 2, axis=-1)
    cos = jnp.cos(ang)[:, None, None]
    sin = jnp.sin(ang)[:, None, None]
    return jnp.concatenate([x1 * cos - x2 * sin, x1 * sin + x2 * cos], axis=-1)


def axial_rope(x):
    n = x.shape[1]
    rows = n // GRID_W
    row = jnp.repeat(jnp.arange(rows, dtype=F32), GRID_W)
    col = jnp.tile(jnp.arange(GRID_W, dtype=F32), rows)
    half = DA_DK // 2
    inv = ROPE_BASE ** (-jnp.arange(0, half, 2, dtype=F32) / half)
    xr, xc = jnp.split(x.astype(F32), 2, axis=-1)
    out = jnp.concatenate([rotate(xr, row[:, None] * inv), rotate(xc, col[:, None] * inv)], axis=-1)
    return out.astype(x.dtype)


def diff_attention(q, k, v, lam):
    bsz, nq = q.shape[:2]
    nb = nq // Q_BLOCK
    qb = jnp.moveaxis(q.reshape((bsz, nb, Q_BLOCK) + q.shape[2:]), 1, 0)

    def block(qi):
        s = jnp.einsum('bqhcd,bkhcd->bhcqk', qi, k).astype(F32) * DA_DK ** -0.5
        p = jax.nn.softmax(s, axis=-1)
        a = p[:, :, 0] - lam * p[:, :, 1]
        return jnp.einsum('bhqk,bkhv->bqhv', a.astype(v.dtype), v)

    o = lax.map(block, qb)
    return jnp.moveaxis(o, 0, 1).reshape(bsz, nq, DA_H, DA_DV)


def diff_attn_mixer(u, qn, kn, lam_p, sub_gain, lam_init, ctx_kv):
    bsz, n, _ = u.shape
    q, k, v = jnp.split(u, 3, axis=-1)
    q = rms_norm(q.reshape(bsz, n, DA_H, 2, DA_DK), qn)
    k = rms_norm(k.reshape(bsz, n, DA_H, 2, DA_DK), kn)
    v = v.reshape(bsz, n, DA_H, DA_DV)
    lp = lam_p.astype(F32)
    lam = jnp.exp(jnp.sum(lp[0] * lp[1])) - jnp.exp(jnp.sum(lp[2] * lp[3])) + lam_init
    if ctx_kv is None:
        q_in, k_all, v_all = q, k, v
    else:
        kc, vc = ctx_kv
        q_in = axial_rope(q)
        k_all = jnp.concatenate([axial_rope(k), kc.astype(k.dtype)], axis=1)
        v_all = jnp.concatenate([v, vc.astype(v.dtype)], axis=1)
    o = diff_attention(q_in, k_all, v_all, lam)
    o = rms_norm(o, sub_gain) * (1.0 - lam_init)
    return o.reshape(bsz, n, MIX_W), k, v


def routed_experts(t, eidx, w, wg, wu, wd):
    n_tok = t.shape[0]
    n_assign = n_tok * TOP_K
    n_blocks = (n_assign + N_EXPERTS * (MOE_BLOCK - 1)) // MOE_BLOCK + 1
    flat_e = eidx.reshape(n_assign)
    flat_tok = jnp.repeat(jnp.arange(n_tok, dtype=jnp.int32), TOP_K)
    flat_w = w.reshape(n_assign).astype(t.dtype)
    order = jnp.argsort(flat_e)
    e_sorted = flat_e[order]
    counts = jnp.bincount(flat_e, length=N_EXPERTS)
    padded = (counts + MOE_BLOCK - 1) // MOE_BLOCK * MOE_BLOCK
    pad_end = jnp.cumsum(padded)
    pad_start = pad_end - padded
    start = jnp.cumsum(counts) - counts
    dest = pad_start[e_sorted] + jnp.arange(n_assign) - start[e_sorted]
    n_slots = n_blocks * MOE_BLOCK
    slot_tok = jnp.zeros((n_slots,), jnp.int32).at[dest].set(flat_tok[order])
    slot_w = jnp.zeros((n_slots,), t.dtype).at[dest].set(flat_w[order])
    block_e = jnp.minimum(jnp.searchsorted(pad_end, jnp.arange(n_blocks) * MOE_BLOCK, side='right'), N_EXPERTS - 1)

    def body(out, xs):
        tok, wt, e = xs
        y = swiglu(t[tok], wg[e], wu[e], wd[e])
        return out.at[tok].add(y * wt[:, None]), None

    out, _ = lax.scan(body, jnp.zeros_like(t),
                      (slot_tok.reshape(n_blocks, MOE_BLOCK), slot_w.reshape(n_blocks, MOE_BLOCK), block_e))
    return out


def moe_ffn(h, router, bias, wg, wu, wd, sg, su, sd):
    bsz, n, d = h.shape
    t = h.reshape(bsz * n, d)
    scores = jax.nn.sigmoid((t @ router).astype(F32))
    sel = scores + bias.astype(F32)
    grp = sel.reshape(-1, N_GROUPS, N_EXPERTS // N_GROUPS)
    gscore = jnp.sum(lax.top_k(grp, 2)[0], axis=-1)
    _, gidx = lax.top_k(gscore, TOPK_GROUPS)
    gmask = jnp.any(gidx[..., None] == jnp.arange(N_GROUPS), axis=1)
    masked = jnp.where(jnp.repeat(gmask, N_EXPERTS // N_GROUPS, axis=1), sel, -jnp.inf)
    _, eidx = lax.top_k(masked, TOP_K)
    w = jnp.take_along_axis(scores, eidx, axis=1)
    w = w / jnp.sum(w, axis=-1, keepdims=True) * ROUTED_SCALE
    out = routed_experts(t, eidx, w, wg, wu, wd) + swiglu(t, sg, su, sd)
    return out.reshape(bsz, n, d)


def trunk_layer(x, mod, P, l, ctx):
    shift1, scale1, gate1, shift2, scale2, gate2 = mod
    bsz, n, _ = x.shape
    lam_init = 0.8 - 0.6 * math.exp(-0.3 * l)
    h = rms_norm(x, P['norm1'][l]) * (1 + scale1) + shift1
    proj = h @ P['w_in'][l]
    hy_u, ml_u, ml_g, lru_u, da_u, gate_u = jnp.split(proj, SPLIT_POINTS, axis=-1)
    if ctx is None:
        kv_ctx = None
        c0 = jnp.zeros((bsz, 2, ML_H, ML_DK, ML_DV), F32)
        n0 = jnp.zeros((bsz, 2, ML_H, ML_DK), F32)
        m0 = jnp.zeros((bsz, 2, ML_H), F32)
        h0 = jnp.zeros((bsz, 2, MIX_W), F32)
    else:
        k_ctx, v_ctx, c0, n0, m0, h0 = ctx
        kv_ctx = (k_ctx, v_ctx)
    y_hy = hyena_mixer(hy_u, P['hy_short'][l], P['hy_w1'][l], P['hy_b1'][l], P['hy_w2'][l], P['hy_b2'][l],
                       P['hy_w3'][l], P['hy_freq'][l], P['hy_decay'][l], P['hy_skip'][l])
    y_ml, cm, nv, m = mlstm_mixer(ml_u, ml_g, P['ml_gate_bias'][l], P['ml_out_norm'][l], c0, n0, m0)
    y_lru, h_last = rglru_mixer(lru_u, P['lru_conv_w'][l], P['lru_conv_b'][l], P['lru_wa'][l], P['lru_ba'][l],
                                P['lru_wx'][l], P['lru_bx'][l], P['lru_lambda'][l], h0)
    y_da, k_new, v_new = diff_attn_mixer(da_u, P['da_q_norm'][l], P['da_k_norm'][l], P['da_lambda'][l],
                                         P['da_sub_norm'][l], lam_init, kv_ctx)
    gates = jax.nn.sigmoid(gate_u.reshape(bsz, n, N_BRANCH, D_MODEL))
    merged = 0.0
    for bi, yb in enumerate((y_hy, y_ml, y_lru, y_da)):
        merged = merged + gates[:, :, bi] * (yb @ P['w_branch'][l, bi])
    x = x + gate1 * (merged @ P['w_out'][l])
    h2 = rms_norm(x, P['norm2'][l]) * (1 + scale2) + shift2
    x = x + gate2 * moe_ffn(h2, P['moe_router'][l], P['moe_bias'][l], P['moe_w_gate'][l], P['moe_w_up'][l],
                            P['moe_w_down'][l], P['sh_w_gate'][l], P['sh_w_up'][l], P['sh_w_down'][l])
    return x, (k_new, v_new, cm, nv, m, h_last)


def setup_inputs(seed: int = 0) -> dict:
    key = jax.random.key(seed)
    ks = iter(jax.random.split(key, 64))

    def nrm(shape, scale):
        return jax.random.normal(next(ks), shape, F32) * scale

    def gain(shape):
        return 1.0 + nrm(shape, 0.02)

    ib = nrm((DEPTH, 2, 1, ML_H), 0.1)
    fb = jnp.linspace(3.0, 6.0, ML_H, dtype=F32) + nrm((DEPTH, 2, 1, ML_H), 0.1)
    a0 = jax.random.uniform(next(ks), (DEPTH, 2, MIX_W), F32, 0.9, 0.999)
    return {
        'x_prompt': nrm((BATCH, SEQ, D_MODEL), 1.0),
        'x_sample': nrm((DEC_BATCH, DEC_SEQ, D_MODEL), 1.0),
        'c': nrm((DEC_BATCH, D_MODEL), 1.0),
        'cache_k': nrm((DEC_BATCH, DEPTH, PAST_LEN, DA_H, 2, DA_DK), 1.0),
        'cache_v': nrm((DEC_BATCH, DEPTH, PAST_LEN, DA_H, DA_DV), 1.0),
        'state_mlstm_C': nrm((DEC_BATCH, DEPTH, 2, ML_H, ML_DK, ML_DV), 1.0),
        'state_mlstm_n': nrm((DEC_BATCH, DEPTH, 2, ML_H, ML_DK), 1.0),
        'state_mlstm_m': nrm((DEC_BATCH, DEPTH, 2, ML_H), 1.0),
        'state_rglru': nrm((DEC_BATCH, DEPTH, 2, MIX_W), 0.5),
        'c_ctx': nrm((D_MODEL,), 1.0),
        'w_mod': nrm((DEPTH, D_MODEL, 6 * D_MODEL), 0.5 * D_MODEL ** -0.5),
        'b_mod': nrm((DEPTH, 6 * D_MODEL), 0.02),
        'norm1': gain((DEPTH, D_MODEL)),
        'norm2': gain((DEPTH, D_MODEL)),
        'w_in': nrm((DEPTH, D_MODEL, N_IN), D_MODEL ** -0.5),
        'hy_short': nrm((DEPTH, HY_SHORT, 3 * MIX_W), HY_SHORT ** -0.5),
        'hy_w1': nrm((DEPTH, HY_PE, HY_FFN), HY_PE ** -0.5),
        'hy_b1': nrm((DEPTH, HY_FFN), 0.1),
        'hy_w2': nrm((DEPTH, HY_FFN, HY_FFN), HY_FFN ** -0.5),
        'hy_b2': nrm((DEPTH, HY_FFN), 0.1),
        'hy_w3': nrm((DEPTH, HY_FFN, 2 * HY_ORDER * MIX_W), HY_FFN ** -0.5),
        'hy_freq': gain((DEPTH, HY_FFN)),
        'hy_decay': jax.random.uniform(next(ks), (DEPTH, 2 * HY_ORDER * MIX_W), F32, 3.0, 15.0),
        'hy_skip': nrm((DEPTH, HY_ORDER, MIX_W), 0.5),
        'ml_gate_bias': jnp.concatenate([ib, fb], axis=2),
        'ml_out_norm': gain((DEPTH, MIX_W)),
        'lru_conv_w': nrm((DEPTH, LRU_CONV, MIX_W), 0.5),
        'lru_conv_b': nrm((DEPTH, MIX_W), 0.02),
        'lru_wa': nrm((DEPTH, 2, LRU_H, LRU_BLK, LRU_BLK), LRU_BLK ** -0.5),
        'lru_ba': nrm((DEPTH, 2, MIX_W), 0.02),
        'lru_wx': nrm((DEPTH, 2, LRU_H, LRU_BLK, LRU_BLK), LRU_BLK ** -0.5),
        'lru_bx': nrm((DEPTH, 2, MIX_W), 0.02),
        'lru_lambda': jnp.log(a0 / (1.0 - a0)),
        'da_q_norm': gain((DEPTH, DA_DK)),
        'da_k_norm': gain((DEPTH, DA_DK)),
        'da_lambda': nrm((DEPTH, 4, DA_DK), 0.1),
        'da_sub_norm': gain((DEPTH, DA_DV)),
        'w_branch': nrm((DEPTH, N_BRANCH, MIX_W, D_MODEL), MIX_W ** -0.5),
        'w_out': nrm((DEPTH, D_MODEL, D_MODEL), D_MODEL ** -0.5),
        'moe_router': nrm((DEPTH, D_MODEL, N_EXPERTS), D_MODEL ** -0.5),
        'moe_bias': nrm((DEPTH, N_EXPERTS), 0.01),
        'moe_w_gate': nrm((DEPTH, N_EXPERTS, D_MODEL, D_EXPERT), D_MODEL ** -0.5),
        'moe_w_up': nrm((DEPTH, N_EXPERTS, D_MODEL, D_EXPERT), D_MODEL ** -0.5),
        'moe_w_down': nrm((DEPTH, N_EXPERTS, D_EXPERT, D_MODEL), D_EXPERT ** -0.5),
        'sh_w_gate': nrm((DEPTH, D_MODEL, D_SHARED), D_MODEL ** -0.5),
        'sh_w_up': nrm((DEPTH, D_MODEL, D_SHARED), D_MODEL ** -0.5),
        'sh_w_down': nrm((DEPTH, D_SHARED, D_MODEL), D_SHARED ** -0.5),
    }


def reference(x_prompt, x_sample, c, cache_k, cache_v, state_mlstm_C, state_mlstm_n, state_mlstm_m, state_rglru,
              c_ctx, w_mod, b_mod, norm1, norm2, w_in, hy_short, hy_w1, hy_b1, hy_w2, hy_b2, hy_w3, hy_freq,
              hy_decay, hy_skip, ml_gate_bias, ml_out_norm, lru_conv_w, lru_conv_b, lru_wa, lru_ba, lru_wx, lru_bx,
              lru_lambda, da_q_norm, da_k_norm, da_lambda, da_sub_norm, w_branch, w_out, moe_router, moe_bias,
              moe_w_gate, moe_w_up, moe_w_down, sh_w_gate, sh_w_up, sh_w_down):
    P = dict(norm1=norm1, norm2=norm2, w_in=w_in, hy_short=hy_short, hy_w1=hy_w1, hy_b1=hy_b1, hy_w2=hy_w2,
             hy_b2=hy_b2, hy_w3=hy_w3, hy_freq=hy_freq, hy_decay=hy_decay, hy_skip=hy_skip,
             ml_gate_bias=ml_gate_bias, ml_out_norm=ml_out_norm, lru_conv_w=lru_conv_w, lru_conv_b=lru_conv_b,
             lru_wa=lru_wa, lru_ba=lru_ba, lru_wx=lru_wx, lru_bx=lru_bx, lru_lambda=lru_lambda,
             da_q_norm=da_q_norm, da_k_norm=da_k_norm, da_lambda=da_lambda, da_sub_norm=da_sub_norm,
             w_branch=w_branch, w_out=w_out, moe_router=moe_router, moe_bias=moe_bias, moe_w_gate=moe_w_gate,
             moe_w_up=moe_w_up, moe_w_down=moe_w_down, sh_w_gate=sh_w_gate, sh_w_up=sh_w_up, sh_w_down=sh_w_down)
    y = x_prompt
    per_layer = []
    for l in range(DEPTH):
        y, st = trunk_layer(y, modulation(c_ctx[None], w_mod[l], b_mod[l]), P, l, None)
        per_layer.append(st)
    new_k = jnp.stack([s[0] for s in per_layer], axis=1)
    new_v = jnp.stack([s[1] for s in per_layer], axis=1)
    new_c = jnp.stack([s[2] for s in per_layer], axis=1)
    new_n = jnp.stack([s[3] for s in per_layer], axis=1)
    new_m = jnp.stack([s[4] for s in per_layer], axis=1)
    new_h = jnp.stack([s[5] for s in per_layer], axis=1)
    z = x_sample
    for l in range(DEPTH):
        ctx = (cache_k[:, l], cache_v[:, l], state_mlstm_C[:, l], state_mlstm_n[:, l], state_mlstm_m[:, l],
               state_rglru[:, l])
        z, _ = trunk_layer(z, modulation(c, w_mod[l], b_mod[l]), P, l, ctx)
    return (y, z, new_k, new_v, new_c, new_n, new_m, new_h)
```

```python
import functools
import math

import numpy as np
import jax
import jax.numpy as jnp
from jax import lax
from jax.experimental import pallas as pl
from jax.experimental.pallas import tpu as pltpu

F32 = jnp.float32
BF16 = jnp.bfloat16
I32 = jnp.int32
HIGHEST = lax.Precision.HIGHEST

D_MODEL = 2048
BATCH = 16
SEQ = 256
DEPTH = 2
DEC_BATCH = 4
DEC_SEQ = 4096
GRID_W = 64
N_BRANCH = 4
MIX_W = 512
HY_BANDS = 16
HY_PE = 1 + 2 * HY_BANDS
HY_FFN = 64
ML_H = 4
ML_DK = 128
LRU_H = 4
LRU_C = 8.0
DA_H = 4
DA_DK = 64
ROPE_BASE = 10000.0
N_EXPERTS = 64
TOP_K = 6
N_GROUPS = 8
TOPK_GROUPS = 4
D_EXPERT = 512
ROUTED_SCALE = 2.5
EPS = 1e-6

LANE = 128
T_CTX = BATCH * SEQ
T_DEN = DEC_BATCH * DEC_SEQ
T_ALL = T_CTX + T_DEN
N_MOD_ROWS = 8
VMEM_LIMIT = 56 * 1024 * 1024

CB_HY = 0
CB_MLQ, CB_MLK, CB_MLV, CB_MLO = 12, 16, 20, 24
CB_LRX, CB_LRY = 28, 32
CB_DAQ, CB_DAK, CB_DAV = 36, 40, 44
N_MAIN = 6144


def _cp(sem, vmem=VMEM_LIMIT):
    return pltpu.CompilerParams(dimension_semantics=sem, vmem_limit_bytes=vmem)


def _mod_row(i, rows_per_block):
    nctx = T_CTX // rows_per_block
    per = DEC_SEQ // rows_per_block
    return jnp.where(i < nctx, 0, 1 + (i - nctx) // per)


def _mod_spec(layer, seg, rows_per_block, ngrid):
    if ngrid == 1:
        return pl.BlockSpec((None, None, 1, D_MODEL), lambda i: (layer, _mod_row(i, rows_per_block) * 6 + seg, 0, 0))
    return pl.BlockSpec((None, None, 1, D_MODEL), lambda i, j: (layer, _mod_row(i, rows_per_block) * 6 + seg, 0, 0))


def _mod_kernel(c_ref, w_ref, b_ref, o_ref):
    c = c_ref[...]
    a = (c * jax.nn.sigmoid(c)).astype(BF16)
    o_ref[...] = jnp.dot(a, w_ref[...].astype(BF16), preferred_element_type=F32) + b_ref[...]


def modulation_all(cvecs, w_mod, b_mod):
    tn = 1024
    n = 6 * D_MODEL
    return pl.pallas_call(
        _mod_kernel,
        grid=(DEPTH, n // tn),
        in_specs=[pl.BlockSpec((N_MOD_ROWS, D_MODEL), lambda l, j: (0, 0)),
                  pl.BlockSpec((None, D_MODEL, tn), lambda l, j: (l, 0, j)),
                  pl.BlockSpec((None, 1, tn), lambda l, j: (l, 0, j))],
        out_specs=pl.BlockSpec((None, N_MOD_ROWS, tn), lambda l, j: (l, 0, j)),
        out_shape=jax.ShapeDtypeStruct((DEPTH, N_MOD_ROWS, n), F32),
        compiler_params=_cp(("parallel", "parallel")),
        name="modulation",
    )(cvecs, w_mod, b_mod.reshape(DEPTH, 1, n))


def _norm_mod(x, g, sc, sh):
    ms = jnp.mean(x * x, axis=-1, keepdims=True)
    return (x * lax.rsqrt(ms + EPS) * g) * (1.0 + sc) + sh


def _inproj_kernel(x_ref, g_ref, sh_ref, sc_ref, w_ref, wg_ref, proj_ref, h_ref, gp_ref, *, tm, sub):
    j = pl.program_id(1)

    @pl.when(j == 0)
    def _():
        g = g_ref[...]
        sc = sc_ref[...]
        sh = sh_ref[...]

        def body(r, _):
            rows = pl.ds(pl.multiple_of(r * sub, sub), sub)
            hb = _norm_mod(x_ref[rows, :], g, sc, sh).astype(BF16)
            h_ref[rows, :] = hb
            gp_ref[rows, :] = jnp.dot(hb, wg_ref[...], preferred_element_type=F32)
            return 0

        lax.fori_loop(0, tm // sub, body, 0)

    proj_ref[...] = jnp.dot(h_ref[...], w_ref[...], preferred_element_type=F32).astype(BF16)


def in_projection(x, norm_g, mod4, layer, w_main, w_gate16):
    tm, tn = 1024, 512
    kern = functools.partial(_inproj_kernel, tm=tm, sub=256)
    return pl.pallas_call(
        kern,
        grid=(T_ALL // tm, N_MAIN // tn),
        in_specs=[pl.BlockSpec((tm, D_MODEL), lambda i, j: (i, 0)),
                  pl.BlockSpec((None, 1, D_MODEL), lambda i, j: (layer, 0, 0)),
                  _mod_spec(layer, 0, tm, 2),
                  _mod_spec(layer, 1, tm, 2),
                  pl.BlockSpec((D_MODEL, tn), lambda i, j: (0, j)),
                  pl.BlockSpec((D_MODEL, LANE), lambda i, j: (0, 0))],
        out_specs=[pl.BlockSpec((tm, tn), lambda i, j: (i, j)),
                   pl.BlockSpec((tm, D_MODEL), lambda i, j: (i, 0)),
                   pl.BlockSpec((tm, LANE), lambda i, j: (i, 0))],
        out_shape=[jax.ShapeDtypeStruct((T_ALL, N_MAIN), BF16),
                   jax.ShapeDtypeStruct((T_ALL, D_MODEL), BF16),
                   jax.ShapeDtypeStruct((T_ALL, LANE), F32)],
        compiler_params=_cp(("parallel", "arbitrary")),
        name="in_projection",
    )(x, norm_g, mod4, mod4, w_main, w_gate16)


def _fft_factors(seq_len):
    n = 2 * seq_len
    n2 = 64 if seq_len >= 2048 else 16
    return n // n2, n2


@functools.lru_cache(maxsize=None)
def _dft_tables(seq_len):
    n1, n2 = _fft_factors(seq_len)
    n = n1 * n2
    a = np.arange(n1, dtype=np.float64)
    th1 = 2.0 * np.pi * np.outer(a, a) / n1
    w1 = np.concatenate([np.cos(th1), -np.sin(th1)], axis=0)
    w4 = np.concatenate([np.cos(th1), -np.sin(th1)], axis=1)[: n1 // 2] / n
    k1 = np.arange(n1, dtype=np.float64)[:, None, None]
    k2 = np.arange(n2, dtype=np.float64)[None, :, None]
    m2 = np.arange(n2, dtype=np.float64)[None, None, :]
    ph = -2.0 * np.pi * (m2 * k1 / n + m2 * k2 / n2)
    gr, gi = np.cos(ph), np.sin(ph)
    g = np.concatenate([np.concatenate([gr, -gi], axis=2), np.concatenate([gi, gr], axis=2)], axis=1)
    gt = np.transpose(g, (0, 2, 1))
    w1h = w1[:, : n1 // 2]
    return tuple(np.asarray(t, np.float32) for t in (w1, w1h, g, gt, w4))


def _dft_tables_bf16(seq_len):
    return tuple(jnp.asarray(t, BF16) for t in _dft_tables(seq_len))


@functools.lru_cache(maxsize=None)
def _filter_positions(seq_len):
    pos = np.concatenate([np.arange(seq_len), [0], np.arange(seq_len - 1, 0, -1)]).astype(np.float64)
    tn = pos / seq_len
    bands = np.linspace(1e-4, HY_BANDS - 1, HY_BANDS)
    ang = (2.0 * math.pi / seq_len) * pos[:, None] * bands
    pe = np.zeros((2 * seq_len, LANE), np.float64)
    pe[:, 0] = tn
    pe[:, 1:1 + HY_BANDS] = np.cos(ang)
    pe[:, 1 + HY_BANDS:HY_PE] = np.sin(ang)
    return np.asarray(pe, np.float32), np.asarray(tn[:, None], np.float32)


def _hy_filter_kernel(pe_ref, tn_ref, w1_ref, b1_ref, w2_ref, b2_ref, fr_ref, w3a_ref, w3b_ref, dca_ref, dcb_ref,
                      k_ref, *, seq_len):
    fr = fr_ref[...]
    z = jnp.sin(fr * (jnp.dot(pe_ref[...], w1_ref[...], precision=HIGHEST, preferred_element_type=F32) + b1_ref[...]))
    z = jnp.sin(fr * (jnp.dot(z, w2_ref[...], precision=HIGHEST, preferred_element_type=F32) + b2_ref[...]))
    tn = tn_ref[...]
    fa = jnp.dot(z, w3a_ref[...], precision=HIGHEST, preferred_element_type=F32) * jnp.exp(-tn * jnp.abs(dca_ref[...]))
    fb = jnp.dot(z, w3b_ref[...], precision=HIGHEST, preferred_element_type=F32) * jnp.exp(-tn * jnp.abs(dcb_ref[...]))
    row = lax.broadcasted_iota(I32, fa.shape, 0)
    k = jnp.where(row < seq_len, fa, jnp.where(row > seq_len, fb, 0.0))
    k = k + jnp.where(row == 0, fb[0:1, :], 0.0)
    k_ref[...] = k / jnp.sum(jnp.abs(k), axis=0, keepdims=True)


def hyena_filter(seq_len, w1p, b1, w2, b2, freq, w3, decay):
    pe, tn = (jnp.asarray(t) for t in _filter_positions(seq_len))
    n = 2 * seq_len
    tc = 128
    nc = (2 * MIX_W) // tc
    kern = functools.partial(_hy_filter_kernel, seq_len=seq_len)
    full = lambda shape: pl.BlockSpec(shape, lambda j: (0,) * len(shape))
    return pl.pallas_call(
        kern,
        grid=(nc,),
        in_specs=[full((n, LANE)), full((n, 1)), full((LANE, HY_FFN)), full((1, HY_FFN)), full((HY_FFN, HY_FFN)),
                  full((1, HY_FFN)), full((1, HY_FFN)),
                  pl.BlockSpec((HY_FFN, tc), lambda j: (0, j)),
                  pl.BlockSpec((HY_FFN, tc), lambda j: (0, nc + j)),
                  pl.BlockSpec((1, tc), lambda j: (0, j)),
                  pl.BlockSpec((1, tc), lambda j: (0, nc + j))],
        out_specs=pl.BlockSpec((n, tc), lambda j: (0, j)),
        out_shape=jax.ShapeDtypeStruct((n, 2 * MIX_W), F32),
        compiler_params=_cp(("parallel",)),
        name=f"hyena_filter_{seq_len}",
    )(pe, tn, w1p, b1, w2, b2, freq, w3, w3, decay, decay)


def _split_bf16(x):
    hi = x.astype(BF16)
    lo = (x - hi.astype(F32)).astype(BF16)
    return hi, lo


def _fft_stage1(load_rows, w1, a_ref, n1, n2, split):
    def body(m, _):
        xs = load_rows(m)
        if split:
            hi, lo = _split_bf16(xs)
            r = jnp.dot(w1, hi, preferred_element_type=F32) + jnp.dot(w1, lo, preferred_element_type=F32)
        else:
            r = jnp.dot(w1, xs.astype(BF16), preferred_element_type=F32)
        a_ref[pl.ds(m, n1, stride=2 * n2), :] = r[:n1]
        a_ref[pl.ds(n2 + m, n1, stride=2 * n2), :] = r[n1:]
        return 0

    lax.fori_loop(0, n2, body, 0)


def _hy_fft_kernel(k_ref, w1_ref, g_ref, kf_ref, a_ref, *, n1, n2):
    w1 = w1_ref[...]
    _fft_stage1(lambda m: k_ref[pl.ds(m, n1, stride=n2), :], w1, a_ref, n1, n2, True)

    def body(p, _):
        rows = pl.ds(pl.multiple_of(p * 2 * n2, 2 * n2), 2 * n2)
        hi, lo = _split_bf16(a_ref[rows, :])
        g = g_ref[p]
        kf_ref[rows, :] = jnp.dot(g, hi, preferred_element_type=F32) + jnp.dot(g, lo, preferred_element_type=F32)
        return 0

    lax.fori_loop(0, n1, body, 0)


def hyena_filter_fft(seq_len, kfilt):
    n1, n2 = _fft_factors(seq_len)
    w1, _, g, _, _ = _dft_tables_bf16(seq_len)
    n = n1 * n2
    tc = 128
    kern = functools.partial(_hy_fft_kernel, n1=n1, n2=n2)
    return pl.pallas_call(
        kern,
        grid=((2 * MIX_W) // tc,),
        in_specs=[pl.BlockSpec((n, tc), lambda j: (0, j)),
                  pl.BlockSpec((2 * n1, n1), lambda j: (0, 0)),
                  pl.BlockSpec((n1, 2 * n2, 2 * n2), lambda j: (0, 0, 0))],
        out_specs=pl.BlockSpec((2 * n, tc), lambda j: (0, j)),
        out_shape=jax.ShapeDtypeStruct((2 * n, 2 * MIX_W), F32),
        scratch_shapes=[pltpu.VMEM((2 * n, tc), F32)],
        compiler_params=_cp(("parallel",)),
        name=f"hyena_filter_fft_{seq_len}",
    )(kfilt, w1, g)


def _shift_rows(x, s, row):
    n = x.shape[0]
    y = pltpu.roll(x, s % n, 0)
    if s > 0:
        return jnp.where(row >= s, y, 0.0)
    return jnp.where(row < n + s, y, 0.0)


def _short_conv3(x_ref, w_ref):
    x = x_ref[...].astype(F32)
    row = lax.broadcasted_iota(I32, x.shape, 0)
    w = w_ref[...]
    return w[0:1] * _shift_rows(x, 1, row) + w[1:2] * x + w[2:3] * _shift_rows(x, -1, row)


def _hy_conv_kernel(v_ref, x1_ref, x2_ref, ws_ref, kf_ref, skip_ref, w1_ref, g_ref, gt_ref, w4_ref, y_ref,
                    z_ref, c_ref, a_ref, *, n1, n2):
    o = pl.program_id(2)
    nh = n1 // 2

    @pl.when(o == 0)
    def _():
        z_ref[...] = _short_conv3(v_ref, ws_ref.at[0])

    w1 = w1_ref[...]
    _fft_stage1(lambda m: z_ref[pl.ds(m, nh, stride=n2), :], w1, a_ref, n1, n2, False)

    def page(p, _):
        rows = pl.ds(pl.multiple_of(p * 2 * n2, 2 * n2), 2 * n2)
        x = jnp.dot(g_ref[p], a_ref[rows, :].astype(BF16), preferred_element_type=F32)
        kf = kf_ref[rows, :]
        xr, xi = x[:n2], x[n2:]
        kr, ki = kf[:n2], kf[n2:]
        yc = jnp.concatenate([xr * kr - xi * ki, xr * ki + xi * kr], axis=0).astype(BF16)
        a_ref[rows, :] = jnp.dot(gt_ref[p], yc, preferred_element_type=F32)
        return 0

    lax.fori_loop(0, n1, page, 0)

    w4 = w4_ref[...]

    def back(m, _):
        dr = a_ref[pl.ds(m, n1, stride=2 * n2), :]
        di = a_ref[pl.ds(n2 + m, n1, stride=2 * n2), :]
        d = jnp.concatenate([dr, di], axis=0).astype(BF16)
        c_ref[pl.ds(m, nh, stride=n2), :] = jnp.dot(w4, d, preferred_element_type=F32)
        return 0

    lax.fori_loop(0, n2, back, 0)

    @pl.when(o == 0)
    def _():
        z = z_ref[...]
        z_ref[...] = _short_conv3(x1_ref, ws_ref.at[1]) * (c_ref[...] + z * skip_ref[0:1])

    @pl.when(o == 1)
    def _():
        z = z_ref[...]
        y_ref[...] = (_short_conv3(x2_ref, ws_ref.at[2]) * (c_ref[...] + z * skip_ref[1:2])).astype(BF16)


def hyena_conv(proj, row_block_off, nseq, seq_len, w_short3, kf, skip):
    n1, n2 = _fft_factors(seq_len)
    _, w1h, g, gt, w4 = _dft_tables_bf16(seq_len)
    n = n1 * n2
    tc = 128
    ncb = MIX_W // tc
    kern = functools.partial(_hy_conv_kernel, n1=n1, n2=n2)
    const3 = lambda c, b, o: (0, 0, 0)
    seg = lambda s: pl.BlockSpec((seq_len, tc), lambda c, b, o: (row_block_off + b, CB_HY + s * ncb + c))
    return pl.pallas_call(
        kern,
        grid=(ncb, nseq, 2),
        in_specs=[seg(0), seg(1), seg(2),
                  pl.BlockSpec((3, 3, tc), lambda c, b, o: (0, 0, c)),
                  pl.BlockSpec((2 * n, tc), lambda c, b, o: (0, o * ncb + c)),
                  pl.BlockSpec((2, tc), lambda c, b, o: (0, c)),
                  pl.BlockSpec((2 * n1, n1 // 2), lambda c, b, o: (0, 0)),
                  pl.BlockSpec((n1, 2 * n2, 2 * n2), const3, pipeline_mode=pl.Buffered(1)),
                  pl.BlockSpec((n1, 2 * n2, 2 * n2), const3, pipeline_mode=pl.Buffered(1)),
                  pl.BlockSpec((n1 // 2, 2 * n1), lambda c, b, o: (0, 0))],
        out_specs=pl.BlockSpec((seq_len, tc), lambda c, b, o: (b, c)),
        out_shape=jax.ShapeDtypeStruct((nseq * seq_len, MIX_W), BF16),
        scratch_shapes=[pltpu.VMEM((seq_len, tc), F32), pltpu.VMEM((seq_len, tc), F32),
                        pltpu.VMEM((2 * n, tc), F32)],
        compiler_params=_cp(("parallel", "parallel", "arbitrary")),
        name=f"hyena_conv_{seq_len}",
    )(proj, proj, proj, w_short3, kf, skip, w1h, g, gt, w4)


def _dot_nt(a, b):
    return lax.dot_general(a, b, (((1,), (1,)), ((), ())), preferred_element_type=F32)


ML_CHUNK = 128


def _log_sigmoid(x):
    return jnp.minimum(x, 0.0) - jnp.log1p(jnp.exp(-jnp.abs(x)))


def _mlstm_kernel(*refs, seq_len, has_state):
    if has_state:
        (q_ref, k_ref, v_ref, o_ref, gr_ref, gc_ref, br_ref, bc_ref, gain_ref, c0_ref, n0_ref, m0_ref,
         y_ref, cout_ref, nout_ref, mout_ref, hs_ref, cs_ref, ns_ref, ms_ref) = refs
    else:
        (q_ref, k_ref, v_ref, o_ref, gr_ref, gc_ref, br_ref, bc_ref, gain_ref,
         y_ref, cout_ref, nout_ref, mout_ref, hs_ref, cs_ref, ns_ref, ms_ref) = refs
    t = ML_CHUNK
    nc = seq_len // t
    scale = ML_DK ** -0.5
    if has_state:
        cs_ref[...] = c0_ref[...]
        ns_ref[...] = n0_ref[...]
        ms_ref[...] = m0_ref[...]
    else:
        cs_ref[...] = jnp.zeros_like(cs_ref)
        ns_ref[...] = jnp.zeros_like(ns_ref)
        ms_ref[...] = jnp.zeros_like(ms_ref)
    hs_ref[...] = jnp.zeros_like(hs_ref)
    row = lax.broadcasted_iota(I32, (t, t), 0)
    col = lax.broadcasted_iota(I32, (t, t), 1)
    br = br_ref[...]
    bc = bc_ref[...]

    def step(j, _):
        for d in (0, 1):
            jj = j if d == 0 else nc - 1 - j
            rows = pl.ds(pl.multiple_of(jj * t, t), t)
            qf = q_ref[rows, :].astype(F32) * scale
            qb = qf.astype(BF16)
            kb = k_ref[rows, :]
            vb = v_ref[rows, :]
            gr = gr_ref[rows, :] + br
            gc = gc_ref[:, rows] + bc
            i_c = gr[:, 2 * d:2 * d + 1]
            f_c = _log_sigmoid(gr[:, 2 * d + 1:2 * d + 2])
            i_r = gc[2 * d:2 * d + 1, :]
            f_r = _log_sigmoid(gc[2 * d + 1:2 * d + 2, :])
            if d == 0:
                mask, mask_t = col <= row, row <= col
            else:
                mask, mask_t = col >= row, row >= col
            b_c = jnp.sum(jnp.where(mask, f_r, 0.0), axis=1, keepdims=True)
            b_r = jnp.sum(jnp.where(mask_t, f_c, 0.0), axis=0, keepdims=True)
            logw = jnp.where(mask, b_c - b_r + i_r, -jnp.inf)
            m = ms_ref[d][:, 0:1]
            g = b_c + m
            mt = jnp.maximum(g, jnp.max(logw, axis=1, keepdims=True))
            s = _dot_nt(qb, kb) * jnp.exp(logw - mt)
            inter = jnp.exp(g - mt)
            cm = cs_ref[d]
            nv = ns_ref[d]
            num = (jnp.dot(s.astype(BF16), vb, preferred_element_type=F32)
                   + inter * jnp.dot(qb, cm.astype(BF16), preferred_element_type=F32))
            den = jnp.sum(s, axis=1, keepdims=True) + inter * jnp.sum(qf * nv, axis=1, keepdims=True)
            h = num / jnp.maximum(jnp.abs(den), jnp.exp(-mt))
            hs_ref[rows, :] = hs_ref[rows, :] + h
            bl = jnp.sum(f_r, axis=1, keepdims=True)
            wlog_r = bl - b_r + i_r
            wlog_c = bl - b_c + i_c
            m_new = jnp.maximum(bl + m, jnp.max(wlog_r, axis=1, keepdims=True))
            dec = jnp.exp(bl + m - m_new)
            kw = jnp.exp(wlog_c - m_new) * kb.astype(F32)
            cs_ref[d] = dec * cm + jnp.dot(kw.T.astype(BF16), vb, preferred_element_type=F32)
            ns_ref[d] = dec * nv + jnp.sum(kw, axis=0, keepdims=True)
            ms_ref[d] = jnp.broadcast_to(m_new, (1, LANE))
        return 0

    lax.fori_loop(0, nc, step, 0)
    hsum = hs_ref[...]
    hn = hsum * lax.rsqrt(jnp.mean(hsum * hsum, axis=-1, keepdims=True) + EPS) * gain_ref[...]
    y_ref[...] = (jax.nn.sigmoid(o_ref[...].astype(F32)) * hn).astype(BF16)
    cout_ref[...] = cs_ref[...]
    nout_ref[...] = ns_ref[...]
    mout_ref[...] = ms_ref[...]


def mlstm_mixer(proj, gates_r, gates_c, bias_r, bias_c, gain, row_block_off, nseq, seq_len, state):
    has_state = state is not None
    kern = functools.partial(_mlstm_kernel, seq_len=seq_len, has_state=has_state)
    col = lambda cb: pl.BlockSpec((seq_len, LANE), lambda b, h: (row_block_off + b, cb + h))
    c_spec = pl.BlockSpec((None, 2, None, ML_DK, ML_DK), lambda b, h: (b, 0, h, 0, 0))
    v_spec = pl.BlockSpec((None, 2, None, 1, LANE), lambda b, h: (b, 0, h, 0, 0))
    in_specs = [col(CB_MLQ), col(CB_MLK), col(CB_MLV), col(CB_MLO),
                pl.BlockSpec((None, seq_len, 4), lambda b, h: (h, row_block_off + b, 0)),
                pl.BlockSpec((None, 4, seq_len), lambda b, h: (h, 0, row_block_off + b)),
                pl.BlockSpec((None, 1, 4), lambda b, h: (h, 0, 0)),
                pl.BlockSpec((None, 4, 1), lambda b, h: (h, 0, 0)),
                pl.BlockSpec((1, LANE), lambda b, h: (0, h))]
    args = [proj, proj, proj, proj, gates_r, gates_c, bias_r, bias_c, gain]
    if has_state:
        in_specs += [c_spec, v_spec, v_spec]
        args += list(state)
    return pl.pallas_call(
        kern,
        grid=(nseq, ML_H),
        in_specs=in_specs,
        out_specs=[pl.BlockSpec((seq_len, LANE), lambda b, h: (b, h)), c_spec, v_spec, v_spec],
        out_shape=[jax.ShapeDtypeStruct((nseq * seq_len, MIX_W), BF16),
                   jax.ShapeDtypeStruct((nseq, 2, ML_H, ML_DK, ML_DK), F32),
                   jax.ShapeDtypeStruct((nseq, 2, ML_H, 1, LANE), F32),
                   jax.ShapeDtypeStruct((nseq, 2, ML_H, 1, LANE), F32)],
        scratch_shapes=[pltpu.VMEM((seq_len, LANE), F32), pltpu.VMEM((2, ML_DK, ML_DK), F32),
                        pltpu.VMEM((2, 1, LANE), F32), pltpu.VMEM((2, 1, LANE), F32)],
        compiler_params=_cp(("parallel", "parallel")),
        name=f"mlstm_{seq_len}",
    )(*args)


def _softplus(x):
    return jnp.maximum(x, 0.0) + jnp.log1p(jnp.exp(-jnp.abs(x)))


def _gelu_tanh(x):
    return 0.5 * x * (1.0 + jnp.tanh(math.sqrt(2.0 / math.pi) * (x + 0.044715 * (x * x * x))))


def _lru_kernel(*refs, seq_len, has_state):
    if has_state:
        (x_ref, y_ref, cw_ref, cb_ref, wa_ref, ba_ref, wx_ref, bx_ref, lam_ref, h0_ref,
         out_ref, hl_ref, a_s, b_s, hsum) = refs
    else:
        (x_ref, y_ref, cw_ref, cb_ref, wa_ref, ba_ref, wx_ref, bx_ref, lam_ref,
         out_ref, hl_ref, a_s, b_s, hsum) = refs
    xf = x_ref[...].astype(F32)
    row = lax.broadcasted_iota(I32, xf.shape, 0)
    w = cw_ref[...]
    x = (w[0:1] * _shift_rows(xf, 2, row) + w[1:2] * _shift_rows(xf, 1, row) + w[2:3] * xf
         + w[3:4] * _shift_rows(xf, -1, row) + cb_ref[...])
    xb = x.astype(BF16)
    for d in (0, 1):
        r = jax.nn.sigmoid(jnp.dot(xb, wa_ref[d].astype(BF16), preferred_element_type=F32) + ba_ref[d])
        i = jax.nn.sigmoid(jnp.dot(xb, wx_ref[d].astype(BF16), preferred_element_type=F32) + bx_ref[d])
        log_a = (-LRU_C) * r * _softplus(-lam_ref[d])
        a_s[d] = jnp.exp(log_a)
        b_s[d] = jnp.sqrt(jnp.maximum(1.0 - jnp.exp(2.0 * log_a), 0.0)) * (i * x)

    nb = seq_len // 8
    sub = lax.broadcasted_iota(I32, (8, LANE), 0)

    def block_scan(a, b, reverse):
        for s in (1, 2, 4):
            sh = 8 - s if reverse else s
            ok = (sub < 8 - s) if reverse else (sub >= s)
            b = jnp.where(ok, a * pltpu.roll(b, sh, 0) + b, b)
            a = jnp.where(ok, a * pltpu.roll(a, sh, 0), a)
        return a, b

    def fwd(blk, carry):
        rows = pl.ds(pl.multiple_of(blk * 8, 8), 8)
        a, b = block_scan(a_s[0, rows, :], b_s[0, rows, :], False)
        h = a * carry + b
        hsum[rows, :] = h
        return h[7:8, :]

    def bwd(i, carry):
        rows = pl.ds(pl.multiple_of((nb - 1 - i) * 8, 8), 8)
        a, b = block_scan(a_s[1, rows, :], b_s[1, rows, :], True)
        h = a * carry + b
        hsum[rows, :] = hsum[rows, :] + h
        return h[0:1, :]

    zero = jnp.zeros((1, LANE), F32)
    hf = lax.fori_loop(0, nb, fwd, h0_ref[0] if has_state else zero, unroll=4)
    hb = lax.fori_loop(0, nb, bwd, h0_ref[1] if has_state else zero, unroll=4)
    out_ref[...] = (hsum[...] * _gelu_tanh(y_ref[...].astype(F32))).astype(BF16)
    hl_ref[0] = hf
    hl_ref[1] = hb


def rglru_mixer(proj, conv_w, conv_b, wa, ba, wx, bx, lam, row_block_off, nseq, seq_len, h0):
    has_state = h0 is not None
    kern = functools.partial(_lru_kernel, seq_len=seq_len, has_state=has_state)
    col = lambda cb: pl.BlockSpec((seq_len, LANE), lambda b, h: (row_block_off + b, cb + h))
    w_spec = pl.BlockSpec((2, None, LANE, LANE), lambda b, h: (0, h, 0, 0))
    v_spec = pl.BlockSpec((2, 1, LANE), lambda b, h: (0, 0, h))
    s_spec = pl.BlockSpec((None, 2, 1, LANE), lambda b, h: (b, 0, 0, h))
    in_specs = [col(CB_LRX), col(CB_LRY),
                pl.BlockSpec((4, LANE), lambda b, h: (0, h)),
                pl.BlockSpec((1, LANE), lambda b, h: (0, h)),
                w_spec, v_spec, w_spec, v_spec, v_spec]
    args = [proj, proj, conv_w, conv_b, wa, ba, wx, bx, lam]
    if has_state:
        in_specs.append(s_spec)
        args.append(h0)
    return pl.pallas_call(
        kern,
        grid=(nseq, LRU_H),
        in_specs=in_specs,
        out_specs=[pl.BlockSpec((seq_len, LANE), lambda b, h: (b, h)), s_spec],
        out_shape=[jax.ShapeDtypeStruct((nseq * seq_len, MIX_W), BF16),
                   jax.ShapeDtypeStruct((nseq, 2, 1, MIX_W), F32)],
        scratch_shapes=[pltpu.VMEM((2, seq_len, LANE), F32), pltpu.VMEM((2, seq_len, LANE), F32),
                        pltpu.VMEM((seq_len, LANE), F32)],
        compiler_params=_cp(("parallel", "parallel")),
        name=f"rglru_{seq_len}",
    )(*args)


@functools.lru_cache(maxsize=None)
def _rope_tables():
    t = np.arange(DEC_SEQ)
    pos_row, pos_col = (t // GRID_W).astype(np.float64), (t % GRID_W).astype(np.float64)
    half = DA_DK // 2
    inv = ROPE_BASE ** (-np.arange(0, half, 2, dtype=np.float64) / half)
    lane = np.arange(LANE)
    sub = lane % DA_DK
    pos = np.where((sub < half)[None, :], pos_row[:, None], pos_col[:, None])
    ang = pos * inv[sub % (half // 2)][None, :]
    first = (sub % half) < (half // 2)
    cos = np.cos(ang)
    sin = np.where(first[None, :], -np.sin(ang), np.sin(ang))
    cos_all = np.concatenate([np.ones((T_CTX, LANE))] + [cos] * DEC_BATCH, axis=0)
    sin_all = np.concatenate([np.zeros((T_CTX, LANE))] + [sin] * DEC_BATCH, axis=0)
    pm = np.kron(np.eye(2), np.full((DA_DK, DA_DK), 1.0 / DA_DK))
    return np.asarray(cos_all, np.float32), np.asarray(sin_all, np.float32), np.asarray(pm, np.float32)


def _qkprep_kernel(q_ref, k_ref, cos_ref, sin_ref, qg_ref, kg_ref, pm_ref, qo_ref, ko_ref, kn_ref):
    pm = pm_ref[...]
    cos = cos_ref[...]
    sin = sin_ref[...]
    lane = lax.broadcasted_iota(I32, cos.shape, 1)
    first = (lane % (DA_DK // 2)) < (DA_DK // 4)

    def norm(x, g):
        ms = jnp.dot(x * x, pm, precision=HIGHEST, preferred_element_type=F32)
        return x * lax.rsqrt(ms + EPS) * g

    def rope(x):
        partner = jnp.where(first, pltpu.roll(x, LANE - DA_DK // 4, 1), pltpu.roll(x, DA_DK // 4, 1))
        return x * cos + partner * sin

    q = norm(q_ref[...].astype(F32), qg_ref[...])
    k = norm(k_ref[...].astype(F32), kg_ref[...])
    kn_ref[...] = k
    qo_ref[...] = (rope(q) * (DA_DK ** -0.5)).astype(BF16)
    ko_ref[...] = rope(k).astype(BF16)


def qk_prepare(proj, q_gain, k_gain):
    cos, sin, pm = (jnp.asarray(t) for t in _rope_tables())
    tm = 512
    blk = lambda cb: pl.BlockSpec((tm, LANE), lambda i, h: (i, cb + h))
    tab = pl.BlockSpec((tm, LANE), lambda i, h: (i, 0))
    one = pl.BlockSpec((1, LANE), lambda i, h: (0, 0))
    out = pl.BlockSpec((tm, LANE), lambda i, h: (i, h))
    return pl.pallas_call(
        _qkprep_kernel,
        grid=(T_ALL // tm, DA_H),
        in_specs=[blk(CB_DAQ), blk(CB_DAK), tab, tab, one, one, pl.BlockSpec((LANE, LANE), lambda i, h: (0, 0))],
        out_specs=[out, out, out],
        out_shape=[jax.ShapeDtypeStruct((T_ALL, MIX_W), BF16), jax.ShapeDtypeStruct((T_ALL, MIX_W), BF16),
                   jax.ShapeDtypeStruct((T_ALL, MIX_W), F32)],
        compiler_params=_cp(("parallel", "parallel")),
        name="qk_prepare",
    )(proj, proj, cos, sin, q_gain, k_gain, pm)


def _attn_kernel(*refs, has_cache, lam_init):
    if has_cache:
        q_ref, k_ref, v_ref, kc_ref, vc_ref, lp_ref, sg_ref, o_ref = refs
    else:
        q_ref, k_ref, v_ref, lp_ref, sg_ref, o_ref = refs
    q = q_ref[...]
    lane = lax.broadcasted_iota(I32, q.shape, 1)
    zero = jnp.zeros_like(q)
    k = k_ref[...]
    lp = lp_ref[...]
    lam = (jnp.exp(jnp.sum(lp[0:1] * lp[1:2], axis=1, keepdims=True))
           - jnp.exp(jnp.sum(lp[2:3] * lp[3:4], axis=1, keepdims=True)) + lam_init)
    if has_cache:
        kc = kc_ref[...].astype(BF16)

    def softmax_half(qc):
        s = _dot_nt(qc, k)
        m = jnp.max(s, axis=-1, keepdims=True)
        if has_cache:
            sc = _dot_nt(qc, kc)
            m = jnp.maximum(m, jnp.max(sc, axis=-1, keepdims=True))
            pc = jnp.exp(sc - m)
        p = jnp.exp(s - m)
        l = jnp.sum(p, axis=-1, keepdims=True)
        if has_cache:
            l = l + jnp.sum(pc, axis=-1, keepdims=True)
            return p / l, pc / l
        return p / l, None

    p0, pc0 = softmax_half(jnp.where(lane < DA_DK, q, zero))
    p1, pc1 = softmax_half(jnp.where(lane >= DA_DK, q, zero))
    o = jnp.dot((p0 - lam * p1).astype(BF16), v_ref[...], preferred_element_type=F32)
    if has_cache:
        o = o + jnp.dot((pc0 - lam * pc1).astype(BF16), vc_ref[...].astype(BF16), preferred_element_type=F32)
    o = o * lax.rsqrt(jnp.mean(o * o, axis=-1, keepdims=True) + EPS) * sg_ref[...]
    o_ref[...] = (o * (1.0 - lam_init)).astype(BF16)


def diff_attention(q_rot, k_rot, proj, lam_p, sub_gain, lam_init, row_block_off, nseq, seq_len, cache, layer):
    has_cache = cache is not None
    tq = 256
    kern = functools.partial(_attn_kernel, has_cache=has_cache, lam_init=lam_init)
    nq = seq_len // tq
    kv = lambda cb: pl.BlockSpec((seq_len, LANE), lambda b, h, i: (row_block_off + b, cb + h))
    in_specs = [pl.BlockSpec((tq, LANE), lambda b, h, i: ((row_block_off + b) * nq + i, h)), kv(0), kv(CB_DAV)]
    args = [q_rot, k_rot, proj]
    if has_cache:
        past = cache[0].shape[2]
        cspec = pl.BlockSpec((None, None, past, LANE), lambda b, h, i: (b, layer, 0, h))
        in_specs += [cspec, cspec]
        args += list(cache)
    in_specs += [pl.BlockSpec((4, DA_DK), lambda b, h, i: (0, 0)), pl.BlockSpec((1, LANE), lambda b, h, i: (0, 0))]
    args += [lam_p, sub_gain]
    return pl.pallas_call(
        kern,
        grid=(nseq, DA_H, nq),
        in_specs=in_specs,
        out_specs=pl.BlockSpec((tq, LANE), lambda b, h, i: (b * nq + i, h)),
        out_shape=jax.ShapeDtypeStruct((nseq * seq_len, MIX_W), BF16),
        compiler_params=_cp(("parallel", "parallel", "arbitrary")),
        name=f"diff_attention_{seq_len}",
    )(*args)


def _merge_kernel(h_ref, y0_ref, y1_ref, y2_ref, y3_ref, wg0_ref, wg1_ref, wg2_ref, wg3_ref, wb_ref, o_ref):
    h = h_ref[...]
    acc = None
    for bi, (y_ref, wg_ref) in enumerate(((y0_ref, wg0_ref), (y1_ref, wg1_ref), (y2_ref, wg2_ref), (y3_ref, wg3_ref))):
        gate = jax.nn.sigmoid(jnp.dot(h, wg_ref[...], preferred_element_type=F32))
        term = gate * jnp.dot(y_ref[...], wb_ref[bi], preferred_element_type=F32)
        acc = term if acc is None else acc + term
    o_ref[...] = acc.astype(BF16)


def branch_merge(h, ys, w_gate, w_branch):
    tm, tn = 1024, 256
    nj = D_MODEL // tn
    yspec = pl.BlockSpec((tm, MIX_W), lambda i, j: (i, 0))
    gspec = lambda bi: pl.BlockSpec((D_MODEL, tn), lambda i, j: (0, bi * nj + j))
    return pl.pallas_call(
        _merge_kernel,
        grid=(T_ALL // tm, nj),
        in_specs=[pl.BlockSpec((tm, D_MODEL), lambda i, j: (i, 0)), yspec, yspec, yspec, yspec,
                  gspec(0), gspec(1), gspec(2), gspec(3),
                  pl.BlockSpec((N_BRANCH, MIX_W, tn), lambda i, j: (0, 0, j))],
        out_specs=pl.BlockSpec((tm, tn), lambda i, j: (i, j)),
        out_shape=jax.ShapeDtypeStruct((T_ALL, D_MODEL), BF16),
        compiler_params=_cp(("parallel", "parallel")),
        name="branch_merge",
    )(h, *ys, w_gate, w_gate, w_gate, w_gate, w_branch)


def _outproj_kernel(m_ref, x_ref, g1_ref, w_ref, n2_ref, sh2_ref, sc2_ref, rt_ref, x1_ref, h2_ref, lg_ref):
    y = jnp.dot(m_ref[...], w_ref[...], preferred_element_type=F32)
    x1 = x_ref[...] + g1_ref[...] * y
    x1_ref[...] = x1
    h2 = _norm_mod(x1, n2_ref[...], sc2_ref[...], sh2_ref[...])
    h2_ref[...] = h2
    lg_ref[...] = lax.dot_general(rt_ref[...], h2, (((1,), (1,)), ((), ())), precision=HIGHEST,
                                  preferred_element_type=F32)


def out_projection(merged, x, mod4, layer, w_out, norm2_g, router_t):
    tm = 256
    return pl.pallas_call(
        _outproj_kernel,
        grid=(T_ALL // tm,),
        in_specs=[pl.BlockSpec((tm, D_MODEL), lambda i: (i, 0)),
                  pl.BlockSpec((tm, D_MODEL), lambda i: (i, 0)),
                  _mod_spec(layer, 2, tm, 1),
                  pl.BlockSpec((D_MODEL, D_MODEL), lambda i: (0, 0)),
                  pl.BlockSpec((None, 1, D_MODEL), lambda i: (layer, 0, 0)),
                  _mod_spec(layer, 3, tm, 1),
                  _mod_spec(layer, 4, tm, 1),
                  pl.BlockSpec((N_EXPERTS, D_MODEL), lambda i: (0, 0))],
        out_specs=[pl.BlockSpec((tm, D_MODEL), lambda i: (i, 0)),
                   pl.BlockSpec((tm, D_MODEL), lambda i: (i, 0)),
                   pl.BlockSpec((N_EXPERTS, tm), lambda i: (0, i))],
        out_shape=[jax.ShapeDtypeStruct((T_ALL, D_MODEL), F32),
                   jax.ShapeDtypeStruct((T_ALL, D_MODEL), F32),
                   jax.ShapeDtypeStruct((N_EXPERTS, T_ALL), F32)],
        compiler_params=_cp(("parallel",)),
        name="out_projection",
    )(merged, x, mod4, w_out, norm2_g, mod4, mod4, router_t)


MOE_TM = 256
N_ASSIGN = T_ALL * TOP_K
MOE_BLOCKS = (N_ASSIGN + N_EXPERTS * (MOE_TM - 1)) // MOE_TM + 1
N_SLOTS = MOE_BLOCKS * MOE_TM
ROUTER_TT = 512
GROUP_SIZE = N_EXPERTS // N_GROUPS


def _router_kernel(lg_ref, bias_ref, tri_ref, eidx_ref, w_ref, rank_ref, cnt_ref, carry_ref):
    i = pl.program_id(0)

    @pl.when(i == 0)
    def _():
        carry_ref[...] = jnp.zeros_like(carry_ref)

    tt = lg_ref.shape[1]
    shape3 = (N_GROUPS, GROUP_SIZE, tt)
    neg = -jnp.inf
    scores = jax.nn.sigmoid(lg_ref[...]).reshape(shape3)
    sel = scores + bias_ref[...].reshape(N_GROUPS, GROUP_SIZE, 1)
    gi = lax.broadcasted_iota(I32, shape3, 0)
    ji = lax.broadcasted_iota(I32, shape3, 1)
    ei = gi * GROUP_SIZE + ji
    m1 = jnp.max(sel, axis=1, keepdims=True)
    first = jnp.min(jnp.where(sel == m1, ji, GROUP_SIZE), axis=1, keepdims=True)
    m2 = jnp.max(jnp.where(ji == first, neg, sel), axis=1, keepdims=True)
    cur = m1 + m2
    g1 = lax.broadcasted_iota(I32, cur.shape, 0)
    gmask = jnp.zeros(cur.shape, jnp.bool_)
    for _ in range(TOPK_GROUPS):
        mx = jnp.max(cur, axis=0, keepdims=True)
        idx = jnp.min(jnp.where(cur == mx, g1, N_GROUPS), axis=0, keepdims=True)
        hit = g1 == idx
        gmask = jnp.logical_or(gmask, hit)
        cur = jnp.where(hit, neg, cur)
    masked = jnp.where(gmask, sel, neg)

    def all_max(a):
        return jnp.max(jnp.max(a, axis=1, keepdims=True), axis=0, keepdims=True)

    def all_min(a):
        return jnp.min(jnp.min(a, axis=1, keepdims=True), axis=0, keepdims=True)

    def all_sum(a):
        return jnp.sum(jnp.sum(a, axis=1, keepdims=True), axis=0, keepdims=True)

    hits, idxs, ws = [], [], []
    for _ in range(TOP_K):
        mx = all_max(masked)
        idx = all_min(jnp.where(masked == mx, ei, N_EXPERTS))
        hit = ei == idx
        hits.append(hit)
        idxs.append(idx)
        ws.append(all_sum(jnp.where(hit, scores, 0.0)))
        masked = jnp.where(hit, neg, masked)
    wsum = ws[0]
    for wk in ws[1:]:
        wsum = wsum + wk
    onehot = jnp.zeros(shape3, F32)
    for hit in hits:
        onehot = onehot + hit.astype(F32)
    oh2 = onehot.reshape(N_EXPERTS, tt)
    before = jnp.dot(oh2.astype(BF16), tri_ref[...], preferred_element_type=F32) + carry_ref[:, 0:1]
    before3 = before.reshape(shape3)
    for k in range(TOP_K):
        eidx_ref[k:k + 1, :] = idxs[k].reshape(1, tt)
        w_ref[k:k + 1, :] = (ws[k] / wsum * ROUTED_SCALE).reshape(1, tt)
        rank_ref[k:k + 1, :] = all_sum(jnp.where(hits[k], before3, 0.0)).reshape(1, tt).astype(I32)
    eidx_ref[TOP_K:, :] = jnp.zeros((8 - TOP_K, tt), I32)
    w_ref[TOP_K:, :] = jnp.zeros((8 - TOP_K, tt), F32)
    rank_ref[TOP_K:, :] = jnp.zeros((8 - TOP_K, tt), I32)
    carry_ref[...] = carry_ref[...] + jnp.sum(oh2, axis=1, keepdims=True)
    cnt_ref[...] = carry_ref[...]


def moe_route(logits_t, bias):
    tt = ROUTER_TT
    tri = jnp.asarray(np.triu(np.ones((tt, tt), np.float32), 1), BF16)
    row8 = pl.BlockSpec((8, tt), lambda i: (0, i))
    return pl.pallas_call(
        _router_kernel,
        grid=(T_ALL // tt,),
        in_specs=[pl.BlockSpec((N_EXPERTS, tt), lambda i: (0, i)),
                  pl.BlockSpec((N_EXPERTS, 1), lambda i: (0, 0)),
                  pl.BlockSpec((tt, tt), lambda i: (0, 0))],
        out_specs=[row8, row8, row8, pl.BlockSpec((N_EXPERTS, LANE), lambda i: (0, 0))],
        out_shape=[jax.ShapeDtypeStruct((8, T_ALL), I32), jax.ShapeDtypeStruct((8, T_ALL), F32),
                   jax.ShapeDtypeStruct((8, T_ALL), I32), jax.ShapeDtypeStruct((N_EXPERTS, LANE), F32)],
        scratch_shapes=[pltpu.VMEM((N_EXPERTS, LANE), F32)],
        compiler_params=_cp(("arbitrary",)),
        name="moe_route",
    )(logits_t, bias, tri)


def _dispatch_kernel(slots_ref, h2_ref, xs_hbm, sem, *, tm):
    i = pl.program_id(0)

    def body(r, _):
        for k in range(TOP_K):
            s = slots_ref[(i * tm + r) * TOP_K + k]
            pltpu.make_async_copy(h2_ref.at[pl.ds(r, 1), :], xs_hbm.at[pl.ds(s, 1), :], sem).start()
        return 0

    lax.fori_loop(0, tm, body, 0)
    for k in range(TOP_K):
        pltpu.make_async_copy(h2_ref, xs_hbm.at[pl.ds(0, tm), :], sem).wait()


def moe_dispatch(slots, h2):
    tm = 256
    return pl.pallas_call(
        functools.partial(_dispatch_kernel, tm=tm),
        grid_spec=pltpu.PrefetchScalarGridSpec(
            num_scalar_prefetch=1,
            grid=(T_ALL // tm,),
            in_specs=[pl.BlockSpec((tm, D_MODEL), lambda i, s: (i, 0))],
            out_specs=pl.BlockSpec(memory_space=pl.ANY),
            scratch_shapes=[pltpu.SemaphoreType.DMA(())]),
        out_shape=jax.ShapeDtypeStruct((N_SLOTS, D_MODEL), F32),
        compiler_params=_cp(("arbitrary",)),
        name="moe_dispatch",
    )(slots, h2)


def _gmm_kernel(be_ref, bv_ref, x_ref, wg_ref, wu_ref, wd_ref, y_ref, wgb, wub, wdb):
    i = pl.program_id(0)
    e = be_ref[i]
    prev = be_ref[jnp.maximum(i - 1, 0)]

    @pl.when(jnp.logical_or(i == 0, e != prev))
    def _():
        step = 512
        for r in range(0, D_MODEL, step):
            wgb[r:r + step, :] = wg_ref[r:r + step, :].astype(BF16)
            wub[r:r + step, :] = wu_ref[r:r + step, :].astype(BF16)
        for r in range(0, D_EXPERT, 128):
            wdb[r:r + 128, :] = wd_ref[r:r + 128, :].astype(BF16)

    nv = bv_ref[i]

    @pl.when(nv > 0)
    def _():
        row = lax.broadcasted_iota(I32, (x_ref.shape[0], 1), 0)
        x = jnp.where(row < nv, x_ref[...], 0.0).astype(BF16)
        g = jnp.dot(x, wgb[...], preferred_element_type=F32)
        u = jnp.dot(x, wub[...], preferred_element_type=F32)
        a = (g * jax.nn.sigmoid(g) * u).astype(BF16)
        y_ref[...] = jnp.dot(a, wdb[...], preferred_element_type=F32)


def moe_experts(block_e, block_valid, xs, w_gate, w_up, w_down, layer):
    tm = MOE_TM
    wspec = lambda a, b: pl.BlockSpec((None, None, a, b), lambda i, be, bv: (layer, be[i], 0, 0))
    return pl.pallas_call(
        _gmm_kernel,
        grid_spec=pltpu.PrefetchScalarGridSpec(
            num_scalar_prefetch=2,
            grid=(MOE_BLOCKS,),
            in_specs=[pl.BlockSpec((tm, D_MODEL), lambda i, be, bv: (i, 0)),
                      wspec(D_MODEL, D_EXPERT), wspec(D_MODEL, D_EXPERT), wspec(D_EXPERT, D_MODEL)],
            out_specs=pl.BlockSpec((tm, D_MODEL), lambda i, be, bv: (i, 0)),
            scratch_shapes=[pltpu.VMEM((D_MODEL, D_EXPERT), BF16), pltpu.VMEM((D_MODEL, D_EXPERT), BF16),
                            pltpu.VMEM((D_EXPERT, D_MODEL), BF16)]),
        out_shape=jax.ShapeDtypeStruct((N_SLOTS, D_MODEL), F32),
        compiler_params=_cp(("arbitrary",)),
        name="moe_experts",
    )(block_e, block_valid, xs, w_gate, w_up, w_down)


def _combine_kernel(slots_ref, x1_ref, h2_ref, w_ref, g2_ref, sg_ref, su_ref, sd_ref, ys_hbm, out_ref, buf, sem,
                    *, tm, nsteps):
    i = pl.program_id(0)

    def issue(step, slot):
        def body(r, _):
            for k in range(TOP_K):
                s = slots_ref[(step * tm + r) * TOP_K + k]
                pltpu.make_async_copy(ys_hbm.at[pl.ds(s, 1), :], buf.at[slot, pl.ds(k * tm + r, 1), :],
                                      sem.at[slot]).start()
            return 0

        lax.fori_loop(0, tm, body, 0)

    @pl.when(i == 0)
    def _():
        issue(0, 0)

    @pl.when(i + 1 < nsteps)
    def _():
        issue(i + 1, (i + 1) % 2)

    slot = i % 2
    pltpu.make_async_copy(ys_hbm.at[pl.ds(0, TOP_K * tm), :], buf.at[slot], sem.at[slot]).wait()
    hb = h2_ref[...].astype(BF16)
    g = jnp.dot(hb, sg_ref[...], preferred_element_type=F32)
    u = jnp.dot(hb, su_ref[...], preferred_element_type=F32)
    acc = jnp.dot((g * jax.nn.sigmoid(g) * u).astype(BF16), sd_ref[...], preferred_element_type=F32)
    w = w_ref[...]
    for k in range(TOP_K):
        acc = acc + w[:, k:k + 1] * buf[slot, k * tm:(k + 1) * tm, :]
    out_ref[...] = x1_ref[...] + g2_ref[...] * acc


def moe_combine(slots, x1, h2, w_tok, mod4, layer, sg, su, sd, ys):
    tm = 128
    nsteps = T_ALL // tm
    kern = functools.partial(_combine_kernel, tm=tm, nsteps=nsteps)
    row = lambda n: pl.BlockSpec((tm, n), lambda i, s: (i, 0))
    return pl.pallas_call(
        kern,
        grid_spec=pltpu.PrefetchScalarGridSpec(
            num_scalar_prefetch=1,
            grid=(nsteps,),
            in_specs=[row(D_MODEL), row(D_MODEL), row(8),
                      pl.BlockSpec((None, None, 1, D_MODEL),
                                   lambda i, s: (layer, _mod_row(i, tm) * 6 + 5, 0, 0)),
                      pl.BlockSpec((D_MODEL, D_EXPERT), lambda i, s: (0, 0)),
                      pl.BlockSpec((D_MODEL, D_EXPERT), lambda i, s: (0, 0)),
                      pl.BlockSpec((D_EXPERT, D_MODEL), lambda i, s: (0, 0)),
                      pl.BlockSpec(memory_space=pl.ANY)],
            out_specs=row(D_MODEL),
            scratch_shapes=[pltpu.VMEM((2, TOP_K * tm, D_MODEL), F32), pltpu.SemaphoreType.DMA((2,))]),
        out_shape=jax.ShapeDtypeStruct((T_ALL, D_MODEL), F32),
        compiler_params=_cp(("arbitrary",)),
        name="moe_combine",
    )(slots, x1, h2, w_tok, mod4, sg, su, sd, ys)


def moe_ffn(x1, h2, logits_t, mod4, layer, moe_bias, w_gate, w_up, w_down, sg, su, sd):
    eidx, w_t, rank, cnt = moe_route(logits_t, moe_bias)
    counts = cnt[:, 0].astype(I32)
    padded = (counts + MOE_TM - 1) // MOE_TM * MOE_TM
    pad_end = jnp.cumsum(padded)
    pad_start = pad_end - padded
    slots = (pad_start[eidx[:TOP_K]] + rank[:TOP_K]).T.reshape(-1)
    starts = jnp.arange(MOE_BLOCKS, dtype=I32) * MOE_TM
    block_e = jnp.minimum(jnp.searchsorted(pad_end, starts, side='right'), N_EXPERTS - 1).astype(I32)
    block_valid = jnp.clip(counts[block_e] - (starts - pad_start[block_e]), 0, MOE_TM).astype(I32)
    xs = moe_dispatch(slots, h2)
    ys = moe_experts(block_e, block_valid, xs, w_gate, w_up, w_down, layer)
    return moe_combine(slots, x1, h2, w_t.T, mod4, layer, sg, su, sd, ys)


def _ml_gate_layouts(gp, gate_bias):
    nt = gp.shape[0]
    g16 = gp[:, :4 * ML_H].reshape(nt, 2, 2, ML_H)
    gr = jnp.transpose(g16, (3, 0, 1, 2)).reshape(ML_H, nt, 4)
    gc = jnp.transpose(gr, (0, 2, 1))
    b = jnp.transpose(gate_bias, (2, 0, 1)).reshape(ML_H, 4)
    return gr, gc, b[:, None, :], b[:, :, None]


def kernel(x_prompt, x_sample, c, cache_k, cache_v, state_mlstm_C, state_mlstm_n, state_mlstm_m, state_rglru,
           c_ctx, w_mod, b_mod, norm1, norm2, w_in, hy_short, hy_w1, hy_b1, hy_w2, hy_b2, hy_w3, hy_freq,
           hy_decay, hy_skip, ml_gate_bias, ml_out_norm, lru_conv_w, lru_conv_b, lru_wa, lru_ba, lru_wx, lru_bx,
           lru_lambda, da_q_norm, da_k_norm, da_lambda, da_sub_norm, w_branch, w_out, moe_router, moe_bias,
           moe_w_gate, moe_w_up, moe_w_down, sh_w_gate, sh_w_up, sh_w_down):
    x = jnp.concatenate([x_prompt.reshape(T_CTX, D_MODEL), x_sample.reshape(T_DEN, D_MODEL)], axis=0)
    cvecs = jnp.concatenate([c_ctx[None], c, jnp.zeros((N_MOD_ROWS - 1 - DEC_BATCH, D_MODEL), F32)], axis=0)
    mod4 = modulation_all(cvecs, w_mod, b_mod).reshape(DEPTH, N_MOD_ROWS * 6, 1, D_MODEL)
    norm1_3 = norm1.reshape(DEPTH, 1, D_MODEL)
    norm2_3 = norm2.reshape(DEPTH, 1, D_MODEL)
    cache_k4 = cache_k.reshape(DEC_BATCH, DEPTH, -1, MIX_W)
    cache_v4 = cache_v.reshape(DEC_BATCH, DEPTH, -1, MIX_W)
    den_off = T_CTX // DEC_SEQ
    new_k, new_v, new_c, new_n, new_m, new_h = [], [], [], [], [], []
    for l in range(DEPTH):
        lam_init = 0.8 - 0.6 * math.exp(-0.3 * l)
        wl = w_in[l]
        n_ml = 4 * ML_H
        w_main = jnp.concatenate([wl[:, :7 * MIX_W], wl[:, 7 * MIX_W + n_ml:12 * MIX_W + n_ml]], axis=1).astype(BF16)
        w_g16 = jnp.pad(wl[:, 7 * MIX_W:7 * MIX_W + n_ml], ((0, 0), (0, LANE - n_ml))).astype(BF16)
        w_gate = wl[:, 12 * MIX_W + n_ml:].astype(BF16)
        proj, h, gp = in_projection(x, norm1_3, mod4, l, w_main, w_g16)

        w1p = jnp.pad(hy_w1[l], ((0, LANE - HY_PE), (0, 0)))
        ws3 = jnp.transpose(hy_short[l].reshape(3, 3, MIX_W), (1, 0, 2))
        y_hy = []
        for off, nseq, sl in ((0, BATCH, SEQ), (den_off, DEC_BATCH, DEC_SEQ)):
            kfilt = hyena_filter(sl, w1p, hy_b1[l][None], hy_w2[l], hy_b2[l][None], hy_freq[l][None], hy_w3[l],
                                 hy_decay[l][None])
            kf = hyena_filter_fft(sl, kfilt)
            y_hy.append(hyena_conv(proj, off, nseq, sl, ws3, kf, hy_skip[l]))

        gr, gc, b_r, b_c = _ml_gate_layouts(gp, ml_gate_bias[l])
        gain = ml_out_norm[l][None]
        y_ml0, c_new, n_new, m_new = mlstm_mixer(proj, gr, gc, b_r, b_c, gain, 0, BATCH, SEQ, None)
        st = (state_mlstm_C[:, l], state_mlstm_n[:, l][:, :, :, None, :],
              jnp.broadcast_to(state_mlstm_m[:, l][:, :, :, None, None], (DEC_BATCH, 2, ML_H, 1, LANE)))
        y_ml1 = mlstm_mixer(proj, gr, gc, b_r, b_c, gain, den_off, DEC_BATCH, DEC_SEQ, st)[0]

        lru_args = (lru_conv_w[l], lru_conv_b[l][None], lru_wa[l], lru_ba[l][:, None, :], lru_wx[l],
                    lru_bx[l][:, None, :], lru_lambda[l][:, None, :])
        y_lr0, h_new = rglru_mixer(proj, *lru_args, 0, BATCH, SEQ, None)
        y_lr1 = rglru_mixer(proj, *lru_args, den_off, DEC_BATCH, DEC_SEQ, state_rglru[:, l][:, :, None, :])[0]

        q_rot, k_rot, k_norm = qk_prepare(proj, jnp.tile(da_q_norm[l], 2)[None], jnp.tile(da_k_norm[l], 2)[None])
        sub_gain = da_sub_norm[l][None]
        y_da0 = diff_attention(q_rot, k_rot, proj, da_lambda[l], sub_gain, lam_init, 0, BATCH, SEQ, None, l)
        y_da1 = diff_attention(q_rot, k_rot, proj, da_lambda[l], sub_gain, lam_init, den_off, DEC_BATCH, DEC_SEQ,
                               (cache_k4, cache_v4), l)

        ys = [jnp.concatenate(p, axis=0) for p in (y_hy, (y_ml0, y_ml1), (y_lr0, y_lr1), (y_da0, y_da1))]
        merged = branch_merge(h, ys, w_gate, w_branch[l].astype(BF16))
        x1, h2, logits_t = out_projection(merged, x, mod4, l, w_out[l].astype(BF16), norm2_3, moe_router[l].T)
        x = moe_ffn(x1, h2, logits_t, mod4, l, moe_bias[l][:, None], moe_w_gate, moe_w_up, moe_w_down,
                    sh_w_gate[l].astype(BF16), sh_w_up[l].astype(BF16), sh_w_down[l].astype(BF16))

        new_k.append(k_norm[:T_CTX].reshape(BATCH, SEQ, DA_H, 2, DA_DK))
        new_v.append(proj[:T_CTX, CB_DAV * LANE:(CB_DAV + DA_H) * LANE].astype(F32).reshape(BATCH, SEQ, DA_H, -1))
        new_c.append(c_new)
        new_n.append(n_new[:, :, :, 0, :])
        new_m.append(m_new[:, :, :, 0, 0])
        new_h.append(h_new[:, :, 0, :])
    y_prompt = x[:T_CTX].reshape(BATCH, SEQ, D_MODEL)
    y_sample = x[T_CTX:].reshape(DEC_BATCH, DEC_SEQ, D_MODEL)
    stack = lambda parts: jnp.stack(parts, axis=1)
    return (y_prompt, y_sample, stack(new_k), stack(new_v), stack(new_c), stack(new_n), stack(new_m), stack(new_h))
```

```python
import functools
import math

import numpy as np
import jax
import jax.numpy as jnp
from jax import lax
from jax.experimental import pallas as pl
from jax.experimental.pallas import tpu as pltpu

F32 = jnp.float32
BF16 = jnp.bfloat16
I32 = jnp.int32
HIGHEST = lax.Precision.HIGHEST

D_MODEL = 2048
BATCH = 16
SEQ = 256
DEPTH = 2
DEC_BATCH = 4
DEC_SEQ = 4096
GRID_W = 64
N_BRANCH = 4
MIX_W = 512
HY_BANDS = 16
HY_PE = 1 + 2 * HY_BANDS
HY_FFN = 64
ML_H = 4
ML_DK = 128
LRU_H = 4
LRU_C = 8.0
DA_H = 4
DA_DK = 64
ROPE_BASE = 10000.0
N_EXPERTS = 64
TOP_K = 6
N_GROUPS = 8
TOPK_GROUPS = 4
D_EXPERT = 512
ROUTED_SCALE = 2.5
EPS = 1e-6

LANE = 128
T_CTX = BATCH * SEQ
T_DEN = DEC_BATCH * DEC_SEQ
T_ALL = T_CTX + T_DEN
N_MOD_ROWS = 8
VMEM_LIMIT = 56 * 1024 * 1024

CB_HY = 0
CB_MLQ, CB_MLK, CB_MLV, CB_MLO = 12, 16, 20, 24
CB_LRX, CB_LRY = 28, 32
CB_DAQ, CB_DAK, CB_DAV = 36, 40, 44
N_MAIN = 6144


def _cp(sem, vmem=VMEM_LIMIT):
    return pltpu.CompilerParams(dimension_semantics=sem, vmem_limit_bytes=vmem)


def _mod_row(i, rows_per_block):
    nctx = T_CTX // rows_per_block
    per = DEC_SEQ // rows_per_block
    return jnp.where(i < nctx, 0, 1 + (i - nctx) // per)


def _mod_spec(layer, seg, rows_per_block, ngrid):
    if ngrid == 1:
        return pl.BlockSpec((None, None, 1, D_MODEL), lambda i: (layer, _mod_row(i, rows_per_block) * 6 + seg, 0, 0))
    return pl.BlockSpec((None, None, 1, D_MODEL), lambda i, j: (layer, _mod_row(i, rows_per_block) * 6 + seg, 0, 0))


def _mod_kernel(c_ref, w_ref, b_ref, o_ref):
    c = c_ref[...]
    a = (c * jax.nn.sigmoid(c)).astype(BF16)
    o_ref[...] = jnp.dot(a, w_ref[...].astype(BF16), preferred_element_type=F32) + b_ref[...]


def modulation_all(cvecs, w_mod, b_mod):
    tn = 1024
    n = 6 * D_MODEL
    return pl.pallas_call(
        _mod_kernel,
        grid=(DEPTH, n // tn),
        in_specs=[pl.BlockSpec((N_MOD_ROWS, D_MODEL), lambda l, j: (0, 0)),
                  pl.BlockSpec((None, D_MODEL, tn), lambda l, j: (l, 0, j)),
                  pl.BlockSpec((None, 1, tn), lambda l, j: (l, 0, j))],
        out_specs=pl.BlockSpec((None, N_MOD_ROWS, tn), lambda l, j: (l, 0, j)),
        out_shape=jax.ShapeDtypeStruct((DEPTH, N_MOD_ROWS, n), F32),
        compiler_params=_cp(("parallel", "parallel")),
        name="modulation",
    )(cvecs, w_mod, b_mod.reshape(DEPTH, 1, n))


def _norm_mod(x, g, sc, sh):
    ms = jnp.mean(x * x, axis=-1, keepdims=True)
    return (x * lax.rsqrt(ms + EPS) * g) * (1.0 + sc) + sh


def _inproj_kernel(x_ref, g_ref, sh_ref, sc_ref, w_ref, wg_ref, proj_ref, h_ref, gp_ref, *, tm, sub):
    j = pl.program_id(1)

    @pl.when(j == 0)
    def _():
        g = g_ref[...]
        sc = sc_ref[...]
        sh = sh_ref[...]

        def body(r, _):
            rows = pl.ds(pl.multiple_of(r * sub, sub), sub)
            hb = _norm_mod(x_ref[rows, :], g, sc, sh).astype(BF16)
            h_ref[rows, :] = hb
            gp_ref[rows, :] = jnp.dot(hb, wg_ref[...], preferred_element_type=F32)
            return 0

        lax.fori_loop(0, tm // sub, body, 0)

    proj_ref[...] = jnp.dot(h_ref[...], w_ref[...], preferred_element_type=F32).astype(BF16)


def in_projection(x, norm_g, mod4, layer, w_main, w_gate16):
    tm, tn = 1024, 512
    kern = functools.partial(_inproj_kernel, tm=tm, sub=256)
    return pl.pallas_call(
        kern,
        grid=(T_ALL // tm, N_MAIN // tn),
        in_specs=[pl.BlockSpec((tm, D_MODEL), lambda i, j: (i, 0)),
                  pl.BlockSpec((None, 1, D_MODEL), lambda i, j: (layer, 0, 0)),
                  _mod_spec(layer, 0, tm, 2),
                  _mod_spec(layer, 1, tm, 2),
                  pl.BlockSpec((D_MODEL, tn), lambda i, j: (0, j)),
                  pl.BlockSpec((D_MODEL, LANE), lambda i, j: (0, 0))],
        out_specs=[pl.BlockSpec((tm, tn), lambda i, j: (i, j)),
                   pl.BlockSpec((tm, D_MODEL), lambda i, j: (i, 0)),
                   pl.BlockSpec((tm, LANE), lambda i, j: (i, 0))],
        out_shape=[jax.ShapeDtypeStruct((T_ALL, N_MAIN), BF16),
                   jax.ShapeDtypeStruct((T_ALL, D_MODEL), BF16),
                   jax.ShapeDtypeStruct((T_ALL, LANE), F32)],
        compiler_params=_cp(("parallel", "arbitrary")),
        name="in_projection",
    )(x, norm_g, mod4, mod4, w_main, w_gate16)


def _fft_factors(seq_len):
    n2 = 64
    return 2 * seq_len // n2, n2


def _fft_pages(seq_len):
    n1, _ = _fft_factors(seq_len)
    used = n1 // 2 + 1
    return used, -(-used // 8) * 8


@functools.lru_cache(maxsize=None)
def _dft_tables(seq_len):
    n1, n2 = _fft_factors(seq_len)
    used, npg = _fft_pages(seq_len)
    n = n1 * n2
    th1 = 2.0 * np.pi * np.outer(np.arange(npg, dtype=np.float64), np.arange(n1, dtype=np.float64)) / n1
    w1 = np.concatenate([np.cos(th1), -np.sin(th1)], axis=0)
    w1h = w1[:, : n1 // 2]
    wt = np.where(np.arange(npg) < used, 2.0, 0.0)
    wt[0] = 1.0
    wt[n1 // 2] = 1.0
    th4 = th1[:, : n1 // 2].T
    w4 = np.concatenate([np.cos(th4) * wt, -np.sin(th4) * wt], axis=1) / n
    k1 = np.arange(npg, dtype=np.float64)[:, None, None]
    k2 = np.arange(n2, dtype=np.float64)[None, :, None]
    m2 = np.arange(n2, dtype=np.float64)[None, None, :]
    ph = -2.0 * np.pi * (m2 * k1 / n + m2 * k2 / n2)
    gr, gi = np.cos(ph), np.sin(ph)
    g = np.concatenate([np.concatenate([gr, -gi], axis=2), np.concatenate([gi, gr], axis=2)], axis=1)
    gt = np.transpose(g, (0, 2, 1))
    return tuple(np.asarray(t, np.float32) for t in (w1, w1h, g, gt, w4))


@functools.lru_cache(maxsize=None)
def _direct_dft_tables(seq_len):
    n = 2 * seq_len
    th = 2.0 * np.pi * np.outer(np.arange(n, dtype=np.float64), np.arange(n, dtype=np.float64)) / n
    wf = np.concatenate([np.cos(th), -np.sin(th)], axis=0)
    wi = np.concatenate([np.cos(th), -np.sin(th)], axis=1)[:seq_len] / n
    return np.asarray(wf, np.float32), np.asarray(wi, np.float32)


def _dft_tables_bf16(seq_len):
    return tuple(jnp.asarray(t, BF16) for t in _dft_tables(seq_len))


@functools.lru_cache(maxsize=None)
def _filter_positions(seq_len):
    pos = np.concatenate([np.arange(seq_len), [0], np.arange(seq_len - 1, 0, -1)]).astype(np.float64)
    tn = pos / seq_len
    bands = np.linspace(1e-4, HY_BANDS - 1, HY_BANDS)
    ang = (2.0 * math.pi / seq_len) * pos[:, None] * bands
    pe = np.zeros((2 * seq_len, LANE), np.float64)
    pe[:, 0] = tn
    pe[:, 1:1 + HY_BANDS] = np.cos(ang)
    pe[:, 1 + HY_BANDS:HY_PE] = np.sin(ang)
    return np.asarray(pe, np.float32), np.asarray(tn[:, None], np.float32)


def _hy_mlp_kernel(pe_ref, w1_ref, b1_ref, w2_ref, b2_ref, fr_ref, z_ref):
    fr = fr_ref[...]
    z = jnp.sin(fr * (jnp.dot(pe_ref[...], w1_ref[...], precision=HIGHEST, preferred_element_type=F32) + b1_ref[...]))
    z_ref[...] = jnp.sin(fr * (jnp.dot(z, w2_ref[...], precision=HIGHEST, preferred_element_type=F32) + b2_ref[...]))


def _hy_filter_kernel(z_ref, tn_ref, w3a_ref, w3b_ref, dca_ref, dcb_ref, k_ref, *, seq_len):
    def taps(rows, w3_ref, dc_ref):
        f = jnp.dot(z_ref[rows, :].astype(BF16), w3_ref[...].astype(BF16), preferred_element_type=F32)
        return f * jnp.exp(-tn_ref[rows, :] * jnp.abs(dc_ref[...]))

    ka = taps(pl.ds(0, seq_len), w3a_ref, dca_ref)
    kb = taps(pl.ds(seq_len, seq_len), w3b_ref, dcb_ref)
    row = lax.broadcasted_iota(I32, ka.shape, 0)
    ka = ka + jnp.where(row == 0, taps(pl.ds(0, 8), w3b_ref, dcb_ref)[0:1], 0.0)
    kb = jnp.where(row == 0, 0.0, kb)
    inv = 1.0 / (jnp.sum(jnp.abs(ka), axis=0, keepdims=True) + jnp.sum(jnp.abs(kb), axis=0, keepdims=True))
    k_ref[0:seq_len, :] = ka * inv
    k_ref[seq_len:, :] = kb * inv


def hyena_filter(seq_len, w1p, b1, w2, b2, freq, w3, decay):
    pe, tn = (jnp.asarray(t) for t in _filter_positions(seq_len))
    n = 2 * seq_len
    tr = min(n, 1024)
    full = lambda shape: pl.BlockSpec(shape, lambda j: (0,) * len(shape))
    z = pl.pallas_call(
        _hy_mlp_kernel,
        grid=(n // tr,),
        in_specs=[pl.BlockSpec((tr, LANE), lambda j: (j, 0)), full((LANE, HY_FFN)), full((1, HY_FFN)),
                  full((HY_FFN, HY_FFN)), full((1, HY_FFN)), full((1, HY_FFN))],
        out_specs=pl.BlockSpec((tr, HY_FFN), lambda j: (j, 0)),
        out_shape=jax.ShapeDtypeStruct((n, HY_FFN), F32),
        compiler_params=_cp(("parallel",)),
        name=f"hyena_filter_mlp_{seq_len}",
    )(pe, w1p, b1, w2, b2, freq)
    tc = 128
    nc = (2 * MIX_W) // tc
    kern = functools.partial(_hy_filter_kernel, seq_len=seq_len)
    return pl.pallas_call(
        kern,
        grid=(nc,),
        in_specs=[full((n, HY_FFN)), full((n, 1)),
                  pl.BlockSpec((HY_FFN, tc), lambda j: (0, j)),
                  pl.BlockSpec((HY_FFN, tc), lambda j: (0, nc + j)),
                  pl.BlockSpec((1, tc), lambda j: (0, j)),
                  pl.BlockSpec((1, tc), lambda j: (0, nc + j))],
        out_specs=pl.BlockSpec((n, tc), lambda j: (0, j)),
        out_shape=jax.ShapeDtypeStruct((n, 2 * MIX_W), F32),
        compiler_params=_cp(("parallel",)),
        name=f"hyena_filter_{seq_len}",
    )(z, tn, w3, w3, decay, decay)


def _split_bf16(x):
    hi = x.astype(BF16)
    lo = (x - hi.astype(F32)).astype(BF16)
    return hi, lo


def _fft_stage1(load_rows, w1, a_ref, npg, n2, split):
    def body(m, _):
        xs = load_rows(m)
        if split:
            hi, lo = _split_bf16(xs)
            r = jnp.dot(w1, hi, preferred_element_type=F32) + jnp.dot(w1, lo, preferred_element_type=F32)
        else:
            r = jnp.dot(w1, xs.astype(BF16), preferred_element_type=F32)
        a_ref[pl.ds(m, npg, stride=2 * n2), :] = r[:npg]
        a_ref[pl.ds(n2 + m, npg, stride=2 * n2), :] = r[npg:]
        return 0

    lax.fori_loop(0, n2, body, 0, unroll=2)


def _hy_fft_kernel(k_ref, w1_ref, g_ref, kf_ref, a_ref, *, n1, n2, npg):
    w1 = w1_ref[...]
    _fft_stage1(lambda m: k_ref[pl.ds(m, n1, stride=n2), :], w1, a_ref, npg, n2, True)

    def body(p, _):
        rows = pl.ds(pl.multiple_of(p * 2 * n2, 2 * n2), 2 * n2)
        hi, lo = _split_bf16(a_ref[rows, :])
        g = g_ref[p]
        kf_ref[rows, :] = jnp.dot(g, hi, preferred_element_type=F32) + jnp.dot(g, lo, preferred_element_type=F32)
        return 0

    lax.fori_loop(0, npg, body, 0, unroll=4)


def hyena_filter_fft(seq_len, kfilt):
    n1, n2 = _fft_factors(seq_len)
    _, npg = _fft_pages(seq_len)
    w1, _, g, _, _ = _dft_tables_bf16(seq_len)
    tc = 128
    kern = functools.partial(_hy_fft_kernel, n1=n1, n2=n2, npg=npg)
    return pl.pallas_call(
        kern,
        grid=((2 * MIX_W) // tc,),
        in_specs=[pl.BlockSpec((n1 * n2, tc), lambda j: (0, j)),
                  pl.BlockSpec((2 * npg, n1), lambda j: (0, 0)),
                  pl.BlockSpec((npg, 2 * n2, 2 * n2), lambda j: (0, 0, 0))],
        out_specs=pl.BlockSpec((npg * 2 * n2, tc), lambda j: (0, j)),
        out_shape=jax.ShapeDtypeStruct((npg * 2 * n2, 2 * MIX_W), F32),
        scratch_shapes=[pltpu.VMEM((npg * 2 * n2, tc), F32)],
        compiler_params=_cp(("parallel",)),
        name=f"hyena_filter_fft_{seq_len}",
    )(kfilt, w1, g)


def _hy_fft_direct_kernel(k_ref, wf_ref, kf_ref):
    hi, lo = _split_bf16(k_ref[...])
    wf = wf_ref[...]
    kf_ref[...] = jnp.dot(wf, hi, preferred_element_type=F32) + jnp.dot(wf, lo, preferred_element_type=F32)


def hyena_filter_fft_direct(seq_len, kfilt):
    n = 2 * seq_len
    wf = jnp.asarray(_direct_dft_tables(seq_len)[0], BF16)
    tc = 256
    return pl.pallas_call(
        _hy_fft_direct_kernel,
        grid=((2 * MIX_W) // tc,),
        in_specs=[pl.BlockSpec((n, tc), lambda j: (0, j)), pl.BlockSpec((2 * n, n), lambda j: (0, 0))],
        out_specs=pl.BlockSpec((2 * n, tc), lambda j: (0, j)),
        out_shape=jax.ShapeDtypeStruct((2 * n, 2 * MIX_W), F32),
        compiler_params=_cp(("parallel",)),
        name=f"hyena_filter_fft_{seq_len}",
    )(kfilt, wf)


def _shift_rows(x, s, row):
    n = x.shape[0]
    y = pltpu.roll(x, s % n, 0)
    if s > 0:
        return jnp.where(row >= s, y, 0.0)
    return jnp.where(row < n + s, y, 0.0)


def _short_conv3(x_ref, w_ref):
    x = x_ref[...].astype(F32)
    row = lax.broadcasted_iota(I32, x.shape, 0)
    w = w_ref[...]
    return w[0:1] * _shift_rows(x, 1, row) + w[1:2] * x + w[2:3] * _shift_rows(x, -1, row)


def _hy_conv_kernel(v_ref, x1_ref, x2_ref, ws_ref, kf_ref, skip_ref, w1_ref, g_ref, gt_ref, w4_ref, y_ref,
                    z_ref, c_ref, a_ref, *, n1, n2, used, npg):
    o = pl.program_id(2)
    nh = n1 // 2

    @pl.when(o == 0)
    def _():
        z_ref[...] = _short_conv3(v_ref, ws_ref.at[0])

    w1 = w1_ref[...]
    _fft_stage1(lambda m: z_ref[pl.ds(m, nh, stride=n2), :], w1, a_ref, npg, n2, False)

    def page(p, _):
        rows = pl.ds(pl.multiple_of(p * 2 * n2, 2 * n2), 2 * n2)
        x = jnp.dot(g_ref[p], a_ref[rows, :].astype(BF16), preferred_element_type=F32)
        kf = kf_ref[rows, :]
        xr, xi = x[:n2], x[n2:]
        kr, ki = kf[:n2], kf[n2:]
        yc = jnp.concatenate([xr * kr - xi * ki, xr * ki + xi * kr], axis=0).astype(BF16)
        a_ref[rows, :] = jnp.dot(gt_ref[p], yc, preferred_element_type=F32)
        return 0

    lax.fori_loop(0, used, page, 0, unroll=5)

    w4 = w4_ref[...]

    def back(m, _):
        dr = a_ref[pl.ds(m, npg, stride=2 * n2), :]
        di = a_ref[pl.ds(n2 + m, npg, stride=2 * n2), :]
        d = jnp.concatenate([dr, di], axis=0).astype(BF16)
        c_ref[pl.ds(m, nh, stride=n2), :] = jnp.dot(w4, d, preferred_element_type=F32)
        return 0

    lax.fori_loop(0, n2, back, 0, unroll=2)

    @pl.when(o == 0)
    def _():
        z = z_ref[...]
        z_ref[...] = _short_conv3(x1_ref, ws_ref.at[1]) * (c_ref[...] + z * skip_ref[0:1])

    @pl.when(o == 1)
    def _():
        z = z_ref[...]
        y_ref[...] = (_short_conv3(x2_ref, ws_ref.at[2]) * (c_ref[...] + z * skip_ref[1:2])).astype(BF16)


def hyena_conv(proj, row_block_off, nseq, seq_len, w_short3, kf, skip):
    n1, n2 = _fft_factors(seq_len)
    used, npg = _fft_pages(seq_len)
    _, w1h, g, gt, w4 = _dft_tables_bf16(seq_len)
    rows_a = npg * 2 * n2
    tc = 128
    ncb = MIX_W // tc
    kern = functools.partial(_hy_conv_kernel, n1=n1, n2=n2, used=used, npg=npg)
    const3 = lambda c, b, o: (0, 0, 0)
    seg = lambda s: pl.BlockSpec((seq_len, tc), lambda c, b, o: (row_block_off + b, CB_HY + s * ncb + c))
    return pl.pallas_call(
        kern,
        grid=(ncb, nseq, 2),
        in_specs=[seg(0), seg(1), seg(2),
                  pl.BlockSpec((3, 3, tc), lambda c, b, o: (0, 0, c)),
                  pl.BlockSpec((rows_a, tc), lambda c, b, o: (0, o * ncb + c)),
                  pl.BlockSpec((2, tc), lambda c, b, o: (0, c)),
                  pl.BlockSpec((2 * npg, n1 // 2), lambda c, b, o: (0, 0)),
                  pl.BlockSpec((npg, 2 * n2, 2 * n2), const3, pipeline_mode=pl.Buffered(1)),
                  pl.BlockSpec((npg, 2 * n2, 2 * n2), const3, pipeline_mode=pl.Buffered(1)),
                  pl.BlockSpec((n1 // 2, 2 * npg), lambda c, b, o: (0, 0))],
        out_specs=pl.BlockSpec((seq_len, tc), lambda c, b, o: (b, c)),
        out_shape=jax.ShapeDtypeStruct((nseq * seq_len, MIX_W), BF16),
        scratch_shapes=[pltpu.VMEM((seq_len, tc), F32), pltpu.VMEM((seq_len, tc), F32),
                        pltpu.VMEM((rows_a, tc), F32)],
        compiler_params=_cp(("parallel", "parallel", "arbitrary")),
        name=f"hyena_conv_{seq_len}",
    )(proj, proj, proj, w_short3, kf, skip, w1h, g, gt, w4)


def _hy_direct_kernel(u_ref, ws_ref, kf_ref, skip_ref, wf_ref, wi_ref, y_ref, *, n):
    w = MIX_W
    wf = wf_ref[...]
    wi = wi_ref[...]
    z = _short_conv3(u_ref.at[:, pl.ds(0, w)], ws_ref.at[0])
    for o in range(2):
        x = jnp.dot(wf, z.astype(BF16), preferred_element_type=F32)
        kf = kf_ref[:, o * w:(o + 1) * w]
        xr, xi = x[:n], x[n:]
        kr, ki = kf[:n], kf[n:]
        yc = jnp.concatenate([xr * kr - xi * ki, xr * ki + xi * kr], axis=0).astype(BF16)
        conv = jnp.dot(wi, yc, preferred_element_type=F32)
        gate = _short_conv3(u_ref.at[:, pl.ds((o + 1) * w, w)], ws_ref.at[o + 1])
        z = gate * (conv + z * skip_ref[o:o + 1])
    y_ref[...] = z.astype(BF16)


def hyena_conv_direct(proj, row_block_off, nseq, seq_len, w_short3, kf, skip):
    n = 2 * seq_len
    wf_np, wi_np = _direct_dft_tables(seq_len)
    wf = jnp.asarray(wf_np[:, :seq_len], BF16)
    wi = jnp.asarray(wi_np, BF16)
    kern = functools.partial(_hy_direct_kernel, n=n)
    return pl.pallas_call(
        kern,
        grid=(nseq,),
        in_specs=[pl.BlockSpec((seq_len, 3 * MIX_W), lambda b: (row_block_off + b, 0)),
                  pl.BlockSpec((3, 3, MIX_W), lambda b: (0, 0, 0)),
                  pl.BlockSpec((2 * n, 2 * MIX_W), lambda b: (0, 0)),
                  pl.BlockSpec((2, MIX_W), lambda b: (0, 0)),
                  pl.BlockSpec((2 * n, seq_len), lambda b: (0, 0)),
                  pl.BlockSpec((seq_len, 2 * n), lambda b: (0, 0))],
        out_specs=pl.BlockSpec((seq_len, MIX_W), lambda b: (b, 0)),
        out_shape=jax.ShapeDtypeStruct((nseq * seq_len, MIX_W), BF16),
        compiler_params=_cp(("parallel",)),
        name=f"hyena_conv_{seq_len}",
    )(proj, w_short3, kf, skip, wf, wi)


def _dot_nt(a, b):
    return lax.dot_general(a, b, (((1,), (1,)), ((), ())), preferred_element_type=F32)


ML_CHUNK = 128


def _log_sigmoid(x):
    return jnp.minimum(x, 0.0) - jnp.log1p(jnp.exp(-jnp.abs(x)))


def _mlstm_kernel(*refs, seq_len, has_state):
    if has_state:
        (q_ref, k_ref, v_ref, o_ref, gr_ref, gc_ref, br_ref, bc_ref, gain_ref, c0_ref, n0_ref, m0_ref,
         y_ref, cout_ref, nout_ref, mout_ref, hs_ref, cs_ref, ns_ref, ms_ref) = refs
    else:
        (q_ref, k_ref, v_ref, o_ref, gr_ref, gc_ref, br_ref, bc_ref, gain_ref,
         y_ref, cout_ref, nout_ref, mout_ref, hs_ref, cs_ref, ns_ref, ms_ref) = refs
    t = ML_CHUNK
    nc = seq_len // t
    scale = ML_DK ** -0.5
    if has_state:
        cs_ref[...] = c0_ref[...]
        ns_ref[...] = n0_ref[...]
        ms_ref[...] = m0_ref[...]
    else:
        cs_ref[...] = jnp.zeros_like(cs_ref)
        ns_ref[...] = jnp.zeros_like(ns_ref)
        ms_ref[...] = jnp.zeros_like(ms_ref)
    hs_ref[...] = jnp.zeros_like(hs_ref)
    row = lax.broadcasted_iota(I32, (t, t), 0)
    col = lax.broadcasted_iota(I32, (t, t), 1)
    br = br_ref[...]
    bc = bc_ref[...]

    def step(j, _):
        for d in (0, 1):
            jj = j if d == 0 else nc - 1 - j
            rows = pl.ds(pl.multiple_of(jj * t, t), t)
            qf = q_ref[rows, :].astype(F32) * scale
            qb = qf.astype(BF16)
            kb = k_ref[rows, :]
            vb = v_ref[rows, :]
            gr = gr_ref[rows, :] + br
            gc = gc_ref[:, rows] + bc
            i_c = gr[:, 2 * d:2 * d + 1]
            f_c = _log_sigmoid(gr[:, 2 * d + 1:2 * d + 2])
            i_r = gc[2 * d:2 * d + 1, :]
            f_r = _log_sigmoid(gc[2 * d + 1:2 * d + 2, :])
            if d == 0:
                mask, mask_t = col <= row, row <= col
            else:
                mask, mask_t = col >= row, row >= col
            b_c = jnp.sum(jnp.where(mask, f_r, 0.0), axis=1, keepdims=True)
            b_r = jnp.sum(jnp.where(mask_t, f_c, 0.0), axis=0, keepdims=True)
            logw = jnp.where(mask, b_c - b_r + i_r, -jnp.inf)
            m = ms_ref[d][:, 0:1]
            g = b_c + m
            mt = jnp.maximum(g, jnp.max(logw, axis=1, keepdims=True))
            s = _dot_nt(qb, kb) * jnp.exp(logw - mt)
            inter = jnp.exp(g - mt)
            cm = cs_ref[d]
            nv = ns_ref[d]
            num = (jnp.dot(s.astype(BF16), vb, preferred_element_type=F32)
                   + inter * jnp.dot(qb, cm.astype(BF16), preferred_element_type=F32))
            den = jnp.sum(s, axis=1, keepdims=True) + inter * jnp.sum(qf * nv, axis=1, keepdims=True)
            h = num / jnp.maximum(jnp.abs(den), jnp.exp(-mt))
            hs_ref[rows, :] = hs_ref[rows, :] + h
            bl = jnp.sum(f_r, axis=1, keepdims=True)
            wlog_r = bl - b_r + i_r
            wlog_c = bl - b_c + i_c
            m_new = jnp.maximum(bl + m, jnp.max(wlog_r, axis=1, keepdims=True))
            dec = jnp.exp(bl + m - m_new)
            kw = jnp.exp(wlog_c - m_new) * kb.astype(F32)
            cs_ref[d] = dec * cm + jnp.dot(kw.T.astype(BF16), vb, preferred_element_type=F32)
            ns_ref[d] = dec * nv + jnp.sum(kw, axis=0, keepdims=True)
            ms_ref[d] = jnp.broadcast_to(m_new, (1, LANE))
        return 0

    lax.fori_loop(0, nc, step, 0)
    hsum = hs_ref[...]
    hn = hsum * lax.rsqrt(jnp.mean(hsum * hsum, axis=-1, keepdims=True) + EPS) * gain_ref[...]
    y_ref[...] = (jax.nn.sigmoid(o_ref[...].astype(F32)) * hn).astype(BF16)
    cout_ref[...] = cs_ref[...]
    nout_ref[...] = ns_ref[...]
    mout_ref[...] = ms_ref[...]


def mlstm_mixer(proj, gates_r, gates_c, bias_r, bias_c, gain, row_block_off, nseq, seq_len, state):
    has_state = state is not None
    kern = functools.partial(_mlstm_kernel, seq_len=seq_len, has_state=has_state)
    col = lambda cb: pl.BlockSpec((seq_len, LANE), lambda b, h: (row_block_off + b, cb + h))
    c_spec = pl.BlockSpec((None, 2, None, ML_DK, ML_DK), lambda b, h: (b, 0, h, 0, 0))
    v_spec = pl.BlockSpec((None, 2, None, 1, LANE), lambda b, h: (b, 0, h, 0, 0))
    in_specs = [col(CB_MLQ), col(CB_MLK), col(CB_MLV), col(CB_MLO),
                pl.BlockSpec((None, seq_len, 4), lambda b, h: (h, row_block_off + b, 0)),
                pl.BlockSpec((None, 4, seq_len), lambda b, h: (h, 0, row_block_off + b)),
                pl.BlockSpec((None, 1, 4), lambda b, h: (h, 0, 0)),
                pl.BlockSpec((None, 4, 1), lambda b, h: (h, 0, 0)),
                pl.BlockSpec((1, LANE), lambda b, h: (0, h))]
    args = [proj, proj, proj, proj, gates_r, gates_c, bias_r, bias_c, gain]
    if has_state:
        in_specs += [c_spec, v_spec, v_spec]
        args += list(state)
    return pl.pallas_call(
        kern,
        grid=(nseq, ML_H),
        in_specs=in_specs,
        out_specs=[pl.BlockSpec((seq_len, LANE), lambda b, h: (b, h)), c_spec, v_spec, v_spec],
        out_shape=[jax.ShapeDtypeStruct((nseq * seq_len, MIX_W), BF16),
                   jax.ShapeDtypeStruct((nseq, 2, ML_H, ML_DK, ML_DK), F32),
                   jax.ShapeDtypeStruct((nseq, 2, ML_H, 1, LANE), F32),
                   jax.ShapeDtypeStruct((nseq, 2, ML_H, 1, LANE), F32)],
        scratch_shapes=[pltpu.VMEM((seq_len, LANE), F32), pltpu.VMEM((2, ML_DK, ML_DK), F32),
                        pltpu.VMEM((2, 1, LANE), F32), pltpu.VMEM((2, 1, LANE), F32)],
        compiler_params=_cp(("parallel", "parallel")),
        name=f"mlstm_{seq_len}",
    )(*args)


def _softplus(x):
    return jnp.maximum(x, 0.0) + jnp.log1p(jnp.exp(-jnp.abs(x)))


def _gelu_tanh(x):
    return 0.5 * x * (1.0 + jnp.tanh(math.sqrt(2.0 / math.pi) * (x + 0.044715 * (x * x * x))))


def _lru_kernel(*refs, seq_len, has_state):
    if has_state:
        (x_ref, y_ref, cw_ref, cb_ref, wa_ref, ba_ref, wx_ref, bx_ref, lam_ref, h0_ref,
         out_ref, hl_ref, a_s, b_s, hsum) = refs
    else:
        (x_ref, y_ref, cw_ref, cb_ref, wa_ref, ba_ref, wx_ref, bx_ref, lam_ref,
         out_ref, hl_ref, a_s, b_s, hsum) = refs
    xf = x_ref[...].astype(F32)
    row = lax.broadcasted_iota(I32, xf.shape, 0)
    w = cw_ref[...]
    x = (w[0:1] * _shift_rows(xf, 2, row) + w[1:2] * _shift_rows(xf, 1, row) + w[2:3] * xf
         + w[3:4] * _shift_rows(xf, -1, row) + cb_ref[...])
    xb = x.astype(BF16)
    for d in (0, 1):
        r = jax.nn.sigmoid(jnp.dot(xb, wa_ref[d].astype(BF16), preferred_element_type=F32) + ba_ref[d])
        i = jax.nn.sigmoid(jnp.dot(xb, wx_ref[d].astype(BF16), preferred_element_type=F32) + bx_ref[d])
        log_a = (-LRU_C) * r * _softplus(-lam_ref[d])
        a_s[d] = jnp.exp(log_a)
        b_s[d] = jnp.sqrt(jnp.maximum(1.0 - jnp.exp(2.0 * log_a), 0.0)) * (i * x)

    nb = seq_len // 8
    sub = lax.broadcasted_iota(I32, (8, LANE), 0)

    def block_scan(a, b, reverse):
        for s in (1, 2, 4):
            sh = 8 - s if reverse else s
            ok = (sub < 8 - s) if reverse else (sub >= s)
            b = jnp.where(ok, a * pltpu.roll(b, sh, 0) + b, b)
            a = jnp.where(ok, a * pltpu.roll(a, sh, 0), a)
        return a, b

    def fwd(blk, carry):
        rows = pl.ds(pl.multiple_of(blk * 8, 8), 8)
        a, b = block_scan(a_s[0, rows, :], b_s[0, rows, :], False)
        h = a * carry + b
        hsum[rows, :] = h
        return h[7:8, :]

    def bwd(i, carry):
        rows = pl.ds(pl.multiple_of((nb - 1 - i) * 8, 8), 8)
        a, b = block_scan(a_s[1, rows, :], b_s[1, rows, :], True)
        h = a * carry + b
        hsum[rows, :] = hsum[rows, :] + h
        return h[0:1, :]

    zero = jnp.zeros((1, LANE), F32)
    hf = lax.fori_loop(0, nb, fwd, h0_ref[0] if has_state else zero, unroll=4)
    hb = lax.fori_loop(0, nb, bwd, h0_ref[1] if has_state else zero, unroll=4)
    out_ref[...] = (hsum[...] * _gelu_tanh(y_ref[...].astype(F32))).astype(BF16)
    hl_ref[0] = hf
    hl_ref[1] = hb


def rglru_mixer(proj, conv_w, conv_b, wa, ba, wx, bx, lam, row_block_off, nseq, seq_len, h0):
    has_state = h0 is not None
    kern = functools.partial(_lru_kernel, seq_len=seq_len, has_state=has_state)
    col = lambda cb: pl.BlockSpec((seq_len, LANE), lambda b, h: (row_block_off + b, cb + h))
    w_spec = pl.BlockSpec((2, None, LANE, LANE), lambda b, h: (0, h, 0, 0))
    v_spec = pl.BlockSpec((2, 1, LANE), lambda b, h: (0, 0, h))
    s_spec = pl.BlockSpec((None, 2, 1, LANE), lambda b, h: (b, 0, 0, h))
    in_specs = [col(CB_LRX), col(CB_LRY),
                pl.BlockSpec((4, LANE), lambda b, h: (0, h)),
                pl.BlockSpec((1, LANE), lambda b, h: (0, h)),
                w_spec, v_spec, w_spec, v_spec, v_spec]
    args = [proj, proj, conv_w, conv_b, wa, ba, wx, bx, lam]
    if has_state:
        in_specs.append(s_spec)
        args.append(h0)
    return pl.pallas_call(
        kern,
        grid=(nseq, LRU_H),
        in_specs=in_specs,
        out_specs=[pl.BlockSpec((seq_len, LANE), lambda b, h: (b, h)), s_spec],
        out_shape=[jax.ShapeDtypeStruct((nseq * seq_len, MIX_W), BF16),
                   jax.ShapeDtypeStruct((nseq, 2, 1, MIX_W), F32)],
        scratch_shapes=[pltpu.VMEM((2, seq_len, LANE), F32), pltpu.VMEM((2, seq_len, LANE), F32),
                        pltpu.VMEM((seq_len, LANE), F32)],
        compiler_params=_cp(("parallel", "parallel")),
        name=f"rglru_{seq_len}",
    )(*args)


@functools.lru_cache(maxsize=None)
def _rope_tables():
    t = np.arange(DEC_SEQ)
    pos_row, pos_col = (t // GRID_W).astype(np.float64), (t % GRID_W).astype(np.float64)
    half = DA_DK // 2
    inv = ROPE_BASE ** (-np.arange(0, half, 2, dtype=np.float64) / half)
    lane = np.arange(LANE)
    sub = lane % DA_DK
    pos = np.where((sub < half)[None, :], pos_row[:, None], pos_col[:, None])
    ang = pos * inv[sub % (half // 2)][None, :]
    first = (sub % half) < (half // 2)
    cos = np.cos(ang)
    sin = np.where(first[None, :], -np.sin(ang), np.sin(ang))
    cos_all = np.concatenate([np.ones((T_CTX, LANE))] + [cos] * DEC_BATCH, axis=0)
    sin_all = np.concatenate([np.zeros((T_CTX, LANE))] + [sin] * DEC_BATCH, axis=0)
    pm = np.kron(np.eye(2), np.full((DA_DK, DA_DK), 1.0 / DA_DK))
    return np.asarray(cos_all, np.float32), np.asarray(sin_all, np.float32), np.asarray(pm, np.float32)


def _qkprep_kernel(q_ref, k_ref, cos_ref, sin_ref, qg_ref, kg_ref, pm_ref, qo_ref, ko_ref, kn_ref):
    pm = pm_ref[...]
    cos = cos_ref[...]
    sin = sin_ref[...]
    lane = lax.broadcasted_iota(I32, cos.shape, 1)
    first = (lane % (DA_DK // 2)) < (DA_DK // 4)

    def norm(x, g):
        ms = jnp.dot(x * x, pm, precision=HIGHEST, preferred_element_type=F32)
        return x * lax.rsqrt(ms + EPS) * g

    def rope(x):
        partner = jnp.where(first, pltpu.roll(x, LANE - DA_DK // 4, 1), pltpu.roll(x, DA_DK // 4, 1))
        return x * cos + partner * sin

    q = norm(q_ref[...].astype(F32), qg_ref[...])
    k = norm(k_ref[...].astype(F32), kg_ref[...])
    kn_ref[...] = k
    qo_ref[...] = (rope(q) * (DA_DK ** -0.5)).astype(BF16)
    ko_ref[...] = rope(k).astype(BF16)


def qk_prepare(proj, q_gain, k_gain):
    cos, sin, pm = (jnp.asarray(t) for t in _rope_tables())
    tm = 512
    blk = lambda cb: pl.BlockSpec((tm, LANE), lambda i, h: (i, cb + h))
    tab = pl.BlockSpec((tm, LANE), lambda i, h: (i, 0))
    one = pl.BlockSpec((1, LANE), lambda i, h: (0, 0))
    out = pl.BlockSpec((tm, LANE), lambda i, h: (i, h))
    return pl.pallas_call(
        _qkprep_kernel,
        grid=(T_ALL // tm, DA_H),
        in_specs=[blk(CB_DAQ), blk(CB_DAK), tab, tab, one, one, pl.BlockSpec((LANE, LANE), lambda i, h: (0, 0))],
        out_specs=[out, out, out],
        out_shape=[jax.ShapeDtypeStruct((T_ALL, MIX_W), BF16), jax.ShapeDtypeStruct((T_ALL, MIX_W), BF16),
                   jax.ShapeDtypeStruct((T_ALL, MIX_W), F32)],
        compiler_params=_cp(("parallel", "parallel")),
        name="qk_prepare",
    )(proj, proj, cos, sin, q_gain, k_gain, pm)


def _attn_kernel(*refs, has_cache, lam_init, seq_len):
    if has_cache:
        q_ref, k_ref, v_ref, kc_ref, vc_ref, lp_ref, sg_ref, o_ref, ka_ref, va_ref = refs
    else:
        q_ref, k_ref, v_ref, lp_ref, sg_ref, o_ref, ka_ref, va_ref = refs

    @pl.when(pl.program_id(2) == 0)
    def _():
        ka_ref[0:seq_len, :] = k_ref[...]
        va_ref[0:seq_len, 0:LANE] = v_ref[...]
        if has_cache:
            ka_ref[seq_len:, :] = kc_ref[...].astype(BF16)
            va_ref[seq_len:, 0:LANE] = vc_ref[...].astype(BF16)
        va_ref[:, LANE:] = jnp.ones((va_ref.shape[0], LANE), BF16)

    q = q_ref[...]
    lane = lax.broadcasted_iota(I32, q.shape, 1)
    zero = jnp.zeros_like(q)
    lp = lp_ref[...]
    lam = (jnp.exp(jnp.sum(lp[0:1] * lp[1:2], axis=1, keepdims=True))
           - jnp.exp(jnp.sum(lp[2:3] * lp[3:4], axis=1, keepdims=True)) + lam_init)

    def softmax_half(qc):
        s = _dot_nt(qc, ka_ref[...])
        p = jnp.exp(s - jnp.max(s, axis=-1, keepdims=True)).astype(BF16)
        r = jnp.dot(p, va_ref[...], preferred_element_type=F32)
        return r[:, :LANE] / r[:, LANE:]

    o = softmax_half(jnp.where(lane < DA_DK, q, zero)) - lam * softmax_half(jnp.where(lane >= DA_DK, q, zero))
    o = o * lax.rsqrt(jnp.mean(o * o, axis=-1, keepdims=True) + EPS) * sg_ref[...]
    o_ref[...] = (o * (1.0 - lam_init)).astype(BF16)


def diff_attention(q_rot, k_rot, proj, lam_p, sub_gain, lam_init, row_block_off, nseq, seq_len, cache, layer):
    has_cache = cache is not None
    tq = 256
    kern = functools.partial(_attn_kernel, has_cache=has_cache, lam_init=lam_init, seq_len=seq_len)
    nq = seq_len // tq
    n_keys = seq_len + (cache[0].shape[2] if has_cache else 0)
    kv = lambda cb: pl.BlockSpec((seq_len, LANE), lambda b, h, i: (row_block_off + b, cb + h))
    in_specs = [pl.BlockSpec((tq, LANE), lambda b, h, i: ((row_block_off + b) * nq + i, h)), kv(0), kv(CB_DAV)]
    args = [q_rot, k_rot, proj]
    if has_cache:
        past = cache[0].shape[2]
        cspec = pl.BlockSpec((None, None, past, LANE), lambda b, h, i: (b, layer, 0, h))
        in_specs += [cspec, cspec]
        args += list(cache)
    in_specs += [pl.BlockSpec((4, DA_DK), lambda b, h, i: (0, 0)), pl.BlockSpec((1, LANE), lambda b, h, i: (0, 0))]
    args += [lam_p, sub_gain]
    return pl.pallas_call(
        kern,
        grid=(nseq, DA_H, nq),
        in_specs=in_specs,
        out_specs=pl.BlockSpec((tq, LANE), lambda b, h, i: (b * nq + i, h)),
        out_shape=jax.ShapeDtypeStruct((nseq * seq_len, MIX_W), BF16),
        scratch_shapes=[pltpu.VMEM((n_keys, LANE), BF16), pltpu.VMEM((n_keys, 2 * LANE), BF16)],
        compiler_params=_cp(("parallel", "parallel", "arbitrary")),
        name=f"diff_attention_{seq_len}",
    )(*args)


def _merge_kernel(h_ref, y0_ref, y1_ref, y2_ref, y3_ref, wg0_ref, wg1_ref, wg2_ref, wg3_ref, wb_ref, o_ref):
    h = h_ref[...]
    acc = None
    for bi, (y_ref, wg_ref) in enumerate(((y0_ref, wg0_ref), (y1_ref, wg1_ref), (y2_ref, wg2_ref), (y3_ref, wg3_ref))):
        gate = jax.nn.sigmoid(jnp.dot(h, wg_ref[...], preferred_element_type=F32))
        term = gate * jnp.dot(y_ref[...], wb_ref[bi], preferred_element_type=F32)
        acc = term if acc is None else acc + term
    o_ref[...] = acc.astype(BF16)


def branch_merge(h, ys, w_gate, w_branch):
    tm, tn = 1024, 256
    nj = D_MODEL // tn
    yspec = pl.BlockSpec((tm, MIX_W), lambda i, j: (i, 0))
    gspec = lambda bi: pl.BlockSpec((D_MODEL, tn), lambda i, j: (0, bi * nj + j))
    return pl.pallas_call(
        _merge_kernel,
        grid=(T_ALL // tm, nj),
        in_specs=[pl.BlockSpec((tm, D_MODEL), lambda i, j: (i, 0)), yspec, yspec, yspec, yspec,
                  gspec(0), gspec(1), gspec(2), gspec(3),
                  pl.BlockSpec((N_BRANCH, MIX_W, tn), lambda i, j: (0, 0, j))],
        out_specs=pl.BlockSpec((tm, tn), lambda i, j: (i, j)),
        out_shape=jax.ShapeDtypeStruct((T_ALL, D_MODEL), BF16),
        compiler_params=_cp(("parallel", "parallel")),
        name="branch_merge",
    )(h, *ys, w_gate, w_gate, w_gate, w_gate, w_branch)


def _outproj_kernel(m_ref, x_ref, g1_ref, w_ref, n2_ref, sh2_ref, sc2_ref, rt_ref, x1_ref, h2_ref, lg_ref):
    y = jnp.dot(m_ref[...], w_ref[...], preferred_element_type=F32)
    x1 = x_ref[...] + g1_ref[...] * y
    x1_ref[...] = x1
    h2 = _norm_mod(x1, n2_ref[...], sc2_ref[...], sh2_ref[...])
    h2_ref[...] = h2
    lg_ref[...] = jnp.dot(h2, rt_ref[...], precision=HIGHEST, preferred_element_type=F32)


def out_projection(merged, x, mod4, layer, w_out, norm2_g, router_t):
    tm = 256
    return pl.pallas_call(
        _outproj_kernel,
        grid=(T_ALL // tm,),
        in_specs=[pl.BlockSpec((tm, D_MODEL), lambda i: (i, 0)),
                  pl.BlockSpec((tm, D_MODEL), lambda i: (i, 0)),
                  _mod_spec(layer, 2, tm, 1),
                  pl.BlockSpec((D_MODEL, D_MODEL), lambda i: (0, 0)),
                  pl.BlockSpec((None, 1, D_MODEL), lambda i: (layer, 0, 0)),
                  _mod_spec(layer, 3, tm, 1),
                  _mod_spec(layer, 4, tm, 1),
                  pl.BlockSpec((D_MODEL, LANE), lambda i: (0, 0))],
        out_specs=[pl.BlockSpec((tm, D_MODEL), lambda i: (i, 0)),
                   pl.BlockSpec((tm, D_MODEL), lambda i: (i, 0)),
                   pl.BlockSpec((tm, LANE), lambda i: (i, 0))],
        out_shape=[jax.ShapeDtypeStruct((T_ALL, D_MODEL), F32),
                   jax.ShapeDtypeStruct((T_ALL, D_MODEL), F32),
                   jax.ShapeDtypeStruct((T_ALL, LANE), F32)],
        compiler_params=_cp(("parallel",)),
        name="out_projection",
    )(merged, x, mod4, w_out, norm2_g, mod4, mod4, router_t)


MOE_TM = 256
N_ASSIGN = T_ALL * TOP_K
MOE_BLOCKS = (N_ASSIGN + N_EXPERTS * (MOE_TM - 1)) // MOE_TM + 1
N_SLOTS = MOE_BLOCKS * MOE_TM
ROUTER_TT = 512
GROUP_SIZE = N_EXPERTS // N_GROUPS


def _router_kernel(lg_ref, bias_ref, tri_ref, eidx_ref, w_ref, rank_ref, cnt_ref, carry_ref):
    i = pl.program_id(0)

    @pl.when(i == 0)
    def _():
        carry_ref[...] = jnp.zeros_like(carry_ref)

    tt = lg_ref.shape[0]
    shape3 = (N_GROUPS, GROUP_SIZE, tt)
    neg = -jnp.inf
    logits_t = lg_ref[...].T[:N_EXPERTS]
    scores = jax.nn.sigmoid(logits_t).reshape(shape3)
    sel = scores + bias_ref[...].reshape(N_GROUPS, GROUP_SIZE, 1)
    gi = lax.broadcasted_iota(I32, shape3, 0)
    ji = lax.broadcasted_iota(I32, shape3, 1)
    ei = gi * GROUP_SIZE + ji
    m1 = jnp.max(sel, axis=1, keepdims=True)
    first = jnp.min(jnp.where(sel == m1, ji, GROUP_SIZE), axis=1, keepdims=True)
    m2 = jnp.max(jnp.where(ji == first, neg, sel), axis=1, keepdims=True)
    cur = m1 + m2
    g1 = lax.broadcasted_iota(I32, cur.shape, 0)
    gmask = jnp.zeros(cur.shape, jnp.bool_)
    for _ in range(TOPK_GROUPS):
        mx = jnp.max(cur, axis=0, keepdims=True)
        idx = jnp.min(jnp.where(cur == mx, g1, N_GROUPS), axis=0, keepdims=True)
        hit = g1 == idx
        gmask = jnp.logical_or(gmask, hit)
        cur = jnp.where(hit, neg, cur)
    masked = jnp.where(gmask, sel, neg)

    def all_max(a):
        return jnp.max(jnp.max(a, axis=1, keepdims=True), axis=0, keepdims=True)

    def all_min(a):
        return jnp.min(jnp.min(a, axis=1, keepdims=True), axis=0, keepdims=True)

    def all_sum(a):
        return jnp.sum(jnp.sum(a, axis=1, keepdims=True), axis=0, keepdims=True)

    hits, idxs, ws = [], [], []
    for _ in range(TOP_K):
        mx = all_max(masked)
        idx = all_min(jnp.where(masked == mx, ei, N_EXPERTS))
        hit = ei == idx
        hits.append(hit)
        idxs.append(idx)
        ws.append(all_sum(jnp.where(hit, scores, 0.0)))
        masked = jnp.where(hit, neg, masked)
    wsum = ws[0]
    for wk in ws[1:]:
        wsum = wsum + wk
    onehot = jnp.zeros(shape3, F32)
    for hit in hits:
        onehot = onehot + hit.astype(F32)
    oh2 = onehot.reshape(N_EXPERTS, tt)
    before = jnp.dot(oh2.astype(BF16), tri_ref[...], preferred_element_type=F32) + carry_ref[:, 0:1]
    before3 = before.reshape(shape3)
    for k in range(TOP_K):
        eidx_ref[k:k + 1, :] = idxs[k].reshape(1, tt)
        w_ref[k:k + 1, :] = (ws[k] / wsum * ROUTED_SCALE).reshape(1, tt)
        rank_ref[k:k + 1, :] = all_sum(jnp.where(hits[k], before3, 0.0)).reshape(1, tt).astype(I32)
    eidx_ref[TOP_K:, :] = jnp.zeros((8 - TOP_K, tt), I32)
    w_ref[TOP_K:, :] = jnp.zeros((8 - TOP_K, tt), F32)
    rank_ref[TOP_K:, :] = jnp.zeros((8 - TOP_K, tt), I32)
    carry_ref[...] = carry_ref[...] + jnp.sum(oh2, axis=1, keepdims=True)
    cnt_ref[...] = carry_ref[...]


def moe_route(logits_t, bias):
    tt = ROUTER_TT
    tri = jnp.asarray(np.triu(np.ones((tt, tt), np.float32), 1), BF16)
    row8 = pl.BlockSpec((8, tt), lambda i: (0, i))
    return pl.pallas_call(
        _router_kernel,
        grid=(T_ALL // tt,),
        in_specs=[pl.BlockSpec((tt, LANE), lambda i: (i, 0)),
                  pl.BlockSpec((N_EXPERTS, 1), lambda i: (0, 0)),
                  pl.BlockSpec((tt, tt), lambda i: (0, 0))],
        out_specs=[row8, row8, row8, pl.BlockSpec((N_EXPERTS, LANE), lambda i: (0, 0))],
        out_shape=[jax.ShapeDtypeStruct((8, T_ALL), I32), jax.ShapeDtypeStruct((8, T_ALL), F32),
                   jax.ShapeDtypeStruct((8, T_ALL), I32), jax.ShapeDtypeStruct((N_EXPERTS, LANE), F32)],
        scratch_shapes=[pltpu.VMEM((N_EXPERTS, LANE), F32)],
        compiler_params=_cp(("arbitrary",)),
        name="moe_route",
    )(logits_t, bias, tri)


def _dispatch_kernel(slots_ref, h2_ref, xs_hbm, sem, *, tm):
    i = pl.program_id(0)

    def body(r, _):
        for k in range(TOP_K):
            s = slots_ref[(i * tm + r) * TOP_K + k]
            pltpu.make_async_copy(h2_ref.at[pl.ds(r, 1), :], xs_hbm.at[pl.ds(s, 1), :], sem).start()
        return 0

    lax.fori_loop(0, tm, body, 0)
    for k in range(TOP_K):
        pltpu.make_async_copy(h2_ref, xs_hbm.at[pl.ds(0, tm), :], sem).wait()


def moe_dispatch(slots, h2):
    tm = 256
    return pl.pallas_call(
        functools.partial(_dispatch_kernel, tm=tm),
        grid_spec=pltpu.PrefetchScalarGridSpec(
            num_scalar_prefetch=1,
            grid=(T_ALL // tm,),
            in_specs=[pl.BlockSpec((tm, D_MODEL), lambda i, s: (i, 0))],
            out_specs=pl.BlockSpec(memory_space=pl.ANY),
            scratch_shapes=[pltpu.SemaphoreType.DMA(())]),
        out_shape=jax.ShapeDtypeStruct((N_SLOTS, D_MODEL), F32),
        compiler_params=_cp(("arbitrary",)),
        name="moe_dispatch",
    )(slots, h2)


def _gmm_kernel(be_ref, bv_ref, x_ref, wg_ref, wu_ref, wd_ref, y_ref, wgb, wub, wdb):
    i = pl.program_id(0)
    e = be_ref[i]
    prev = be_ref[jnp.maximum(i - 1, 0)]

    @pl.when(jnp.logical_or(i == 0, e != prev))
    def _():
        step = 512
        for r in range(0, D_MODEL, step):
            wgb[r:r + step, :] = wg_ref[r:r + step, :].astype(BF16)
            wub[r:r + step, :] = wu_ref[r:r + step, :].astype(BF16)
        for r in range(0, D_EXPERT, 128):
            wdb[r:r + 128, :] = wd_ref[r:r + 128, :].astype(BF16)

    nv = bv_ref[i]

    @pl.when(nv == 0)
    def _():
        y_ref[...] = jnp.zeros_like(y_ref)

    @pl.when(nv > 0)
    def _():
        row = lax.broadcasted_iota(I32, (x_ref.shape[0], 1), 0)
        x = jnp.where(row < nv, x_ref[...], 0.0).astype(BF16)
        g = jnp.dot(x, wgb[...], preferred_element_type=F32)
        u = jnp.dot(x, wub[...], preferred_element_type=F32)
        a = (g * jax.nn.sigmoid(g) * u).astype(BF16)
        y_ref[...] = jnp.dot(a, wdb[...], preferred_element_type=F32)


def moe_experts(block_e, block_valid, xs, w_gate, w_up, w_down, layer):
    tm = MOE_TM
    wspec = lambda a, b: pl.BlockSpec((None, None, a, b), lambda i, be, bv: (layer, be[i], 0, 0))
    return pl.pallas_call(
        _gmm_kernel,
        grid_spec=pltpu.PrefetchScalarGridSpec(
            num_scalar_prefetch=2,
            grid=(MOE_BLOCKS,),
            in_specs=[pl.BlockSpec((tm, D_MODEL), lambda i, be, bv: (i, 0)),
                      wspec(D_MODEL, D_EXPERT), wspec(D_MODEL, D_EXPERT), wspec(D_EXPERT, D_MODEL)],
            out_specs=pl.BlockSpec((tm, D_MODEL), lambda i, be, bv: (i, 0)),
            scratch_shapes=[pltpu.VMEM((D_MODEL, D_EXPERT), BF16), pltpu.VMEM((D_MODEL, D_EXPERT), BF16),
                            pltpu.VMEM((D_EXPERT, D_MODEL), BF16)]),
        out_shape=jax.ShapeDtypeStruct((N_SLOTS, D_MODEL), F32),
        compiler_params=_cp(("arbitrary",)),
        name="moe_experts",
    )(block_e, block_valid, xs, w_gate, w_up, w_down)


def _combine_kernel(slots_ref, x1_ref, h2_ref, w_ref, g2_ref, sg_ref, su_ref, sd_ref, ys_hbm, out_ref, buf, sem,
                    *, tm, nsteps):
    i = pl.program_id(0)

    def issue(step, slot):
        def body(r, _):
            for k in range(TOP_K):
                s = slots_ref[(step * tm + r) * TOP_K + k]
                pltpu.make_async_copy(ys_hbm.at[pl.ds(s, 1), :], buf.at[slot, pl.ds(k * tm + r, 1), :],
                                      sem.at[slot]).start()
            return 0

        lax.fori_loop(0, tm, body, 0)

    @pl.when(i == 0)
    def _():
        issue(0, 0)

    @pl.when(i + 1 < nsteps)
    def _():
        issue(i + 1, (i + 1) % 2)

    slot = i % 2
    pltpu.make_async_copy(ys_hbm.at[pl.ds(0, TOP_K * tm), :], buf.at[slot], sem.at[slot]).wait()
    hb = h2_ref[...].astype(BF16)
    g = jnp.dot(hb, sg_ref[...], preferred_element_type=F32)
    u = jnp.dot(hb, su_ref[...], preferred_element_type=F32)
    acc = jnp.dot((g * jax.nn.sigmoid(g) * u).astype(BF16), sd_ref[...], preferred_element_type=F32)
    w = w_ref[...]
    for k in range(TOP_K):
        acc = acc + w[:, k:k + 1] * buf[slot, k * tm:(k + 1) * tm, :]
    out_ref[...] = x1_ref[...] + g2_ref[...] * acc


def moe_combine(slots, x1, h2, w_tok, mod4, layer, sg, su, sd, ys):
    tm = 128
    nsteps = T_ALL // tm
    kern = functools.partial(_combine_kernel, tm=tm, nsteps=nsteps)
    row = lambda n: pl.BlockSpec((tm, n), lambda i, s: (i, 0))
    return pl.pallas_call(
        kern,
        grid_spec=pltpu.PrefetchScalarGridSpec(
            num_scalar_prefetch=1,
            grid=(nsteps,),
            in_specs=[row(D_MODEL), row(D_MODEL), row(8),
                      pl.BlockSpec((None, None, 1, D_MODEL),
                                   lambda i, s: (layer, _mod_row(i, tm) * 6 + 5, 0, 0)),
                      pl.BlockSpec((D_MODEL, D_EXPERT), lambda i, s: (0, 0)),
                      pl.BlockSpec((D_MODEL, D_EXPERT), lambda i, s: (0, 0)),
                      pl.BlockSpec((D_EXPERT, D_MODEL), lambda i, s: (0, 0)),
                      pl.BlockSpec(memory_space=pl.ANY)],
            out_specs=row(D_MODEL),
            scratch_shapes=[pltpu.VMEM((2, TOP_K * tm, D_MODEL), F32), pltpu.SemaphoreType.DMA((2,))]),
        out_shape=jax.ShapeDtypeStruct((T_ALL, D_MODEL), F32),
        compiler_params=_cp(("arbitrary",)),
        name="moe_combine",
    )(slots, x1, h2, w_tok, mod4, sg, su, sd, ys)


def moe_ffn(x1, h2, logits_t, mod4, layer, moe_bias, w_gate, w_up, w_down, sg, su, sd):
    eidx, w_t, rank, cnt = moe_route(logits_t, moe_bias)
    counts = cnt[:, 0].astype(I32)
    padded = (counts + MOE_TM - 1) // MOE_TM * MOE_TM
    pad_end = jnp.cumsum(padded)
    pad_start = pad_end - padded
    experts = jnp.arange(N_EXPERTS, dtype=I32)
    base = jnp.sum(jnp.where(eidx[:TOP_K, :, None] == experts, pad_start, 0), axis=-1)
    slots = (base + rank[:TOP_K]).T.reshape(-1)
    starts = jnp.arange(MOE_BLOCKS, dtype=I32) * MOE_TM
    block_e = jnp.minimum(jnp.sum((pad_end[None, :] <= starts[:, None]).astype(I32), axis=1), N_EXPERTS - 1)
    mine = block_e[:, None] == experts
    left = jnp.sum(jnp.where(mine, counts + pad_start, 0), axis=1) - starts
    block_valid = jnp.clip(left, 0, MOE_TM).astype(I32)
    xs = moe_dispatch(slots, h2)
    ys = moe_experts(block_e, block_valid, xs, w_gate, w_up, w_down, layer)
    return moe_combine(slots, x1, h2, w_t.T, mod4, layer, sg, su, sd, ys)


def _ml_gate_layouts(gp, gate_bias):
    nt = gp.shape[0]
    g16 = gp[:, :4 * ML_H].reshape(nt, 2, 2, ML_H)
    gr = jnp.transpose(g16, (3, 0, 1, 2)).reshape(ML_H, nt, 4)
    gc = jnp.transpose(gr, (0, 2, 1))
    b = jnp.transpose(gate_bias, (2, 0, 1)).reshape(ML_H, 4)
    return gr, gc, b[:, None, :], b[:, :, None]


def kernel(x_prompt, x_sample, c, cache_k, cache_v, state_mlstm_C, state_mlstm_n, state_mlstm_m, state_rglru,
           c_ctx, w_mod, b_mod, norm1, norm2, w_in, hy_short, hy_w1, hy_b1, hy_w2, hy_b2, hy_w3, hy_freq,
           hy_decay, hy_skip, ml_gate_bias, ml_out_norm, lru_conv_w, lru_conv_b, lru_wa, lru_ba, lru_wx, lru_bx,
           lru_lambda, da_q_norm, da_k_norm, da_lambda, da_sub_norm, w_branch, w_out, moe_router, moe_bias,
           moe_w_gate, moe_w_up, moe_w_down, sh_w_gate, sh_w_up, sh_w_down):
    x = jnp.concatenate([x_prompt.reshape(T_CTX, D_MODEL), x_sample.reshape(T_DEN, D_MODEL)], axis=0)
    cvecs = jnp.concatenate([c_ctx[None], c, jnp.zeros((N_MOD_ROWS - 1 - DEC_BATCH, D_MODEL), F32)], axis=0)
    mod4 = modulation_all(cvecs, w_mod, b_mod).reshape(DEPTH, N_MOD_ROWS * 6, 1, D_MODEL)
    norm1_3 = norm1.reshape(DEPTH, 1, D_MODEL)
    norm2_3 = norm2.reshape(DEPTH, 1, D_MODEL)
    cache_k4 = cache_k.reshape(DEC_BATCH, DEPTH, -1, MIX_W)
    cache_v4 = cache_v.reshape(DEC_BATCH, DEPTH, -1, MIX_W)
    den_off = T_CTX // DEC_SEQ
    new_k, new_v, new_c, new_n, new_m, new_h = [], [], [], [], [], []
    for l in range(DEPTH):
        lam_init = 0.8 - 0.6 * math.exp(-0.3 * l)
        wl = w_in[l]
        n_ml = 4 * ML_H
        w_main = jnp.concatenate([wl[:, :7 * MIX_W], wl[:, 7 * MIX_W + n_ml:12 * MIX_W + n_ml]], axis=1).astype(BF16)
        w_g16 = jnp.pad(wl[:, 7 * MIX_W:7 * MIX_W + n_ml], ((0, 0), (0, LANE - n_ml))).astype(BF16)
        w_gate = wl[:, 12 * MIX_W + n_ml:].astype(BF16)
        proj, h, gp = in_projection(x, norm1_3, mod4, l, w_main, w_g16)

        w1p = jnp.pad(hy_w1[l], ((0, LANE - HY_PE), (0, 0)))
        ws3 = jnp.transpose(hy_short[l].reshape(3, 3, MIX_W), (1, 0, 2))
        y_hy = []
        for off, nseq, sl in ((0, BATCH, SEQ), (den_off, DEC_BATCH, DEC_SEQ)):
            kfilt = hyena_filter(sl, w1p, hy_b1[l][None], hy_w2[l], hy_b2[l][None], hy_freq[l][None], hy_w3[l],
                                 hy_decay[l][None])
            if sl <= 512:
                y_hy.append(hyena_conv_direct(proj, off, nseq, sl, ws3, hyena_filter_fft_direct(sl, kfilt), hy_skip[l]))
            else:
                y_hy.append(hyena_conv(proj, off, nseq, sl, ws3, hyena_filter_fft(sl, kfilt), hy_skip[l]))

        gr, gc, b_r, b_c = _ml_gate_layouts(gp, ml_gate_bias[l])
        gain = ml_out_norm[l][None]
        y_ml0, c_new, n_new, m_new = mlstm_mixer(proj, gr, gc, b_r, b_c, gain, 0, BATCH, SEQ, None)
        st = (state_mlstm_C[:, l], state_mlstm_n[:, l][:, :, :, None, :],
              jnp.broadcast_to(state_mlstm_m[:, l][:, :, :, None, None], (DEC_BATCH, 2, ML_H, 1, LANE)))
        y_ml1 = mlstm_mixer(proj, gr, gc, b_r, b_c, gain, den_off, DEC_BATCH, DEC_SEQ, st)[0]

        lru_args = (lru_conv_w[l], lru_conv_b[l][None], lru_wa[l], lru_ba[l][:, None, :], lru_wx[l],
                    lru_bx[l][:, None, :], lru_lambda[l][:, None, :])
        y_lr0, h_new = rglru_mixer(proj, *lru_args, 0, BATCH, SEQ, None)
        y_lr1 = rglru_mixer(proj, *lru_args, den_off, DEC_BATCH, DEC_SEQ, state_rglru[:, l][:, :, None, :])[0]

        q_rot, k_rot, k_norm = qk_prepare(proj, jnp.tile(da_q_norm[l], 2)[None], jnp.tile(da_k_norm[l], 2)[None])
        sub_gain = da_sub_norm[l][None]
        y_da0 = diff_attention(q_rot, k_rot, proj, da_lambda[l], sub_gain, lam_init, 0, BATCH, SEQ, None, l)
        y_da1 = diff_attention(q_rot, k_rot, proj, da_lambda[l], sub_gain, lam_init, den_off, DEC_BATCH, DEC_SEQ,
                               (cache_k4, cache_v4), l)

        ys = [jnp.concatenate(p, axis=0) for p in (y_hy, (y_ml0, y_ml1), (y_lr0, y_lr1), (y_da0, y_da1))]
        merged = branch_merge(h, ys, w_gate, w_branch[l].astype(BF16))
        x1, h2, logits_t = out_projection(merged, x, mod4, l, w_out[l].astype(BF16), norm2_3,
                                          jnp.pad(moe_router[l], ((0, 0), (0, LANE - N_EXPERTS))))
        x = moe_ffn(x1, h2, logits_t, mod4, l, moe_bias[l][:, None], moe_w_gate, moe_w_up, moe_w_down,
                    sh_w_gate[l].astype(BF16), sh_w_up[l].astype(BF16), sh_w_down[l].astype(BF16))

        new_k.append(k_norm[:T_CTX].reshape(BATCH, SEQ, DA_H, 2, DA_DK))
        new_v.append(proj[:T_CTX, CB_DAV * LANE:(CB_DAV + DA_H) * LANE].astype(F32).reshape(BATCH, SEQ, DA_H, -1))
        new_c.append(c_new)
        new_n.append(n_new[:, :, :, 0, :])
        new_m.append(m_new[:, :, :, 0, 0])
        new_h.append(h_new[:, :, 0, :])
    y_prompt = x[:T_CTX].reshape(BATCH, SEQ, D_MODEL)
    y_sample = x[T_CTX:].reshape(DEC_BATCH, DEC_SEQ, D_MODEL)
    stack = lambda parts: jnp.stack(parts, axis=1)
    return (y_prompt, y_sample, stack(new_k), stack(new_v), stack(new_c), stack(new_n), stack(new_m), stack(new_h))
```

```python
import functools
import math

import numpy as np
import jax
import jax.numpy as jnp
from jax import lax
from jax.experimental import pallas as pl
from jax.experimental.pallas import tpu as pltpu

F32 = jnp.float32
BF16 = jnp.bfloat16
I32 = jnp.int32
HIGHEST = lax.Precision.HIGHEST

D_MODEL = 2048
BATCH = 16
SEQ = 256
DEPTH = 2
DEC_BATCH = 4
DEC_SEQ = 4096
GRID_W = 64
N_BRANCH = 4
MIX_W = 512
HY_BANDS = 16
HY_PE = 1 + 2 * HY_BANDS
HY_FFN = 64
ML_H = 4
ML_DK = 128
LRU_H = 4
LRU_C = 8.0
DA_H = 4
DA_DK = 64
ROPE_BASE = 10000.0
N_EXPERTS = 64
TOP_K = 6
N_GROUPS = 8
TOPK_GROUPS = 4
D_EXPERT = 512
ROUTED_SCALE = 2.5
EPS = 1e-6

LANE = 128
T_CTX = BATCH * SEQ
T_DEN = DEC_BATCH * DEC_SEQ
T_ALL = T_CTX + T_DEN
N_MOD_ROWS = 8
VMEM_LIMIT = 56 * 1024 * 1024

CB_HY = 0
CB_MLQ, CB_MLK, CB_MLV, CB_MLO = 12, 16, 20, 24
CB_LRX, CB_LRY = 28, 32
CB_DAQ, CB_DAK, CB_DAV = 36, 40, 44
N_MAIN = 6144


def _cp(sem, vmem=VMEM_LIMIT, unchecked_dma=False):
    return pltpu.CompilerParams(dimension_semantics=sem, vmem_limit_bytes=vmem, disable_bounds_checks=unchecked_dma)


def _mod_row(i, rows_per_block):
    nctx = T_CTX // rows_per_block
    per = DEC_SEQ // rows_per_block
    return jnp.where(i < nctx, 0, 1 + (i - nctx) // per)


def _mod_spec(layer, seg, rows_per_block, ngrid):
    if ngrid == 1:
        return pl.BlockSpec((None, None, 1, D_MODEL), lambda i: (layer, _mod_row(i, rows_per_block) * 6 + seg, 0, 0))
    return pl.BlockSpec((None, None, 1, D_MODEL), lambda i, j: (layer, _mod_row(i, rows_per_block) * 6 + seg, 0, 0))


def _mod_kernel(c_ref, w_ref, b_ref, o_ref):
    c = c_ref[...]
    a = (c * jax.nn.sigmoid(c)).astype(BF16)
    o_ref[...] = jnp.dot(a, w_ref[...].astype(BF16), preferred_element_type=F32) + b_ref[...]


def modulation_all(cvecs, w_mod, b_mod):
    tn = 1024
    n = 6 * D_MODEL
    return pl.pallas_call(
        _mod_kernel,
        grid=(DEPTH, n // tn),
        in_specs=[pl.BlockSpec((N_MOD_ROWS, D_MODEL), lambda l, j: (0, 0)),
                  pl.BlockSpec((None, D_MODEL, tn), lambda l, j: (l, 0, j)),
                  pl.BlockSpec((None, 1, tn), lambda l, j: (l, 0, j))],
        out_specs=pl.BlockSpec((None, N_MOD_ROWS, tn), lambda l, j: (l, 0, j)),
        out_shape=jax.ShapeDtypeStruct((DEPTH, N_MOD_ROWS, n), F32),
        compiler_params=_cp(("parallel", "parallel")),
        name="modulation",
    )(cvecs, w_mod, b_mod.reshape(DEPTH, 1, n))


def _norm_mod(x, g, sc, sh):
    ms = jnp.mean(x * x, axis=-1, keepdims=True)
    return (x * lax.rsqrt(ms + EPS) * g) * (1.0 + sc) + sh


def _inproj_kernel(x_ref, g_ref, sh_ref, sc_ref, w_ref, wg_ref, proj_ref, h_ref, gp_ref, *, tm, sub):
    j = pl.program_id(1)

    @pl.when(j == 0)
    def _():
        g = g_ref[...]
        sc = sc_ref[...]
        sh = sh_ref[...]

        def body(r, _):
            rows = pl.ds(pl.multiple_of(r * sub, sub), sub)
            hb = _norm_mod(x_ref[rows, :], g, sc, sh).astype(BF16)
            h_ref[rows, :] = hb
            gp_ref[rows, :] = jnp.dot(hb, wg_ref[...], preferred_element_type=F32)
            return 0

        lax.fori_loop(0, tm // sub, body, 0)

    proj_ref[...] = jnp.dot(h_ref[...], w_ref[...], preferred_element_type=F32).astype(BF16)


def in_projection(x, norm_g, mod4, layer, w_main, w_gate16):
    tm, tn = 1024, 512
    kern = functools.partial(_inproj_kernel, tm=tm, sub=256)
    return pl.pallas_call(
        kern,
        grid=(T_ALL // tm, N_MAIN // tn),
        in_specs=[pl.BlockSpec((tm, D_MODEL), lambda i, j: (i, 0)),
                  pl.BlockSpec((None, 1, D_MODEL), lambda i, j: (layer, 0, 0)),
                  _mod_spec(layer, 0, tm, 2),
                  _mod_spec(layer, 1, tm, 2),
                  pl.BlockSpec((D_MODEL, tn), lambda i, j: (0, j)),
                  pl.BlockSpec((D_MODEL, LANE), lambda i, j: (0, 0))],
        out_specs=[pl.BlockSpec((tm, tn), lambda i, j: (i, j)),
                   pl.BlockSpec((tm, D_MODEL), lambda i, j: (i, 0)),
                   pl.BlockSpec((tm, LANE), lambda i, j: (i, 0))],
        out_shape=[jax.ShapeDtypeStruct((T_ALL, N_MAIN), BF16),
                   jax.ShapeDtypeStruct((T_ALL, D_MODEL), BF16),
                   jax.ShapeDtypeStruct((T_ALL, LANE), F32)],
        compiler_params=_cp(("parallel", "arbitrary")),
        name="in_projection",
    )(x, norm_g, mod4, mod4, w_main, w_gate16)


def _fft_factors(seq_len):
    n2 = 64
    return 2 * seq_len // n2, n2


def _fft_pages(seq_len):
    n1, _ = _fft_factors(seq_len)
    used = n1 // 2 + 1
    return used, -(-used // 8) * 8


@functools.lru_cache(maxsize=None)
def _dft_tables(seq_len):
    n1, n2 = _fft_factors(seq_len)
    used, npg = _fft_pages(seq_len)
    n = n1 * n2
    th1 = 2.0 * np.pi * np.outer(np.arange(npg, dtype=np.float64), np.arange(n1, dtype=np.float64)) / n1
    w1 = np.concatenate([np.cos(th1), -np.sin(th1)], axis=0)
    w1h = w1[:, : n1 // 2]
    wt = np.where(np.arange(npg) < used, 2.0, 0.0)
    wt[0] = 1.0
    wt[n1 // 2] = 1.0
    th4 = th1[:, : n1 // 2].T
    w4 = np.concatenate([np.cos(th4) * wt, -np.sin(th4) * wt], axis=1) / n
    k1 = np.arange(npg, dtype=np.float64)[:, None, None]
    k2 = np.arange(n2, dtype=np.float64)[None, :, None]
    m2 = np.arange(n2, dtype=np.float64)[None, None, :]
    ph = -2.0 * np.pi * (m2 * k1 / n + m2 * k2 / n2)
    gr, gi = np.cos(ph), np.sin(ph)
    g = np.concatenate([np.concatenate([gr, -gi], axis=2), np.concatenate([gi, gr], axis=2)], axis=1)
    gt = np.transpose(g, (0, 2, 1))
    return tuple(np.asarray(t, np.float32) for t in (w1, w1h, g, gt, w4))


@functools.lru_cache(maxsize=None)
def _direct_dft_tables(seq_len):
    n = 2 * seq_len
    th = 2.0 * np.pi * np.outer(np.arange(n, dtype=np.float64), np.arange(n, dtype=np.float64)) / n
    wf = np.concatenate([np.cos(th), -np.sin(th)], axis=0)
    wi = np.concatenate([np.cos(th), -np.sin(th)], axis=1)[:seq_len] / n
    return np.asarray(wf, np.float32), np.asarray(wi, np.float32)


def _dft_tables_bf16(seq_len):
    return tuple(jnp.asarray(t, BF16) for t in _dft_tables(seq_len))


@functools.lru_cache(maxsize=None)
def _filter_positions(seq_len):
    pos = np.concatenate([np.arange(seq_len), [0], np.arange(seq_len - 1, 0, -1)]).astype(np.float64)
    tn = pos / seq_len
    bands = np.linspace(1e-4, HY_BANDS - 1, HY_BANDS)
    ang = (2.0 * math.pi / seq_len) * pos[:, None] * bands
    pe = np.zeros((2 * seq_len, LANE), np.float64)
    pe[:, 0] = tn
    pe[:, 1:1 + HY_BANDS] = np.cos(ang)
    pe[:, 1 + HY_BANDS:HY_PE] = np.sin(ang)
    return np.asarray(pe, np.float32), np.asarray(tn[:, None], np.float32)


def _hy_mlp_kernel(pe_ref, w1_ref, b1_ref, w2_ref, b2_ref, fr_ref, z_ref):
    fr = fr_ref[...]
    z = jnp.sin(fr * (jnp.dot(pe_ref[...], w1_ref[...], precision=HIGHEST, preferred_element_type=F32) + b1_ref[...]))
    z_ref[...] = jnp.sin(fr * (jnp.dot(z, w2_ref[...], precision=HIGHEST, preferred_element_type=F32) + b2_ref[...]))


def _hy_filter_kernel(z_ref, tn_ref, w3a_ref, w3b_ref, dca_ref, dcb_ref, k_ref, *, seq_len):
    def taps(rows, w3_ref, dc_ref):
        f = jnp.dot(z_ref[rows, :].astype(BF16), w3_ref[...].astype(BF16), preferred_element_type=F32)
        return f * jnp.exp(-tn_ref[rows, :] * jnp.abs(dc_ref[...]))

    ka = taps(pl.ds(0, seq_len), w3a_ref, dca_ref)
    kb = taps(pl.ds(seq_len, seq_len), w3b_ref, dcb_ref)
    row = lax.broadcasted_iota(I32, ka.shape, 0)
    ka = ka + jnp.where(row == 0, taps(pl.ds(0, 8), w3b_ref, dcb_ref)[0:1], 0.0)
    kb = jnp.where(row == 0, 0.0, kb)
    inv = 1.0 / (jnp.sum(jnp.abs(ka), axis=0, keepdims=True) + jnp.sum(jnp.abs(kb), axis=0, keepdims=True))
    k_ref[0:seq_len, :] = ka * inv
    k_ref[seq_len:, :] = kb * inv


def hyena_filter(seq_len, w1p, b1, w2, b2, freq, w3, decay):
    pe, tn = (jnp.asarray(t) for t in _filter_positions(seq_len))
    n = 2 * seq_len
    tr = min(n, 1024)
    full = lambda shape: pl.BlockSpec(shape, lambda j: (0,) * len(shape))
    z = pl.pallas_call(
        _hy_mlp_kernel,
        grid=(n // tr,),
        in_specs=[pl.BlockSpec((tr, LANE), lambda j: (j, 0)), full((LANE, HY_FFN)), full((1, HY_FFN)),
                  full((HY_FFN, HY_FFN)), full((1, HY_FFN)), full((1, HY_FFN))],
        out_specs=pl.BlockSpec((tr, HY_FFN), lambda j: (j, 0)),
        out_shape=jax.ShapeDtypeStruct((n, HY_FFN), F32),
        compiler_params=_cp(("parallel",)),
        name=f"hyena_filter_mlp_{seq_len}",
    )(pe, w1p, b1, w2, b2, freq)
    tc = 128
    nc = (2 * MIX_W) // tc
    kern = functools.partial(_hy_filter_kernel, seq_len=seq_len)
    return pl.pallas_call(
        kern,
        grid=(nc,),
        in_specs=[full((n, HY_FFN)), full((n, 1)),
                  pl.BlockSpec((HY_FFN, tc), lambda j: (0, j)),
                  pl.BlockSpec((HY_FFN, tc), lambda j: (0, nc + j)),
                  pl.BlockSpec((1, tc), lambda j: (0, j)),
                  pl.BlockSpec((1, tc), lambda j: (0, nc + j))],
        out_specs=pl.BlockSpec((n, tc), lambda j: (0, j)),
        out_shape=jax.ShapeDtypeStruct((n, 2 * MIX_W), F32),
        compiler_params=_cp(("parallel",)),
        name=f"hyena_filter_{seq_len}",
    )(z, tn, w3, w3, decay, decay)


def _split_bf16(x):
    hi = x.astype(BF16)
    lo = (x - hi.astype(F32)).astype(BF16)
    return hi, lo


def _fft_stage1(load_rows, w1, a_ref, npg, n2, split):
    def body(m, _):
        xs = load_rows(m)
        if split:
            hi, lo = _split_bf16(xs)
            r = jnp.dot(w1, hi, preferred_element_type=F32) + jnp.dot(w1, lo, preferred_element_type=F32)
        else:
            r = jnp.dot(w1, xs.astype(BF16), preferred_element_type=F32)
        a_ref[pl.ds(m, npg, stride=2 * n2), :] = r[:npg]
        a_ref[pl.ds(n2 + m, npg, stride=2 * n2), :] = r[npg:]
        return 0

    lax.fori_loop(0, n2, body, 0, unroll=2)


def _hy_fft_kernel(k_ref, w1_ref, g_ref, kf_ref, a_ref, *, n1, n2, npg):
    w1 = w1_ref[...]
    _fft_stage1(lambda m: k_ref[pl.ds(m, n1, stride=n2), :], w1, a_ref, npg, n2, True)

    def body(p, _):
        rows = pl.ds(pl.multiple_of(p * 2 * n2, 2 * n2), 2 * n2)
        hi, lo = _split_bf16(a_ref[rows, :])
        g = g_ref[p]
        kf_ref[rows, :] = jnp.dot(g, hi, preferred_element_type=F32) + jnp.dot(g, lo, preferred_element_type=F32)
        return 0

    lax.fori_loop(0, npg, body, 0, unroll=4)


def hyena_filter_fft(seq_len, kfilt):
    n1, n2 = _fft_factors(seq_len)
    _, npg = _fft_pages(seq_len)
    w1, _, g, _, _ = _dft_tables_bf16(seq_len)
    tc = 128
    kern = functools.partial(_hy_fft_kernel, n1=n1, n2=n2, npg=npg)
    return pl.pallas_call(
        kern,
        grid=((2 * MIX_W) // tc,),
        in_specs=[pl.BlockSpec((n1 * n2, tc), lambda j: (0, j)),
                  pl.BlockSpec((2 * npg, n1), lambda j: (0, 0)),
                  pl.BlockSpec((npg, 2 * n2, 2 * n2), lambda j: (0, 0, 0))],
        out_specs=pl.BlockSpec((npg * 2 * n2, tc), lambda j: (0, j)),
        out_shape=jax.ShapeDtypeStruct((npg * 2 * n2, 2 * MIX_W), F32),
        scratch_shapes=[pltpu.VMEM((npg * 2 * n2, tc), F32)],
        compiler_params=_cp(("parallel",)),
        name=f"hyena_filter_fft_{seq_len}",
    )(kfilt, w1, g)


def _hy_fft_direct_kernel(k_ref, wf_ref, kf_ref):
    hi, lo = _split_bf16(k_ref[...])
    wf = wf_ref[...]
    kf_ref[...] = jnp.dot(wf, hi, preferred_element_type=F32) + jnp.dot(wf, lo, preferred_element_type=F32)


def hyena_filter_fft_direct(seq_len, kfilt):
    n = 2 * seq_len
    wf = jnp.asarray(_direct_dft_tables(seq_len)[0], BF16)
    tc = 256
    return pl.pallas_call(
        _hy_fft_direct_kernel,
        grid=((2 * MIX_W) // tc,),
        in_specs=[pl.BlockSpec((n, tc), lambda j: (0, j)), pl.BlockSpec((2 * n, n), lambda j: (0, 0))],
        out_specs=pl.BlockSpec((2 * n, tc), lambda j: (0, j)),
        out_shape=jax.ShapeDtypeStruct((2 * n, 2 * MIX_W), F32),
        compiler_params=_cp(("parallel",)),
        name=f"hyena_filter_fft_{seq_len}",
    )(kfilt, wf)


def _shift_rows(x, s, row):
    n = x.shape[0]
    y = pltpu.roll(x, s % n, 0)
    if s > 0:
        return jnp.where(row >= s, y, 0.0)
    return jnp.where(row < n + s, y, 0.0)


def _short_conv3(x_ref, w_ref):
    x = x_ref[...].astype(F32)
    row = lax.broadcasted_iota(I32, x.shape, 0)
    w = w_ref[...]
    return w[0:1] * _shift_rows(x, 1, row) + w[1:2] * x + w[2:3] * _shift_rows(x, -1, row)


def _hy_conv_kernel(v_ref, x1_ref, x2_ref, ws_ref, kf_ref, skip_ref, w1_ref, g_ref, gt_ref, w4_ref, y_ref,
                    z_ref, c_ref, a_ref, *, n1, n2, used, npg):
    o = pl.program_id(2)
    nh = n1 // 2

    @pl.when(o == 0)
    def _():
        z_ref[...] = _short_conv3(v_ref, ws_ref.at[0])

    w1 = w1_ref[...]
    _fft_stage1(lambda m: z_ref[pl.ds(m, nh, stride=n2), :], w1, a_ref, npg, n2, False)

    def page(p, _):
        rows = pl.ds(pl.multiple_of(p * 2 * n2, 2 * n2), 2 * n2)
        x = jnp.dot(g_ref[p], a_ref[rows, :].astype(BF16), preferred_element_type=F32)
        kf = kf_ref[rows, :]
        xr, xi = x[:n2], x[n2:]
        kr, ki = kf[:n2], kf[n2:]
        yc = jnp.concatenate([xr * kr - xi * ki, xr * ki + xi * kr], axis=0).astype(BF16)
        a_ref[rows, :] = jnp.dot(gt_ref[p], yc, preferred_element_type=F32)
        return 0

    lax.fori_loop(0, used, page, 0, unroll=5)

    w4 = w4_ref[...]

    def back(m, _):
        dr = a_ref[pl.ds(m, npg, stride=2 * n2), :]
        di = a_ref[pl.ds(n2 + m, npg, stride=2 * n2), :]
        d = jnp.concatenate([dr, di], axis=0).astype(BF16)
        c_ref[pl.ds(m, nh, stride=n2), :] = jnp.dot(w4, d, preferred_element_type=F32)
        return 0

    lax.fori_loop(0, n2, back, 0, unroll=2)

    @pl.when(o == 0)
    def _():
        z = z_ref[...]
        z_ref[...] = _short_conv3(x1_ref, ws_ref.at[1]) * (c_ref[...] + z * skip_ref[0:1])

    @pl.when(o == 1)
    def _():
        z = z_ref[...]
        y_ref[...] = (_short_conv3(x2_ref, ws_ref.at[2]) * (c_ref[...] + z * skip_ref[1:2])).astype(BF16)


def hyena_conv(proj, row_block_off, nseq, seq_len, w_short3, kf, skip):
    n1, n2 = _fft_factors(seq_len)
    used, npg = _fft_pages(seq_len)
    _, w1h, g, gt, w4 = _dft_tables_bf16(seq_len)
    rows_a = npg * 2 * n2
    tc = 128
    ncb = MIX_W // tc
    kern = functools.partial(_hy_conv_kernel, n1=n1, n2=n2, used=used, npg=npg)
    const3 = lambda c, b, o: (0, 0, 0)
    seg = lambda s: pl.BlockSpec((seq_len, tc), lambda c, b, o: (row_block_off + b, CB_HY + s * ncb + c))
    return pl.pallas_call(
        kern,
        grid=(ncb, nseq, 2),
        in_specs=[seg(0), seg(1), seg(2),
                  pl.BlockSpec((3, 3, tc), lambda c, b, o: (0, 0, c)),
                  pl.BlockSpec((rows_a, tc), lambda c, b, o: (0, o * ncb + c)),
                  pl.BlockSpec((2, tc), lambda c, b, o: (0, c)),
                  pl.BlockSpec((2 * npg, n1 // 2), lambda c, b, o: (0, 0)),
                  pl.BlockSpec((npg, 2 * n2, 2 * n2), const3, pipeline_mode=pl.Buffered(1)),
                  pl.BlockSpec((npg, 2 * n2, 2 * n2), const3, pipeline_mode=pl.Buffered(1)),
                  pl.BlockSpec((n1 // 2, 2 * npg), lambda c, b, o: (0, 0))],
        out_specs=pl.BlockSpec((seq_len, tc), lambda c, b, o: (b, c)),
        out_shape=jax.ShapeDtypeStruct((nseq * seq_len, MIX_W), BF16),
        scratch_shapes=[pltpu.VMEM((seq_len, tc), F32), pltpu.VMEM((seq_len, tc), F32),
                        pltpu.VMEM((rows_a, tc), F32)],
        compiler_params=_cp(("parallel", "parallel", "arbitrary")),
        name=f"hyena_conv_{seq_len}",
    )(proj, proj, proj, w_short3, kf, skip, w1h, g, gt, w4)


def _hy_direct_kernel(u_ref, ws_ref, kf_ref, skip_ref, wf_ref, wi_ref, y_ref, *, n):
    w = MIX_W
    wf = wf_ref[...]
    wi = wi_ref[...]
    z = _short_conv3(u_ref.at[:, pl.ds(0, w)], ws_ref.at[0])
    for o in range(2):
        x = jnp.dot(wf, z.astype(BF16), preferred_element_type=F32)
        kf = kf_ref[:, o * w:(o + 1) * w]
        xr, xi = x[:n], x[n:]
        kr, ki = kf[:n], kf[n:]
        yc = jnp.concatenate([xr * kr - xi * ki, xr * ki + xi * kr], axis=0).astype(BF16)
        conv = jnp.dot(wi, yc, preferred_element_type=F32)
        gate = _short_conv3(u_ref.at[:, pl.ds((o + 1) * w, w)], ws_ref.at[o + 1])
        z = gate * (conv + z * skip_ref[o:o + 1])
    y_ref[...] = z.astype(BF16)


def hyena_conv_direct(proj, row_block_off, nseq, seq_len, w_short3, kf, skip):
    n = 2 * seq_len
    wf_np, wi_np = _direct_dft_tables(seq_len)
    wf = jnp.asarray(wf_np[:, :seq_len], BF16)
    wi = jnp.asarray(wi_np, BF16)
    kern = functools.partial(_hy_direct_kernel, n=n)
    return pl.pallas_call(
        kern,
        grid=(nseq,),
        in_specs=[pl.BlockSpec((seq_len, 3 * MIX_W), lambda b: (row_block_off + b, 0)),
                  pl.BlockSpec((3, 3, MIX_W), lambda b: (0, 0, 0)),
                  pl.BlockSpec((2 * n, 2 * MIX_W), lambda b: (0, 0)),
                  pl.BlockSpec((2, MIX_W), lambda b: (0, 0)),
                  pl.BlockSpec((2 * n, seq_len), lambda b: (0, 0)),
                  pl.BlockSpec((seq_len, 2 * n), lambda b: (0, 0))],
        out_specs=pl.BlockSpec((seq_len, MIX_W), lambda b: (b, 0)),
        out_shape=jax.ShapeDtypeStruct((nseq * seq_len, MIX_W), BF16),
        compiler_params=_cp(("parallel",)),
        name=f"hyena_conv_{seq_len}",
    )(proj, w_short3, kf, skip, wf, wi)


def _dot_nt(a, b):
    return lax.dot_general(a, b, (((1,), (1,)), ((), ())), preferred_element_type=F32)


ML_CHUNK = 128


def _log_sigmoid(x):
    return jnp.minimum(x, 0.0) - jnp.log1p(jnp.exp(-jnp.abs(x)))


def _mlstm_kernel(*refs, seq_len, has_state):
    if has_state:
        (q_ref, k_ref, v_ref, o_ref, gr_ref, gc_ref, br_ref, bc_ref, gain_ref, c0_ref, n0_ref, m0_ref,
         y_ref, cout_ref, nout_ref, mout_ref, hs_ref, cs_ref, ns_ref, ms_ref) = refs
    else:
        (q_ref, k_ref, v_ref, o_ref, gr_ref, gc_ref, br_ref, bc_ref, gain_ref,
         y_ref, cout_ref, nout_ref, mout_ref, hs_ref, cs_ref, ns_ref, ms_ref) = refs
    t = ML_CHUNK
    nc = seq_len // t
    scale = ML_DK ** -0.5
    if has_state:
        cs_ref[...] = c0_ref[...]
        ns_ref[...] = n0_ref[...]
        ms_ref[...] = m0_ref[...]
    else:
        cs_ref[...] = jnp.zeros_like(cs_ref)
        ns_ref[...] = jnp.zeros_like(ns_ref)
        ms_ref[...] = jnp.zeros_like(ms_ref)
    hs_ref[...] = jnp.zeros_like(hs_ref)
    row = lax.broadcasted_iota(I32, (t, t), 0)
    col = lax.broadcasted_iota(I32, (t, t), 1)
    br = br_ref[...]
    bc = bc_ref[...]

    def step(j, _):
        for d in (0, 1):
            jj = j if d == 0 else nc - 1 - j
            rows = pl.ds(pl.multiple_of(jj * t, t), t)
            qf = q_ref[rows, :].astype(F32) * scale
            qb = qf.astype(BF16)
            kb = k_ref[rows, :]
            vb = v_ref[rows, :]
            gr = gr_ref[rows, :] + br
            gc = gc_ref[:, rows] + bc
            i_c = gr[:, 2 * d:2 * d + 1]
            f_c = _log_sigmoid(gr[:, 2 * d + 1:2 * d + 2])
            i_r = gc[2 * d:2 * d + 1, :]
            f_r = _log_sigmoid(gc[2 * d + 1:2 * d + 2, :])
            if d == 0:
                mask, mask_t = col <= row, row <= col
            else:
                mask, mask_t = col >= row, row >= col
            b_c = jnp.sum(jnp.where(mask, f_r, 0.0), axis=1, keepdims=True)
            b_r = jnp.sum(jnp.where(mask_t, f_c, 0.0), axis=0, keepdims=True)
            logw = jnp.where(mask, b_c - b_r + i_r, -jnp.inf)
            m = ms_ref[d][:, 0:1]
            g = b_c + m
            mt = jnp.maximum(g, jnp.max(logw, axis=1, keepdims=True))
            s = _dot_nt(qb, kb) * jnp.exp(logw - mt)
            inter = jnp.exp(g - mt)
            cm = cs_ref[d]
            nv = ns_ref[d]
            num = (jnp.dot(s.astype(BF16), vb, preferred_element_type=F32)
                   + inter * jnp.dot(qb, cm.astype(BF16), preferred_element_type=F32))
            den = jnp.sum(s, axis=1, keepdims=True) + inter * jnp.sum(qf * nv, axis=1, keepdims=True)
            h = num / jnp.maximum(jnp.abs(den), jnp.exp(-mt))
            hs_ref[rows, :] = hs_ref[rows, :] + h
            bl = jnp.sum(f_r, axis=1, keepdims=True)
            wlog_r = bl - b_r + i_r
            wlog_c = bl - b_c + i_c
            m_new = jnp.maximum(bl + m, jnp.max(wlog_r, axis=1, keepdims=True))
            dec = jnp.exp(bl + m - m_new)
            kw = jnp.exp(wlog_c - m_new) * kb.astype(F32)
            cs_ref[d] = dec * cm + jnp.dot(kw.T.astype(BF16), vb, preferred_element_type=F32)
            ns_ref[d] = dec * nv + jnp.sum(kw, axis=0, keepdims=True)
            ms_ref[d] = jnp.broadcast_to(m_new, (1, LANE))
        return 0

    lax.fori_loop(0, nc, step, 0, unroll=2)
    hsum = hs_ref[...]
    hn = hsum * lax.rsqrt(jnp.mean(hsum * hsum, axis=-1, keepdims=True) + EPS) * gain_ref[...]
    y_ref[...] = (jax.nn.sigmoid(o_ref[...].astype(F32)) * hn).astype(BF16)
    cout_ref[...] = cs_ref[...]
    nout_ref[...] = ns_ref[...]
    mout_ref[...] = ms_ref[...]


def mlstm_mixer(proj, gates_r, gates_c, bias_r, bias_c, gain, row_block_off, nseq, seq_len, state):
    has_state = state is not None
    kern = functools.partial(_mlstm_kernel, seq_len=seq_len, has_state=has_state)
    col = lambda cb: pl.BlockSpec((seq_len, LANE), lambda b, h: (row_block_off + b, cb + h))
    c_spec = pl.BlockSpec((None, 2, None, ML_DK, ML_DK), lambda b, h: (b, 0, h, 0, 0))
    v_spec = pl.BlockSpec((None, 2, None, 1, LANE), lambda b, h: (b, 0, h, 0, 0))
    in_specs = [col(CB_MLQ), col(CB_MLK), col(CB_MLV), col(CB_MLO),
                pl.BlockSpec((None, seq_len, 4), lambda b, h: (h, row_block_off + b, 0)),
                pl.BlockSpec((None, 4, seq_len), lambda b, h: (h, 0, row_block_off + b)),
                pl.BlockSpec((None, 1, 4), lambda b, h: (h, 0, 0)),
                pl.BlockSpec((None, 4, 1), lambda b, h: (h, 0, 0)),
                pl.BlockSpec((1, LANE), lambda b, h: (0, h))]
    args = [proj, proj, proj, proj, gates_r, gates_c, bias_r, bias_c, gain]
    if has_state:
        in_specs += [c_spec, v_spec, v_spec]
        args += list(state)
    return pl.pallas_call(
        kern,
        grid=(nseq, ML_H),
        in_specs=in_specs,
        out_specs=[pl.BlockSpec((seq_len, LANE), lambda b, h: (b, h)), c_spec, v_spec, v_spec],
        out_shape=[jax.ShapeDtypeStruct((nseq * seq_len, MIX_W), BF16),
                   jax.ShapeDtypeStruct((nseq, 2, ML_H, ML_DK, ML_DK), F32),
                   jax.ShapeDtypeStruct((nseq, 2, ML_H, 1, LANE), F32),
                   jax.ShapeDtypeStruct((nseq, 2, ML_H, 1, LANE), F32)],
        scratch_shapes=[pltpu.VMEM((seq_len, LANE), F32), pltpu.VMEM((2, ML_DK, ML_DK), F32),
                        pltpu.VMEM((2, 1, LANE), F32), pltpu.VMEM((2, 1, LANE), F32)],
        compiler_params=_cp(("parallel", "parallel")),
        name=f"mlstm_{seq_len}",
    )(*args)


def _softplus(x):
    return jnp.maximum(x, 0.0) + jnp.log1p(jnp.exp(-jnp.abs(x)))


def _gelu_tanh(x):
    return 0.5 * x * (1.0 + jnp.tanh(math.sqrt(2.0 / math.pi) * (x + 0.044715 * (x * x * x))))


def _lru_kernel(*refs, seq_len, has_state):
    if has_state:
        (x_ref, y_ref, cw_ref, cb_ref, wa_ref, ba_ref, wx_ref, bx_ref, lam_ref, h0_ref,
         out_ref, hl_ref, a_s, b_s, hsum) = refs
    else:
        (x_ref, y_ref, cw_ref, cb_ref, wa_ref, ba_ref, wx_ref, bx_ref, lam_ref,
         out_ref, hl_ref, a_s, b_s, hsum) = refs
    xf = x_ref[...].astype(F32)
    row = lax.broadcasted_iota(I32, xf.shape, 0)
    w = cw_ref[...]
    x = (w[0:1] * _shift_rows(xf, 2, row) + w[1:2] * _shift_rows(xf, 1, row) + w[2:3] * xf
         + w[3:4] * _shift_rows(xf, -1, row) + cb_ref[...])
    xb = x.astype(BF16)
    for d in (0, 1):
        r = jax.nn.sigmoid(jnp.dot(xb, wa_ref[d].astype(BF16), preferred_element_type=F32) + ba_ref[d])
        i = jax.nn.sigmoid(jnp.dot(xb, wx_ref[d].astype(BF16), preferred_element_type=F32) + bx_ref[d])
        log_a = (-LRU_C) * r * _softplus(-lam_ref[d])
        a_s[d] = jnp.exp(log_a)
        b_s[d] = jnp.sqrt(jnp.maximum(1.0 - jnp.exp(2.0 * log_a), 0.0)) * (i * x)

    nb = seq_len // 8
    sub = lax.broadcasted_iota(I32, (8, LANE), 0)

    def block_scan(a, b, reverse):
        for s in (1, 2, 4):
            sh = 8 - s if reverse else s
            ok = (sub < 8 - s) if reverse else (sub >= s)
            b = jnp.where(ok, a * pltpu.roll(b, sh, 0) + b, b)
            a = jnp.where(ok, a * pltpu.roll(a, sh, 0), a)
        return a, b

    def fwd(blk, carry):
        rows = pl.ds(pl.multiple_of(blk * 8, 8), 8)
        a, b = block_scan(a_s[0, rows, :], b_s[0, rows, :], False)
        h = a * carry + b
        hsum[rows, :] = h
        return h[7:8, :]

    def bwd(i, carry):
        rows = pl.ds(pl.multiple_of((nb - 1 - i) * 8, 8), 8)
        a, b = block_scan(a_s[1, rows, :], b_s[1, rows, :], True)
        h = a * carry + b
        hsum[rows, :] = hsum[rows, :] + h
        return h[0:1, :]

    zero = jnp.zeros((1, LANE), F32)
    hf = lax.fori_loop(0, nb, fwd, h0_ref[0] if has_state else zero, unroll=4)
    hb = lax.fori_loop(0, nb, bwd, h0_ref[1] if has_state else zero, unroll=4)
    out_ref[...] = (hsum[...] * _gelu_tanh(y_ref[...].astype(F32))).astype(BF16)
    hl_ref[0] = hf
    hl_ref[1] = hb


def rglru_mixer(proj, conv_w, conv_b, wa, ba, wx, bx, lam, row_block_off, nseq, seq_len, h0):
    has_state = h0 is not None
    kern = functools.partial(_lru_kernel, seq_len=seq_len, has_state=has_state)
    col = lambda cb: pl.BlockSpec((seq_len, LANE), lambda b, h: (row_block_off + b, cb + h))
    w_spec = pl.BlockSpec((2, None, LANE, LANE), lambda b, h: (0, h, 0, 0))
    v_spec = pl.BlockSpec((2, 1, LANE), lambda b, h: (0, 0, h))
    s_spec = pl.BlockSpec((None, 2, 1, LANE), lambda b, h: (b, 0, 0, h))
    in_specs = [col(CB_LRX), col(CB_LRY),
                pl.BlockSpec((4, LANE), lambda b, h: (0, h)),
                pl.BlockSpec((1, LANE), lambda b, h: (0, h)),
                w_spec, v_spec, w_spec, v_spec, v_spec]
    args = [proj, proj, conv_w, conv_b, wa, ba, wx, bx, lam]
    if has_state:
        in_specs.append(s_spec)
        args.append(h0)
    return pl.pallas_call(
        kern,
        grid=(nseq, LRU_H),
        in_specs=in_specs,
        out_specs=[pl.BlockSpec((seq_len, LANE), lambda b, h: (b, h)), s_spec],
        out_shape=[jax.ShapeDtypeStruct((nseq * seq_len, MIX_W), BF16),
                   jax.ShapeDtypeStruct((nseq, 2, 1, MIX_W), F32)],
        scratch_shapes=[pltpu.VMEM((2, seq_len, LANE), F32), pltpu.VMEM((2, seq_len, LANE), F32),
                        pltpu.VMEM((seq_len, LANE), F32)],
        compiler_params=_cp(("parallel", "parallel")),
        name=f"rglru_{seq_len}",
    )(*args)


@functools.lru_cache(maxsize=None)
def _rope_tables():
    t = np.arange(DEC_SEQ)
    pos_row, pos_col = (t // GRID_W).astype(np.float64), (t % GRID_W).astype(np.float64)
    half = DA_DK // 2
    inv = ROPE_BASE ** (-np.arange(0, half, 2, dtype=np.float64) / half)
    lane = np.arange(LANE)
    sub = lane % DA_DK
    pos = np.where((sub < half)[None, :], pos_row[:, None], pos_col[:, None])
    ang = pos * inv[sub % (half // 2)][None, :]
    first = (sub % half) < (half // 2)
    cos = np.cos(ang)
    sin = np.where(first[None, :], -np.sin(ang), np.sin(ang))
    cos_all = np.concatenate([np.ones((T_CTX, LANE))] + [cos] * DEC_BATCH, axis=0)
    sin_all = np.concatenate([np.zeros((T_CTX, LANE))] + [sin] * DEC_BATCH, axis=0)
    pm = np.kron(np.eye(2), np.full((DA_DK, DA_DK), 1.0 / DA_DK))
    return np.asarray(cos_all, np.float32), np.asarray(sin_all, np.float32), np.asarray(pm, np.float32)


def _qkprep_kernel(q_ref, k_ref, cos_ref, sin_ref, qg_ref, kg_ref, pm_ref, qo_ref, ko_ref, kn_ref):
    pm = pm_ref[...]
    cos = cos_ref[...]
    sin = sin_ref[...]
    lane = lax.broadcasted_iota(I32, cos.shape, 1)
    first = (lane % (DA_DK // 2)) < (DA_DK // 4)

    def norm(x, g):
        ms = jnp.dot(x * x, pm, precision=HIGHEST, preferred_element_type=F32)
        return x * lax.rsqrt(ms + EPS) * g

    def rope(x):
        partner = jnp.where(first, pltpu.roll(x, LANE - DA_DK // 4, 1), pltpu.roll(x, DA_DK // 4, 1))
        return x * cos + partner * sin

    q = norm(q_ref[...].astype(F32), qg_ref[...])
    k = norm(k_ref[...].astype(F32), kg_ref[...])
    kn_ref[...] = k
    qo_ref[...] = (rope(q) * (DA_DK ** -0.5)).astype(BF16)
    ko_ref[...] = rope(k).astype(BF16)


def qk_prepare(proj, q_gain, k_gain):
    cos, sin, pm = (jnp.asarray(t) for t in _rope_tables())
    tm = 512
    blk = lambda cb: pl.BlockSpec((tm, LANE), lambda i, h: (i, cb + h))
    tab = pl.BlockSpec((tm, LANE), lambda i, h: (i, 0))
    one = pl.BlockSpec((1, LANE), lambda i, h: (0, 0))
    out = pl.BlockSpec((tm, LANE), lambda i, h: (i, h))
    return pl.pallas_call(
        _qkprep_kernel,
        grid=(T_ALL // tm, DA_H),
        in_specs=[blk(CB_DAQ), blk(CB_DAK), tab, tab, one, one, pl.BlockSpec((LANE, LANE), lambda i, h: (0, 0))],
        out_specs=[out, out, out],
        out_shape=[jax.ShapeDtypeStruct((T_ALL, MIX_W), BF16), jax.ShapeDtypeStruct((T_ALL, MIX_W), BF16),
                   jax.ShapeDtypeStruct((T_ALL, MIX_W), F32)],
        compiler_params=_cp(("parallel", "parallel")),
        name="qk_prepare",
    )(proj, proj, cos, sin, q_gain, k_gain, pm)


def _attn_kernel(*refs, has_cache, lam_init, seq_len):
    if has_cache:
        q_ref, k_ref, v_ref, kc_ref, vc_ref, lp_ref, sg_ref, o_ref, ka_ref, va_ref = refs
    else:
        q_ref, k_ref, v_ref, lp_ref, sg_ref, o_ref, ka_ref, va_ref = refs

    @pl.when(pl.program_id(2) == 0)
    def _():
        ka_ref[0:seq_len, :] = k_ref[...]
        va_ref[0:seq_len, 0:LANE] = v_ref[...]
        if has_cache:
            ka_ref[seq_len:, :] = kc_ref[...].astype(BF16)
            va_ref[seq_len:, 0:LANE] = vc_ref[...].astype(BF16)
        va_ref[:, LANE:] = jnp.ones((va_ref.shape[0], LANE), BF16)

    q = q_ref[...]
    lane = lax.broadcasted_iota(I32, q.shape, 1)
    zero = jnp.zeros_like(q)
    lp = lp_ref[...]
    lam = (jnp.exp(jnp.sum(lp[0:1] * lp[1:2], axis=1, keepdims=True))
           - jnp.exp(jnp.sum(lp[2:3] * lp[3:4], axis=1, keepdims=True)) + lam_init)

    def softmax_half(qc):
        s = _dot_nt(qc, ka_ref[...])
        p = jnp.exp(s - jnp.max(s, axis=-1, keepdims=True)).astype(BF16)
        r = jnp.dot(p, va_ref[...], preferred_element_type=F32)
        return r[:, :LANE] / r[:, LANE:]

    o = softmax_half(jnp.where(lane < DA_DK, q, zero)) - lam * softmax_half(jnp.where(lane >= DA_DK, q, zero))
    o = o * lax.rsqrt(jnp.mean(o * o, axis=-1, keepdims=True) + EPS) * sg_ref[...]
    o_ref[...] = (o * (1.0 - lam_init)).astype(BF16)


def diff_attention(q_rot, k_rot, proj, lam_p, sub_gain, lam_init, row_block_off, nseq, seq_len, cache, layer):
    has_cache = cache is not None
    tq = 256
    kern = functools.partial(_attn_kernel, has_cache=has_cache, lam_init=lam_init, seq_len=seq_len)
    nq = seq_len // tq
    n_keys = seq_len + (cache[0].shape[2] if has_cache else 0)
    kv = lambda cb: pl.BlockSpec((seq_len, LANE), lambda b, h, i: (row_block_off + b, cb + h))
    in_specs = [pl.BlockSpec((tq, LANE), lambda b, h, i: ((row_block_off + b) * nq + i, h)), kv(0), kv(CB_DAV)]
    args = [q_rot, k_rot, proj]
    if has_cache:
        past = cache[0].shape[2]
        cspec = pl.BlockSpec((None, None, past, LANE), lambda b, h, i: (b, layer, 0, h))
        in_specs += [cspec, cspec]
        args += list(cache)
    in_specs += [pl.BlockSpec((4, DA_DK), lambda b, h, i: (0, 0)), pl.BlockSpec((1, LANE), lambda b, h, i: (0, 0))]
    args += [lam_p, sub_gain]
    return pl.pallas_call(
        kern,
        grid=(nseq, DA_H, nq),
        in_specs=in_specs,
        out_specs=pl.BlockSpec((tq, LANE), lambda b, h, i: (b * nq + i, h)),
        out_shape=jax.ShapeDtypeStruct((nseq * seq_len, MIX_W), BF16),
        scratch_shapes=[pltpu.VMEM((n_keys, LANE), BF16), pltpu.VMEM((n_keys, 2 * LANE), BF16)],
        compiler_params=_cp(("parallel", "parallel", "arbitrary")),
        name=f"diff_attention_{seq_len}",
    )(*args)


def _merge_kernel(h_ref, y0_ref, y1_ref, y2_ref, y3_ref, wg0_ref, wg1_ref, wg2_ref, wg3_ref, wb_ref, o_ref):
    h = h_ref[...]
    acc = None
    for bi, (y_ref, wg_ref) in enumerate(((y0_ref, wg0_ref), (y1_ref, wg1_ref), (y2_ref, wg2_ref), (y3_ref, wg3_ref))):
        gate = jax.nn.sigmoid(jnp.dot(h, wg_ref[...], preferred_element_type=F32))
        term = gate * jnp.dot(y_ref[...], wb_ref[bi], preferred_element_type=F32)
        acc = term if acc is None else acc + term
    o_ref[...] = acc.astype(BF16)


def branch_merge(h, ys, w_gate, w_branch):
    tm, tn = 1024, 256
    nj = D_MODEL // tn
    yspec = pl.BlockSpec((tm, MIX_W), lambda i, j: (i, 0))
    gspec = lambda bi: pl.BlockSpec((D_MODEL, tn), lambda i, j: (0, bi * nj + j))
    return pl.pallas_call(
        _merge_kernel,
        grid=(T_ALL // tm, nj),
        in_specs=[pl.BlockSpec((tm, D_MODEL), lambda i, j: (i, 0)), yspec, yspec, yspec, yspec,
                  gspec(0), gspec(1), gspec(2), gspec(3),
                  pl.BlockSpec((N_BRANCH, MIX_W, tn), lambda i, j: (0, 0, j))],
        out_specs=pl.BlockSpec((tm, tn), lambda i, j: (i, j)),
        out_shape=jax.ShapeDtypeStruct((T_ALL, D_MODEL), BF16),
        compiler_params=_cp(("parallel", "parallel")),
        name="branch_merge",
    )(h, *ys, w_gate, w_gate, w_gate, w_gate, w_branch)


def _outproj_kernel(m_ref, x_ref, g1_ref, w_ref, n2_ref, sh2_ref, sc2_ref, rt_ref, rl_ref, x1_ref, h2_ref, lg_ref):
    half = m_ref.shape[0] // 2
    for r in (0, half):
        rows = slice(r, r + half)
        y = jnp.dot(m_ref[rows, :], w_ref[...], preferred_element_type=F32)
        x1 = x_ref[rows, :] + g1_ref[...] * y
        x1_ref[rows, :] = x1
        h2 = _norm_mod(x1, n2_ref[...], sc2_ref[...], sh2_ref[...])
        h2_ref[rows, :] = h2
        hi, lo = _split_bf16(h2)
        lg_ref[rows, :] = (jnp.dot(hi, rt_ref[...], preferred_element_type=F32)
                           + jnp.dot(lo, rt_ref[...], preferred_element_type=F32)
                           + jnp.dot(hi, rl_ref[...], preferred_element_type=F32))


def out_projection(merged, x, mod4, layer, w_out, norm2_g, router):
    tm = 256
    router_hi = router.astype(BF16)
    router_lo = (router - router_hi.astype(F32)).astype(BF16)
    return pl.pallas_call(
        _outproj_kernel,
        grid=(T_ALL // tm,),
        in_specs=[pl.BlockSpec((tm, D_MODEL), lambda i: (i, 0)),
                  pl.BlockSpec((tm, D_MODEL), lambda i: (i, 0)),
                  _mod_spec(layer, 2, tm, 1),
                  pl.BlockSpec((D_MODEL, D_MODEL), lambda i: (0, 0)),
                  pl.BlockSpec((None, 1, D_MODEL), lambda i: (layer, 0, 0)),
                  _mod_spec(layer, 3, tm, 1),
                  _mod_spec(layer, 4, tm, 1),
                  pl.BlockSpec((D_MODEL, LANE), lambda i: (0, 0)),
                  pl.BlockSpec((D_MODEL, LANE), lambda i: (0, 0))],
        out_specs=[pl.BlockSpec((tm, D_MODEL), lambda i: (i, 0)),
                   pl.BlockSpec((tm, D_MODEL), lambda i: (i, 0)),
                   pl.BlockSpec((tm, LANE), lambda i: (i, 0))],
        out_shape=[jax.ShapeDtypeStruct((T_ALL, D_MODEL), F32),
                   jax.ShapeDtypeStruct((T_ALL, D_MODEL), F32),
                   jax.ShapeDtypeStruct((T_ALL, LANE), F32)],
        compiler_params=_cp(("parallel",)),
        name="out_projection",
    )(merged, x, mod4, w_out, norm2_g, mod4, mod4, router_hi, router_lo)


MOE_TM = 256
N_ASSIGN = T_ALL * TOP_K
MOE_BLOCKS = (N_ASSIGN + N_EXPERTS * (MOE_TM - 1)) // MOE_TM + 1
ROUTER_TT = 512
GROUP_SIZE = N_EXPERTS // N_GROUPS


def _router_kernel(lg_ref, bias_ref, tri_ref, eidx_ref, w_ref, rank_ref, cnt_ref, carry_ref):
    i = pl.program_id(0)

    @pl.when(i == 0)
    def _():
        carry_ref[...] = jnp.zeros_like(carry_ref)

    tt = lg_ref.shape[0]
    shape3 = (N_GROUPS, GROUP_SIZE, tt)
    neg = -jnp.inf
    logits_t = lg_ref[...].T[:N_EXPERTS]
    scores = jax.nn.sigmoid(logits_t).reshape(shape3)
    sel = scores + bias_ref[...].reshape(N_GROUPS, GROUP_SIZE, 1)
    gi = lax.broadcasted_iota(I32, shape3, 0)
    ji = lax.broadcasted_iota(I32, shape3, 1)
    ei = gi * GROUP_SIZE + ji
    m1 = jnp.max(sel, axis=1, keepdims=True)
    first = jnp.min(jnp.where(sel == m1, ji, GROUP_SIZE), axis=1, keepdims=True)
    m2 = jnp.max(jnp.where(ji == first, neg, sel), axis=1, keepdims=True)
    cur = m1 + m2
    g1 = lax.broadcasted_iota(I32, cur.shape, 0)
    gmask = jnp.zeros(cur.shape, jnp.bool_)
    for _ in range(TOPK_GROUPS):
        mx = jnp.max(cur, axis=0, keepdims=True)
        idx = jnp.min(jnp.where(cur == mx, g1, N_GROUPS), axis=0, keepdims=True)
        hit = g1 == idx
        gmask = jnp.logical_or(gmask, hit)
        cur = jnp.where(hit, neg, cur)
    masked = jnp.where(gmask, sel, neg)

    def all_max(a):
        return jnp.max(jnp.max(a, axis=1, keepdims=True), axis=0, keepdims=True)

    def all_min(a):
        return jnp.min(jnp.min(a, axis=1, keepdims=True), axis=0, keepdims=True)

    def all_sum(a):
        return jnp.sum(jnp.sum(a, axis=1, keepdims=True), axis=0, keepdims=True)

    hits, idxs, ws = [], [], []
    for _ in range(TOP_K):
        mx = all_max(masked)
        idx = all_min(jnp.where(masked == mx, ei, N_EXPERTS))
        hit = ei == idx
        hits.append(hit)
        idxs.append(idx)
        ws.append(all_sum(jnp.where(hit, scores, 0.0)))
        masked = jnp.where(hit, neg, masked)
    wsum = ws[0]
    for wk in ws[1:]:
        wsum = wsum + wk
    onehot = jnp.zeros(shape3, F32)
    for hit in hits:
        onehot = onehot + hit.astype(F32)
    oh2 = onehot.reshape(N_EXPERTS, tt)
    before = jnp.dot(oh2.astype(BF16), tri_ref[...], preferred_element_type=F32) + carry_ref[:, 0:1]
    before3 = before.reshape(shape3)
    for k in range(TOP_K):
        eidx_ref[k:k + 1, :] = idxs[k].reshape(1, tt)
        w_ref[k:k + 1, :] = (ws[k] / wsum * ROUTED_SCALE).reshape(1, tt)
        rank_ref[k:k + 1, :] = all_sum(jnp.where(hits[k], before3, 0.0)).reshape(1, tt).astype(I32)
    eidx_ref[TOP_K:, :] = jnp.zeros((8 - TOP_K, tt), I32)
    w_ref[TOP_K:, :] = jnp.zeros((8 - TOP_K, tt), F32)
    rank_ref[TOP_K:, :] = jnp.zeros((8 - TOP_K, tt), I32)
    carry_ref[...] = carry_ref[...] + jnp.sum(oh2, axis=1, keepdims=True)
    cnt_ref[...] = carry_ref[...]


def moe_route(logits_t, bias):
    tt = ROUTER_TT
    tri = jnp.asarray(np.triu(np.ones((tt, tt), np.float32), 1), BF16)
    row8 = pl.BlockSpec((8, tt), lambda i: (0, i))
    return pl.pallas_call(
        _router_kernel,
        grid=(T_ALL // tt,),
        in_specs=[pl.BlockSpec((tt, LANE), lambda i: (i, 0)),
                  pl.BlockSpec((N_EXPERTS, 1), lambda i: (0, 0)),
                  pl.BlockSpec((tt, tt), lambda i: (0, 0))],
        out_specs=[row8, row8, row8, pl.BlockSpec((N_EXPERTS, LANE), lambda i: (0, 0))],
        out_shape=[jax.ShapeDtypeStruct((8, T_ALL), I32), jax.ShapeDtypeStruct((8, T_ALL), F32),
                   jax.ShapeDtypeStruct((8, T_ALL), I32), jax.ShapeDtypeStruct((N_EXPERTS, LANE), F32)],
        scratch_shapes=[pltpu.VMEM((N_EXPERTS, LANE), F32)],
        compiler_params=_cp(("arbitrary",)),
        name="moe_route",
    )(logits_t, bias, tri)


TOK_BITS = 15
TOK_MASK = (1 << TOK_BITS) - 1
K_STRIDE = 1 << TOK_BITS
Y_ROWS = TOP_K * K_STRIDE


def _gmm_kernel(be_ref, bp_ref, bv_ref, na_ref, code_ref, h2_hbm, wg_ref, wu_ref, wd_ref, out_hbm,
                xa, xb, ya, yb, wgb, wub, wdb, gsem, ssem):
    i = pl.program_id(0)
    tm = MOE_TM
    na = na_ref[0]

    def gather(blk, xdst, sem):
        p0 = bp_ref[blk]
        last = jnp.maximum(bv_ref[blk] - 1, 0)
        for r in range(tm):
            tok = code_ref[p0 + jnp.minimum(r, last)] & TOK_MASK
            pltpu.make_async_copy(h2_hbm.at[pl.ds(tok, 1), :], xdst.at[pl.ds(r, 1), :], sem).start()

    def scatter(blk, ysrc, spare, sem):
        p0 = bp_ref[blk]
        nv = bv_ref[blk]
        last = jnp.maximum(nv - 1, 0)
        for r in range(tm):
            c = code_ref[p0 + jnp.minimum(r, last)]
            dest = jnp.where(r < nv, c, spare + r)
            pltpu.make_async_copy(ysrc.at[pl.ds(r, 1), :], out_hbm.at[pl.ds(dest, 1), :], sem).start()

    def wait_gather(buf, sem):
        pltpu.make_async_copy(h2_hbm.at[pl.ds(0, tm), :], buf, sem).wait()

    def wait_scatter(buf, sem):
        pltpu.make_async_copy(buf, out_hbm.at[pl.ds(0, tm), :], sem).wait()

    @pl.when(i == 0)
    def _():
        ya[...] = jnp.zeros_like(ya)
        yb[...] = jnp.zeros_like(yb)
        gather(0, xa, gsem.at[0])
        for r in range(tm):
            pltpu.make_async_copy(ya.at[pl.ds(r, 1), :], out_hbm.at[pl.ds(Y_ROWS + r, 1), :], ssem.at[0]).start()
            pltpu.make_async_copy(yb.at[pl.ds(r, 1), :], out_hbm.at[pl.ds(Y_ROWS + tm + r, 1), :], ssem.at[1]).start()

    e = be_ref[i]
    prev = be_ref[jnp.maximum(i - 1, 0)]

    @pl.when(jnp.logical_and(i < na, jnp.logical_or(i == 0, e != prev)))
    def _():
        step = 512
        for r in range(0, D_MODEL, step):
            wgb[r:r + step, :] = wg_ref[r:r + step, :].astype(BF16)
            wub[r:r + step, :] = wu_ref[r:r + step, :].astype(BF16)
        for r in range(0, D_EXPERT, 128):
            wdb[r:r + 128, :] = wd_ref[r:r + 128, :].astype(BF16)

    def block(xc, xn, yc, yo, par):
        wait_gather(xc, gsem.at[par])
        wait_scatter(yc, ssem.at[par])
        gather(jnp.minimum(i + 1, na - 1), xn, gsem.at[1 - par])
        nv = bv_ref[i]
        row = lax.broadcasted_iota(I32, (tm, 1), 0)
        x = jnp.where(row < nv, xc[...], 0.0).astype(BF16)
        g = jnp.dot(x, wgb[...], preferred_element_type=F32)
        u = jnp.dot(x, wub[...], preferred_element_type=F32)
        a = (g * jax.nn.sigmoid(g) * u).astype(BF16)
        yc[...] = jnp.dot(a, wdb[...], preferred_element_type=F32)
        scatter(i, yc, Y_ROWS + par * tm, ssem.at[par])

        @pl.when(i == na - 1)
        def _():
            wait_gather(xn, gsem.at[1 - par])
            wait_scatter(yc, ssem.at[par])
            wait_scatter(yo, ssem.at[1 - par])

    @pl.when(jnp.logical_and(i < na, i % 2 == 0))
    def _():
        block(xa, xb, ya, yb, 0)

    @pl.when(jnp.logical_and(i < na, i % 2 == 1))
    def _():
        block(xb, xa, yb, ya, 1)


def moe_experts(block_e, block_p0, block_valid, n_active, code, h2, w_gate, w_up, w_down, layer):
    tm = MOE_TM
    wspec = lambda a, b: pl.BlockSpec((None, None, a, b), lambda i, be, bp, bv, na, cd: (layer, be[i], 0, 0))
    rowbuf = pltpu.VMEM((tm, D_MODEL), F32)
    return pl.pallas_call(
        _gmm_kernel,
        grid_spec=pltpu.PrefetchScalarGridSpec(
            num_scalar_prefetch=5,
            grid=(MOE_BLOCKS,),
            in_specs=[pl.BlockSpec(memory_space=pl.ANY),
                      wspec(D_MODEL, D_EXPERT), wspec(D_MODEL, D_EXPERT), wspec(D_EXPERT, D_MODEL)],
            out_specs=pl.BlockSpec(memory_space=pl.ANY),
            scratch_shapes=[rowbuf, rowbuf, rowbuf, rowbuf,
                            pltpu.VMEM((D_MODEL, D_EXPERT), BF16), pltpu.VMEM((D_MODEL, D_EXPERT), BF16),
                            pltpu.VMEM((D_EXPERT, D_MODEL), BF16),
                            pltpu.SemaphoreType.DMA((2,)), pltpu.SemaphoreType.DMA((2,))]),
        out_shape=jax.ShapeDtypeStruct((Y_ROWS + 2 * tm, D_MODEL), F32),
        compiler_params=_cp(("arbitrary",), unchecked_dma=True),
        name="moe_experts",
    )(block_e, block_p0, block_valid, n_active, code, h2, w_gate, w_up, w_down)


def _combine_kernel(x1_ref, h2_ref, w_ref, g2_ref, sg_ref, su_ref, sd_ref, y0, y1, y2, y3, y4, y5, out_ref):
    hb = h2_ref[...].astype(BF16)
    g = jnp.dot(hb, sg_ref[...], preferred_element_type=F32)
    u = jnp.dot(hb, su_ref[...], preferred_element_type=F32)
    acc = jnp.dot((g * jax.nn.sigmoid(g) * u).astype(BF16), sd_ref[...], preferred_element_type=F32)
    w = w_ref[...]
    for k, y_ref in enumerate((y0, y1, y2, y3, y4, y5)):
        acc = acc + w[:, k:k + 1] * y_ref[...]
    out_ref[...] = x1_ref[...] + g2_ref[...] * acc


def moe_combine(x1, h2, w_tok, mod4, layer, sg, su, sd, ys):
    tm = 128
    nsteps = T_ALL // tm
    row = lambda n: pl.BlockSpec((tm, n), lambda i: (i, 0))
    yk = lambda k: pl.BlockSpec((tm, D_MODEL), lambda i: (k * (K_STRIDE // tm) + i, 0))
    return pl.pallas_call(
        _combine_kernel,
        grid=(nsteps,),
        in_specs=[row(D_MODEL), row(D_MODEL), row(8), _mod_spec(layer, 5, tm, 1),
                  pl.BlockSpec((D_MODEL, D_EXPERT), lambda i: (0, 0)),
                  pl.BlockSpec((D_MODEL, D_EXPERT), lambda i: (0, 0)),
                  pl.BlockSpec((D_EXPERT, D_MODEL), lambda i: (0, 0))] + [yk(k) for k in range(TOP_K)],
        out_specs=row(D_MODEL),
        out_shape=jax.ShapeDtypeStruct((T_ALL, D_MODEL), F32),
        compiler_params=_cp(("parallel",)),
        name="moe_combine",
    )(x1, h2, w_tok, mod4, sg, su, sd, *([ys] * TOP_K))


def moe_ffn(x1, h2, logits_t, mod4, layer, moe_bias, w_gate, w_up, w_down, sg, su, sd):
    eidx, w_t, _, cnt = moe_route(logits_t, moe_bias)
    order = jnp.argsort(eidx[:TOP_K].T.reshape(-1), stable=True).astype(I32)
    code = ((order % TOP_K) << TOK_BITS) | (order // TOP_K)
    counts = cnt[:, 0].astype(I32)
    start = jnp.cumsum(counts) - counts
    nblk = (counts + MOE_TM - 1) // MOE_TM
    blk_end = jnp.cumsum(nblk)
    blk_start = blk_end - nblk
    experts = jnp.arange(N_EXPERTS, dtype=I32)
    blocks = jnp.arange(MOE_BLOCKS, dtype=I32)
    block_e = jnp.minimum(jnp.sum((blk_end[None, :] <= blocks[:, None]).astype(I32), axis=1), N_EXPERTS - 1)
    pick = lambda v: jnp.sum(jnp.where(block_e[:, None] == experts, v, 0), axis=1)
    within = (blocks - pick(blk_start)) * MOE_TM
    block_valid = jnp.clip(pick(counts) - within, 0, MOE_TM).astype(I32)
    block_p0 = jnp.minimum(pick(start) + within, N_ASSIGN - 1).astype(I32)
    ys = moe_experts(block_e, block_p0, block_valid, blk_end[-1:], code, h2, w_gate, w_up, w_down, layer)
    return moe_combine(x1, h2, w_t.T, mod4, layer, sg, su, sd, ys)


def _ml_gate_layouts(gp, gate_bias):
    nt = gp.shape[0]
    g16 = gp[:, :4 * ML_H].reshape(nt, 2, 2, ML_H)
    gr = jnp.transpose(g16, (3, 0, 1, 2)).reshape(ML_H, nt, 4)
    gc = jnp.transpose(gr, (0, 2, 1))
    b = jnp.transpose(gate_bias, (2, 0, 1)).reshape(ML_H, 4)
    return gr, gc, b[:, None, :], b[:, :, None]


def kernel(x_prompt, x_sample, c, cache_k, cache_v, state_mlstm_C, state_mlstm_n, state_mlstm_m, state_rglru,
           c_ctx, w_mod, b_mod, norm1, norm2, w_in, hy_short, hy_w1, hy_b1, hy_w2, hy_b2, hy_w3, hy_freq,
           hy_decay, hy_skip, ml_gate_bias, ml_out_norm, lru_conv_w, lru_conv_b, lru_wa, lru_ba, lru_wx, lru_bx,
           lru_lambda, da_q_norm, da_k_norm, da_lambda, da_sub_norm, w_branch, w_out, moe_router, moe_bias,
           moe_w_gate, moe_w_up, moe_w_down, sh_w_gate, sh_w_up, sh_w_down):
    x = jnp.concatenate([x_prompt.reshape(T_CTX, D_MODEL), x_sample.reshape(T_DEN, D_MODEL)], axis=0)
    cvecs = jnp.concatenate([c_ctx[None], c, jnp.zeros((N_MOD_ROWS - 1 - DEC_BATCH, D_MODEL), F32)], axis=0)
    mod4 = modulation_all(cvecs, w_mod, b_mod).reshape(DEPTH, N_MOD_ROWS * 6, 1, D_MODEL)
    norm1_3 = norm1.reshape(DEPTH, 1, D_MODEL)
    norm2_3 = norm2.reshape(DEPTH, 1, D_MODEL)
    cache_k4 = cache_k.reshape(DEC_BATCH, DEPTH, -1, MIX_W)
    cache_v4 = cache_v.reshape(DEC_BATCH, DEPTH, -1, MIX_W)
    den_off = T_CTX // DEC_SEQ
    new_k, new_v, new_c, new_n, new_m, new_h = [], [], [], [], [], []
    for l in range(DEPTH):
        lam_init = 0.8 - 0.6 * math.exp(-0.3 * l)
        wl = w_in[l]
        n_ml = 4 * ML_H
        w_main = jnp.concatenate([wl[:, :7 * MIX_W], wl[:, 7 * MIX_W + n_ml:12 * MIX_W + n_ml]], axis=1).astype(BF16)
        w_g16 = jnp.pad(wl[:, 7 * MIX_W:7 * MIX_W + n_ml], ((0, 0), (0, LANE - n_ml))).astype(BF16)
        w_gate = wl[:, 12 * MIX_W + n_ml:].astype(BF16)
        proj, h, gp = in_projection(x, norm1_3, mod4, l, w_main, w_g16)

        w1p = jnp.pad(hy_w1[l], ((0, LANE - HY_PE), (0, 0)))
        ws3 = jnp.transpose(hy_short[l].reshape(3, 3, MIX_W), (1, 0, 2))
        y_hy = []
        for off, nseq, sl in ((0, BATCH, SEQ), (den_off, DEC_BATCH, DEC_SEQ)):
            kfilt = hyena_filter(sl, w1p, hy_b1[l][None], hy_w2[l], hy_b2[l][None], hy_freq[l][None], hy_w3[l],
                                 hy_decay[l][None])
            if sl <= 512:
                y_hy.append(hyena_conv_direct(proj, off, nseq, sl, ws3, hyena_filter_fft_direct(sl, kfilt), hy_skip[l]))
            else:
                y_hy.append(hyena_conv(proj, off, nseq, sl, ws3, hyena_filter_fft(sl, kfilt), hy_skip[l]))

        gr, gc, b_r, b_c = _ml_gate_layouts(gp, ml_gate_bias[l])
        gain = ml_out_norm[l][None]
        y_ml0, c_new, n_new, m_new = mlstm_mixer(proj, gr, gc, b_r, b_c, gain, 0, BATCH, SEQ, None)
        st = (state_mlstm_C[:, l], state_mlstm_n[:, l][:, :, :, None, :],
              jnp.broadcast_to(state_mlstm_m[:, l][:, :, :, None, None], (DEC_BATCH, 2, ML_H, 1, LANE)))
        y_ml1 = mlstm_mixer(proj, gr, gc, b_r, b_c, gain, den_off, DEC_BATCH, DEC_SEQ, st)[0]

        lru_args = (lru_conv_w[l], lru_conv_b[l][None], lru_wa[l], lru_ba[l][:, None, :], lru_wx[l],
                    lru_bx[l][:, None, :], lru_lambda[l][:, None, :])
        y_lr0, h_new = rglru_mixer(proj, *lru_args, 0, BATCH, SEQ, None)
        y_lr1 = rglru_mixer(proj, *lru_args, den_off, DEC_BATCH, DEC_SEQ, state_rglru[:, l][:, :, None, :])[0]

        q_rot, k_rot, k_norm = qk_prepare(proj, jnp.tile(da_q_norm[l], 2)[None], jnp.tile(da_k_norm[l], 2)[None])
        sub_gain = da_sub_norm[l][None]
        y_da0 = diff_attention(q_rot, k_rot, proj, da_lambda[l], sub_gain, lam_init, 0, BATCH, SEQ, None, l)
        y_da1 = diff_attention(q_rot, k_rot, proj, da_lambda[l], sub_gain, lam_init, den_off, DEC_BATCH, DEC_SEQ,
                               (cache_k4, cache_v4), l)

        ys = [jnp.concatenate(p, axis=0) for p in (y_hy, (y_ml0, y_ml1), (y_lr0, y_lr1), (y_da0, y_da1))]
        merged = branch_merge(h, ys, w_gate, w_branch[l].astype(BF16))
        x1, h2, logits_t = out_projection(merged, x, mod4, l, w_out[l].astype(BF16), norm2_3,
                                          jnp.pad(moe_router[l], ((0, 0), (0, LANE - N_EXPERTS))))
        x = moe_ffn(x1, h2, logits_t, mod4, l, moe_bias[l][:, None], moe_w_gate, moe_w_up, moe_w_down,
                    sh_w_gate[l].astype(BF16), sh_w_up[l].astype(BF16), sh_w_down[l].astype(BF16))

        new_k.append(k_norm[:T_CTX].reshape(BATCH, SEQ, DA_H, 2, DA_DK))
        new_v.append(proj[:T_CTX, CB_DAV * LANE:(CB_DAV + DA_H) * LANE].astype(F32).reshape(BATCH, SEQ, DA_H, -1))
        new_c.append(c_new)
        new_n.append(n_new[:, :, :, 0, :])
        new_m.append(m_new[:, :, :, 0, 0])
        new_h.append(h_new[:, :, 0, :])
    y_prompt = x[:T_CTX].reshape(BATCH, SEQ, D_MODEL)
    y_sample = x[T_CTX:].reshape(DEC_BATCH, DEC_SEQ, D_MODEL)
    stack = lambda parts: jnp.stack(parts, axis=1)
    return (y_prompt, y_sample, stack(new_k), stack(new_v), stack(new_c), stack(new_n), stack(new_m), stack(new_h))
```

```python
import functools
import math

import numpy as np
import jax
import jax.numpy as jnp
from jax import lax
from jax.experimental import pallas as pl
from jax.experimental.pallas import tpu as pltpu

F32 = jnp.float32
BF16 = jnp.bfloat16
I32 = jnp.int32
HIGHEST = lax.Precision.HIGHEST

D_MODEL = 2048
BATCH = 16
SEQ = 256
DEPTH = 2
DEC_BATCH = 4
DEC_SEQ = 4096
GRID_W = 64
N_BRANCH = 4
MIX_W = 512
HY_BANDS = 16
HY_PE = 1 + 2 * HY_BANDS
HY_FFN = 64
ML_H = 4
ML_DK = 128
LRU_H = 4
LRU_C = 8.0
DA_H = 4
DA_DK = 64
ROPE_BASE = 10000.0
N_EXPERTS = 64
TOP_K = 6
N_GROUPS = 8
TOPK_GROUPS = 4
D_EXPERT = 512
ROUTED_SCALE = 2.5
EPS = 1e-6

LANE = 128
T_CTX = BATCH * SEQ
T_DEN = DEC_BATCH * DEC_SEQ
T_ALL = T_CTX + T_DEN
N_MOD_ROWS = 8
VMEM_LIMIT = 56 * 1024 * 1024

CB_HY = 0
CB_MLQ, CB_MLK, CB_MLV, CB_MLO = 12, 16, 20, 24
CB_LRX, CB_LRY = 28, 32
CB_DAQ, CB_DAK, CB_DAV = 36, 40, 44
N_MAIN = 6144


def _cp(sem, vmem=VMEM_LIMIT, unchecked_dma=False):
    return pltpu.CompilerParams(dimension_semantics=sem, vmem_limit_bytes=vmem, disable_bounds_checks=unchecked_dma)


def _mod_row(i, rows_per_block):
    nctx = T_CTX // rows_per_block
    per = DEC_SEQ // rows_per_block
    return jnp.where(i < nctx, 0, 1 + (i - nctx) // per)


def _mod_spec(layer, seg, rows_per_block, ngrid):
    if ngrid == 1:
        return pl.BlockSpec((None, None, 1, D_MODEL), lambda i: (layer, _mod_row(i, rows_per_block) * 6 + seg, 0, 0))
    return pl.BlockSpec((None, None, 1, D_MODEL), lambda i, j: (layer, _mod_row(i, rows_per_block) * 6 + seg, 0, 0))


def _mod_kernel(c_ref, w_ref, b_ref, o_ref):
    c = c_ref[...]
    a = (c * jax.nn.sigmoid(c)).astype(BF16)
    o_ref[...] = jnp.dot(a, w_ref[...].astype(BF16), preferred_element_type=F32) + b_ref[...]


def modulation_all(cvecs, w_mod, b_mod):
    tn = 1024
    n = 6 * D_MODEL
    return pl.pallas_call(
        _mod_kernel,
        grid=(DEPTH, n // tn),
        in_specs=[pl.BlockSpec((N_MOD_ROWS, D_MODEL), lambda l, j: (0, 0)),
                  pl.BlockSpec((None, D_MODEL, tn), lambda l, j: (l, 0, j)),
                  pl.BlockSpec((None, 1, tn), lambda l, j: (l, 0, j))],
        out_specs=pl.BlockSpec((None, N_MOD_ROWS, tn), lambda l, j: (l, 0, j)),
        out_shape=jax.ShapeDtypeStruct((DEPTH, N_MOD_ROWS, n), F32),
        compiler_params=_cp(("parallel", "parallel")),
        name="modulation",
    )(cvecs, w_mod, b_mod.reshape(DEPTH, 1, n))


def _norm_mod(x, g, sc, sh):
    ms = jnp.mean(x * x, axis=-1, keepdims=True)
    return (x * lax.rsqrt(ms + EPS) * g) * (1.0 + sc) + sh


def _inproj_kernel(x_ref, g_ref, sh_ref, sc_ref, w_ref, wg_ref, proj_ref, h_ref, gp_ref, *, tm, sub):
    j = pl.program_id(1)

    @pl.when(j == 0)
    def _():
        g = g_ref[...]
        sc = sc_ref[...]
        sh = sh_ref[...]

        def body(r, _):
            rows = pl.ds(pl.multiple_of(r * sub, sub), sub)
            hb = _norm_mod(x_ref[rows, :], g, sc, sh).astype(BF16)
            h_ref[rows, :] = hb
            gp_ref[rows, :] = jnp.dot(hb, wg_ref[...], preferred_element_type=F32)
            return 0

        lax.fori_loop(0, tm // sub, body, 0)

    proj_ref[...] = jnp.dot(h_ref[...], w_ref[...], preferred_element_type=F32).astype(BF16)


def in_projection(x, norm_g, mod4, layer, w_main, w_gate16):
    tm, tn = 1024, 1024
    kern = functools.partial(_inproj_kernel, tm=tm, sub=256)
    return pl.pallas_call(
        kern,
        grid=(T_ALL // tm, N_MAIN // tn),
        in_specs=[pl.BlockSpec((tm, D_MODEL), lambda i, j: (i, 0)),
                  pl.BlockSpec((None, 1, D_MODEL), lambda i, j: (layer, 0, 0)),
                  _mod_spec(layer, 0, tm, 2),
                  _mod_spec(layer, 1, tm, 2),
                  pl.BlockSpec((D_MODEL, tn), lambda i, j: (0, j)),
                  pl.BlockSpec((D_MODEL, LANE), lambda i, j: (0, 0))],
        out_specs=[pl.BlockSpec((tm, tn), lambda i, j: (i, j)),
                   pl.BlockSpec((tm, D_MODEL), lambda i, j: (i, 0)),
                   pl.BlockSpec((tm, LANE), lambda i, j: (i, 0))],
        out_shape=[jax.ShapeDtypeStruct((T_ALL, N_MAIN), BF16),
                   jax.ShapeDtypeStruct((T_ALL, D_MODEL), BF16),
                   jax.ShapeDtypeStruct((T_ALL, LANE), F32)],
        compiler_params=_cp(("parallel", "arbitrary")),
        name="in_projection",
    )(x, norm_g, mod4, mod4, w_main, w_gate16)


def _fft_factors(seq_len):
    n2 = 64
    return 2 * seq_len // n2, n2


def _fft_pages(seq_len):
    n1, _ = _fft_factors(seq_len)
    used = n1 // 2 + 1
    return used, -(-used // 8) * 8


@functools.lru_cache(maxsize=None)
def _dft_tables(seq_len):
    n1, n2 = _fft_factors(seq_len)
    used, npg = _fft_pages(seq_len)
    n = n1 * n2
    th1 = 2.0 * np.pi * np.outer(np.arange(npg, dtype=np.float64), np.arange(n1, dtype=np.float64)) / n1
    w1 = np.concatenate([np.cos(th1), -np.sin(th1)], axis=0)
    w1h = w1[:, : n1 // 2]
    wt = np.where(np.arange(npg) < used, 2.0, 0.0)
    wt[0] = 1.0
    wt[n1 // 2] = 1.0
    th4 = th1[:, : n1 // 2].T
    w4 = np.concatenate([np.cos(th4) * wt, -np.sin(th4) * wt], axis=1) / n
    k1 = np.arange(npg, dtype=np.float64)[:, None, None]
    k2 = np.arange(n2, dtype=np.float64)[None, :, None]
    m2 = np.arange(n2, dtype=np.float64)[None, None, :]
    ph = -2.0 * np.pi * (m2 * k1 / n + m2 * k2 / n2)
    gr, gi = np.cos(ph), np.sin(ph)
    g = np.concatenate([np.concatenate([gr, -gi], axis=2), np.concatenate([gi, gr], axis=2)], axis=1)
    gt = np.transpose(g, (0, 2, 1))
    return tuple(np.asarray(t, np.float32) for t in (w1, w1h, g, gt, w4))


@functools.lru_cache(maxsize=None)
def _direct_dft_tables(seq_len):
    n = 2 * seq_len
    th = 2.0 * np.pi * np.outer(np.arange(n, dtype=np.float64), np.arange(n, dtype=np.float64)) / n
    wf = np.concatenate([np.cos(th), -np.sin(th)], axis=0)
    wi = np.concatenate([np.cos(th), -np.sin(th)], axis=1)[:seq_len] / n
    return np.asarray(wf, np.float32), np.asarray(wi, np.float32)


def _dft_tables_bf16(seq_len):
    return tuple(jnp.asarray(t, BF16) for t in _dft_tables(seq_len))


@functools.lru_cache(maxsize=None)
def _filter_positions(seq_len):
    pos = np.concatenate([np.arange(seq_len), [0], np.arange(seq_len - 1, 0, -1)]).astype(np.float64)
    tn = pos / seq_len
    bands = np.linspace(1e-4, HY_BANDS - 1, HY_BANDS)
    ang = (2.0 * math.pi / seq_len) * pos[:, None] * bands
    pe = np.zeros((2 * seq_len, LANE), np.float64)
    pe[:, 0] = tn
    pe[:, 1:1 + HY_BANDS] = np.cos(ang)
    pe[:, 1 + HY_BANDS:HY_PE] = np.sin(ang)
    return np.asarray(pe, np.float32), np.asarray(tn[:, None], np.float32)


def _hy_mlp_kernel(pe_ref, w1_ref, b1_ref, w2_ref, b2_ref, fr_ref, z_ref):
    fr = fr_ref[...]
    z = jnp.sin(fr * (jnp.dot(pe_ref[...], w1_ref[...], precision=HIGHEST, preferred_element_type=F32) + b1_ref[...]))
    z_ref[...] = jnp.sin(fr * (jnp.dot(z, w2_ref[...], precision=HIGHEST, preferred_element_type=F32) + b2_ref[...]))


def _hy_filter_kernel(z_ref, tn_ref, w3a_ref, w3b_ref, dca_ref, dcb_ref, k_ref, *, seq_len):
    def taps(rows, w3_ref, dc_ref):
        f = jnp.dot(z_ref[rows, :].astype(BF16), w3_ref[...].astype(BF16), preferred_element_type=F32)
        return f * jnp.exp(-tn_ref[rows, :] * jnp.abs(dc_ref[...]))

    ka = taps(pl.ds(0, seq_len), w3a_ref, dca_ref)
    kb = taps(pl.ds(seq_len, seq_len), w3b_ref, dcb_ref)
    row = lax.broadcasted_iota(I32, ka.shape, 0)
    ka = ka + jnp.where(row == 0, taps(pl.ds(0, 8), w3b_ref, dcb_ref)[0:1], 0.0)
    kb = jnp.where(row == 0, 0.0, kb)
    inv = 1.0 / (jnp.sum(jnp.abs(ka), axis=0, keepdims=True) + jnp.sum(jnp.abs(kb), axis=0, keepdims=True))
    k_ref[0:seq_len, :] = ka * inv
    k_ref[seq_len:, :] = kb * inv


def hyena_filter(seq_len, w1p, b1, w2, b2, freq, w3, decay):
    pe, tn = (jnp.asarray(t) for t in _filter_positions(seq_len))
    n = 2 * seq_len
    tr = min(n, 1024)
    full = lambda shape: pl.BlockSpec(shape, lambda j: (0,) * len(shape))
    z = pl.pallas_call(
        _hy_mlp_kernel,
        grid=(n // tr,),
        in_specs=[pl.BlockSpec((tr, LANE), lambda j: (j, 0)), full((LANE, HY_FFN)), full((1, HY_FFN)),
                  full((HY_FFN, HY_FFN)), full((1, HY_FFN)), full((1, HY_FFN))],
        out_specs=pl.BlockSpec((tr, HY_FFN), lambda j: (j, 0)),
        out_shape=jax.ShapeDtypeStruct((n, HY_FFN), F32),
        compiler_params=_cp(("parallel",)),
        name=f"hyena_filter_mlp_{seq_len}",
    )(pe, w1p, b1, w2, b2, freq)
    tc = 128
    nc = (2 * MIX_W) // tc
    kern = functools.partial(_hy_filter_kernel, seq_len=seq_len)
    return pl.pallas_call(
        kern,
        grid=(nc,),
        in_specs=[full((n, HY_FFN)), full((n, 1)),
                  pl.BlockSpec((HY_FFN, tc), lambda j: (0, j)),
                  pl.BlockSpec((HY_FFN, tc), lambda j: (0, nc + j)),
                  pl.BlockSpec((1, tc), lambda j: (0, j)),
                  pl.BlockSpec((1, tc), lambda j: (0, nc + j))],
        out_specs=pl.BlockSpec((n, tc), lambda j: (0, j)),
        out_shape=jax.ShapeDtypeStruct((n, 2 * MIX_W), F32),
        compiler_params=_cp(("parallel",)),
        name=f"hyena_filter_{seq_len}",
    )(z, tn, w3, w3, decay, decay)


def _split_bf16(x):
    hi = x.astype(BF16)
    lo = (x - hi.astype(F32)).astype(BF16)
    return hi, lo


def _fft_stage1(load_rows, w1, a_ref, npg, n2, split):
    def body(m, _):
        xs = load_rows(m)
        if split:
            hi, lo = _split_bf16(xs)
            r = jnp.dot(w1, hi, preferred_element_type=F32) + jnp.dot(w1, lo, preferred_element_type=F32)
        else:
            r = jnp.dot(w1, xs.astype(BF16), preferred_element_type=F32)
        a_ref[pl.ds(m, npg, stride=2 * n2), :] = r[:npg]
        a_ref[pl.ds(n2 + m, npg, stride=2 * n2), :] = r[npg:]
        return 0

    lax.fori_loop(0, n2, body, 0, unroll=2)


def _hy_fft_kernel(k_ref, w1_ref, g_ref, kf_ref, a_ref, *, n1, n2, npg):
    w1 = w1_ref[...]
    _fft_stage1(lambda m: k_ref[pl.ds(m, n1, stride=n2), :], w1, a_ref, npg, n2, True)

    def body(p, _):
        rows = pl.ds(pl.multiple_of(p * 2 * n2, 2 * n2), 2 * n2)
        hi, lo = _split_bf16(a_ref[rows, :])
        g = g_ref[p]
        kf_ref[rows, :] = jnp.dot(g, hi, preferred_element_type=F32) + jnp.dot(g, lo, preferred_element_type=F32)
        return 0

    lax.fori_loop(0, npg, body, 0, unroll=4)


def hyena_filter_fft(seq_len, kfilt):
    n1, n2 = _fft_factors(seq_len)
    _, npg = _fft_pages(seq_len)
    w1, _, g, _, _ = _dft_tables_bf16(seq_len)
    tc = 128
    kern = functools.partial(_hy_fft_kernel, n1=n1, n2=n2, npg=npg)
    return pl.pallas_call(
        kern,
        grid=((2 * MIX_W) // tc,),
        in_specs=[pl.BlockSpec((n1 * n2, tc), lambda j: (0, j)),
                  pl.BlockSpec((2 * npg, n1), lambda j: (0, 0)),
                  pl.BlockSpec((npg, 2 * n2, 2 * n2), lambda j: (0, 0, 0))],
        out_specs=pl.BlockSpec((npg * 2 * n2, tc), lambda j: (0, j)),
        out_shape=jax.ShapeDtypeStruct((npg * 2 * n2, 2 * MIX_W), F32),
        scratch_shapes=[pltpu.VMEM((npg * 2 * n2, tc), F32)],
        compiler_params=_cp(("parallel",)),
        name=f"hyena_filter_fft_{seq_len}",
    )(kfilt, w1, g)


def _hy_fft_direct_kernel(k_ref, wf_ref, kf_ref):
    hi, lo = _split_bf16(k_ref[...])
    wf = wf_ref[...]
    kf_ref[...] = jnp.dot(wf, hi, preferred_element_type=F32) + jnp.dot(wf, lo, preferred_element_type=F32)


def hyena_filter_fft_direct(seq_len, kfilt):
    n = 2 * seq_len
    wf = jnp.asarray(_direct_dft_tables(seq_len)[0], BF16)
    tc = 256
    return pl.pallas_call(
        _hy_fft_direct_kernel,
        grid=((2 * MIX_W) // tc,),
        in_specs=[pl.BlockSpec((n, tc), lambda j: (0, j)), pl.BlockSpec((2 * n, n), lambda j: (0, 0))],
        out_specs=pl.BlockSpec((2 * n, tc), lambda j: (0, j)),
        out_shape=jax.ShapeDtypeStruct((2 * n, 2 * MIX_W), F32),
        compiler_params=_cp(("parallel",)),
        name=f"hyena_filter_fft_{seq_len}",
    )(kfilt, wf)


def _shift_rows(x, s, row):
    n = x.shape[0]
    y = pltpu.roll(x, s % n, 0)
    if s > 0:
        return jnp.where(row >= s, y, 0.0)
    return jnp.where(row < n + s, y, 0.0)


def _short_conv3(x_ref, w_ref):
    x = x_ref[...].astype(F32)
    row = lax.broadcasted_iota(I32, x.shape, 0)
    w = w_ref[...]
    return w[0:1] * _shift_rows(x, 1, row) + w[1:2] * x + w[2:3] * _shift_rows(x, -1, row)


def _hy_conv_kernel(v_ref, x1_ref, x2_ref, ws_ref, kf_ref, skip_ref, w1_ref, g_ref, gt_ref, w4_ref, y_ref,
                    z_ref, c_ref, a_ref, *, n1, n2, used, npg):
    o = pl.program_id(2)
    nh = n1 // 2

    @pl.when(o == 0)
    def _():
        z_ref[...] = _short_conv3(v_ref, ws_ref.at[0])

    w1 = w1_ref[...]
    _fft_stage1(lambda m: z_ref[pl.ds(m, nh, stride=n2), :], w1, a_ref, npg, n2, False)

    def page(p, _):
        rows = pl.ds(pl.multiple_of(p * 2 * n2, 2 * n2), 2 * n2)
        x = jnp.dot(g_ref[p], a_ref[rows, :].astype(BF16), preferred_element_type=F32)
        kf = kf_ref[rows, :]
        xr, xi = x[:n2], x[n2:]
        kr, ki = kf[:n2], kf[n2:]
        yc = jnp.concatenate([xr * kr - xi * ki, xr * ki + xi * kr], axis=0).astype(BF16)
        a_ref[rows, :] = jnp.dot(gt_ref[p], yc, preferred_element_type=F32)
        return 0

    lax.fori_loop(0, used, page, 0, unroll=5)

    w4 = w4_ref[...]

    def back(m, _):
        dr = a_ref[pl.ds(m, npg, stride=2 * n2), :]
        di = a_ref[pl.ds(n2 + m, npg, stride=2 * n2), :]
        d = jnp.concatenate([dr, di], axis=0).astype(BF16)
        c_ref[pl.ds(m, nh, stride=n2), :] = jnp.dot(w4, d, preferred_element_type=F32)
        return 0

    lax.fori_loop(0, n2, back, 0, unroll=2)

    @pl.when(o == 0)
    def _():
        z = z_ref[...]
        z_ref[...] = _short_conv3(x1_ref, ws_ref.at[1]) * (c_ref[...] + z * skip_ref[0:1])

    @pl.when(o == 1)
    def _():
        z = z_ref[...]
        y_ref[...] = (_short_conv3(x2_ref, ws_ref.at[2]) * (c_ref[...] + z * skip_ref[1:2])).astype(BF16)


def hyena_conv(proj, row_block_off, nseq, seq_len, w_short3, kf, skip):
    n1, n2 = _fft_factors(seq_len)
    used, npg = _fft_pages(seq_len)
    _, w1h, g, gt, w4 = _dft_tables_bf16(seq_len)
    rows_a = npg * 2 * n2
    tc = 128
    ncb = MIX_W // tc
    kern = functools.partial(_hy_conv_kernel, n1=n1, n2=n2, used=used, npg=npg)
    const3 = lambda c, b, o: (0, 0, 0)
    seg = lambda s: pl.BlockSpec((seq_len, tc), lambda c, b, o: (row_block_off + b, CB_HY + s * ncb + c))
    return pl.pallas_call(
        kern,
        grid=(ncb, nseq, 2),
        in_specs=[seg(0), seg(1), seg(2),
                  pl.BlockSpec((3, 3, tc), lambda c, b, o: (0, 0, c)),
                  pl.BlockSpec((rows_a, tc), lambda c, b, o: (0, o * ncb + c)),
                  pl.BlockSpec((2, tc), lambda c, b, o: (0, c)),
                  pl.BlockSpec((2 * npg, n1 // 2), lambda c, b, o: (0, 0)),
                  pl.BlockSpec((npg, 2 * n2, 2 * n2), const3, pipeline_mode=pl.Buffered(1)),
                  pl.BlockSpec((npg, 2 * n2, 2 * n2), const3, pipeline_mode=pl.Buffered(1)),
                  pl.BlockSpec((n1 // 2, 2 * npg), lambda c, b, o: (0, 0))],
        out_specs=pl.BlockSpec((seq_len, tc), lambda c, b, o: (b, c)),
        out_shape=jax.ShapeDtypeStruct((nseq * seq_len, MIX_W), BF16),
        scratch_shapes=[pltpu.VMEM((seq_len, tc), F32), pltpu.VMEM((seq_len, tc), F32),
                        pltpu.VMEM((rows_a, tc), F32)],
        compiler_params=_cp(("parallel", "parallel", "arbitrary")),
        name=f"hyena_conv_{seq_len}",
    )(proj, proj, proj, w_short3, kf, skip, w1h, g, gt, w4)


def _hy_direct_kernel(u_ref, ws_ref, kf_ref, skip_ref, wf_ref, wi_ref, y_ref, *, n):
    w = MIX_W
    wf = wf_ref[...]
    wi = wi_ref[...]
    z = _short_conv3(u_ref.at[:, pl.ds(0, w)], ws_ref.at[0])
    for o in range(2):
        x = jnp.dot(wf, z.astype(BF16), preferred_element_type=F32)
        kf = kf_ref[:, o * w:(o + 1) * w]
        xr, xi = x[:n], x[n:]
        kr, ki = kf[:n], kf[n:]
        yc = jnp.concatenate([xr * kr - xi * ki, xr * ki + xi * kr], axis=0).astype(BF16)
        conv = jnp.dot(wi, yc, preferred_element_type=F32)
        gate = _short_conv3(u_ref.at[:, pl.ds((o + 1) * w, w)], ws_ref.at[o + 1])
        z = gate * (conv + z * skip_ref[o:o + 1])
    y_ref[...] = z.astype(BF16)


def hyena_conv_direct(proj, row_block_off, nseq, seq_len, w_short3, kf, skip):
    n = 2 * seq_len
    wf_np, wi_np = _direct_dft_tables(seq_len)
    wf = jnp.asarray(wf_np[:, :seq_len], BF16)
    wi = jnp.asarray(wi_np, BF16)
    kern = functools.partial(_hy_direct_kernel, n=n)
    return pl.pallas_call(
        kern,
        grid=(nseq,),
        in_specs=[pl.BlockSpec((seq_len, 3 * MIX_W), lambda b: (row_block_off + b, 0)),
                  pl.BlockSpec((3, 3, MIX_W), lambda b: (0, 0, 0)),
                  pl.BlockSpec((2 * n, 2 * MIX_W), lambda b: (0, 0)),
                  pl.BlockSpec((2, MIX_W), lambda b: (0, 0)),
                  pl.BlockSpec((2 * n, seq_len), lambda b: (0, 0)),
                  pl.BlockSpec((seq_len, 2 * n), lambda b: (0, 0))],
        out_specs=pl.BlockSpec((seq_len, MIX_W), lambda b: (b, 0)),
        out_shape=jax.ShapeDtypeStruct((nseq * seq_len, MIX_W), BF16),
        compiler_params=_cp(("parallel",)),
        name=f"hyena_conv_{seq_len}",
    )(proj, w_short3, kf, skip, wf, wi)


def _dot_nt(a, b):
    return lax.dot_general(a, b, (((1,), (1,)), ((), ())), preferred_element_type=F32)


ML_CHUNK = 128


def _log_sigmoid(x):
    return jnp.minimum(x, 0.0) - jnp.log1p(jnp.exp(-jnp.abs(x)))


def _mlstm_kernel(*refs, seq_len, has_state):
    if has_state:
        (q_ref, k_ref, v_ref, o_ref, gr_ref, gc_ref, br_ref, bc_ref, gain_ref, c0_ref, n0_ref, m0_ref,
         y_ref, cout_ref, nout_ref, mout_ref, hs_ref, cs_ref, ns_ref, ms_ref) = refs
    else:
        (q_ref, k_ref, v_ref, o_ref, gr_ref, gc_ref, br_ref, bc_ref, gain_ref,
         y_ref, cout_ref, nout_ref, mout_ref, hs_ref, cs_ref, ns_ref, ms_ref) = refs
    t = ML_CHUNK
    nc = seq_len // t
    scale = ML_DK ** -0.5
    if has_state:
        cs_ref[...] = c0_ref[...]
        ns_ref[...] = n0_ref[...]
        ms_ref[...] = m0_ref[...]
    else:
        cs_ref[...] = jnp.zeros_like(cs_ref)
        ns_ref[...] = jnp.zeros_like(ns_ref)
        ms_ref[...] = jnp.zeros_like(ms_ref)
    hs_ref[...] = jnp.zeros_like(hs_ref)
    row = lax.broadcasted_iota(I32, (t, t), 0)
    col = lax.broadcasted_iota(I32, (t, t), 1)
    br = br_ref[...]
    bc = bc_ref[...]

    def step(j, _):
        for d in (0, 1):
            jj = j if d == 0 else nc - 1 - j
            rows = pl.ds(pl.multiple_of(jj * t, t), t)
            qf = q_ref[rows, :].astype(F32) * scale
            qb = qf.astype(BF16)
            kb = k_ref[rows, :]
            vb = v_ref[rows, :]
            gr = gr_ref[rows, :] + br
            gc = gc_ref[:, rows] + bc
            i_c = gr[:, 2 * d:2 * d + 1]
            f_c = _log_sigmoid(gr[:, 2 * d + 1:2 * d + 2])
            i_r = gc[2 * d:2 * d + 1, :]
            f_r = _log_sigmoid(gc[2 * d + 1:2 * d + 2, :])
            if d == 0:
                mask, mask_t = col <= row, row <= col
            else:
                mask, mask_t = col >= row, row >= col
            b_c = jnp.sum(jnp.where(mask, f_r, 0.0), axis=1, keepdims=True)
            b_r = jnp.sum(jnp.where(mask_t, f_c, 0.0), axis=0, keepdims=True)
            logw = jnp.where(mask, b_c - b_r + i_r, -jnp.inf)
            m = ms_ref[d][:, 0:1]
            g = b_c + m
            mt = jnp.maximum(g, jnp.max(logw, axis=1, keepdims=True))
            s = _dot_nt(qb, kb) * jnp.exp(logw - mt)
            inter = jnp.exp(g - mt)
            cm = cs_ref[d]
            nv = ns_ref[d]
            num = (jnp.dot(s.astype(BF16), vb, preferred_element_type=F32)
                   + inter * jnp.dot(qb, cm.astype(BF16), preferred_element_type=F32))
            den = jnp.sum(s, axis=1, keepdims=True) + inter * jnp.sum(qf * nv, axis=1, keepdims=True)
            h = num / jnp.maximum(jnp.abs(den), jnp.exp(-mt))
            hs_ref[rows, :] = hs_ref[rows, :] + h
            bl = jnp.sum(f_r, axis=1, keepdims=True)
            wlog_r = bl - b_r + i_r
            wlog_c = bl - b_c + i_c
            m_new = jnp.maximum(bl + m, jnp.max(wlog_r, axis=1, keepdims=True))
            dec = jnp.exp(bl + m - m_new)
            kw = jnp.exp(wlog_c - m_new) * kb.astype(F32)
            cs_ref[d] = dec * cm + jnp.dot(kw.T.astype(BF16), vb, preferred_element_type=F32)
            ns_ref[d] = dec * nv + jnp.sum(kw, axis=0, keepdims=True)
            ms_ref[d] = jnp.broadcast_to(m_new, (1, LANE))
        return 0

    lax.fori_loop(0, nc, step, 0, unroll=2)
    hsum = hs_ref[...]
    hn = hsum * lax.rsqrt(jnp.mean(hsum * hsum, axis=-1, keepdims=True) + EPS) * gain_ref[...]
    y_ref[...] = (jax.nn.sigmoid(o_ref[...].astype(F32)) * hn).astype(BF16)
    cout_ref[...] = cs_ref[...]
    nout_ref[...] = ns_ref[...]
    mout_ref[...] = ms_ref[...]


def mlstm_mixer(proj, gates_r, gates_c, bias_r, bias_c, gain, row_block_off, nseq, seq_len, state):
    has_state = state is not None
    kern = functools.partial(_mlstm_kernel, seq_len=seq_len, has_state=has_state)
    col = lambda cb: pl.BlockSpec((seq_len, LANE), lambda b, h: (row_block_off + b, cb + h))
    c_spec = pl.BlockSpec((None, 2, None, ML_DK, ML_DK), lambda b, h: (b, 0, h, 0, 0))
    v_spec = pl.BlockSpec((None, 2, None, 1, LANE), lambda b, h: (b, 0, h, 0, 0))
    in_specs = [col(CB_MLQ), col(CB_MLK), col(CB_MLV), col(CB_MLO),
                pl.BlockSpec((None, seq_len, 4), lambda b, h: (h, row_block_off + b, 0)),
                pl.BlockSpec((None, 4, seq_len), lambda b, h: (h, 0, row_block_off + b)),
                pl.BlockSpec((None, 1, 4), lambda b, h: (h, 0, 0)),
                pl.BlockSpec((None, 4, 1), lambda b, h: (h, 0, 0)),
                pl.BlockSpec((1, LANE), lambda b, h: (0, h))]
    args = [proj, proj, proj, proj, gates_r, gates_c, bias_r, bias_c, gain]
    if has_state:
        in_specs += [c_spec, v_spec, v_spec]
        args += list(state)
    return pl.pallas_call(
        kern,
        grid=(nseq, ML_H),
        in_specs=in_specs,
        out_specs=[pl.BlockSpec((seq_len, LANE), lambda b, h: (b, h)), c_spec, v_spec, v_spec],
        out_shape=[jax.ShapeDtypeStruct((nseq * seq_len, MIX_W), BF16),
                   jax.ShapeDtypeStruct((nseq, 2, ML_H, ML_DK, ML_DK), F32),
                   jax.ShapeDtypeStruct((nseq, 2, ML_H, 1, LANE), F32),
                   jax.ShapeDtypeStruct((nseq, 2, ML_H, 1, LANE), F32)],
        scratch_shapes=[pltpu.VMEM((seq_len, LANE), F32), pltpu.VMEM((2, ML_DK, ML_DK), F32),
                        pltpu.VMEM((2, 1, LANE), F32), pltpu.VMEM((2, 1, LANE), F32)],
        compiler_params=_cp(("parallel", "parallel")),
        name=f"mlstm_{seq_len}",
    )(*args)


def _softplus(x):
    return jnp.maximum(x, 0.0) + jnp.log1p(jnp.exp(-jnp.abs(x)))


def _gelu_tanh(x):
    return 0.5 * x * (1.0 + jnp.tanh(math.sqrt(2.0 / math.pi) * (x + 0.044715 * (x * x * x))))


def _lru_kernel(*refs, seq_len, has_state):
    if has_state:
        (x_ref, y_ref, cw_ref, cb_ref, wa_ref, ba_ref, wx_ref, bx_ref, lam_ref, h0_ref,
         out_ref, hl_ref, a_s, b_s, hsum) = refs
    else:
        (x_ref, y_ref, cw_ref, cb_ref, wa_ref, ba_ref, wx_ref, bx_ref, lam_ref,
         out_ref, hl_ref, a_s, b_s, hsum) = refs
    xf = x_ref[...].astype(F32)
    row = lax.broadcasted_iota(I32, xf.shape, 0)
    w = cw_ref[...]
    x = (w[0:1] * _shift_rows(xf, 2, row) + w[1:2] * _shift_rows(xf, 1, row) + w[2:3] * xf
         + w[3:4] * _shift_rows(xf, -1, row) + cb_ref[...])
    xb = x.astype(BF16)
    for d in (0, 1):
        r = jax.nn.sigmoid(jnp.dot(xb, wa_ref[d].astype(BF16), preferred_element_type=F32) + ba_ref[d])
        i = jax.nn.sigmoid(jnp.dot(xb, wx_ref[d].astype(BF16), preferred_element_type=F32) + bx_ref[d])
        log_a = (-LRU_C) * r * _softplus(-lam_ref[d])
        a_s[d] = jnp.exp(log_a)
        b_s[d] = jnp.sqrt(jnp.maximum(1.0 - jnp.exp(2.0 * log_a), 0.0)) * (i * x)

    nb = seq_len // 8
    sub = lax.broadcasted_iota(I32, (8, LANE), 0)

    def block_scan(a, b, reverse):
        for s in (1, 2, 4):
            sh = 8 - s if reverse else s
            ok = (sub < 8 - s) if reverse else (sub >= s)
            b = jnp.where(ok, a * pltpu.roll(b, sh, 0) + b, b)
            a = jnp.where(ok, a * pltpu.roll(a, sh, 0), a)
        return a, b

    def fwd(blk, carry):
        rows = pl.ds(pl.multiple_of(blk * 8, 8), 8)
        a, b = block_scan(a_s[0, rows, :], b_s[0, rows, :], False)
        h = a * carry + b
        hsum[rows, :] = h
        return h[7:8, :]

    def bwd(i, carry):
        rows = pl.ds(pl.multiple_of((nb - 1 - i) * 8, 8), 8)
        a, b = block_scan(a_s[1, rows, :], b_s[1, rows, :], True)
        h = a * carry + b
        hsum[rows, :] = hsum[rows, :] + h
        return h[0:1, :]

    zero = jnp.zeros((1, LANE), F32)
    hf = lax.fori_loop(0, nb, fwd, h0_ref[0] if has_state else zero, unroll=4)
    hb = lax.fori_loop(0, nb, bwd, h0_ref[1] if has_state else zero, unroll=4)
    out_ref[...] = (hsum[...] * _gelu_tanh(y_ref[...].astype(F32))).astype(BF16)
    hl_ref[0] = hf
    hl_ref[1] = hb


def rglru_mixer(proj, conv_w, conv_b, wa, ba, wx, bx, lam, row_block_off, nseq, seq_len, h0):
    has_state = h0 is not None
    kern = functools.partial(_lru_kernel, seq_len=seq_len, has_state=has_state)
    col = lambda cb: pl.BlockSpec((seq_len, LANE), lambda b, h: (row_block_off + b, cb + h))
    w_spec = pl.BlockSpec((2, None, LANE, LANE), lambda b, h: (0, h, 0, 0))
    v_spec = pl.BlockSpec((2, 1, LANE), lambda b, h: (0, 0, h))
    s_spec = pl.BlockSpec((None, 2, 1, LANE), lambda b, h: (b, 0, 0, h))
    in_specs = [col(CB_LRX), col(CB_LRY),
                pl.BlockSpec((4, LANE), lambda b, h: (0, h)),
                pl.BlockSpec((1, LANE), lambda b, h: (0, h)),
                w_spec, v_spec, w_spec, v_spec, v_spec]
    args = [proj, proj, conv_w, conv_b, wa, ba, wx, bx, lam]
    if has_state:
        in_specs.append(s_spec)
        args.append(h0)
    return pl.pallas_call(
        kern,
        grid=(nseq, LRU_H),
        in_specs=in_specs,
        out_specs=[pl.BlockSpec((seq_len, LANE), lambda b, h: (b, h)), s_spec],
        out_shape=[jax.ShapeDtypeStruct((nseq * seq_len, MIX_W), BF16),
                   jax.ShapeDtypeStruct((nseq, 2, 1, MIX_W), F32)],
        scratch_shapes=[pltpu.VMEM((2, seq_len, LANE), F32), pltpu.VMEM((2, seq_len, LANE), F32),
                        pltpu.VMEM((seq_len, LANE), F32)],
        compiler_params=_cp(("parallel", "parallel")),
        name=f"rglru_{seq_len}",
    )(*args)


@functools.lru_cache(maxsize=None)
def _rope_tables():
    t = np.arange(DEC_SEQ)
    pos_row, pos_col = (t // GRID_W).astype(np.float64), (t % GRID_W).astype(np.float64)
    half = DA_DK // 2
    inv = ROPE_BASE ** (-np.arange(0, half, 2, dtype=np.float64) / half)
    lane = np.arange(LANE)
    sub = lane % DA_DK
    pos = np.where((sub < half)[None, :], pos_row[:, None], pos_col[:, None])
    ang = pos * inv[sub % (half // 2)][None, :]
    first = (sub % half) < (half // 2)
    cos = np.cos(ang)
    sin = np.where(first[None, :], -np.sin(ang), np.sin(ang))
    cos_all = np.concatenate([np.ones((T_CTX, LANE))] + [cos] * DEC_BATCH, axis=0)
    sin_all = np.concatenate([np.zeros((T_CTX, LANE))] + [sin] * DEC_BATCH, axis=0)
    pm = np.kron(np.eye(2), np.full((DA_DK, DA_DK), 1.0 / DA_DK))
    return np.asarray(cos_all, np.float32), np.asarray(sin_all, np.float32), np.asarray(pm, np.float32)


def _qkprep_kernel(q_ref, k_ref, cos_ref, sin_ref, qg_ref, kg_ref, pm_ref, qo_ref, ko_ref, kn_ref):
    pm = pm_ref[...]
    cos = cos_ref[...]
    sin = sin_ref[...]
    lane = lax.broadcasted_iota(I32, cos.shape, 1)
    first = (lane % (DA_DK // 2)) < (DA_DK // 4)

    def norm(x, g):
        ms = jnp.dot(x * x, pm, precision=HIGHEST, preferred_element_type=F32)
        return x * lax.rsqrt(ms + EPS) * g

    def rope(x):
        partner = jnp.where(first, pltpu.roll(x, LANE - DA_DK // 4, 1), pltpu.roll(x, DA_DK // 4, 1))
        return x * cos + partner * sin

    q = norm(q_ref[...].astype(F32), qg_ref[...])
    k = norm(k_ref[...].astype(F32), kg_ref[...])
    kn_ref[...] = k
    qo_ref[...] = (rope(q) * (DA_DK ** -0.5)).astype(BF16)
    ko_ref[...] = rope(k).astype(BF16)


def qk_prepare(proj, q_gain, k_gain):
    cos, sin, pm = (jnp.asarray(t) for t in _rope_tables())
    tm = 512
    blk = lambda cb: pl.BlockSpec((tm, LANE), lambda i, h: (i, cb + h))
    tab = pl.BlockSpec((tm, LANE), lambda i, h: (i, 0))
    one = pl.BlockSpec((1, LANE), lambda i, h: (0, 0))
    out = pl.BlockSpec((tm, LANE), lambda i, h: (i, h))
    return pl.pallas_call(
        _qkprep_kernel,
        grid=(T_ALL // tm, DA_H),
        in_specs=[blk(CB_DAQ), blk(CB_DAK), tab, tab, one, one, pl.BlockSpec((LANE, LANE), lambda i, h: (0, 0))],
        out_specs=[out, out, out],
        out_shape=[jax.ShapeDtypeStruct((T_ALL, MIX_W), BF16), jax.ShapeDtypeStruct((T_ALL, MIX_W), BF16),
                   jax.ShapeDtypeStruct((T_ALL, MIX_W), F32)],
        compiler_params=_cp(("parallel", "parallel")),
        name="qk_prepare",
    )(proj, proj, cos, sin, q_gain, k_gain, pm)


def _attn_kernel(*refs, has_cache, lam_init, seq_len):
    if has_cache:
        q_ref, k_ref, v_ref, kc_ref, vc_ref, lp_ref, sg_ref, o_ref, ka_ref, va_ref = refs
    else:
        q_ref, k_ref, v_ref, lp_ref, sg_ref, o_ref, ka_ref, va_ref = refs

    @pl.when(pl.program_id(2) == 0)
    def _():
        ka_ref[0:seq_len, :] = k_ref[...]
        va_ref[0:seq_len, 0:LANE] = v_ref[...]
        if has_cache:
            ka_ref[seq_len:, :] = kc_ref[...].astype(BF16)
            va_ref[seq_len:, 0:LANE] = vc_ref[...].astype(BF16)
        va_ref[:, LANE:] = jnp.ones((va_ref.shape[0], LANE), BF16)

    q = q_ref[...]
    lane = lax.broadcasted_iota(I32, q.shape, 1)
    zero = jnp.zeros_like(q)
    lp = lp_ref[...]
    lam = (jnp.exp(jnp.sum(lp[0:1] * lp[1:2], axis=1, keepdims=True))
           - jnp.exp(jnp.sum(lp[2:3] * lp[3:4], axis=1, keepdims=True)) + lam_init)

    def softmax_half(qc):
        s = _dot_nt(qc, ka_ref[...])
        p = jnp.exp(s - jnp.max(s, axis=-1, keepdims=True)).astype(BF16)
        r = jnp.dot(p, va_ref[...], preferred_element_type=F32)
        return r[:, :LANE] / r[:, LANE:]

    o = softmax_half(jnp.where(lane < DA_DK, q, zero)) - lam * softmax_half(jnp.where(lane >= DA_DK, q, zero))
    o = o * lax.rsqrt(jnp.mean(o * o, axis=-1, keepdims=True) + EPS) * sg_ref[...]
    o_ref[...] = (o * (1.0 - lam_init)).astype(BF16)


def diff_attention(q_rot, k_rot, proj, lam_p, sub_gain, lam_init, row_block_off, nseq, seq_len, cache, layer):
    has_cache = cache is not None
    tq = 256
    kern = functools.partial(_attn_kernel, has_cache=has_cache, lam_init=lam_init, seq_len=seq_len)
    nq = seq_len // tq
    n_keys = seq_len + (cache[0].shape[2] if has_cache else 0)
    kv = lambda cb: pl.BlockSpec((seq_len, LANE), lambda b, h, i: (row_block_off + b, cb + h))
    in_specs = [pl.BlockSpec((tq, LANE), lambda b, h, i: ((row_block_off + b) * nq + i, h)), kv(0), kv(CB_DAV)]
    args = [q_rot, k_rot, proj]
    if has_cache:
        past = cache[0].shape[2]
        cspec = pl.BlockSpec((None, None, past, LANE), lambda b, h, i: (b, layer, 0, h))
        in_specs += [cspec, cspec]
        args += list(cache)
    in_specs += [pl.BlockSpec((4, DA_DK), lambda b, h, i: (0, 0)), pl.BlockSpec((1, LANE), lambda b, h, i: (0, 0))]
    args += [lam_p, sub_gain]
    return pl.pallas_call(
        kern,
        grid=(nseq, DA_H, nq),
        in_specs=in_specs,
        out_specs=pl.BlockSpec((tq, LANE), lambda b, h, i: (b * nq + i, h)),
        out_shape=jax.ShapeDtypeStruct((nseq * seq_len, MIX_W), BF16),
        scratch_shapes=[pltpu.VMEM((n_keys, LANE), BF16), pltpu.VMEM((n_keys, 2 * LANE), BF16)],
        compiler_params=_cp(("parallel", "parallel", "arbitrary")),
        name=f"diff_attention_{seq_len}",
    )(*args)


def _merge_kernel(h_ref, y0_ref, y1_ref, y2_ref, y3_ref, wg0_ref, wg1_ref, wg2_ref, wg3_ref, wb_ref, o_ref):
    h = h_ref[...]
    acc = None
    for bi, (y_ref, wg_ref) in enumerate(((y0_ref, wg0_ref), (y1_ref, wg1_ref), (y2_ref, wg2_ref), (y3_ref, wg3_ref))):
        gate = jax.nn.sigmoid(jnp.dot(h, wg_ref[...], preferred_element_type=F32))
        term = gate * jnp.dot(y_ref[...], wb_ref[bi], preferred_element_type=F32)
        acc = term if acc is None else acc + term
    o_ref[...] = acc.astype(BF16)


def branch_merge(h, ys, w_gate, w_branch):
    tm, tn = 1024, 256
    nj = D_MODEL // tn
    yspec = pl.BlockSpec((tm, MIX_W), lambda i, j: (i, 0))
    gspec = lambda bi: pl.BlockSpec((D_MODEL, tn), lambda i, j: (0, bi * nj + j))
    return pl.pallas_call(
        _merge_kernel,
        grid=(T_ALL // tm, nj),
        in_specs=[pl.BlockSpec((tm, D_MODEL), lambda i, j: (i, 0)), yspec, yspec, yspec, yspec,
                  gspec(0), gspec(1), gspec(2), gspec(3),
                  pl.BlockSpec((N_BRANCH, MIX_W, tn), lambda i, j: (0, 0, j))],
        out_specs=pl.BlockSpec((tm, tn), lambda i, j: (i, j)),
        out_shape=jax.ShapeDtypeStruct((T_ALL, D_MODEL), BF16),
        compiler_params=_cp(("parallel", "parallel")),
        name="branch_merge",
    )(h, *ys, w_gate, w_gate, w_gate, w_gate, w_branch)


def _outproj_kernel(m_ref, x_ref, g1_ref, w_ref, n2_ref, sh2_ref, sc2_ref, rt_ref, rl_ref, x1_ref, h2_ref, lg_ref):
    half = m_ref.shape[0] // 2
    for r in (0, half):
        rows = slice(r, r + half)
        y = jnp.dot(m_ref[rows, :], w_ref[...], preferred_element_type=F32)
        x1 = x_ref[rows, :] + g1_ref[...] * y
        x1_ref[rows, :] = x1
        h2 = _norm_mod(x1, n2_ref[...], sc2_ref[...], sh2_ref[...])
        h2_ref[rows, :] = _pack_bf16_pairs(h2)
        hi, lo = _split_bf16(h2)
        lg_ref[rows, :] = (jnp.dot(hi, rt_ref[...], preferred_element_type=F32)
                           + jnp.dot(lo, rt_ref[...], preferred_element_type=F32)
                           + jnp.dot(hi, rl_ref[...], preferred_element_type=F32))


def out_projection(merged, x, mod4, layer, w_out, norm2_g, router):
    tm = 256
    router_hi = router.astype(BF16)
    router_lo = (router - router_hi.astype(F32)).astype(BF16)
    return pl.pallas_call(
        _outproj_kernel,
        grid=(T_ALL // tm,),
        in_specs=[pl.BlockSpec((tm, D_MODEL), lambda i: (i, 0)),
                  pl.BlockSpec((tm, D_MODEL), lambda i: (i, 0)),
                  _mod_spec(layer, 2, tm, 1),
                  pl.BlockSpec((D_MODEL, D_MODEL), lambda i: (0, 0)),
                  pl.BlockSpec((None, 1, D_MODEL), lambda i: (layer, 0, 0)),
                  _mod_spec(layer, 3, tm, 1),
                  _mod_spec(layer, 4, tm, 1),
                  pl.BlockSpec((D_MODEL, LANE), lambda i: (0, 0)),
                  pl.BlockSpec((D_MODEL, LANE), lambda i: (0, 0))],
        out_specs=[pl.BlockSpec((tm, D_MODEL), lambda i: (i, 0)),
                   pl.BlockSpec((tm, D_MODEL // 2), lambda i: (i, 0)),
                   pl.BlockSpec((tm, LANE), lambda i: (i, 0))],
        out_shape=[jax.ShapeDtypeStruct((T_ALL, D_MODEL), F32),
                   jax.ShapeDtypeStruct((T_ALL, D_MODEL // 2), I32),
                   jax.ShapeDtypeStruct((T_ALL, LANE), F32)],
        compiler_params=_cp(("parallel",)),
        name="out_projection",
    )(merged, x, mod4, w_out, norm2_g, mod4, mod4, router_hi, router_lo)


MOE_TM = 256
N_ASSIGN = T_ALL * TOP_K
MOE_BLOCKS = (N_ASSIGN + N_EXPERTS * (MOE_TM - 1)) // MOE_TM + 1
ROUTER_TT = 512
GROUP_SIZE = N_EXPERTS // N_GROUPS


def _router_kernel(lg_ref, bias_ref, tri_ref, eidx_ref, w_ref, rank_ref, cnt_ref, carry_ref):
    i = pl.program_id(0)

    @pl.when(i == 0)
    def _():
        carry_ref[...] = jnp.zeros_like(carry_ref)

    tt = lg_ref.shape[0]
    shape3 = (N_GROUPS, GROUP_SIZE, tt)
    neg = -jnp.inf
    logits_t = lg_ref[...].T[:N_EXPERTS]
    scores = jax.nn.sigmoid(logits_t).reshape(shape3)
    sel = scores + bias_ref[...].reshape(N_GROUPS, GROUP_SIZE, 1)
    gi = lax.broadcasted_iota(I32, shape3, 0)
    ji = lax.broadcasted_iota(I32, shape3, 1)
    ei = gi * GROUP_SIZE + ji
    m1 = jnp.max(sel, axis=1, keepdims=True)
    first = jnp.min(jnp.where(sel == m1, ji, GROUP_SIZE), axis=1, keepdims=True)
    m2 = jnp.max(jnp.where(ji == first, neg, sel), axis=1, keepdims=True)
    cur = m1 + m2
    g1 = lax.broadcasted_iota(I32, cur.shape, 0)
    gmask = jnp.zeros(cur.shape, jnp.bool_)
    for _ in range(TOPK_GROUPS):
        mx = jnp.max(cur, axis=0, keepdims=True)
        idx = jnp.min(jnp.where(cur == mx, g1, N_GROUPS), axis=0, keepdims=True)
        hit = g1 == idx
        gmask = jnp.logical_or(gmask, hit)
        cur = jnp.where(hit, neg, cur)
    masked = jnp.where(gmask, sel, neg)

    def all_max(a):
        return jnp.max(jnp.max(a, axis=1, keepdims=True), axis=0, keepdims=True)

    def all_min(a):
        return jnp.min(jnp.min(a, axis=1, keepdims=True), axis=0, keepdims=True)

    def all_sum(a):
        return jnp.sum(jnp.sum(a, axis=1, keepdims=True), axis=0, keepdims=True)

    hits, idxs, ws = [], [], []
    for _ in range(TOP_K):
        mx = all_max(masked)
        idx = all_min(jnp.where(masked == mx, ei, N_EXPERTS))
        hit = ei == idx
        hits.append(hit)
        idxs.append(idx)
        ws.append(all_sum(jnp.where(hit, scores, 0.0)))
        masked = jnp.where(hit, neg, masked)
    wsum = ws[0]
    for wk in ws[1:]:
        wsum = wsum + wk
    onehot = jnp.zeros(shape3, F32)
    for hit in hits:
        onehot = onehot + hit.astype(F32)
    oh2 = onehot.reshape(N_EXPERTS, tt)
    before = jnp.dot(oh2.astype(BF16), tri_ref[...], preferred_element_type=F32) + carry_ref[:, 0:1]
    before3 = before.reshape(shape3)
    for k in range(TOP_K):
        eidx_ref[k:k + 1, :] = idxs[k].reshape(1, tt)
        w_ref[k:k + 1, :] = (ws[k] / wsum * ROUTED_SCALE).reshape(1, tt)
        rank_ref[k:k + 1, :] = all_sum(jnp.where(hits[k], before3, 0.0)).reshape(1, tt).astype(I32)
    eidx_ref[TOP_K:, :] = jnp.zeros((8 - TOP_K, tt), I32)
    w_ref[TOP_K:, :] = jnp.zeros((8 - TOP_K, tt), F32)
    rank_ref[TOP_K:, :] = jnp.zeros((8 - TOP_K, tt), I32)
    carry_ref[...] = carry_ref[...] + jnp.sum(oh2, axis=1, keepdims=True)
    cnt_ref[...] = carry_ref[...]


def moe_route(logits_t, bias):
    tt = ROUTER_TT
    tri = jnp.asarray(np.triu(np.ones((tt, tt), np.float32), 1), BF16)
    row8 = pl.BlockSpec((8, tt), lambda i: (0, i))
    return pl.pallas_call(
        _router_kernel,
        grid=(T_ALL // tt,),
        in_specs=[pl.BlockSpec((tt, LANE), lambda i: (i, 0)),
                  pl.BlockSpec((N_EXPERTS, 1), lambda i: (0, 0)),
                  pl.BlockSpec((tt, tt), lambda i: (0, 0))],
        out_specs=[row8, row8, row8, pl.BlockSpec((N_EXPERTS, LANE), lambda i: (0, 0))],
        out_shape=[jax.ShapeDtypeStruct((8, T_ALL), I32), jax.ShapeDtypeStruct((8, T_ALL), F32),
                   jax.ShapeDtypeStruct((8, T_ALL), I32), jax.ShapeDtypeStruct((N_EXPERTS, LANE), F32)],
        scratch_shapes=[pltpu.VMEM((N_EXPERTS, LANE), F32)],
        compiler_params=_cp(("arbitrary",)),
        name="moe_route",
    )(logits_t, bias, tri)


N_SLOTS = MOE_BLOCKS * MOE_TM
D_PACK = D_MODEL // 2


def _pack_bf16_pairs(x):
    n = x.shape[1] // 2
    hi = pltpu.bitcast(x[:, :n].astype(BF16).astype(F32), I32)
    lo = pltpu.bitcast(x[:, n:].astype(BF16).astype(F32), I32)
    return hi | lax.shift_right_logical(lo, 16)


def _unpack_bf16_pairs(u):
    hi = pltpu.bitcast(u & jnp.int32(-65536), F32)
    lo = pltpu.bitcast(lax.shift_left(u, 16), F32)
    return jnp.concatenate([hi, lo], axis=1)


def _dispatch_kernel(slots_ref, h2p_ref, xs_hbm, sem, *, tm):
    i = pl.program_id(0)

    def body(r, _):
        for k in range(TOP_K):
            s = slots_ref[(i * tm + r) * TOP_K + k]
            pltpu.make_async_copy(h2p_ref.at[pl.ds(r, 1), :], xs_hbm.at[pl.ds(s, 1), :], sem).start()
        return 0

    lax.fori_loop(0, tm, body, 0)
    for k in range(TOP_K):
        pltpu.make_async_copy(h2p_ref, xs_hbm.at[pl.ds(0, tm), :], sem).wait()


def moe_dispatch(slots, h2p):
    tm = 256
    return pl.pallas_call(
        functools.partial(_dispatch_kernel, tm=tm),
        grid_spec=pltpu.PrefetchScalarGridSpec(
            num_scalar_prefetch=1,
            grid=(T_ALL // tm,),
            in_specs=[pl.BlockSpec((tm, D_PACK), lambda i, s: (i, 0))],
            out_specs=pl.BlockSpec(memory_space=pl.ANY),
            scratch_shapes=[pltpu.SemaphoreType.DMA(())]),
        out_shape=jax.ShapeDtypeStruct((N_SLOTS, D_PACK), I32),
        compiler_params=_cp(("arbitrary",), unchecked_dma=True),
        name="moe_dispatch",
    )(slots, h2p)


def _gmm_kernel(be_ref, bv_ref, x_ref, wg_ref, wu_ref, wd_ref, y_ref, wgb, wub, wdb):
    i = pl.program_id(0)
    e = be_ref[i]
    prev = be_ref[jnp.maximum(i - 1, 0)]

    @pl.when(jnp.logical_or(i == 0, e != prev))
    def _():
        step = 512
        for r in range(0, D_MODEL, step):
            wgb[r:r + step, :] = wg_ref[r:r + step, :].astype(BF16)
            wub[r:r + step, :] = wu_ref[r:r + step, :].astype(BF16)
        for r in range(0, D_EXPERT, 128):
            wdb[r:r + 128, :] = wd_ref[r:r + 128, :].astype(BF16)

    nv = bv_ref[i]

    @pl.when(nv == 0)
    def _():
        y_ref[...] = jnp.zeros_like(y_ref)

    @pl.when(nv > 0)
    def _():
        row = lax.broadcasted_iota(I32, (x_ref.shape[0], 1), 0)
        xin = jnp.where(row < nv, x_ref[...], 0)
        x = _unpack_bf16_pairs(xin).astype(BF16)
        g = jnp.dot(x, wgb[...], preferred_element_type=F32)
        u = jnp.dot(x, wub[...], preferred_element_type=F32)
        a = (g * jax.nn.sigmoid(g) * u).astype(BF16)
        y_ref[...] = _pack_bf16_pairs(jnp.dot(a, wdb[...], preferred_element_type=F32))


def moe_experts(block_e, block_valid, xs, w_gate, w_up, w_down, layer):
    tm = MOE_TM
    wspec = lambda a, b: pl.BlockSpec((None, None, a, b), lambda i, be, bv: (layer, be[i], 0, 0))
    return pl.pallas_call(
        _gmm_kernel,
        grid_spec=pltpu.PrefetchScalarGridSpec(
            num_scalar_prefetch=2,
            grid=(MOE_BLOCKS,),
            in_specs=[pl.BlockSpec((tm, D_PACK), lambda i, be, bv: (i, 0)),
                      wspec(D_MODEL, D_EXPERT), wspec(D_MODEL, D_EXPERT), wspec(D_EXPERT, D_MODEL)],
            out_specs=pl.BlockSpec((tm, D_PACK), lambda i, be, bv: (i, 0)),
            scratch_shapes=[pltpu.VMEM((D_MODEL, D_EXPERT), BF16), pltpu.VMEM((D_MODEL, D_EXPERT), BF16),
                            pltpu.VMEM((D_EXPERT, D_MODEL), BF16)]),
        out_shape=jax.ShapeDtypeStruct((N_SLOTS, D_PACK), I32),
        compiler_params=_cp(("arbitrary",)),
        name="moe_experts",
    )(block_e, block_valid, xs, w_gate, w_up, w_down)


def _combine_kernel(slots_ref, x1_ref, h2p_ref, w_ref, g2_ref, sg_ref, su_ref, sd_ref, ys_hbm, out_ref, buf, sem,
                    *, tm, nsteps):
    i = pl.program_id(0)

    def issue(step, slot):
        def body(r, _):
            for k in range(TOP_K):
                s = slots_ref[(step * tm + r) * TOP_K + k]
                pltpu.make_async_copy(ys_hbm.at[pl.ds(s, 1), :], buf.at[slot, pl.ds(k * tm + r, 1), :],
                                      sem.at[slot]).start()
            return 0

        lax.fori_loop(0, tm, body, 0)

    @pl.when(i == 0)
    def _():
        issue(0, 0)

    @pl.when(i + 1 < nsteps)
    def _():
        issue(i + 1, (i + 1) % 2)

    slot = i % 2
    pltpu.make_async_copy(ys_hbm.at[pl.ds(0, TOP_K * tm), :], buf.at[slot], sem.at[slot]).wait()
    hb = _unpack_bf16_pairs(h2p_ref[...]).astype(BF16)
    g = jnp.dot(hb, sg_ref[...], preferred_element_type=F32)
    u = jnp.dot(hb, su_ref[...], preferred_element_type=F32)
    acc = jnp.dot((g * jax.nn.sigmoid(g) * u).astype(BF16), sd_ref[...], preferred_element_type=F32)
    w = w_ref[...]
    for k in range(TOP_K):
        acc = acc + w[:, k:k + 1] * _unpack_bf16_pairs(buf[slot, k * tm:(k + 1) * tm, :])
    out_ref[...] = x1_ref[...] + g2_ref[...] * acc


def moe_combine(slots, x1, h2p, w_tok, mod4, layer, sg, su, sd, ys):
    tm = 128
    nsteps = T_ALL // tm
    kern = functools.partial(_combine_kernel, tm=tm, nsteps=nsteps)
    row = lambda n: pl.BlockSpec((tm, n), lambda i, s: (i, 0))
    return pl.pallas_call(
        kern,
        grid_spec=pltpu.PrefetchScalarGridSpec(
            num_scalar_prefetch=1,
            grid=(nsteps,),
            in_specs=[row(D_MODEL), row(D_PACK), row(8),
                      pl.BlockSpec((None, None, 1, D_MODEL),
                                   lambda i, s: (layer, _mod_row(i, tm) * 6 + 5, 0, 0)),
                      pl.BlockSpec((D_MODEL, D_EXPERT), lambda i, s: (0, 0)),
                      pl.BlockSpec((D_MODEL, D_EXPERT), lambda i, s: (0, 0)),
                      pl.BlockSpec((D_EXPERT, D_MODEL), lambda i, s: (0, 0)),
                      pl.BlockSpec(memory_space=pl.ANY)],
            out_specs=row(D_MODEL),
            scratch_shapes=[pltpu.VMEM((2, TOP_K * tm, D_PACK), I32), pltpu.SemaphoreType.DMA((2,))]),
        out_shape=jax.ShapeDtypeStruct((T_ALL, D_MODEL), F32),
        compiler_params=_cp(("arbitrary",), unchecked_dma=True),
        name="moe_combine",
    )(slots, x1, h2p, w_tok, mod4, sg, su, sd, ys)


def moe_ffn(x1, h2p, logits_t, mod4, layer, moe_bias, w_gate, w_up, w_down, sg, su, sd):
    eidx, w_t, rank, cnt = moe_route(logits_t, moe_bias)
    counts = cnt[:, 0].astype(I32)
    padded = (counts + MOE_TM - 1) // MOE_TM * MOE_TM
    pad_end = jnp.cumsum(padded)
    pad_start = pad_end - padded
    experts = jnp.arange(N_EXPERTS, dtype=I32)
    base = jnp.sum(jnp.where(eidx[:TOP_K, :, None] == experts, pad_start, 0), axis=-1)
    slots = (base + rank[:TOP_K]).T.reshape(-1)
    starts = jnp.arange(MOE_BLOCKS, dtype=I32) * MOE_TM
    block_e = jnp.minimum(jnp.sum((pad_end[None, :] <= starts[:, None]).astype(I32), axis=1), N_EXPERTS - 1)
    mine = block_e[:, None] == experts
    left = jnp.sum(jnp.where(mine, counts + pad_start, 0), axis=1) - starts
    block_valid = jnp.clip(left, 0, MOE_TM).astype(I32)
    xs = moe_dispatch(slots, h2p)
    ys = moe_experts(block_e, block_valid, xs, w_gate, w_up, w_down, layer)
    return moe_combine(slots, x1, h2p, w_t.T, mod4, layer, sg, su, sd, ys)


def _ml_gate_layouts(gp, gate_bias):
    nt = gp.shape[0]
    g16 = gp[:, :4 * ML_H].reshape(nt, 2, 2, ML_H)
    gr = jnp.transpose(g16, (3, 0, 1, 2)).reshape(ML_H, nt, 4)
    gc = jnp.transpose(gr, (0, 2, 1))
    b = jnp.transpose(gate_bias, (2, 0, 1)).reshape(ML_H, 4)
    return gr, gc, b[:, None, :], b[:, :, None]


def kernel(x_prompt, x_sample, c, cache_k, cache_v, state_mlstm_C, state_mlstm_n, state_mlstm_m, state_rglru,
           c_ctx, w_mod, b_mod, norm1, norm2, w_in, hy_short, hy_w1, hy_b1, hy_w2, hy_b2, hy_w3, hy_freq,
           hy_decay, hy_skip, ml_gate_bias, ml_out_norm, lru_conv_w, lru_conv_b, lru_wa, lru_ba, lru_wx, lru_bx,
           lru_lambda, da_q_norm, da_k_norm, da_lambda, da_sub_norm, w_branch, w_out, moe_router, moe_bias,
           moe_w_gate, moe_w_up, moe_w_down, sh_w_gate, sh_w_up, sh_w_down):
    x = jnp.concatenate([x_prompt.reshape(T_CTX, D_MODEL), x_sample.reshape(T_DEN, D_MODEL)], axis=0)
    cvecs = jnp.concatenate([c_ctx[None], c, jnp.zeros((N_MOD_ROWS - 1 - DEC_BATCH, D_MODEL), F32)], axis=0)
    mod4 = modulation_all(cvecs, w_mod, b_mod).reshape(DEPTH, N_MOD_ROWS * 6, 1, D_MODEL)
    norm1_3 = norm1.reshape(DEPTH, 1, D_MODEL)
    norm2_3 = norm2.reshape(DEPTH, 1, D_MODEL)
    cache_k4 = cache_k.reshape(DEC_BATCH, DEPTH, -1, MIX_W)
    cache_v4 = cache_v.reshape(DEC_BATCH, DEPTH, -1, MIX_W)
    den_off = T_CTX // DEC_SEQ
    new_k, new_v, new_c, new_n, new_m, new_h = [], [], [], [], [], []
    for l in range(DEPTH):
        lam_init = 0.8 - 0.6 * math.exp(-0.3 * l)
        wl = w_in[l]
        n_ml = 4 * ML_H
        w_main = jnp.concatenate([wl[:, :7 * MIX_W], wl[:, 7 * MIX_W + n_ml:12 * MIX_W + n_ml]], axis=1).astype(BF16)
        w_g16 = jnp.pad(wl[:, 7 * MIX_W:7 * MIX_W + n_ml], ((0, 0), (0, LANE - n_ml))).astype(BF16)
        w_gate = wl[:, 12 * MIX_W + n_ml:].astype(BF16)
        proj, h, gp = in_projection(x, norm1_3, mod4, l, w_main, w_g16)

        w1p = jnp.pad(hy_w1[l], ((0, LANE - HY_PE), (0, 0)))
        ws3 = jnp.transpose(hy_short[l].reshape(3, 3, MIX_W), (1, 0, 2))
        y_hy = []
        for off, nseq, sl in ((0, BATCH, SEQ), (den_off, DEC_BATCH, DEC_SEQ)):
            kfilt = hyena_filter(sl, w1p, hy_b1[l][None], hy_w2[l], hy_b2[l][None], hy_freq[l][None], hy_w3[l],
                                 hy_decay[l][None])
            if sl <= 512:
                y_hy.append(hyena_conv_direct(proj, off, nseq, sl, ws3, hyena_filter_fft_direct(sl, kfilt), hy_skip[l]))
            else:
                y_hy.append(hyena_conv(proj, off, nseq, sl, ws3, hyena_filter_fft(sl, kfilt), hy_skip[l]))

        gr, gc, b_r, b_c = _ml_gate_layouts(gp, ml_gate_bias[l])
        gain = ml_out_norm[l][None]
        y_ml0, c_new, n_new, m_new = mlstm_mixer(proj, gr, gc, b_r, b_c, gain, 0, BATCH, SEQ, None)
        st = (state_mlstm_C[:, l], state_mlstm_n[:, l][:, :, :, None, :],
              jnp.broadcast_to(state_mlstm_m[:, l][:, :, :, None, None], (DEC_BATCH, 2, ML_H, 1, LANE)))
        y_ml1 = mlstm_mixer(proj, gr, gc, b_r, b_c, gain, den_off, DEC_BATCH, DEC_SEQ, st)[0]

        lru_args = (lru_conv_w[l], lru_conv_b[l][None], lru_wa[l], lru_ba[l][:, None, :], lru_wx[l],
                    lru_bx[l][:, None, :], lru_lambda[l][:, None, :])
        y_lr0, h_new = rglru_mixer(proj, *lru_args, 0, BATCH, SEQ, None)
        y_lr1 = rglru_mixer(proj, *lru_args, den_off, DEC_BATCH, DEC_SEQ, state_rglru[:, l][:, :, None, :])[0]

        q_rot, k_rot, k_norm = qk_prepare(proj, jnp.tile(da_q_norm[l], 2)[None], jnp.tile(da_k_norm[l], 2)[None])
        sub_gain = da_sub_norm[l][None]
        y_da0 = diff_attention(q_rot, k_rot, proj, da_lambda[l], sub_gain, lam_init, 0, BATCH, SEQ, None, l)
        y_da1 = diff_attention(q_rot, k_rot, proj, da_lambda[l], sub_gain, lam_init, den_off, DEC_BATCH, DEC_SEQ,
                               (cache_k4, cache_v4), l)

        ys = [jnp.concatenate(p, axis=0) for p in (y_hy, (y_ml0, y_ml1), (y_lr0, y_lr1), (y_da0, y_da1))]
        merged = branch_merge(h, ys, w_gate, w_branch[l].astype(BF16))
        x1, h2, logits_t = out_projection(merged, x, mod4, l, w_out[l].astype(BF16), norm2_3,
                                          jnp.pad(moe_router[l], ((0, 0), (0, LANE - N_EXPERTS))))
        x = moe_ffn(x1, h2, logits_t, mod4, l, moe_bias[l][:, None], moe_w_gate, moe_w_up, moe_w_down,
                    sh_w_gate[l].astype(BF16), sh_w_up[l].astype(BF16), sh_w_down[l].astype(BF16))

        new_k.append(k_norm[:T_CTX].reshape(BATCH, SEQ, DA_H, 2, DA_DK))
        new_v.append(proj[:T_CTX, CB_DAV * LANE:(CB_DAV + DA_H) * LANE].astype(F32).reshape(BATCH, SEQ, DA_H, -1))
        new_c.append(c_new)
        new_n.append(n_new[:, :, :, 0, :])
        new_m.append(m_new[:, :, :, 0, 0])
        new_h.append(h_new[:, :, 0, :])
    y_prompt = x[:T_CTX].reshape(BATCH, SEQ, D_MODEL)
    y_sample = x[T_CTX:].reshape(DEC_BATCH, DEC_SEQ, D_MODEL)
    stack = lambda parts: jnp.stack(parts, axis=1)
    return (y_prompt, y_sample, stack(new_k), stack(new_v), stack(new_c), stack(new_n), stack(new_m), stack(new_h))
```

```python
import functools
import math

import numpy as np
import jax
import jax.numpy as jnp
from jax import lax
from jax.experimental import pallas as pl
from jax.experimental.pallas import tpu as pltpu

F32 = jnp.float32
BF16 = jnp.bfloat16
I32 = jnp.int32
HIGHEST = lax.Precision.HIGHEST

D_MODEL = 2048
BATCH = 16
SEQ = 256
DEPTH = 2
DEC_BATCH = 4
DEC_SEQ = 4096
GRID_W = 64
N_BRANCH = 4
MIX_W = 512
HY_BANDS = 16
HY_PE = 1 + 2 * HY_BANDS
HY_FFN = 64
ML_H = 4
ML_DK = 128
LRU_H = 4
LRU_C = 8.0
DA_H = 4
DA_DK = 64
ROPE_BASE = 10000.0
N_EXPERTS = 64
TOP_K = 6
N_GROUPS = 8
TOPK_GROUPS = 4
D_EXPERT = 512
ROUTED_SCALE = 2.5
EPS = 1e-6

LANE = 128
T_CTX = BATCH * SEQ
T_DEN = DEC_BATCH * DEC_SEQ
T_ALL = T_CTX + T_DEN
N_MOD_ROWS = 8
VMEM_LIMIT = 56 * 1024 * 1024

CB_HY = 0
CB_MLQ, CB_MLK, CB_MLV, CB_MLO = 12, 16, 20, 24
CB_LRX, CB_LRY = 28, 32
CB_DAQ, CB_DAK, CB_DAV = 36, 40, 44
N_MAIN = 6144


def _cp(sem, vmem=VMEM_LIMIT, unchecked_dma=False):
    return pltpu.CompilerParams(dimension_semantics=sem, vmem_limit_bytes=vmem, disable_bounds_checks=unchecked_dma)


def _mod_row(i, rows_per_block):
    nctx = T_CTX // rows_per_block
    per = DEC_SEQ // rows_per_block
    return jnp.where(i < nctx, 0, 1 + (i - nctx) // per)


def _mod_spec(layer, seg, rows_per_block, ngrid):
    if ngrid == 1:
        return pl.BlockSpec((None, None, 1, D_MODEL), lambda i: (layer, _mod_row(i, rows_per_block) * 6 + seg, 0, 0))
    return pl.BlockSpec((None, None, 1, D_MODEL), lambda i, j: (layer, _mod_row(i, rows_per_block) * 6 + seg, 0, 0))


def _mod_kernel(c_ref, w_ref, b_ref, o_ref):
    c = c_ref[...]
    a = (c * jax.nn.sigmoid(c)).astype(BF16)
    o_ref[...] = jnp.dot(a, w_ref[...].astype(BF16), preferred_element_type=F32) + b_ref[...]


def modulation_all(cvecs, w_mod, b_mod):
    tn = 1024
    n = 6 * D_MODEL
    return pl.pallas_call(
        _mod_kernel,
        grid=(DEPTH, n // tn),
        in_specs=[pl.BlockSpec((N_MOD_ROWS, D_MODEL), lambda l, j: (0, 0)),
                  pl.BlockSpec((None, D_MODEL, tn), lambda l, j: (l, 0, j)),
                  pl.BlockSpec((None, 1, tn), lambda l, j: (l, 0, j))],
        out_specs=pl.BlockSpec((None, N_MOD_ROWS, tn), lambda l, j: (l, 0, j)),
        out_shape=jax.ShapeDtypeStruct((DEPTH, N_MOD_ROWS, n), F32),
        compiler_params=_cp(("parallel", "parallel")),
        name="modulation",
    )(cvecs, w_mod, b_mod.reshape(DEPTH, 1, n))


def _norm_mod(x, g, sc, sh):
    ms = jnp.mean(x * x, axis=-1, keepdims=True)
    return (x * lax.rsqrt(ms + EPS) * g) * (1.0 + sc) + sh


def _inproj_kernel(x_ref, g_ref, sh_ref, sc_ref, w_ref, wg_ref, proj_ref, h_ref, gp_ref, *, tm, sub):
    j = pl.program_id(1)

    @pl.when(j == 0)
    def _():
        g = g_ref[...]
        sc = sc_ref[...]
        sh = sh_ref[...]

        def body(r, _):
            rows = pl.ds(pl.multiple_of(r * sub, sub), sub)
            hb = _norm_mod(x_ref[rows, :], g, sc, sh).astype(BF16)
            h_ref[rows, :] = hb
            gp_ref[rows, :] = jnp.dot(hb, wg_ref[...], preferred_element_type=F32)
            return 0

        lax.fori_loop(0, tm // sub, body, 0)

    proj_ref[...] = jnp.dot(h_ref[...], w_ref[...], preferred_element_type=F32).astype(BF16)


def in_projection(x, norm_g, mod4, layer, w_main, w_gate16):
    tm, tn = 1024, 1024
    kern = functools.partial(_inproj_kernel, tm=tm, sub=256)
    return pl.pallas_call(
        kern,
        grid=(T_ALL // tm, N_MAIN // tn),
        in_specs=[pl.BlockSpec((tm, D_MODEL), lambda i, j: (i, 0)),
                  pl.BlockSpec((None, 1, D_MODEL), lambda i, j: (layer, 0, 0)),
                  _mod_spec(layer, 0, tm, 2),
                  _mod_spec(layer, 1, tm, 2),
                  pl.BlockSpec((D_MODEL, tn), lambda i, j: (0, j)),
                  pl.BlockSpec((D_MODEL, LANE), lambda i, j: (0, 0))],
        out_specs=[pl.BlockSpec((tm, tn), lambda i, j: (i, j)),
                   pl.BlockSpec((tm, D_MODEL), lambda i, j: (i, 0)),
                   pl.BlockSpec((tm, LANE), lambda i, j: (i, 0))],
        out_shape=[jax.ShapeDtypeStruct((T_ALL, N_MAIN), BF16),
                   jax.ShapeDtypeStruct((T_ALL, D_MODEL), BF16),
                   jax.ShapeDtypeStruct((T_ALL, LANE), F32)],
        compiler_params=_cp(("parallel", "arbitrary")),
        name="in_projection",
    )(x, norm_g, mod4, mod4, w_main, w_gate16)


def _fft_factors(seq_len):
    n2 = 64
    return 2 * seq_len // n2, n2


def _fft_pages(seq_len):
    n1, _ = _fft_factors(seq_len)
    used = n1 // 2 + 1
    return used, -(-used // 8) * 8


@functools.lru_cache(maxsize=None)
def _dft_tables(seq_len):
    n1, n2 = _fft_factors(seq_len)
    used, npg = _fft_pages(seq_len)
    n = n1 * n2
    th1 = 2.0 * np.pi * np.outer(np.arange(npg, dtype=np.float64), np.arange(n1, dtype=np.float64)) / n1
    w1 = np.concatenate([np.cos(th1), -np.sin(th1)], axis=0)
    w1h = w1[:, : n1 // 2]
    wt = np.where(np.arange(npg) < used, 2.0, 0.0)
    wt[0] = 1.0
    wt[n1 // 2] = 1.0
    th4 = th1[:, : n1 // 2].T
    w4 = np.concatenate([np.cos(th4) * wt, -np.sin(th4) * wt], axis=1) / n
    k1 = np.arange(npg, dtype=np.float64)[:, None, None]
    k2 = np.arange(n2, dtype=np.float64)[None, :, None]
    m2 = np.arange(n2, dtype=np.float64)[None, None, :]
    ph = -2.0 * np.pi * (m2 * k1 / n + m2 * k2 / n2)
    gr, gi = np.cos(ph), np.sin(ph)
    g = np.concatenate([np.concatenate([gr, -gi], axis=2), np.concatenate([gi, gr], axis=2)], axis=1)
    gt = np.transpose(g, (0, 2, 1))
    return tuple(np.asarray(t, np.float32) for t in (w1, w1h, g, gt, w4))


@functools.lru_cache(maxsize=None)
def _direct_dft_tables(seq_len):
    n = 2 * seq_len
    th = 2.0 * np.pi * np.outer(np.arange(n, dtype=np.float64), np.arange(n, dtype=np.float64)) / n
    wf = np.concatenate([np.cos(th), -np.sin(th)], axis=0)
    wi = np.concatenate([np.cos(th), -np.sin(th)], axis=1)[:seq_len] / n
    return np.asarray(wf, np.float32), np.asarray(wi, np.float32)


def _dft_tables_bf16(seq_len):
    return tuple(jnp.asarray(t, BF16) for t in _dft_tables(seq_len))


@functools.lru_cache(maxsize=None)
def _filter_positions(seq_len):
    pos = np.concatenate([np.arange(seq_len), [0], np.arange(seq_len - 1, 0, -1)]).astype(np.float64)
    tn = pos / seq_len
    bands = np.linspace(1e-4, HY_BANDS - 1, HY_BANDS)
    ang = (2.0 * math.pi / seq_len) * pos[:, None] * bands
    pe = np.zeros((2 * seq_len, LANE), np.float64)
    pe[:, 0] = tn
    pe[:, 1:1 + HY_BANDS] = np.cos(ang)
    pe[:, 1 + HY_BANDS:HY_PE] = np.sin(ang)
    return np.asarray(pe, np.float32), np.asarray(tn[:, None], np.float32)


def _hy_mlp_kernel(pe_ref, w1_ref, b1_ref, w2_ref, b2_ref, fr_ref, z_ref):
    fr = fr_ref[...]
    z = jnp.sin(fr * (jnp.dot(pe_ref[...], w1_ref[...], precision=HIGHEST, preferred_element_type=F32) + b1_ref[...]))
    z_ref[...] = jnp.sin(fr * (jnp.dot(z, w2_ref[...], precision=HIGHEST, preferred_element_type=F32) + b2_ref[...]))


def _hy_filter_kernel(z_ref, tn_ref, w3a_ref, w3b_ref, dca_ref, dcb_ref, k_ref, *, seq_len):
    def taps(rows, w3_ref, dc_ref):
        f = jnp.dot(z_ref[rows, :].astype(BF16), w3_ref[...].astype(BF16), preferred_element_type=F32)
        return f * jnp.exp(-tn_ref[rows, :] * jnp.abs(dc_ref[...]))

    ka = taps(pl.ds(0, seq_len), w3a_ref, dca_ref)
    kb = taps(pl.ds(seq_len, seq_len), w3b_ref, dcb_ref)
    row = lax.broadcasted_iota(I32, ka.shape, 0)
    ka = ka + jnp.where(row == 0, taps(pl.ds(0, 8), w3b_ref, dcb_ref)[0:1], 0.0)
    kb = jnp.where(row == 0, 0.0, kb)
    inv = 1.0 / (jnp.sum(jnp.abs(ka), axis=0, keepdims=True) + jnp.sum(jnp.abs(kb), axis=0, keepdims=True))
    k_ref[0:seq_len, :] = ka * inv
    k_ref[seq_len:, :] = kb * inv


def hyena_filter(seq_len, w1p, b1, w2, b2, freq, w3, decay):
    pe, tn = (jnp.asarray(t) for t in _filter_positions(seq_len))
    n = 2 * seq_len
    tr = min(n, 1024)
    full = lambda shape: pl.BlockSpec(shape, lambda j: (0,) * len(shape))
    z = pl.pallas_call(
        _hy_mlp_kernel,
        grid=(n // tr,),
        in_specs=[pl.BlockSpec((tr, LANE), lambda j: (j, 0)), full((LANE, HY_FFN)), full((1, HY_FFN)),
                  full((HY_FFN, HY_FFN)), full((1, HY_FFN)), full((1, HY_FFN))],
        out_specs=pl.BlockSpec((tr, HY_FFN), lambda j: (j, 0)),
        out_shape=jax.ShapeDtypeStruct((n, HY_FFN), F32),
        compiler_params=_cp(("parallel",)),
        name=f"hyena_filter_mlp_{seq_len}",
    )(pe, w1p, b1, w2, b2, freq)
    tc = 128
    nc = (2 * MIX_W) // tc
    kern = functools.partial(_hy_filter_kernel, seq_len=seq_len)
    return pl.pallas_call(
        kern,
        grid=(nc,),
        in_specs=[full((n, HY_FFN)), full((n, 1)),
                  pl.BlockSpec((HY_FFN, tc), lambda j: (0, j)),
                  pl.BlockSpec((HY_FFN, tc), lambda j: (0, nc + j)),
                  pl.BlockSpec((1, tc), lambda j: (0, j)),
                  pl.BlockSpec((1, tc), lambda j: (0, nc + j))],
        out_specs=pl.BlockSpec((n, tc), lambda j: (0, j)),
        out_shape=jax.ShapeDtypeStruct((n, 2 * MIX_W), F32),
        compiler_params=_cp(("parallel",)),
        name=f"hyena_filter_{seq_len}",
    )(z, tn, w3, w3, decay, decay)


def _split_bf16(x):
    hi = x.astype(BF16)
    lo = (x - hi.astype(F32)).astype(BF16)
    return hi, lo


def _fft_stage1(load_rows, w1, a_ref, npg, n2, split):
    def body(m, _):
        xs = load_rows(m)
        if split:
            hi, lo = _split_bf16(xs)
            r = jnp.dot(w1, hi, preferred_element_type=F32) + jnp.dot(w1, lo, preferred_element_type=F32)
        else:
            r = jnp.dot(w1, xs.astype(BF16), preferred_element_type=F32)
        a_ref[pl.ds(m, npg, stride=2 * n2), :] = r[:npg]
        a_ref[pl.ds(n2 + m, npg, stride=2 * n2), :] = r[npg:]
        return 0

    lax.fori_loop(0, n2, body, 0, unroll=2)


def _hy_fft_kernel(k_ref, w1_ref, g_ref, kf_ref, a_ref, *, n1, n2, npg):
    w1 = w1_ref[...]
    _fft_stage1(lambda m: k_ref[pl.ds(m, n1, stride=n2), :], w1, a_ref, npg, n2, True)

    def body(p, _):
        rows = pl.ds(pl.multiple_of(p * 2 * n2, 2 * n2), 2 * n2)
        hi, lo = _split_bf16(a_ref[rows, :])
        g = g_ref[p]
        kf_ref[rows, :] = jnp.dot(g, hi, preferred_element_type=F32) + jnp.dot(g, lo, preferred_element_type=F32)
        return 0

    lax.fori_loop(0, npg, body, 0, unroll=4)


def hyena_filter_fft(seq_len, kfilt):
    n1, n2 = _fft_factors(seq_len)
    _, npg = _fft_pages(seq_len)
    w1, _, g, _, _ = _dft_tables_bf16(seq_len)
    tc = 128
    kern = functools.partial(_hy_fft_kernel, n1=n1, n2=n2, npg=npg)
    return pl.pallas_call(
        kern,
        grid=((2 * MIX_W) // tc,),
        in_specs=[pl.BlockSpec((n1 * n2, tc), lambda j: (0, j)),
                  pl.BlockSpec((2 * npg, n1), lambda j: (0, 0)),
                  pl.BlockSpec((npg, 2 * n2, 2 * n2), lambda j: (0, 0, 0))],
        out_specs=pl.BlockSpec((npg * 2 * n2, tc), lambda j: (0, j)),
        out_shape=jax.ShapeDtypeStruct((npg * 2 * n2, 2 * MIX_W), F32),
        scratch_shapes=[pltpu.VMEM((npg * 2 * n2, tc), F32)],
        compiler_params=_cp(("parallel",)),
        name=f"hyena_filter_fft_{seq_len}",
    )(kfilt, w1, g)


def _hy_fft_direct_kernel(k_ref, wf_ref, kf_ref):
    hi, lo = _split_bf16(k_ref[...])
    wf = wf_ref[...]
    kf_ref[...] = jnp.dot(wf, hi, preferred_element_type=F32) + jnp.dot(wf, lo, preferred_element_type=F32)


def hyena_filter_fft_direct(seq_len, kfilt):
    n = 2 * seq_len
    wf = jnp.asarray(_direct_dft_tables(seq_len)[0], BF16)
    tc = 256
    return pl.pallas_call(
        _hy_fft_direct_kernel,
        grid=((2 * MIX_W) // tc,),
        in_specs=[pl.BlockSpec((n, tc), lambda j: (0, j)), pl.BlockSpec((2 * n, n), lambda j: (0, 0))],
        out_specs=pl.BlockSpec((2 * n, tc), lambda j: (0, j)),
        out_shape=jax.ShapeDtypeStruct((2 * n, 2 * MIX_W), F32),
        compiler_params=_cp(("parallel",)),
        name=f"hyena_filter_fft_{seq_len}",
    )(kfilt, wf)


def _shift_rows(x, s, row):
    n = x.shape[0]
    y = pltpu.roll(x, s % n, 0)
    if s > 0:
        return jnp.where(row >= s, y, 0.0)
    return jnp.where(row < n + s, y, 0.0)


def _short_conv3(x_ref, w_ref):
    x = x_ref[...].astype(F32)
    row = lax.broadcasted_iota(I32, x.shape, 0)
    w = w_ref[...]
    return w[0:1] * _shift_rows(x, 1, row) + w[1:2] * x + w[2:3] * _shift_rows(x, -1, row)


def _hy_conv_kernel(v_ref, x1_ref, x2_ref, ws_ref, kf_ref, skip_ref, w1_ref, g_ref, gt_ref, w4_ref, y_ref,
                    z_ref, c_ref, a_ref, *, n1, n2, used, npg):
    o = pl.program_id(2)
    nh = n1 // 2

    @pl.when(o == 0)
    def _():
        z_ref[...] = _short_conv3(v_ref, ws_ref.at[0])

    w1 = w1_ref[...]
    _fft_stage1(lambda m: z_ref[pl.ds(m, nh, stride=n2), :], w1, a_ref, npg, n2, False)

    def page(p, _):
        rows = pl.ds(pl.multiple_of(p * 2 * n2, 2 * n2), 2 * n2)
        x = jnp.dot(g_ref[p], a_ref[rows, :].astype(BF16), preferred_element_type=F32)
        kf = kf_ref[rows, :]
        xr, xi = x[:n2], x[n2:]
        kr, ki = kf[:n2], kf[n2:]
        yc = jnp.concatenate([xr * kr - xi * ki, xr * ki + xi * kr], axis=0).astype(BF16)
        a_ref[rows, :] = jnp.dot(gt_ref[p], yc, preferred_element_type=F32)
        return 0

    lax.fori_loop(0, used, page, 0, unroll=5)

    w4 = w4_ref[...]

    def back(m, _):
        dr = a_ref[pl.ds(m, npg, stride=2 * n2), :]
        di = a_ref[pl.ds(n2 + m, npg, stride=2 * n2), :]
        d = jnp.concatenate([dr, di], axis=0).astype(BF16)
        c_ref[pl.ds(m, nh, stride=n2), :] = jnp.dot(w4, d, preferred_element_type=F32)
        return 0

    lax.fori_loop(0, n2, back, 0, unroll=2)

    @pl.when(o == 0)
    def _():
        z = z_ref[...]
        z_ref[...] = _short_conv3(x1_ref, ws_ref.at[1]) * (c_ref[...] + z * skip_ref[0:1])

    @pl.when(o == 1)
    def _():
        z = z_ref[...]
        y_ref[...] = (_short_conv3(x2_ref, ws_ref.at[2]) * (c_ref[...] + z * skip_ref[1:2])).astype(BF16)


def hyena_conv(proj, row_block_off, nseq, seq_len, w_short3, kf, skip):
    n1, n2 = _fft_factors(seq_len)
    used, npg = _fft_pages(seq_len)
    _, w1h, g, gt, w4 = _dft_tables_bf16(seq_len)
    rows_a = npg * 2 * n2
    tc = 128
    ncb = MIX_W // tc
    kern = functools.partial(_hy_conv_kernel, n1=n1, n2=n2, used=used, npg=npg)
    const3 = lambda c, b, o: (0, 0, 0)
    seg = lambda s: pl.BlockSpec((seq_len, tc), lambda c, b, o: (row_block_off + b, CB_HY + s * ncb + c))
    return pl.pallas_call(
        kern,
        grid=(ncb, nseq, 2),
        in_specs=[seg(0), seg(1), seg(2),
                  pl.BlockSpec((3, 3, tc), lambda c, b, o: (0, 0, c)),
                  pl.BlockSpec((rows_a, tc), lambda c, b, o: (0, o * ncb + c)),
                  pl.BlockSpec((2, tc), lambda c, b, o: (0, c)),
                  pl.BlockSpec((2 * npg, n1 // 2), lambda c, b, o: (0, 0)),
                  pl.BlockSpec((npg, 2 * n2, 2 * n2), const3, pipeline_mode=pl.Buffered(1)),
                  pl.BlockSpec((npg, 2 * n2, 2 * n2), const3, pipeline_mode=pl.Buffered(1)),
                  pl.BlockSpec((n1 // 2, 2 * npg), lambda c, b, o: (0, 0))],
        out_specs=pl.BlockSpec((seq_len, tc), lambda c, b, o: (b, c)),
        out_shape=jax.ShapeDtypeStruct((nseq * seq_len, MIX_W), BF16),
        scratch_shapes=[pltpu.VMEM((seq_len, tc), F32), pltpu.VMEM((seq_len, tc), F32),
                        pltpu.VMEM((rows_a, tc), F32)],
        compiler_params=_cp(("parallel", "parallel", "arbitrary")),
        name=f"hyena_conv_{seq_len}",
    )(proj, proj, proj, w_short3, kf, skip, w1h, g, gt, w4)


def _hy_direct_kernel(u_ref, ws_ref, kf_ref, skip_ref, wf_ref, wi_ref, y_ref, *, n):
    w = MIX_W
    wf = wf_ref[...]
    wi = wi_ref[...]
    z = _short_conv3(u_ref.at[:, pl.ds(0, w)], ws_ref.at[0])
    for o in range(2):
        x = jnp.dot(wf, z.astype(BF16), preferred_element_type=F32)
        kf = kf_ref[:, o * w:(o + 1) * w]
        xr, xi = x[:n], x[n:]
        kr, ki = kf[:n], kf[n:]
        yc = jnp.concatenate([xr * kr - xi * ki, xr * ki + xi * kr], axis=0).astype(BF16)
        conv = jnp.dot(wi, yc, preferred_element_type=F32)
        gate = _short_conv3(u_ref.at[:, pl.ds((o + 1) * w, w)], ws_ref.at[o + 1])
        z = gate * (conv + z * skip_ref[o:o + 1])
    y_ref[...] = z.astype(BF16)


def hyena_conv_direct(proj, row_block_off, nseq, seq_len, w_short3, kf, skip):
    n = 2 * seq_len
    wf_np, wi_np = _direct_dft_tables(seq_len)
    wf = jnp.asarray(wf_np[:, :seq_len], BF16)
    wi = jnp.asarray(wi_np, BF16)
    kern = functools.partial(_hy_direct_kernel, n=n)
    return pl.pallas_call(
        kern,
        grid=(nseq,),
        in_specs=[pl.BlockSpec((seq_len, 3 * MIX_W), lambda b: (row_block_off + b, 0)),
                  pl.BlockSpec((3, 3, MIX_W), lambda b: (0, 0, 0)),
                  pl.BlockSpec((2 * n, 2 * MIX_W), lambda b: (0, 0)),
                  pl.BlockSpec((2, MIX_W), lambda b: (0, 0)),
                  pl.BlockSpec((2 * n, seq_len), lambda b: (0, 0)),
                  pl.BlockSpec((seq_len, 2 * n), lambda b: (0, 0))],
        out_specs=pl.BlockSpec((seq_len, MIX_W), lambda b: (b, 0)),
        out_shape=jax.ShapeDtypeStruct((nseq * seq_len, MIX_W), BF16),
        compiler_params=_cp(("parallel",)),
        name=f"hyena_conv_{seq_len}",
    )(proj, w_short3, kf, skip, wf, wi)


def _dot_nt(a, b):
    return lax.dot_general(a, b, (((1,), (1,)), ((), ())), preferred_element_type=F32)


ML_CHUNK = 128


def _log_sigmoid(x):
    return jnp.minimum(x, 0.0) - jnp.log1p(jnp.exp(-jnp.abs(x)))


def _mlstm_kernel(*refs, seq_len, has_state):
    if has_state:
        (q_ref, k_ref, v_ref, o_ref, gr_ref, gc_ref, br_ref, bc_ref, gain_ref, c0_ref, n0_ref, m0_ref,
         y_ref, cout_ref, nout_ref, mout_ref, hs_ref, cs_ref, ns_ref, ms_ref) = refs
    else:
        (q_ref, k_ref, v_ref, o_ref, gr_ref, gc_ref, br_ref, bc_ref, gain_ref,
         y_ref, cout_ref, nout_ref, mout_ref, hs_ref, cs_ref, ns_ref, ms_ref) = refs
    t = ML_CHUNK
    nc = seq_len // t
    scale = ML_DK ** -0.5
    if has_state:
        cs_ref[...] = c0_ref[...]
        ns_ref[...] = n0_ref[...]
        ms_ref[...] = m0_ref[...]
    else:
        cs_ref[...] = jnp.zeros_like(cs_ref)
        ns_ref[...] = jnp.zeros_like(ns_ref)
        ms_ref[...] = jnp.zeros_like(ms_ref)
    hs_ref[...] = jnp.zeros_like(hs_ref)
    row = lax.broadcasted_iota(I32, (t, t), 0)
    col = lax.broadcasted_iota(I32, (t, t), 1)
    br = br_ref[...]
    bc = bc_ref[...]

    def step(j, _):
        for d in (0, 1):
            jj = j if d == 0 else nc - 1 - j
            rows = pl.ds(pl.multiple_of(jj * t, t), t)
            qf = q_ref[rows, :].astype(F32) * scale
            qb = qf.astype(BF16)
            kb = k_ref[rows, :]
            vb = v_ref[rows, :]
            gr = gr_ref[rows, :] + br
            gc = gc_ref[:, rows] + bc
            i_c = gr[:, 2 * d:2 * d + 1]
            f_c = _log_sigmoid(gr[:, 2 * d + 1:2 * d + 2])
            i_r = gc[2 * d:2 * d + 1, :]
            f_r = _log_sigmoid(gc[2 * d + 1:2 * d + 2, :])
            if d == 0:
                mask, mask_t = col <= row, row <= col
            else:
                mask, mask_t = col >= row, row >= col
            b_c = jnp.sum(jnp.where(mask, f_r, 0.0), axis=1, keepdims=True)
            b_r = jnp.sum(jnp.where(mask_t, f_c, 0.0), axis=0, keepdims=True)
            logw = jnp.where(mask, b_c - b_r + i_r, -jnp.inf)
            m = ms_ref[d][:, 0:1]
            g = b_c + m
            mt = jnp.maximum(g, jnp.max(logw, axis=1, keepdims=True))
            s = _dot_nt(qb, kb) * jnp.exp(logw - mt)
            inter = jnp.exp(g - mt)
            cm = cs_ref[d]
            nv = ns_ref[d]
            num = (jnp.dot(s.astype(BF16), vb, preferred_element_type=F32)
                   + inter * jnp.dot(qb, cm.astype(BF16), preferred_element_type=F32))
            den = jnp.sum(s, axis=1, keepdims=True) + inter * jnp.sum(qf * nv, axis=1, keepdims=True)
            h = num / jnp.maximum(jnp.abs(den), jnp.exp(-mt))
            hs_ref[rows, :] = hs_ref[rows, :] + h
            bl = jnp.sum(f_r, axis=1, keepdims=True)
            wlog_r = bl - b_r + i_r
            wlog_c = bl - b_c + i_c
            m_new = jnp.maximum(bl + m, jnp.max(wlog_r, axis=1, keepdims=True))
            dec = jnp.exp(bl + m - m_new)
            kw = jnp.exp(wlog_c - m_new) * kb.astype(F32)
            cs_ref[d] = dec * cm + jnp.dot(kw.T.astype(BF16), vb, preferred_element_type=F32)
            ns_ref[d] = dec * nv + jnp.sum(kw, axis=0, keepdims=True)
            ms_ref[d] = jnp.broadcast_to(m_new, (1, LANE))
        return 0

    lax.fori_loop(0, nc, step, 0, unroll=2)
    hsum = hs_ref[...]
    hn = hsum * lax.rsqrt(jnp.mean(hsum * hsum, axis=-1, keepdims=True) + EPS) * gain_ref[...]
    y_ref[...] = (jax.nn.sigmoid(o_ref[...].astype(F32)) * hn).astype(BF16)
    cout_ref[...] = cs_ref[...]
    nout_ref[...] = ns_ref[...]
    mout_ref[...] = ms_ref[...]


def mlstm_mixer(proj, gates_r, gates_c, bias_r, bias_c, gain, row_block_off, nseq, seq_len, state):
    has_state = state is not None
    kern = functools.partial(_mlstm_kernel, seq_len=seq_len, has_state=has_state)
    col = lambda cb: pl.BlockSpec((seq_len, LANE), lambda b, h: (row_block_off + b, cb + h))
    c_spec = pl.BlockSpec((None, 2, None, ML_DK, ML_DK), lambda b, h: (b, 0, h, 0, 0))
    v_spec = pl.BlockSpec((None, 2, None, 1, LANE), lambda b, h: (b, 0, h, 0, 0))
    in_specs = [col(CB_MLQ), col(CB_MLK), col(CB_MLV), col(CB_MLO),
                pl.BlockSpec((None, seq_len, 4), lambda b, h: (h, row_block_off + b, 0)),
                pl.BlockSpec((None, 4, seq_len), lambda b, h: (h, 0, row_block_off + b)),
                pl.BlockSpec((None, 1, 4), lambda b, h: (h, 0, 0)),
                pl.BlockSpec((None, 4, 1), lambda b, h: (h, 0, 0)),
                pl.BlockSpec((1, LANE), lambda b, h: (0, h))]
    args = [proj, proj, proj, proj, gates_r, gates_c, bias_r, bias_c, gain]
    if has_state:
        in_specs += [c_spec, v_spec, v_spec]
        args += list(state)
    return pl.pallas_call(
        kern,
        grid=(nseq, ML_H),
        in_specs=in_specs,
        out_specs=[pl.BlockSpec((seq_len, LANE), lambda b, h: (b, h)), c_spec, v_spec, v_spec],
        out_shape=[jax.ShapeDtypeStruct((nseq * seq_len, MIX_W), BF16),
                   jax.ShapeDtypeStruct((nseq, 2, ML_H, ML_DK, ML_DK), F32),
                   jax.ShapeDtypeStruct((nseq, 2, ML_H, 1, LANE), F32),
                   jax.ShapeDtypeStruct((nseq, 2, ML_H, 1, LANE), F32)],
        scratch_shapes=[pltpu.VMEM((seq_len, LANE), F32), pltpu.VMEM((2, ML_DK, ML_DK), F32),
                        pltpu.VMEM((2, 1, LANE), F32), pltpu.VMEM((2, 1, LANE), F32)],
        compiler_params=_cp(("parallel", "parallel")),
        name=f"mlstm_{seq_len}",
    )(*args)


def _softplus(x):
    return jnp.maximum(x, 0.0) + jnp.log1p(jnp.exp(-jnp.abs(x)))


def _gelu_tanh(x):
    return 0.5 * x * (1.0 + jnp.tanh(math.sqrt(2.0 / math.pi) * (x + 0.044715 * (x * x * x))))


def _lru_kernel(*refs, seq_len, has_state):
    if has_state:
        (x_ref, y_ref, cw_ref, cb_ref, wa_ref, ba_ref, wx_ref, bx_ref, lam_ref, h0_ref,
         out_ref, hl_ref, a_s, b_s, hsum) = refs
    else:
        (x_ref, y_ref, cw_ref, cb_ref, wa_ref, ba_ref, wx_ref, bx_ref, lam_ref,
         out_ref, hl_ref, a_s, b_s, hsum) = refs
    xf = x_ref[...].astype(F32)
    row = lax.broadcasted_iota(I32, xf.shape, 0)
    w = cw_ref[...]
    x = (w[0:1] * _shift_rows(xf, 2, row) + w[1:2] * _shift_rows(xf, 1, row) + w[2:3] * xf
         + w[3:4] * _shift_rows(xf, -1, row) + cb_ref[...])
    xb = x.astype(BF16)
    for d in (0, 1):
        r = jax.nn.sigmoid(jnp.dot(xb, wa_ref[d].astype(BF16), preferred_element_type=F32) + ba_ref[d])
        i = jax.nn.sigmoid(jnp.dot(xb, wx_ref[d].astype(BF16), preferred_element_type=F32) + bx_ref[d])
        log_a = (-LRU_C) * r * _softplus(-lam_ref[d])
        a_s[d] = jnp.exp(log_a)
        b_s[d] = jnp.sqrt(jnp.maximum(1.0 - jnp.exp(2.0 * log_a), 0.0)) * (i * x)

    nb = seq_len // 8
    sub = lax.broadcasted_iota(I32, (8, LANE), 0)

    def block_scan(a, b, reverse):
        for s in (1, 2, 4):
            sh = 8 - s if reverse else s
            ok = (sub < 8 - s) if reverse else (sub >= s)
            b = jnp.where(ok, a * pltpu.roll(b, sh, 0) + b, b)
            a = jnp.where(ok, a * pltpu.roll(a, sh, 0), a)
        return a, b

    def fwd(blk, carry):
        rows = pl.ds(pl.multiple_of(blk * 8, 8), 8)
        a, b = block_scan(a_s[0, rows, :], b_s[0, rows, :], False)
        h = a * carry + b
        hsum[rows, :] = h
        return h[7:8, :]

    def bwd(i, carry):
        rows = pl.ds(pl.multiple_of((nb - 1 - i) * 8, 8), 8)
        a, b = block_scan(a_s[1, rows, :], b_s[1, rows, :], True)
        h = a * carry + b
        hsum[rows, :] = hsum[rows, :] + h
        return h[0:1, :]

    zero = jnp.zeros((1, LANE), F32)
    hf = lax.fori_loop(0, nb, fwd, h0_ref[0] if has_state else zero, unroll=4)
    hb = lax.fori_loop(0, nb, bwd, h0_ref[1] if has_state else zero, unroll=4)
    out_ref[...] = (hsum[...] * _gelu_tanh(y_ref[...].astype(F32))).astype(BF16)
    hl_ref[0] = hf
    hl_ref[1] = hb


def rglru_mixer(proj, conv_w, conv_b, wa, ba, wx, bx, lam, row_block_off, nseq, seq_len, h0):
    has_state = h0 is not None
    kern = functools.partial(_lru_kernel, seq_len=seq_len, has_state=has_state)
    col = lambda cb: pl.BlockSpec((seq_len, LANE), lambda b, h: (row_block_off + b, cb + h))
    w_spec = pl.BlockSpec((2, None, LANE, LANE), lambda b, h: (0, h, 0, 0))
    v_spec = pl.BlockSpec((2, 1, LANE), lambda b, h: (0, 0, h))
    s_spec = pl.BlockSpec((None, 2, 1, LANE), lambda b, h: (b, 0, 0, h))
    in_specs = [col(CB_LRX), col(CB_LRY),
                pl.BlockSpec((4, LANE), lambda b, h: (0, h)),
                pl.BlockSpec((1, LANE), lambda b, h: (0, h)),
                w_spec, v_spec, w_spec, v_spec, v_spec]
    args = [proj, proj, conv_w, conv_b, wa, ba, wx, bx, lam]
    if has_state:
        in_specs.append(s_spec)
        args.append(h0)
    return pl.pallas_call(
        kern,
        grid=(nseq, LRU_H),
        in_specs=in_specs,
        out_specs=[pl.BlockSpec((seq_len, LANE), lambda b, h: (b, h)), s_spec],
        out_shape=[jax.ShapeDtypeStruct((nseq * seq_len, MIX_W), BF16),
                   jax.ShapeDtypeStruct((nseq, 2, 1, MIX_W), F32)],
        scratch_shapes=[pltpu.VMEM((2, seq_len, LANE), F32), pltpu.VMEM((2, seq_len, LANE), F32),
                        pltpu.VMEM((seq_len, LANE), F32)],
        compiler_params=_cp(("parallel", "parallel")),
        name=f"rglru_{seq_len}",
    )(*args)


@functools.lru_cache(maxsize=None)
def _rope_tables():
    t = np.arange(DEC_SEQ)
    pos_row, pos_col = (t // GRID_W).astype(np.float64), (t % GRID_W).astype(np.float64)
    half = DA_DK // 2
    inv = ROPE_BASE ** (-np.arange(0, half, 2, dtype=np.float64) / half)
    lane = np.arange(LANE)
    sub = lane % DA_DK
    pos = np.where((sub < half)[None, :], pos_row[:, None], pos_col[:, None])
    ang = pos * inv[sub % (half // 2)][None, :]
    first = (sub % half) < (half // 2)
    cos = np.cos(ang)
    sin = np.where(first[None, :], -np.sin(ang), np.sin(ang))
    cos_all = np.concatenate([np.ones((T_CTX, LANE))] + [cos] * DEC_BATCH, axis=0)
    sin_all = np.concatenate([np.zeros((T_CTX, LANE))] + [sin] * DEC_BATCH, axis=0)
    pm = np.kron(np.eye(2), np.full((DA_DK, DA_DK), 1.0 / DA_DK))
    return np.asarray(cos_all, np.float32), np.asarray(sin_all, np.float32), np.asarray(pm, np.float32)


def _qkprep_kernel(q_ref, k_ref, cos_ref, sin_ref, qg_ref, kg_ref, pm_ref, qo_ref, ko_ref, kn_ref):
    pmb = pm_ref[...].astype(BF16)
    cos = cos_ref[...]
    sin = sin_ref[...]
    lane = lax.broadcasted_iota(I32, cos.shape, 1)
    first = (lane % (DA_DK // 2)) < (DA_DK // 4)

    def norm(x, g):
        hi, lo = _split_bf16(x * x)
        ms = jnp.dot(hi, pmb, preferred_element_type=F32) + jnp.dot(lo, pmb, preferred_element_type=F32)
        return x * lax.rsqrt(ms + EPS) * g

    def rope(x):
        partner = jnp.where(first, pltpu.roll(x, LANE - DA_DK // 4, 1), pltpu.roll(x, DA_DK // 4, 1))
        return x * cos + partner * sin

    q = norm(q_ref[...].astype(F32), qg_ref[...])
    k = norm(k_ref[...].astype(F32), kg_ref[...])
    kn_ref[...] = k
    qo_ref[...] = (rope(q) * (DA_DK ** -0.5)).astype(BF16)
    ko_ref[...] = rope(k).astype(BF16)


def qk_prepare(proj, q_gain, k_gain):
    cos, sin, pm = (jnp.asarray(t) for t in _rope_tables())
    tm = 512
    blk = lambda cb: pl.BlockSpec((tm, LANE), lambda i, h: (i, cb + h))
    tab = pl.BlockSpec((tm, LANE), lambda i, h: (i, 0))
    one = pl.BlockSpec((1, LANE), lambda i, h: (0, 0))
    out = pl.BlockSpec((tm, LANE), lambda i, h: (i, h))
    return pl.pallas_call(
        _qkprep_kernel,
        grid=(T_ALL // tm, DA_H),
        in_specs=[blk(CB_DAQ), blk(CB_DAK), tab, tab, one, one, pl.BlockSpec((LANE, LANE), lambda i, h: (0, 0))],
        out_specs=[out, out, out],
        out_shape=[jax.ShapeDtypeStruct((T_ALL, MIX_W), BF16), jax.ShapeDtypeStruct((T_ALL, MIX_W), BF16),
                   jax.ShapeDtypeStruct((T_ALL, MIX_W), F32)],
        compiler_params=_cp(("parallel", "parallel")),
        name="qk_prepare",
    )(proj, proj, cos, sin, q_gain, k_gain, pm)


def _attn_kernel(*refs, has_cache, lam_init, seq_len):
    if has_cache:
        q_ref, k_ref, v_ref, kc_ref, vc_ref, lp_ref, sg_ref, o_ref, ka_ref, va_ref = refs
    else:
        q_ref, k_ref, v_ref, lp_ref, sg_ref, o_ref, ka_ref, va_ref = refs

    @pl.when(pl.program_id(2) == 0)
    def _():
        ka_ref[0:seq_len, :] = k_ref[...]
        va_ref[0:seq_len, 0:LANE] = v_ref[...]
        if has_cache:
            ka_ref[seq_len:, :] = kc_ref[...].astype(BF16)
            va_ref[seq_len:, 0:LANE] = vc_ref[...].astype(BF16)
        va_ref[:, LANE:] = jnp.ones((va_ref.shape[0], LANE), BF16)

    q = q_ref[...]
    lane = lax.broadcasted_iota(I32, q.shape, 1)
    zero = jnp.zeros_like(q)
    lp = lp_ref[...]
    lam = (jnp.exp(jnp.sum(lp[0:1] * lp[1:2], axis=1, keepdims=True))
           - jnp.exp(jnp.sum(lp[2:3] * lp[3:4], axis=1, keepdims=True)) + lam_init)

    def softmax_half(qc):
        s = _dot_nt(qc, ka_ref[...])
        p = jnp.exp(s - jnp.max(s, axis=-1, keepdims=True)).astype(BF16)
        r = jnp.dot(p, va_ref[...], preferred_element_type=F32)
        return r[:, :LANE] / r[:, LANE:]

    o = softmax_half(jnp.where(lane < DA_DK, q, zero)) - lam * softmax_half(jnp.where(lane >= DA_DK, q, zero))
    o = o * lax.rsqrt(jnp.mean(o * o, axis=-1, keepdims=True) + EPS) * sg_ref[...]
    o_ref[...] = (o * (1.0 - lam_init)).astype(BF16)


def diff_attention(q_rot, k_rot, proj, lam_p, sub_gain, lam_init, row_block_off, nseq, seq_len, cache, layer):
    has_cache = cache is not None
    tq = 256
    kern = functools.partial(_attn_kernel, has_cache=has_cache, lam_init=lam_init, seq_len=seq_len)
    nq = seq_len // tq
    n_keys = seq_len + (cache[0].shape[2] if has_cache else 0)
    kv = lambda cb: pl.BlockSpec((seq_len, LANE), lambda b, h, i: (row_block_off + b, cb + h))
    in_specs = [pl.BlockSpec((tq, LANE), lambda b, h, i: ((row_block_off + b) * nq + i, h)), kv(0), kv(CB_DAV)]
    args = [q_rot, k_rot, proj]
    if has_cache:
        past = cache[0].shape[2]
        cspec = pl.BlockSpec((None, None, past, LANE), lambda b, h, i: (b, layer, 0, h))
        in_specs += [cspec, cspec]
        args += list(cache)
    in_specs += [pl.BlockSpec((4, DA_DK), lambda b, h, i: (0, 0)), pl.BlockSpec((1, LANE), lambda b, h, i: (0, 0))]
    args += [lam_p, sub_gain]
    return pl.pallas_call(
        kern,
        grid=(nseq, DA_H, nq),
        in_specs=in_specs,
        out_specs=pl.BlockSpec((tq, LANE), lambda b, h, i: (b * nq + i, h)),
        out_shape=jax.ShapeDtypeStruct((nseq * seq_len, MIX_W), BF16),
        scratch_shapes=[pltpu.VMEM((n_keys, LANE), BF16), pltpu.VMEM((n_keys, 2 * LANE), BF16)],
        compiler_params=_cp(("parallel", "parallel", "arbitrary")),
        name=f"diff_attention_{seq_len}",
    )(*args)


def _merge_kernel(h_ref, y0_ref, y1_ref, y2_ref, y3_ref, wg0_ref, wg1_ref, wg2_ref, wg3_ref, wb_ref, o_ref):
    h = h_ref[...]
    acc = None
    for bi, (y_ref, wg_ref) in enumerate(((y0_ref, wg0_ref), (y1_ref, wg1_ref), (y2_ref, wg2_ref), (y3_ref, wg3_ref))):
        gate = jax.nn.sigmoid(jnp.dot(h, wg_ref[...], preferred_element_type=F32))
        term = gate * jnp.dot(y_ref[...], wb_ref[bi], preferred_element_type=F32)
        acc = term if acc is None else acc + term
    o_ref[...] = acc.astype(BF16)


def branch_merge(h, ys, w_gate, w_branch):
    tm, tn = 1024, 256
    nj = D_MODEL // tn
    yspec = pl.BlockSpec((tm, MIX_W), lambda i, j: (i, 0))
    gspec = lambda bi: pl.BlockSpec((D_MODEL, tn), lambda i, j: (0, bi * nj + j))
    return pl.pallas_call(
        _merge_kernel,
        grid=(T_ALL // tm, nj),
        in_specs=[pl.BlockSpec((tm, D_MODEL), lambda i, j: (i, 0)), yspec, yspec, yspec, yspec,
                  gspec(0), gspec(1), gspec(2), gspec(3),
                  pl.BlockSpec((N_BRANCH, MIX_W, tn), lambda i, j: (0, 0, j))],
        out_specs=pl.BlockSpec((tm, tn), lambda i, j: (i, j)),
        out_shape=jax.ShapeDtypeStruct((T_ALL, D_MODEL), BF16),
        compiler_params=_cp(("parallel", "parallel")),
        name="branch_merge",
    )(h, *ys, w_gate, w_gate, w_gate, w_gate, w_branch)


def _outproj_kernel(m_ref, x_ref, g1_ref, w_ref, n2_ref, sh2_ref, sc2_ref, rt_ref, rl_ref, x1_ref, h2_ref, lg_ref):
    half = m_ref.shape[0] // 2
    for r in (0, half):
        rows = slice(r, r + half)
        y = jnp.dot(m_ref[rows, :], w_ref[...], preferred_element_type=F32)
        x1 = x_ref[rows, :] + g1_ref[...] * y
        x1_ref[rows, :] = x1
        h2 = _norm_mod(x1, n2_ref[...], sc2_ref[...], sh2_ref[...])
        h2_ref[rows, :] = _pack_bf16_pairs(h2)
        hi, lo = _split_bf16(h2)
        lg_ref[rows, :] = (jnp.dot(hi, rt_ref[...], preferred_element_type=F32)
                           + jnp.dot(lo, rt_ref[...], preferred_element_type=F32)
                           + jnp.dot(hi, rl_ref[...], preferred_element_type=F32))


def out_projection(merged, x, mod4, layer, w_out, norm2_g, router):
    tm = 256
    router_hi = router.astype(BF16)
    router_lo = (router - router_hi.astype(F32)).astype(BF16)
    return pl.pallas_call(
        _outproj_kernel,
        grid=(T_ALL // tm,),
        in_specs=[pl.BlockSpec((tm, D_MODEL), lambda i: (i, 0)),
                  pl.BlockSpec((tm, D_MODEL), lambda i: (i, 0)),
                  _mod_spec(layer, 2, tm, 1),
                  pl.BlockSpec((D_MODEL, D_MODEL), lambda i: (0, 0)),
                  pl.BlockSpec((None, 1, D_MODEL), lambda i: (layer, 0, 0)),
                  _mod_spec(layer, 3, tm, 1),
                  _mod_spec(layer, 4, tm, 1),
                  pl.BlockSpec((D_MODEL, LANE), lambda i: (0, 0)),
                  pl.BlockSpec((D_MODEL, LANE), lambda i: (0, 0))],
        out_specs=[pl.BlockSpec((tm, D_MODEL), lambda i: (i, 0)),
                   pl.BlockSpec((tm, D_MODEL // 2), lambda i: (i, 0)),
                   pl.BlockSpec((tm, LANE), lambda i: (i, 0))],
        out_shape=[jax.ShapeDtypeStruct((T_ALL, D_MODEL), F32),
                   jax.ShapeDtypeStruct((T_ALL, D_MODEL // 2), I32),
                   jax.ShapeDtypeStruct((T_ALL, LANE), F32)],
        compiler_params=_cp(("parallel",)),
        name="out_projection",
    )(merged, x, mod4, w_out, norm2_g, mod4, mod4, router_hi, router_lo)


MOE_TM = 256
N_ASSIGN = T_ALL * TOP_K
MOE_BLOCKS = (N_ASSIGN + N_EXPERTS * (MOE_TM - 1)) // MOE_TM + 1
ROUTER_TT = 512
GROUP_SIZE = N_EXPERTS // N_GROUPS


def _router_kernel(lg_ref, bias_ref, tri_ref, eidx_ref, w_ref, rank_ref, cnt_ref, carry_ref):
    i = pl.program_id(0)

    @pl.when(i == 0)
    def _():
        carry_ref[...] = jnp.zeros_like(carry_ref)

    tt = lg_ref.shape[0]
    shape3 = (N_GROUPS, GROUP_SIZE, tt)
    neg = -jnp.inf
    logits_t = lg_ref[...].T[:N_EXPERTS]
    scores = jax.nn.sigmoid(logits_t).reshape(shape3)
    sel = scores + bias_ref[...].reshape(N_GROUPS, GROUP_SIZE, 1)
    gi = lax.broadcasted_iota(I32, shape3, 0)
    ji = lax.broadcasted_iota(I32, shape3, 1)
    ei = gi * GROUP_SIZE + ji
    m1 = jnp.max(sel, axis=1, keepdims=True)
    first = jnp.min(jnp.where(sel == m1, ji, GROUP_SIZE), axis=1, keepdims=True)
    m2 = jnp.max(jnp.where(ji == first, neg, sel), axis=1, keepdims=True)
    cur = m1 + m2
    g1 = lax.broadcasted_iota(I32, cur.shape, 0)
    gmask = jnp.zeros(cur.shape, jnp.bool_)
    for _ in range(TOPK_GROUPS):
        mx = jnp.max(cur, axis=0, keepdims=True)
        idx = jnp.min(jnp.where(cur == mx, g1, N_GROUPS), axis=0, keepdims=True)
        hit = g1 == idx
        gmask = jnp.logical_or(gmask, hit)
        cur = jnp.where(hit, neg, cur)
    masked = jnp.where(gmask, sel, neg)

    def all_max(a):
        return jnp.max(jnp.max(a, axis=1, keepdims=True), axis=0, keepdims=True)

    def all_min(a):
        return jnp.min(jnp.min(a, axis=1, keepdims=True), axis=0, keepdims=True)

    def all_sum(a):
        return jnp.sum(jnp.sum(a, axis=1, keepdims=True), axis=0, keepdims=True)

    hits, idxs, ws = [], [], []
    for _ in range(TOP_K):
        mx = all_max(masked)
        idx = all_min(jnp.where(masked == mx, ei, N_EXPERTS))
        hit = ei == idx
        hits.append(hit)
        idxs.append(idx)
        ws.append(all_sum(jnp.where(hit, scores, 0.0)))
        masked = jnp.where(hit, neg, masked)
    wsum = ws[0]
    for wk in ws[1:]:
        wsum = wsum + wk
    onehot = jnp.zeros(shape3, F32)
    for hit in hits:
        onehot = onehot + hit.astype(F32)
    oh2 = onehot.reshape(N_EXPERTS, tt)
    before = jnp.dot(oh2.astype(BF16), tri_ref[...], preferred_element_type=F32) + carry_ref[:, 0:1]
    before3 = before.reshape(shape3)
    for k in range(TOP_K):
        eidx_ref[k:k + 1, :] = idxs[k].reshape(1, tt)
        w_ref[k:k + 1, :] = (ws[k] / wsum * ROUTED_SCALE).reshape(1, tt)
        rank_ref[k:k + 1, :] = all_sum(jnp.where(hits[k], before3, 0.0)).reshape(1, tt).astype(I32)
    eidx_ref[TOP_K:, :] = jnp.zeros((8 - TOP_K, tt), I32)
    w_ref[TOP_K:, :] = jnp.zeros((8 - TOP_K, tt), F32)
    rank_ref[TOP_K:, :] = jnp.zeros((8 - TOP_K, tt), I32)
    carry_ref[...] = carry_ref[...] + jnp.sum(oh2, axis=1, keepdims=True)
    cnt_ref[...] = carry_ref[...]


def moe_route(logits_t, bias):
    tt = ROUTER_TT
    tri = jnp.asarray(np.triu(np.ones((tt, tt), np.float32), 1), BF16)
    row8 = pl.BlockSpec((8, tt), lambda i: (0, i))
    return pl.pallas_call(
        _router_kernel,
        grid=(T_ALL // tt,),
        in_specs=[pl.BlockSpec((tt, LANE), lambda i: (i, 0)),
                  pl.BlockSpec((N_EXPERTS, 1), lambda i: (0, 0)),
                  pl.BlockSpec((tt, tt), lambda i: (0, 0))],
        out_specs=[row8, row8, row8, pl.BlockSpec((N_EXPERTS, LANE), lambda i: (0, 0))],
        out_shape=[jax.ShapeDtypeStruct((8, T_ALL), I32), jax.ShapeDtypeStruct((8, T_ALL), F32),
                   jax.ShapeDtypeStruct((8, T_ALL), I32), jax.ShapeDtypeStruct((N_EXPERTS, LANE), F32)],
        scratch_shapes=[pltpu.VMEM((N_EXPERTS, LANE), F32)],
        compiler_params=_cp(("arbitrary",)),
        name="moe_route",
    )(logits_t, bias, tri)


N_SLOTS = MOE_BLOCKS * MOE_TM
D_PACK = D_MODEL // 2


def _pack_bf16_pairs(x):
    n = x.shape[1] // 2
    hi = pltpu.bitcast(x[:, :n].astype(BF16).astype(F32), I32)
    lo = pltpu.bitcast(x[:, n:].astype(BF16).astype(F32), I32)
    return hi | lax.shift_right_logical(lo, 16)


def _unpack_bf16_pairs(u):
    hi = pltpu.bitcast(u & jnp.int32(-65536), F32)
    lo = pltpu.bitcast(lax.shift_left(u, 16), F32)
    return jnp.concatenate([hi, lo], axis=1)


def _dispatch_kernel(slots_ref, h2p_ref, xs_hbm, sem, *, tm):
    i = pl.program_id(0)

    def body(r, _):
        for k in range(TOP_K):
            s = slots_ref[(i * tm + r) * TOP_K + k]
            pltpu.make_async_copy(h2p_ref.at[pl.ds(r, 1), :], xs_hbm.at[pl.ds(s, 1), :], sem).start()
        return 0

    lax.fori_loop(0, tm, body, 0)
    for k in range(TOP_K):
        pltpu.make_async_copy(h2p_ref, xs_hbm.at[pl.ds(0, tm), :], sem).wait()


def moe_dispatch(slots, h2p):
    tm = 256
    return pl.pallas_call(
        functools.partial(_dispatch_kernel, tm=tm),
        grid_spec=pltpu.PrefetchScalarGridSpec(
            num_scalar_prefetch=1,
            grid=(T_ALL // tm,),
            in_specs=[pl.BlockSpec((tm, D_PACK), lambda i, s: (i, 0))],
            out_specs=pl.BlockSpec(memory_space=pl.ANY),
            scratch_shapes=[pltpu.SemaphoreType.DMA(())]),
        out_shape=jax.ShapeDtypeStruct((N_SLOTS, D_PACK), I32),
        compiler_params=_cp(("arbitrary",), unchecked_dma=True),
        name="moe_dispatch",
    )(slots, h2p)


def _gmm_kernel(be_ref, bv_ref, first_ref, par_ref, nxt_ref, x_ref, wg_hbm, wu_hbm, wd_hbm, y_ref,
                wgf, wuf, wdf, wgb, wub, wdb, wsem, *, layer):
    i = pl.program_id(0)

    def fetch(e, slot):
        return (pltpu.make_async_copy(wg_hbm.at[layer, e], wgf.at[slot], wsem.at[slot]),
                pltpu.make_async_copy(wu_hbm.at[layer, e], wuf.at[slot], wsem.at[slot]),
                pltpu.make_async_copy(wd_hbm.at[layer, e], wdf.at[slot], wsem.at[slot]))

    @pl.when(i == 0)
    def _():
        for cp in fetch(be_ref[0], 0):
            cp.start()

    @pl.when(first_ref[i] == 1)
    def _():
        slot = par_ref[i]
        for cp in fetch(be_ref[i], slot):
            cp.wait()
        nxt = nxt_ref[i]

        @pl.when(nxt >= 0)
        def _():
            for cp in fetch(nxt, 1 - slot):
                cp.start()

        step = 512
        for r in range(0, D_MODEL, step):
            wgb[r:r + step, :] = wgf[slot, r:r + step, :].astype(BF16)
            wub[r:r + step, :] = wuf[slot, r:r + step, :].astype(BF16)
        for r in range(0, D_EXPERT, 128):
            wdb[r:r + 128, :] = wdf[slot, r:r + 128, :].astype(BF16)

    nv = bv_ref[i]

    @pl.when(nv == 0)
    def _():
        y_ref[...] = jnp.zeros_like(y_ref)

    @pl.when(nv > 0)
    def _():
        row = lax.broadcasted_iota(I32, (x_ref.shape[0], 1), 0)
        xin = jnp.where(row < nv, x_ref[...], 0)
        x = _unpack_bf16_pairs(xin).astype(BF16)
        g = jnp.dot(x, wgb[...], preferred_element_type=F32)
        u = jnp.dot(x, wub[...], preferred_element_type=F32)
        a = (g * jax.nn.sigmoid(g) * u).astype(BF16)
        y_ref[...] = _pack_bf16_pairs(jnp.dot(a, wdb[...], preferred_element_type=F32))


def moe_experts(block_e, block_valid, block_first, block_par, block_next, xs, w_gate, w_up, w_down, layer):
    tm = MOE_TM
    row = pl.BlockSpec((tm, D_PACK), lambda i, *_: (i, 0))
    anyspace = pl.BlockSpec(memory_space=pl.ANY)
    return pl.pallas_call(
        functools.partial(_gmm_kernel, layer=layer),
        grid_spec=pltpu.PrefetchScalarGridSpec(
            num_scalar_prefetch=5,
            grid=(MOE_BLOCKS,),
            in_specs=[row, anyspace, anyspace, anyspace],
            out_specs=row,
            scratch_shapes=[pltpu.VMEM((2, D_MODEL, D_EXPERT), F32), pltpu.VMEM((2, D_MODEL, D_EXPERT), F32),
                            pltpu.VMEM((2, D_EXPERT, D_MODEL), F32),
                            pltpu.VMEM((D_MODEL, D_EXPERT), BF16), pltpu.VMEM((D_MODEL, D_EXPERT), BF16),
                            pltpu.VMEM((D_EXPERT, D_MODEL), BF16), pltpu.SemaphoreType.DMA((2,))]),
        out_shape=jax.ShapeDtypeStruct((N_SLOTS, D_PACK), I32),
        compiler_params=_cp(("arbitrary",)),
        name="moe_experts",
    )(block_e, block_valid, block_first, block_par, block_next, xs, w_gate, w_up, w_down)


def _combine_kernel(slots_ref, x1_ref, h2p_ref, w_ref, g2_ref, sg_ref, su_ref, sd_ref, ys_hbm, out_ref, buf, sem,
                    *, tm, nsteps):
    i = pl.program_id(0)

    def issue(step, slot):
        def body(r, _):
            for k in range(TOP_K):
                s = slots_ref[(step * tm + r) * TOP_K + k]
                pltpu.make_async_copy(ys_hbm.at[pl.ds(s, 1), :], buf.at[slot, pl.ds(k * tm + r, 1), :],
                                      sem.at[slot]).start()
            return 0

        lax.fori_loop(0, tm, body, 0)

    @pl.when(i == 0)
    def _():
        issue(0, 0)

    @pl.when(i + 1 < nsteps)
    def _():
        issue(i + 1, (i + 1) % 2)

    slot = i % 2
    pltpu.make_async_copy(ys_hbm.at[pl.ds(0, TOP_K * tm), :], buf.at[slot], sem.at[slot]).wait()
    hb = _unpack_bf16_pairs(h2p_ref[...]).astype(BF16)
    g = jnp.dot(hb, sg_ref[...], preferred_element_type=F32)
    u = jnp.dot(hb, su_ref[...], preferred_element_type=F32)
    acc = jnp.dot((g * jax.nn.sigmoid(g) * u).astype(BF16), sd_ref[...], preferred_element_type=F32)
    w = w_ref[...]
    for k in range(TOP_K):
        acc = acc + w[:, k:k + 1] * _unpack_bf16_pairs(buf[slot, k * tm:(k + 1) * tm, :])
    out_ref[...] = x1_ref[...] + g2_ref[...] * acc


def moe_combine(slots, x1, h2p, w_tok, mod4, layer, sg, su, sd, ys):
    tm = 256
    nsteps = T_ALL // tm
    kern = functools.partial(_combine_kernel, tm=tm, nsteps=nsteps)
    row = lambda n: pl.BlockSpec((tm, n), lambda i, s: (i, 0))
    return pl.pallas_call(
        kern,
        grid_spec=pltpu.PrefetchScalarGridSpec(
            num_scalar_prefetch=1,
            grid=(nsteps,),
            in_specs=[row(D_MODEL), row(D_PACK), row(8),
                      pl.BlockSpec((None, None, 1, D_MODEL),
                                   lambda i, s: (layer, _mod_row(i, tm) * 6 + 5, 0, 0)),
                      pl.BlockSpec((D_MODEL, D_EXPERT), lambda i, s: (0, 0)),
                      pl.BlockSpec((D_MODEL, D_EXPERT), lambda i, s: (0, 0)),
                      pl.BlockSpec((D_EXPERT, D_MODEL), lambda i, s: (0, 0)),
                      pl.BlockSpec(memory_space=pl.ANY)],
            out_specs=row(D_MODEL),
            scratch_shapes=[pltpu.VMEM((2, TOP_K * tm, D_PACK), I32), pltpu.SemaphoreType.DMA((2,))]),
        out_shape=jax.ShapeDtypeStruct((T_ALL, D_MODEL), F32),
        compiler_params=_cp(("arbitrary",), unchecked_dma=True),
        name="moe_combine",
    )(slots, x1, h2p, w_tok, mod4, sg, su, sd, ys)


def moe_ffn(x1, h2p, logits_t, mod4, layer, moe_bias, w_gate, w_up, w_down, sg, su, sd):
    eidx, w_t, rank, cnt = moe_route(logits_t, moe_bias)
    counts = cnt[:, 0].astype(I32)
    padded = (counts + MOE_TM - 1) // MOE_TM * MOE_TM
    pad_end = jnp.cumsum(padded)
    pad_start = pad_end - padded
    experts = jnp.arange(N_EXPERTS, dtype=I32)
    base = jnp.sum(jnp.where(eidx[:TOP_K, :, None] == experts, pad_start, 0), axis=-1)
    slots = (base + rank[:TOP_K]).T.reshape(-1)
    starts = jnp.arange(MOE_BLOCKS, dtype=I32) * MOE_TM
    block_e = jnp.minimum(jnp.sum((pad_end[None, :] <= starts[:, None]).astype(I32), axis=1), N_EXPERTS - 1)
    mine = block_e[:, None] == experts
    left = jnp.sum(jnp.where(mine, counts + pad_start, 0), axis=1) - starts
    block_valid = jnp.clip(left, 0, MOE_TM).astype(I32)
    nonempty = counts > 0
    pick = lambda v: jnp.sum(jnp.where(mine, v, 0), axis=1)
    block_first = jnp.logical_and(starts == pick(pad_start), block_valid > 0).astype(I32)
    block_par = pick(jnp.cumsum(nonempty.astype(I32)) - 1) % 2
    later = lax.cummin(jnp.where(nonempty, experts, N_EXPERTS), axis=0, reverse=True)
    nxt_e = jnp.concatenate([later[1:], jnp.full((1,), N_EXPERTS, I32)])
    block_next = pick(jnp.where(nxt_e >= N_EXPERTS, -1, nxt_e))
    xs = moe_dispatch(slots, h2p)
    ys = moe_experts(block_e, block_valid, block_first, block_par, block_next, xs, w_gate, w_up, w_down, layer)
    return moe_combine(slots, x1, h2p, w_t.T, mod4, layer, sg, su, sd, ys)


def _ml_gate_layouts(gp, gate_bias):
    nt = gp.shape[0]
    g16 = gp[:, :4 * ML_H].reshape(nt, 2, 2, ML_H)
    gr = jnp.transpose(g16, (3, 0, 1, 2)).reshape(ML_H, nt, 4)
    gc = jnp.transpose(gr, (0, 2, 1))
    b = jnp.transpose(gate_bias, (2, 0, 1)).reshape(ML_H, 4)
    return gr, gc, b[:, None, :], b[:, :, None]


def kernel(x_prompt, x_sample, c, cache_k, cache_v, state_mlstm_C, state_mlstm_n, state_mlstm_m, state_rglru,
           c_ctx, w_mod, b_mod, norm1, norm2, w_in, hy_short, hy_w1, hy_b1, hy_w2, hy_b2, hy_w3, hy_freq,
           hy_decay, hy_skip, ml_gate_bias, ml_out_norm, lru_conv_w, lru_conv_b, lru_wa, lru_ba, lru_wx, lru_bx,
           lru_lambda, da_q_norm, da_k_norm, da_lambda, da_sub_norm, w_branch, w_out, moe_router, moe_bias,
           moe_w_gate, moe_w_up, moe_w_down, sh_w_gate, sh_w_up, sh_w_down):
    x = jnp.concatenate([x_prompt.reshape(T_CTX, D_MODEL), x_sample.reshape(T_DEN, D_MODEL)], axis=0)
    cvecs = jnp.concatenate([c_ctx[None], c, jnp.zeros((N_MOD_ROWS - 1 - DEC_BATCH, D_MODEL), F32)], axis=0)
    mod4 = modulation_all(cvecs, w_mod, b_mod).reshape(DEPTH, N_MOD_ROWS * 6, 1, D_MODEL)
    norm1_3 = norm1.reshape(DEPTH, 1, D_MODEL)
    norm2_3 = norm2.reshape(DEPTH, 1, D_MODEL)
    cache_k4 = cache_k.reshape(DEC_BATCH, DEPTH, -1, MIX_W)
    cache_v4 = cache_v.reshape(DEC_BATCH, DEPTH, -1, MIX_W)
    den_off = T_CTX // DEC_SEQ
    new_k, new_v, new_c, new_n, new_m, new_h = [], [], [], [], [], []
    for l in range(DEPTH):
        lam_init = 0.8 - 0.6 * math.exp(-0.3 * l)
        wl = w_in[l]
        n_ml = 4 * ML_H
        w_main = jnp.concatenate([wl[:, :7 * MIX_W], wl[:, 7 * MIX_W + n_ml:12 * MIX_W + n_ml]], axis=1).astype(BF16)
        w_g16 = jnp.pad(wl[:, 7 * MIX_W:7 * MIX_W + n_ml], ((0, 0), (0, LANE - n_ml))).astype(BF16)
        w_gate = wl[:, 12 * MIX_W + n_ml:].astype(BF16)
        proj, h, gp = in_projection(x, norm1_3, mod4, l, w_main, w_g16)

        w1p = jnp.pad(hy_w1[l], ((0, LANE - HY_PE), (0, 0)))
        ws3 = jnp.transpose(hy_short[l].reshape(3, 3, MIX_W), (1, 0, 2))
        y_hy = []
        for off, nseq, sl in ((0, BATCH, SEQ), (den_off, DEC_BATCH, DEC_SEQ)):
            kfilt = hyena_filter(sl, w1p, hy_b1[l][None], hy_w2[l], hy_b2[l][None], hy_freq[l][None], hy_w3[l],
                                 hy_decay[l][None])
            if sl <= 512:
                y_hy.append(hyena_conv_direct(proj, off, nseq, sl, ws3, hyena_filter_fft_direct(sl, kfilt), hy_skip[l]))
            else:
                y_hy.append(hyena_conv(proj, off, nseq, sl, ws3, hyena_filter_fft(sl, kfilt), hy_skip[l]))

        gr, gc, b_r, b_c = _ml_gate_layouts(gp, ml_gate_bias[l])
        gain = ml_out_norm[l][None]
        y_ml0, c_new, n_new, m_new = mlstm_mixer(proj, gr, gc, b_r, b_c, gain, 0, BATCH, SEQ, None)
        st = (state_mlstm_C[:, l], state_mlstm_n[:, l][:, :, :, None, :],
              jnp.broadcast_to(state_mlstm_m[:, l][:, :, :, None, None], (DEC_BATCH, 2, ML_H, 1, LANE)))
        y_ml1 = mlstm_mixer(proj, gr, gc, b_r, b_c, gain, den_off, DEC_BATCH, DEC_SEQ, st)[0]

        lru_args = (lru_conv_w[l], lru_conv_b[l][None], lru_wa[l], lru_ba[l][:, None, :], lru_wx[l],
                    lru_bx[l][:, None, :], lru_lambda[l][:, None, :])
        y_lr0, h_new = rglru_mixer(proj, *lru_args, 0, BATCH, SEQ, None)
        y_lr1 = rglru_mixer(proj, *lru_args, den_off, DEC_BATCH, DEC_SEQ, state_rglru[:, l][:, :, None, :])[0]

        q_rot, k_rot, k_norm = qk_prepare(proj, jnp.tile(da_q_norm[l], 2)[None], jnp.tile(da_k_norm[l], 2)[None])
        sub_gain = da_sub_norm[l][None]
        y_da0 = diff_attention(q_rot, k_rot, proj, da_lambda[l], sub_gain, lam_init, 0, BATCH, SEQ, None, l)
        y_da1 = diff_attention(q_rot, k_rot, proj, da_lambda[l], sub_gain, lam_init, den_off, DEC_BATCH, DEC_SEQ,
                               (cache_k4, cache_v4), l)

        ys = [jnp.concatenate(p, axis=0) for p in (y_hy, (y_ml0, y_ml1), (y_lr0, y_lr1), (y_da0, y_da1))]
        merged = branch_merge(h, ys, w_gate, w_branch[l].astype(BF16))
        x1, h2, logits_t = out_projection(merged, x, mod4, l, w_out[l].astype(BF16), norm2_3,
                                          jnp.pad(moe_router[l], ((0, 0), (0, LANE - N_EXPERTS))))
        x = moe_ffn(x1, h2, logits_t, mod4, l, moe_bias[l][:, None], moe_w_gate, moe_w_up, moe_w_down,
                    sh_w_gate[l].astype(BF16), sh_w_up[l].astype(BF16), sh_w_down[l].astype(BF16))

        new_k.append(k_norm[:T_CTX].reshape(BATCH, SEQ, DA_H, 2, DA_DK))
        new_v.append(proj[:T_CTX, CB_DAV * LANE:(CB_DAV + DA_H) * LANE].astype(F32).reshape(BATCH, SEQ, DA_H, -1))
        new_c.append(c_new)
        new_n.append(n_new[:, :, :, 0, :])
        new_m.append(m_new[:, :, :, 0, 0])
        new_h.append(h_new[:, :, 0, :])
    y_prompt = x[:T_CTX].reshape(BATCH, SEQ, D_MODEL)
    y_sample = x[T_CTX:].reshape(DEC_BATCH, DEC_SEQ, D_MODEL)
    stack = lambda parts: jnp.stack(parts, axis=1)
    return (y_prompt, y_sample, stack(new_k), stack(new_v), stack(new_c), stack(new_n), stack(new_m), stack(new_h))
```

```python
import functools
import math

import numpy as np
import jax
import jax.numpy as jnp
from jax import lax
from jax.experimental import pallas as pl
from jax.experimental.pallas import tpu as pltpu

F32 = jnp.float32
BF16 = jnp.bfloat16
I32 = jnp.int32
HIGHEST = lax.Precision.HIGHEST

D_MODEL = 2048
BATCH = 16
SEQ = 256
DEPTH = 2
DEC_BATCH = 4
DEC_SEQ = 4096
GRID_W = 64
N_BRANCH = 4
MIX_W = 512
HY_BANDS = 16
HY_PE = 1 + 2 * HY_BANDS
HY_FFN = 64
ML_H = 4
ML_DK = 128
LRU_H = 4
LRU_C = 8.0
DA_H = 4
DA_DK = 64
ROPE_BASE = 10000.0
N_EXPERTS = 64
TOP_K = 6
N_GROUPS = 8
TOPK_GROUPS = 4
D_EXPERT = 512
ROUTED_SCALE = 2.5
EPS = 1e-6

LANE = 128
T_CTX = BATCH * SEQ
T_DEN = DEC_BATCH * DEC_SEQ
T_ALL = T_CTX + T_DEN
N_MOD_ROWS = 8
VMEM_LIMIT = 56 * 1024 * 1024

CB_HY = 0
CB_MLQ, CB_MLK, CB_MLV, CB_MLO = 12, 16, 20, 24
CB_LRX, CB_LRY = 28, 32
CB_DAQ, CB_DAK, CB_DAV = 36, 40, 44
N_MAIN = 6144


def _cp(sem, vmem=VMEM_LIMIT, unchecked_dma=False):
    return pltpu.CompilerParams(dimension_semantics=sem, vmem_limit_bytes=vmem, disable_bounds_checks=unchecked_dma)


def _mod_row(i, rows_per_block):
    nctx = T_CTX // rows_per_block
    per = DEC_SEQ // rows_per_block
    return jnp.where(i < nctx, 0, 1 + (i - nctx) // per)


def _mod_spec(layer, seg, rows_per_block, ngrid):
    if ngrid == 1:
        return pl.BlockSpec((None, None, 1, D_MODEL), lambda i: (layer, _mod_row(i, rows_per_block) * 6 + seg, 0, 0))
    return pl.BlockSpec((None, None, 1, D_MODEL), lambda i, j: (layer, _mod_row(i, rows_per_block) * 6 + seg, 0, 0))


def _mod_kernel(c_ref, w_ref, b_ref, o_ref):
    c = c_ref[...]
    a = (c * jax.nn.sigmoid(c)).astype(BF16)
    o_ref[...] = jnp.dot(a, w_ref[...].astype(BF16), preferred_element_type=F32) + b_ref[...]


def modulation_all(cvecs, w_mod, b_mod):
    tn = 1024
    n = 6 * D_MODEL
    return pl.pallas_call(
        _mod_kernel,
        grid=(DEPTH, n // tn),
        in_specs=[pl.BlockSpec((N_MOD_ROWS, D_MODEL), lambda l, j: (0, 0)),
                  pl.BlockSpec((None, D_MODEL, tn), lambda l, j: (l, 0, j)),
                  pl.BlockSpec((None, 1, tn), lambda l, j: (l, 0, j))],
        out_specs=pl.BlockSpec((None, N_MOD_ROWS, tn), lambda l, j: (l, 0, j)),
        out_shape=jax.ShapeDtypeStruct((DEPTH, N_MOD_ROWS, n), F32),
        compiler_params=_cp(("parallel", "parallel")),
        name="modulation",
    )(cvecs, w_mod, b_mod.reshape(DEPTH, 1, n))


def _norm_mod(x, g, sc, sh):
    ms = jnp.mean(x * x, axis=-1, keepdims=True)
    return (x * lax.rsqrt(ms + EPS) * g) * (1.0 + sc) + sh


def _inproj_kernel(x_ref, g_ref, sh_ref, sc_ref, w_ref, wg_ref, proj_ref, h_ref, gp_ref, *, tm, sub):
    j = pl.program_id(1)

    @pl.when(j == 0)
    def _():
        g = g_ref[...]
        sc = sc_ref[...]
        sh = sh_ref[...]

        def body(r, _):
            rows = pl.ds(pl.multiple_of(r * sub, sub), sub)
            hb = _norm_mod(x_ref[rows, :], g, sc, sh).astype(BF16)
            h_ref[rows, :] = hb
            gp_ref[rows, :] = jnp.dot(hb, wg_ref[...], preferred_element_type=F32)
            return 0

        lax.fori_loop(0, tm // sub, body, 0)

    proj_ref[...] = jnp.dot(h_ref[...], w_ref[...], preferred_element_type=F32).astype(BF16)


def in_projection(x, norm_g, mod4, layer, w_main, w_gate16):
    tm, tn = 1024, 1024
    kern = functools.partial(_inproj_kernel, tm=tm, sub=256)
    return pl.pallas_call(
        kern,
        grid=(T_ALL // tm, N_MAIN // tn),
        in_specs=[pl.BlockSpec((tm, D_MODEL), lambda i, j: (i, 0)),
                  pl.BlockSpec((None, 1, D_MODEL), lambda i, j: (layer, 0, 0)),
                  _mod_spec(layer, 0, tm, 2),
                  _mod_spec(layer, 1, tm, 2),
                  pl.BlockSpec((D_MODEL, tn), lambda i, j: (0, j)),
                  pl.BlockSpec((D_MODEL, LANE), lambda i, j: (0, 0))],
        out_specs=[pl.BlockSpec((tm, tn), lambda i, j: (i, j)),
                   pl.BlockSpec((tm, D_MODEL), lambda i, j: (i, 0)),
                   pl.BlockSpec((tm, LANE), lambda i, j: (i, 0))],
        out_shape=[jax.ShapeDtypeStruct((T_ALL, N_MAIN), BF16),
                   jax.ShapeDtypeStruct((T_ALL, D_MODEL), BF16),
                   jax.ShapeDtypeStruct((T_ALL, LANE), F32)],
        compiler_params=_cp(("parallel", "arbitrary")),
        name="in_projection",
    )(x, norm_g, mod4, mod4, w_main, w_gate16)


def _fft_factors(seq_len):
    n2 = 64
    return 2 * seq_len // n2, n2


def _fft_pages(seq_len):
    n1, _ = _fft_factors(seq_len)
    used = n1 // 2 + 1
    return used, -(-used // 8) * 8


@functools.lru_cache(maxsize=None)
def _dft_tables(seq_len):
    n1, n2 = _fft_factors(seq_len)
    used, npg = _fft_pages(seq_len)
    n = n1 * n2
    th1 = 2.0 * np.pi * np.outer(np.arange(npg, dtype=np.float64), np.arange(n1, dtype=np.float64)) / n1
    w1 = np.concatenate([np.cos(th1), -np.sin(th1)], axis=0)
    w1h = w1[:, : n1 // 2]
    wt = np.where(np.arange(npg) < used, 2.0, 0.0)
    wt[0] = 1.0
    wt[n1 // 2] = 1.0
    th4 = th1[:, : n1 // 2].T
    w4 = np.concatenate([np.cos(th4) * wt, -np.sin(th4) * wt], axis=1) / n
    k1 = np.arange(npg, dtype=np.float64)[:, None, None]
    k2 = np.arange(n2, dtype=np.float64)[None, :, None]
    m2 = np.arange(n2, dtype=np.float64)[None, None, :]
    ph = -2.0 * np.pi * (m2 * k1 / n + m2 * k2 / n2)
    gr, gi = np.cos(ph), np.sin(ph)
    g = np.concatenate([np.concatenate([gr, -gi], axis=2), np.concatenate([gi, gr], axis=2)], axis=1)
    gt = np.transpose(g, (0, 2, 1))
    return tuple(np.asarray(t, np.float32) for t in (w1, w1h, g, gt, w4))


@functools.lru_cache(maxsize=None)
def _direct_dft_tables(seq_len):
    n = 2 * seq_len
    th = 2.0 * np.pi * np.outer(np.arange(n, dtype=np.float64), np.arange(n, dtype=np.float64)) / n
    wf = np.concatenate([np.cos(th), -np.sin(th)], axis=0)
    wi = np.concatenate([np.cos(th), -np.sin(th)], axis=1)[:seq_len] / n
    return np.asarray(wf, np.float32), np.asarray(wi, np.float32)


def _dft_tables_bf16(seq_len):
    return tuple(jnp.asarray(t, BF16) for t in _dft_tables(seq_len))


@functools.lru_cache(maxsize=None)
def _filter_positions(seq_len):
    pos = np.concatenate([np.arange(seq_len), [0], np.arange(seq_len - 1, 0, -1)]).astype(np.float64)
    tn = pos / seq_len
    bands = np.linspace(1e-4, HY_BANDS - 1, HY_BANDS)
    ang = (2.0 * math.pi / seq_len) * pos[:, None] * bands
    pe = np.zeros((2 * seq_len, LANE), np.float64)
    pe[:, 0] = tn
    pe[:, 1:1 + HY_BANDS] = np.cos(ang)
    pe[:, 1 + HY_BANDS:HY_PE] = np.sin(ang)
    return np.asarray(pe, np.float32), np.asarray(tn[:, None], np.float32)


def _hy_mlp_kernel(pe_ref, w1_ref, b1_ref, w2_ref, b2_ref, fr_ref, z_ref):
    fr = fr_ref[...]
    z = jnp.sin(fr * (jnp.dot(pe_ref[...], w1_ref[...], precision=HIGHEST, preferred_element_type=F32) + b1_ref[...]))
    z_ref[...] = jnp.sin(fr * (jnp.dot(z, w2_ref[...], precision=HIGHEST, preferred_element_type=F32) + b2_ref[...]))


def _hy_filter_kernel(z_ref, tn_ref, w3a_ref, w3b_ref, dca_ref, dcb_ref, k_ref, *, seq_len):
    def taps(rows, w3_ref, dc_ref):
        f = jnp.dot(z_ref[rows, :].astype(BF16), w3_ref[...].astype(BF16), preferred_element_type=F32)
        return f * jnp.exp(-tn_ref[rows, :] * jnp.abs(dc_ref[...]))

    ka = taps(pl.ds(0, seq_len), w3a_ref, dca_ref)
    kb = taps(pl.ds(seq_len, seq_len), w3b_ref, dcb_ref)
    row = lax.broadcasted_iota(I32, ka.shape, 0)
    ka = ka + jnp.where(row == 0, taps(pl.ds(0, 8), w3b_ref, dcb_ref)[0:1], 0.0)
    kb = jnp.where(row == 0, 0.0, kb)
    inv = 1.0 / (jnp.sum(jnp.abs(ka), axis=0, keepdims=True) + jnp.sum(jnp.abs(kb), axis=0, keepdims=True))
    k_ref[0:seq_len, :] = ka * inv
    k_ref[seq_len:, :] = kb * inv


def hyena_filter(seq_len, w1p, b1, w2, b2, freq, w3, decay):
    pe, tn = (jnp.asarray(t) for t in _filter_positions(seq_len))
    n = 2 * seq_len
    tr = min(n, 1024)
    full = lambda shape: pl.BlockSpec(shape, lambda j: (0,) * len(shape))
    z = pl.pallas_call(
        _hy_mlp_kernel,
        grid=(n // tr,),
        in_specs=[pl.BlockSpec((tr, LANE), lambda j: (j, 0)), full((LANE, HY_FFN)), full((1, HY_FFN)),
                  full((HY_FFN, HY_FFN)), full((1, HY_FFN)), full((1, HY_FFN))],
        out_specs=pl.BlockSpec((tr, HY_FFN), lambda j: (j, 0)),
        out_shape=jax.ShapeDtypeStruct((n, HY_FFN), F32),
        compiler_params=_cp(("parallel",)),
        name=f"hyena_filter_mlp_{seq_len}",
    )(pe, w1p, b1, w2, b2, freq)
    tc = 128
    nc = (2 * MIX_W) // tc
    kern = functools.partial(_hy_filter_kernel, seq_len=seq_len)
    return pl.pallas_call(
        kern,
        grid=(nc,),
        in_specs=[full((n, HY_FFN)), full((n, 1)),
                  pl.BlockSpec((HY_FFN, tc), lambda j: (0, j)),
                  pl.BlockSpec((HY_FFN, tc), lambda j: (0, nc + j)),
                  pl.BlockSpec((1, tc), lambda j: (0, j)),
                  pl.BlockSpec((1, tc), lambda j: (0, nc + j))],
        out_specs=pl.BlockSpec((n, tc), lambda j: (0, j)),
        out_shape=jax.ShapeDtypeStruct((n, 2 * MIX_W), F32),
        compiler_params=_cp(("parallel",)),
        name=f"hyena_filter_{seq_len}",
    )(z, tn, w3, w3, decay, decay)


def _split_bf16(x):
    hi = x.astype(BF16)
    lo = (x - hi.astype(F32)).astype(BF16)
    return hi, lo


def _fft_stage1(load_rows, w1, a_ref, npg, n2, split):
    def body(m, _):
        xs = load_rows(m)
        if split:
            hi, lo = _split_bf16(xs)
            r = jnp.dot(w1, hi, preferred_element_type=F32) + jnp.dot(w1, lo, preferred_element_type=F32)
        else:
            r = jnp.dot(w1, xs.astype(BF16), preferred_element_type=F32)
        a_ref[pl.ds(m, npg, stride=2 * n2), :] = r[:npg]
        a_ref[pl.ds(n2 + m, npg, stride=2 * n2), :] = r[npg:]
        return 0

    lax.fori_loop(0, n2, body, 0, unroll=2)


def _hy_fft_kernel(k_ref, w1_ref, g_ref, kf_ref, a_ref, *, n1, n2, npg):
    w1 = w1_ref[...]
    _fft_stage1(lambda m: k_ref[pl.ds(m, n1, stride=n2), :], w1, a_ref, npg, n2, True)

    def body(p, _):
        rows = pl.ds(pl.multiple_of(p * 2 * n2, 2 * n2), 2 * n2)
        hi, lo = _split_bf16(a_ref[rows, :])
        g = g_ref[p]
        kf_ref[rows, :] = jnp.dot(g, hi, preferred_element_type=F32) + jnp.dot(g, lo, preferred_element_type=F32)
        return 0

    lax.fori_loop(0, npg, body, 0, unroll=4)


def hyena_filter_fft(seq_len, kfilt):
    n1, n2 = _fft_factors(seq_len)
    _, npg = _fft_pages(seq_len)
    w1, _, g, _, _ = _dft_tables_bf16(seq_len)
    tc = 128
    kern = functools.partial(_hy_fft_kernel, n1=n1, n2=n2, npg=npg)
    return pl.pallas_call(
        kern,
        grid=((2 * MIX_W) // tc,),
        in_specs=[pl.BlockSpec((n1 * n2, tc), lambda j: (0, j)),
                  pl.BlockSpec((2 * npg, n1), lambda j: (0, 0)),
                  pl.BlockSpec((npg, 2 * n2, 2 * n2), lambda j: (0, 0, 0))],
        out_specs=pl.BlockSpec((npg * 2 * n2, tc), lambda j: (0, j)),
        out_shape=jax.ShapeDtypeStruct((npg * 2 * n2, 2 * MIX_W), F32),
        scratch_shapes=[pltpu.VMEM((npg * 2 * n2, tc), F32)],
        compiler_params=_cp(("parallel",)),
        name=f"hyena_filter_fft_{seq_len}",
    )(kfilt, w1, g)


def _hy_fft_direct_kernel(k_ref, wf_ref, kf_ref):
    hi, lo = _split_bf16(k_ref[...])
    wf = wf_ref[...]
    kf_ref[...] = jnp.dot(wf, hi, preferred_element_type=F32) + jnp.dot(wf, lo, preferred_element_type=F32)


def hyena_filter_fft_direct(seq_len, kfilt):
    n = 2 * seq_len
    wf = jnp.asarray(_direct_dft_tables(seq_len)[0], BF16)
    tc = 256
    return pl.pallas_call(
        _hy_fft_direct_kernel,
        grid=((2 * MIX_W) // tc,),
        in_specs=[pl.BlockSpec((n, tc), lambda j: (0, j)), pl.BlockSpec((2 * n, n), lambda j: (0, 0))],
        out_specs=pl.BlockSpec((2 * n, tc), lambda j: (0, j)),
        out_shape=jax.ShapeDtypeStruct((2 * n, 2 * MIX_W), F32),
        compiler_params=_cp(("parallel",)),
        name=f"hyena_filter_fft_{seq_len}",
    )(kfilt, wf)


def _shift_rows(x, s, row):
    n = x.shape[0]
    y = pltpu.roll(x, s % n, 0)
    if s > 0:
        return jnp.where(row >= s, y, 0.0)
    return jnp.where(row < n + s, y, 0.0)


def _short_conv3(x_ref, w_ref):
    x = x_ref[...].astype(F32)
    row = lax.broadcasted_iota(I32, x.shape, 0)
    w = w_ref[...]
    return w[0:1] * _shift_rows(x, 1, row) + w[1:2] * x + w[2:3] * _shift_rows(x, -1, row)


def _hy_conv_kernel(v_ref, x1_ref, x2_ref, ws_ref, kf_ref, skip_ref, w1_ref, g_ref, gt_ref, w4_ref, y_ref,
                    z_ref, c_ref, a_ref, *, n1, n2, used, npg):
    o = pl.program_id(2)
    nh = n1 // 2

    @pl.when(o == 0)
    def _():
        z_ref[...] = _short_conv3(v_ref, ws_ref.at[0])

    w1 = w1_ref[...]
    _fft_stage1(lambda m: z_ref[pl.ds(m, nh, stride=n2), :], w1, a_ref, npg, n2, False)

    def page(p, _):
        rows = pl.ds(pl.multiple_of(p * 2 * n2, 2 * n2), 2 * n2)
        x = jnp.dot(g_ref[p], a_ref[rows, :].astype(BF16), preferred_element_type=F32)
        kf = kf_ref[rows, :]
        xr, xi = x[:n2], x[n2:]
        kr, ki = kf[:n2], kf[n2:]
        yc = jnp.concatenate([xr * kr - xi * ki, xr * ki + xi * kr], axis=0).astype(BF16)
        a_ref[rows, :] = jnp.dot(gt_ref[p], yc, preferred_element_type=F32)
        return 0

    lax.fori_loop(0, used, page, 0, unroll=5)

    w4 = w4_ref[...]

    def back(m, _):
        dr = a_ref[pl.ds(m, npg, stride=2 * n2), :]
        di = a_ref[pl.ds(n2 + m, npg, stride=2 * n2), :]
        d = jnp.concatenate([dr, di], axis=0).astype(BF16)
        c_ref[pl.ds(m, nh, stride=n2), :] = jnp.dot(w4, d, preferred_element_type=F32)
        return 0

    lax.fori_loop(0, n2, back, 0, unroll=2)

    @pl.when(o == 0)
    def _():
        z = z_ref[...]
        z_ref[...] = _short_conv3(x1_ref, ws_ref.at[1]) * (c_ref[...] + z * skip_ref[0:1])

    @pl.when(o == 1)
    def _():
        z = z_ref[...]
        y_ref[...] = (_short_conv3(x2_ref, ws_ref.at[2]) * (c_ref[...] + z * skip_ref[1:2])).astype(BF16)


def hyena_conv(proj, row_block_off, nseq, seq_len, w_short3, kf, skip):
    n1, n2 = _fft_factors(seq_len)
    used, npg = _fft_pages(seq_len)
    _, w1h, g, gt, w4 = _dft_tables_bf16(seq_len)
    rows_a = npg * 2 * n2
    tc = 128
    ncb = MIX_W // tc
    kern = functools.partial(_hy_conv_kernel, n1=n1, n2=n2, used=used, npg=npg)
    const3 = lambda c, b, o: (0, 0, 0)
    seg = lambda s: pl.BlockSpec((seq_len, tc), lambda c, b, o: (row_block_off + b, CB_HY + s * ncb + c))
    return pl.pallas_call(
        kern,
        grid=(ncb, nseq, 2),
        in_specs=[seg(0), seg(1), seg(2),
                  pl.BlockSpec((3, 3, tc), lambda c, b, o: (0, 0, c)),
                  pl.BlockSpec((rows_a, tc), lambda c, b, o: (0, o * ncb + c)),
                  pl.BlockSpec((2, tc), lambda c, b, o: (0, c)),
                  pl.BlockSpec((2 * npg, n1 // 2), lambda c, b, o: (0, 0)),
                  pl.BlockSpec((npg, 2 * n2, 2 * n2), const3, pipeline_mode=pl.Buffered(1)),
                  pl.BlockSpec((npg, 2 * n2, 2 * n2), const3, pipeline_mode=pl.Buffered(1)),
                  pl.BlockSpec((n1 // 2, 2 * npg), lambda c, b, o: (0, 0))],
        out_specs=pl.BlockSpec((seq_len, tc), lambda c, b, o: (b, c)),
        out_shape=jax.ShapeDtypeStruct((nseq * seq_len, MIX_W), BF16),
        scratch_shapes=[pltpu.VMEM((seq_len, tc), F32), pltpu.VMEM((seq_len, tc), F32),
                        pltpu.VMEM((rows_a, tc), F32)],
        compiler_params=_cp(("parallel", "parallel", "arbitrary")),
        name=f"hyena_conv_{seq_len}",
    )(proj, proj, proj, w_short3, kf, skip, w1h, g, gt, w4)


def _hy_direct_kernel(u_ref, ws_ref, kf_ref, skip_ref, wf_ref, wi_ref, y_ref, *, n):
    w = MIX_W
    wf = wf_ref[...]
    wi = wi_ref[...]
    z = _short_conv3(u_ref.at[:, pl.ds(0, w)], ws_ref.at[0])
    for o in range(2):
        x = jnp.dot(wf, z.astype(BF16), preferred_element_type=F32)
        kf = kf_ref[:, o * w:(o + 1) * w]
        xr, xi = x[:n], x[n:]
        kr, ki = kf[:n], kf[n:]
        yc = jnp.concatenate([xr * kr - xi * ki, xr * ki + xi * kr], axis=0).astype(BF16)
        conv = jnp.dot(wi, yc, preferred_element_type=F32)
        gate = _short_conv3(u_ref.at[:, pl.ds((o + 1) * w, w)], ws_ref.at[o + 1])
        z = gate * (conv + z * skip_ref[o:o + 1])
    y_ref[...] = z.astype(BF16)


def hyena_conv_direct(proj, row_block_off, nseq, seq_len, w_short3, kf, skip):
    n = 2 * seq_len
    wf_np, wi_np = _direct_dft_tables(seq_len)
    wf = jnp.asarray(wf_np[:, :seq_len], BF16)
    wi = jnp.asarray(wi_np, BF16)
    kern = functools.partial(_hy_direct_kernel, n=n)
    return pl.pallas_call(
        kern,
        grid=(nseq,),
        in_specs=[pl.BlockSpec((seq_len, 3 * MIX_W), lambda b: (row_block_off + b, 0)),
                  pl.BlockSpec((3, 3, MIX_W), lambda b: (0, 0, 0)),
                  pl.BlockSpec((2 * n, 2 * MIX_W), lambda b: (0, 0)),
                  pl.BlockSpec((2, MIX_W), lambda b: (0, 0)),
                  pl.BlockSpec((2 * n, seq_len), lambda b: (0, 0)),
                  pl.BlockSpec((seq_len, 2 * n), lambda b: (0, 0))],
        out_specs=pl.BlockSpec((seq_len, MIX_W), lambda b: (b, 0)),
        out_shape=jax.ShapeDtypeStruct((nseq * seq_len, MIX_W), BF16),
        compiler_params=_cp(("parallel",)),
        name=f"hyena_conv_{seq_len}",
    )(proj, w_short3, kf, skip, wf, wi)


def _dot_nt(a, b):
    return lax.dot_general(a, b, (((1,), (1,)), ((), ())), preferred_element_type=F32)


ML_CHUNK = 256


def _log_sigmoid(x):
    return jnp.minimum(x, 0.0) - jnp.log1p(jnp.exp(-jnp.abs(x)))


def _mlstm_kernel(*refs, seq_len, has_state):
    if has_state:
        (q_ref, k_ref, v_ref, o_ref, gr_ref, gc_ref, br_ref, bc_ref, gain_ref, c0_ref, n0_ref, m0_ref,
         y_ref, cout_ref, nout_ref, mout_ref, hs0, hs1, cs0, cs1, ns0, ns1, ms0, ms1) = refs
    else:
        (q_ref, k_ref, v_ref, o_ref, gr_ref, gc_ref, br_ref, bc_ref, gain_ref,
         y_ref, cout_ref, nout_ref, mout_ref, hs0, hs1, cs0, cs1, ns0, ns1, ms0, ms1) = refs
    hs, cs, ns, ms = (hs0, hs1), (cs0, cs1), (ns0, ns1), (ms0, ms1)
    t = ML_CHUNK
    nc = seq_len // t
    scale = ML_DK ** -0.5
    for d in (0, 1):
        if has_state:
            cs[d][...] = c0_ref[d]
            ns[d][...] = n0_ref[d]
            ms[d][...] = m0_ref[d]
        else:
            cs[d][...] = jnp.zeros_like(cs[d])
            ns[d][...] = jnp.zeros_like(ns[d])
            ms[d][...] = jnp.zeros_like(ms[d])
    row = lax.broadcasted_iota(I32, (t, t), 0)
    col = lax.broadcasted_iota(I32, (t, t), 1)
    br = br_ref[...]
    bc = bc_ref[...]
    tris = (jnp.where(row <= col, 1.0, 0.0).astype(BF16), jnp.where(row >= col, 1.0, 0.0).astype(BF16))

    def cumsum_row(f_r, tri):
        f8 = jnp.broadcast_to(f_r, (8, t))
        hi = f8.astype(BF16)
        r1 = f8 - hi.astype(F32)
        mid = r1.astype(BF16)
        lo = (r1 - mid.astype(F32)).astype(BF16)
        out = (jnp.dot(hi, tri, preferred_element_type=F32) + jnp.dot(mid, tri, preferred_element_type=F32)
               + jnp.dot(lo, tri, preferred_element_type=F32))
        return out[0:1]

    def step(j, _):
        for d in (0, 1):
            jj = j if d == 0 else nc - 1 - j
            rows = pl.ds(pl.multiple_of(jj * t, t), t)
            qf = q_ref[rows, :].astype(F32) * scale
            qb = qf.astype(BF16)
            kb = k_ref[rows, :]
            vb = v_ref[rows, :]
            gr = gr_ref[rows, :] + br
            gc = gc_ref[:, rows] + bc
            i_c = gr[:, 2 * d:2 * d + 1]
            i_r = gc[2 * d:2 * d + 1, :]
            f_r = _log_sigmoid(gc[2 * d + 1:2 * d + 2, :])
            mask = (col <= row) if d == 0 else (col >= row)
            b_c = jnp.sum(jnp.where(mask, f_r, 0.0), axis=1, keepdims=True)
            b_r = cumsum_row(f_r, tris[d])
            logw = jnp.where(mask, b_c - b_r + i_r, -jnp.inf)
            m = ms[d][:, 0:1]
            g = b_c + m
            mt = jnp.maximum(g, jnp.max(logw, axis=1, keepdims=True))
            s = _dot_nt(qb, kb) * jnp.exp(logw - mt)
            inter = jnp.exp(g - mt)
            cm = cs[d][...]
            nv = ns[d][...]
            num = (jnp.dot(s.astype(BF16), vb, preferred_element_type=F32)
                   + inter * jnp.dot(qb, cm.astype(BF16), preferred_element_type=F32))
            den = jnp.sum(s, axis=1, keepdims=True) + inter * jnp.sum(qf * nv, axis=1, keepdims=True)
            hs[d][rows, :] = num / jnp.maximum(jnp.abs(den), jnp.exp(-mt))
            bl = jnp.sum(f_r, axis=1, keepdims=True)
            wlog_r = bl - b_r + i_r
            wlog_c = bl - b_c + i_c
            m_new = jnp.maximum(bl + m, jnp.max(wlog_r, axis=1, keepdims=True))
            dec = jnp.exp(bl + m - m_new)
            kw = jnp.exp(wlog_c - m_new) * kb.astype(F32)
            cs[d][...] = dec * cm + jnp.dot(kw.T.astype(BF16), vb, preferred_element_type=F32)
            ns[d][...] = dec * nv + jnp.sum(kw, axis=0, keepdims=True)
            ms[d][...] = jnp.broadcast_to(m_new, (1, LANE))
        return 0

    lax.fori_loop(0, nc, step, 0, unroll=2)
    hsum = hs0[...] + hs1[...]
    hn = hsum * lax.rsqrt(jnp.mean(hsum * hsum, axis=-1, keepdims=True) + EPS) * gain_ref[...]
    y_ref[...] = (jax.nn.sigmoid(o_ref[...].astype(F32)) * hn).astype(BF16)
    for d in (0, 1):
        cout_ref[d] = cs[d][...]
        nout_ref[d] = ns[d][...]
        mout_ref[d] = ms[d][...]


def mlstm_mixer(proj, gates_r, gates_c, bias_r, bias_c, gain, row_block_off, nseq, seq_len, state):
    has_state = state is not None
    kern = functools.partial(_mlstm_kernel, seq_len=seq_len, has_state=has_state)
    col = lambda cb: pl.BlockSpec((seq_len, LANE), lambda b, h: (row_block_off + b, cb + h))
    c_spec = pl.BlockSpec((None, 2, None, ML_DK, ML_DK), lambda b, h: (b, 0, h, 0, 0))
    v_spec = pl.BlockSpec((None, 2, None, 1, LANE), lambda b, h: (b, 0, h, 0, 0))
    in_specs = [col(CB_MLQ), col(CB_MLK), col(CB_MLV), col(CB_MLO),
                pl.BlockSpec((None, seq_len, 4), lambda b, h: (h, row_block_off + b, 0)),
                pl.BlockSpec((None, 4, seq_len), lambda b, h: (h, 0, row_block_off + b)),
                pl.BlockSpec((None, 1, 4), lambda b, h: (h, 0, 0)),
                pl.BlockSpec((None, 4, 1), lambda b, h: (h, 0, 0)),
                pl.BlockSpec((1, LANE), lambda b, h: (0, h))]
    args = [proj, proj, proj, proj, gates_r, gates_c, bias_r, bias_c, gain]
    if has_state:
        in_specs += [c_spec, v_spec, v_spec]
        args += list(state)
    return pl.pallas_call(
        kern,
        grid=(nseq, ML_H),
        in_specs=in_specs,
        out_specs=[pl.BlockSpec((seq_len, LANE), lambda b, h: (b, h)), c_spec, v_spec, v_spec],
        out_shape=[jax.ShapeDtypeStruct((nseq * seq_len, MIX_W), BF16),
                   jax.ShapeDtypeStruct((nseq, 2, ML_H, ML_DK, ML_DK), F32),
                   jax.ShapeDtypeStruct((nseq, 2, ML_H, 1, LANE), F32),
                   jax.ShapeDtypeStruct((nseq, 2, ML_H, 1, LANE), F32)],
        scratch_shapes=([pltpu.VMEM((seq_len, LANE), F32)] * 2 + [pltpu.VMEM((ML_DK, ML_DK), F32)] * 2
                        + [pltpu.VMEM((1, LANE), F32)] * 4),
        compiler_params=_cp(("parallel", "parallel")),
        name=f"mlstm_{seq_len}",
    )(*args)


def _softplus(x):
    return jnp.maximum(x, 0.0) + jnp.log1p(jnp.exp(-jnp.abs(x)))


def _gelu_tanh(x):
    return 0.5 * x * (1.0 + jnp.tanh(math.sqrt(2.0 / math.pi) * (x + 0.044715 * (x * x * x))))


def _lru_kernel(*refs, seq_len, has_state):
    if has_state:
        (x_ref, y_ref, cw_ref, cb_ref, wa_ref, ba_ref, wx_ref, bx_ref, lam_ref, h0_ref,
         out_ref, hl_ref, a_s, b_s, hsum, hsum_b) = refs
    else:
        (x_ref, y_ref, cw_ref, cb_ref, wa_ref, ba_ref, wx_ref, bx_ref, lam_ref,
         out_ref, hl_ref, a_s, b_s, hsum, hsum_b) = refs
    xf = x_ref[...].astype(F32)
    row = lax.broadcasted_iota(I32, xf.shape, 0)
    w = cw_ref[...]
    x = (w[0:1] * _shift_rows(xf, 2, row) + w[1:2] * _shift_rows(xf, 1, row) + w[2:3] * xf
         + w[3:4] * _shift_rows(xf, -1, row) + cb_ref[...])
    xb = x.astype(BF16)
    for d in (0, 1):
        r = jax.nn.sigmoid(jnp.dot(xb, wa_ref[d].astype(BF16), preferred_element_type=F32) + ba_ref[d])
        i = jax.nn.sigmoid(jnp.dot(xb, wx_ref[d].astype(BF16), preferred_element_type=F32) + bx_ref[d])
        log_a = (-LRU_C) * r * _softplus(-lam_ref[d])
        a_s[d] = jnp.exp(log_a)
        b_s[d] = jnp.sqrt(jnp.maximum(1.0 - jnp.exp(2.0 * log_a), 0.0)) * (i * x)

    nb = seq_len // 8
    sub = lax.broadcasted_iota(I32, (8, LANE), 0)

    def block_scan(a, b, reverse):
        for s in (1, 2, 4):
            sh = 8 - s if reverse else s
            ok = (sub < 8 - s) if reverse else (sub >= s)
            b = jnp.where(ok, a * pltpu.roll(b, sh, 0) + b, b)
            a = jnp.where(ok, a * pltpu.roll(a, sh, 0), a)
        return a, b

    def both(i, carry):
        cf, cb = carry
        rows_f = pl.ds(pl.multiple_of(i * 8, 8), 8)
        rows_b = pl.ds(pl.multiple_of((nb - 1 - i) * 8, 8), 8)
        a, b = block_scan(a_s[0, rows_f, :], b_s[0, rows_f, :], False)
        hf = a * cf + b
        hsum[rows_f, :] = hf
        a, b = block_scan(a_s[1, rows_b, :], b_s[1, rows_b, :], True)
        hb = a * cb + b
        hsum_b[rows_b, :] = hb
        return hf[7:8, :], hb[0:1, :]

    zero = jnp.zeros((1, LANE), F32)
    init = (h0_ref[0], h0_ref[1]) if has_state else (zero, zero)
    hf, hb = lax.fori_loop(0, nb, both, init, unroll=4)
    out_ref[...] = ((hsum[...] + hsum_b[...]) * _gelu_tanh(y_ref[...].astype(F32))).astype(BF16)
    hl_ref[0] = hf
    hl_ref[1] = hb


def rglru_mixer(proj, conv_w, conv_b, wa, ba, wx, bx, lam, row_block_off, nseq, seq_len, h0):
    has_state = h0 is not None
    kern = functools.partial(_lru_kernel, seq_len=seq_len, has_state=has_state)
    col = lambda cb: pl.BlockSpec((seq_len, LANE), lambda b, h: (row_block_off + b, cb + h))
    w_spec = pl.BlockSpec((2, None, LANE, LANE), lambda b, h: (0, h, 0, 0))
    v_spec = pl.BlockSpec((2, 1, LANE), lambda b, h: (0, 0, h))
    s_spec = pl.BlockSpec((None, 2, 1, LANE), lambda b, h: (b, 0, 0, h))
    in_specs = [col(CB_LRX), col(CB_LRY),
                pl.BlockSpec((4, LANE), lambda b, h: (0, h)),
                pl.BlockSpec((1, LANE), lambda b, h: (0, h)),
                w_spec, v_spec, w_spec, v_spec, v_spec]
    args = [proj, proj, conv_w, conv_b, wa, ba, wx, bx, lam]
    if has_state:
        in_specs.append(s_spec)
        args.append(h0)
    return pl.pallas_call(
        kern,
        grid=(nseq, LRU_H),
        in_specs=in_specs,
        out_specs=[pl.BlockSpec((seq_len, LANE), lambda b, h: (b, h)), s_spec],
        out_shape=[jax.ShapeDtypeStruct((nseq * seq_len, MIX_W), BF16),
                   jax.ShapeDtypeStruct((nseq, 2, 1, MIX_W), F32)],
        scratch_shapes=[pltpu.VMEM((2, seq_len, LANE), F32), pltpu.VMEM((2, seq_len, LANE), F32),
                        pltpu.VMEM((seq_len, LANE), F32), pltpu.VMEM((seq_len, LANE), F32)],
        compiler_params=_cp(("parallel", "parallel")),
        name=f"rglru_{seq_len}",
    )(*args)


@functools.lru_cache(maxsize=None)
def _rope_tables():
    t = np.arange(DEC_SEQ)
    pos_row, pos_col = (t // GRID_W).astype(np.float64), (t % GRID_W).astype(np.float64)
    half = DA_DK // 2
    inv = ROPE_BASE ** (-np.arange(0, half, 2, dtype=np.float64) / half)
    lane = np.arange(LANE)
    sub = lane % DA_DK
    pos = np.where((sub < half)[None, :], pos_row[:, None], pos_col[:, None])
    ang = pos * inv[sub % (half // 2)][None, :]
    first = (sub % half) < (half // 2)
    cos = np.cos(ang)
    sin = np.where(first[None, :], -np.sin(ang), np.sin(ang))
    cos_all = np.concatenate([np.ones((T_CTX, LANE))] + [cos] * DEC_BATCH, axis=0)
    sin_all = np.concatenate([np.zeros((T_CTX, LANE))] + [sin] * DEC_BATCH, axis=0)
    pm = np.kron(np.eye(2), np.full((DA_DK, DA_DK), 1.0 / DA_DK))
    return np.asarray(cos_all, np.float32), np.asarray(sin_all, np.float32), np.asarray(pm, np.float32)


def _qkprep_kernel(q_ref, k_ref, cos_ref, sin_ref, qg_ref, kg_ref, pm_ref, qo_ref, ko_ref, kn_ref):
    pmb = pm_ref[...].astype(BF16)
    cos = cos_ref[...]
    sin = sin_ref[...]
    lane = lax.broadcasted_iota(I32, cos.shape, 1)
    first = (lane % (DA_DK // 2)) < (DA_DK // 4)

    def norm(x, g):
        hi, lo = _split_bf16(x * x)
        ms = jnp.dot(hi, pmb, preferred_element_type=F32) + jnp.dot(lo, pmb, preferred_element_type=F32)
        return x * lax.rsqrt(ms + EPS) * g

    def rope(x):
        partner = jnp.where(first, pltpu.roll(x, LANE - DA_DK // 4, 1), pltpu.roll(x, DA_DK // 4, 1))
        return x * cos + partner * sin

    q = norm(q_ref[...].astype(F32), qg_ref[...])
    k = norm(k_ref[...].astype(F32), kg_ref[...])
    kn_ref[...] = k
    qo_ref[...] = (rope(q) * (DA_DK ** -0.5)).astype(BF16)
    ko_ref[...] = rope(k).astype(BF16)


def qk_prepare(proj, q_gain, k_gain):
    cos, sin, pm = (jnp.asarray(t) for t in _rope_tables())
    tm = 512
    blk = lambda cb: pl.BlockSpec((tm, LANE), lambda i, h: (i, cb + h))
    tab = pl.BlockSpec((tm, LANE), lambda i, h: (i, 0))
    one = pl.BlockSpec((1, LANE), lambda i, h: (0, 0))
    out = pl.BlockSpec((tm, LANE), lambda i, h: (i, h))
    return pl.pallas_call(
        _qkprep_kernel,
        grid=(T_ALL // tm, DA_H),
        in_specs=[blk(CB_DAQ), blk(CB_DAK), tab, tab, one, one, pl.BlockSpec((LANE, LANE), lambda i, h: (0, 0))],
        out_specs=[out, out, out],
        out_shape=[jax.ShapeDtypeStruct((T_ALL, MIX_W), BF16), jax.ShapeDtypeStruct((T_ALL, MIX_W), BF16),
                   jax.ShapeDtypeStruct((T_ALL, MIX_W), F32)],
        compiler_params=_cp(("parallel", "parallel")),
        name="qk_prepare",
    )(proj, proj, cos, sin, q_gain, k_gain, pm)


def _attn_kernel(*refs, has_cache, lam_init, seq_len):
    if has_cache:
        q_ref, k_ref, v_ref, kc_ref, vc_ref, lp_ref, sg_ref, o_ref, ka_ref, va_ref = refs
    else:
        q_ref, k_ref, v_ref, lp_ref, sg_ref, o_ref, ka_ref, va_ref = refs

    @pl.when(pl.program_id(2) == 0)
    def _():
        ka_ref[0:seq_len, :] = k_ref[...]
        va_ref[0:seq_len, 0:LANE] = v_ref[...]
        if has_cache:
            ka_ref[seq_len:, :] = kc_ref[...].astype(BF16)
            va_ref[seq_len:, 0:LANE] = vc_ref[...].astype(BF16)
        va_ref[:, LANE:] = jnp.ones((va_ref.shape[0], LANE), BF16)

    q = q_ref[...]
    lane = lax.broadcasted_iota(I32, q.shape, 1)
    zero = jnp.zeros_like(q)
    lp = lp_ref[...]
    lam = (jnp.exp(jnp.sum(lp[0:1] * lp[1:2], axis=1, keepdims=True))
           - jnp.exp(jnp.sum(lp[2:3] * lp[3:4], axis=1, keepdims=True)) + lam_init)

    def softmax_half(qc):
        s = _dot_nt(qc, ka_ref[...])
        p = jnp.exp(s - jnp.max(s, axis=-1, keepdims=True)).astype(BF16)
        r = jnp.dot(p, va_ref[...], preferred_element_type=F32)
        return r[:, :LANE] / r[:, LANE:]

    o = softmax_half(jnp.where(lane < DA_DK, q, zero)) - lam * softmax_half(jnp.where(lane >= DA_DK, q, zero))
    o = o * lax.rsqrt(jnp.mean(o * o, axis=-1, keepdims=True) + EPS) * sg_ref[...]
    o_ref[...] = (o * (1.0 - lam_init)).astype(BF16)


def diff_attention(q_rot, k_rot, proj, lam_p, sub_gain, lam_init, row_block_off, nseq, seq_len, cache, layer):
    has_cache = cache is not None
    tq = 256
    kern = functools.partial(_attn_kernel, has_cache=has_cache, lam_init=lam_init, seq_len=seq_len)
    nq = seq_len // tq
    n_keys = seq_len + (cache[0].shape[2] if has_cache else 0)
    kv = lambda cb: pl.BlockSpec((seq_len, LANE), lambda b, h, i: (row_block_off + b, cb + h))
    in_specs = [pl.BlockSpec((tq, LANE), lambda b, h, i: ((row_block_off + b) * nq + i, h)), kv(0), kv(CB_DAV)]
    args = [q_rot, k_rot, proj]
    if has_cache:
        past = cache[0].shape[2]
        cspec = pl.BlockSpec((None, None, past, LANE), lambda b, h, i: (b, layer, 0, h))
        in_specs += [cspec, cspec]
        args += list(cache)
    in_specs += [pl.BlockSpec((4, DA_DK), lambda b, h, i: (0, 0)), pl.BlockSpec((1, LANE), lambda b, h, i: (0, 0))]
    args += [lam_p, sub_gain]
    return pl.pallas_call(
        kern,
        grid=(nseq, DA_H, nq),
        in_specs=in_specs,
        out_specs=pl.BlockSpec((tq, LANE), lambda b, h, i: (b * nq + i, h)),
        out_shape=jax.ShapeDtypeStruct((nseq * seq_len, MIX_W), BF16),
        scratch_shapes=[pltpu.VMEM((n_keys, LANE), BF16), pltpu.VMEM((n_keys, 2 * LANE), BF16)],
        compiler_params=_cp(("parallel", "parallel", "arbitrary")),
        name=f"diff_attention_{seq_len}",
    )(*args)


def _merge_kernel(h_ref, y0_ref, y1_ref, y2_ref, y3_ref, wg0_ref, wg1_ref, wg2_ref, wg3_ref, wb_ref, o_ref):
    h = h_ref[...]
    acc = None
    for bi, (y_ref, wg_ref) in enumerate(((y0_ref, wg0_ref), (y1_ref, wg1_ref), (y2_ref, wg2_ref), (y3_ref, wg3_ref))):
        gate = jax.nn.sigmoid(jnp.dot(h, wg_ref[...], preferred_element_type=F32))
        term = gate * jnp.dot(y_ref[...], wb_ref[bi], preferred_element_type=F32)
        acc = term if acc is None else acc + term
    o_ref[...] = acc.astype(BF16)


def branch_merge(h, ys, w_gate, w_branch):
    tm, tn = 1024, 256
    nj = D_MODEL // tn
    yspec = pl.BlockSpec((tm, MIX_W), lambda i, j: (i, 0))
    gspec = lambda bi: pl.BlockSpec((D_MODEL, tn), lambda i, j: (0, bi * nj + j))
    return pl.pallas_call(
        _merge_kernel,
        grid=(T_ALL // tm, nj),
        in_specs=[pl.BlockSpec((tm, D_MODEL), lambda i, j: (i, 0)), yspec, yspec, yspec, yspec,
                  gspec(0), gspec(1), gspec(2), gspec(3),
                  pl.BlockSpec((N_BRANCH, MIX_W, tn), lambda i, j: (0, 0, j))],
        out_specs=pl.BlockSpec((tm, tn), lambda i, j: (i, j)),
        out_shape=jax.ShapeDtypeStruct((T_ALL, D_MODEL), BF16),
        compiler_params=_cp(("parallel", "parallel")),
        name="branch_merge",
    )(h, *ys, w_gate, w_gate, w_gate, w_gate, w_branch)


def _outproj_kernel(m_ref, x_ref, g1_ref, w_ref, n2_ref, sh2_ref, sc2_ref, rt_ref, rl_ref, x1_ref, h2_ref, lg_ref):
    half = m_ref.shape[0] // 2
    for r in (0, half):
        rows = slice(r, r + half)
        y = jnp.dot(m_ref[rows, :], w_ref[...], preferred_element_type=F32)
        x1 = x_ref[rows, :] + g1_ref[...] * y
        x1_ref[rows, :] = x1
        h2 = _norm_mod(x1, n2_ref[...], sc2_ref[...], sh2_ref[...])
        h2_ref[rows, :] = _pack_bf16_pairs(h2)
        hi, lo = _split_bf16(h2)
        lg_ref[rows, :] = (jnp.dot(hi, rt_ref[...], preferred_element_type=F32)
                           + jnp.dot(lo, rt_ref[...], preferred_element_type=F32)
                           + jnp.dot(hi, rl_ref[...], preferred_element_type=F32))


def out_projection(merged, x, mod4, layer, w_out, norm2_g, router):
    tm = 256
    router_hi = router.astype(BF16)
    router_lo = (router - router_hi.astype(F32)).astype(BF16)
    return pl.pallas_call(
        _outproj_kernel,
        grid=(T_ALL // tm,),
        in_specs=[pl.BlockSpec((tm, D_MODEL), lambda i: (i, 0)),
                  pl.BlockSpec((tm, D_MODEL), lambda i: (i, 0)),
                  _mod_spec(layer, 2, tm, 1),
                  pl.BlockSpec((D_MODEL, D_MODEL), lambda i: (0, 0)),
                  pl.BlockSpec((None, 1, D_MODEL), lambda i: (layer, 0, 0)),
                  _mod_spec(layer, 3, tm, 1),
                  _mod_spec(layer, 4, tm, 1),
                  pl.BlockSpec((D_MODEL, LANE), lambda i: (0, 0)),
                  pl.BlockSpec((D_MODEL, LANE), lambda i: (0, 0))],
        out_specs=[pl.BlockSpec((tm, D_MODEL), lambda i: (i, 0)),
                   pl.BlockSpec((tm, D_MODEL // 2), lambda i: (i, 0)),
                   pl.BlockSpec((tm, LANE), lambda i: (i, 0))],
        out_shape=[jax.ShapeDtypeStruct((T_ALL, D_MODEL), F32),
                   jax.ShapeDtypeStruct((T_ALL, D_MODEL // 2), I32),
                   jax.ShapeDtypeStruct((T_ALL, LANE), F32)],
        compiler_params=_cp(("parallel",)),
        name="out_projection",
    )(merged, x, mod4, w_out, norm2_g, mod4, mod4, router_hi, router_lo)


MOE_TM = 256
N_ASSIGN = T_ALL * TOP_K
MOE_BLOCKS = (N_ASSIGN + N_EXPERTS * (MOE_TM - 1)) // MOE_TM + 1
ROUTER_TT = 512
GROUP_SIZE = N_EXPERTS // N_GROUPS


def _router_kernel(lg_ref, bias_ref, tri_ref, eidx_ref, w_ref, rank_ref, cnt_ref, carry_ref):
    i = pl.program_id(0)

    @pl.when(i == 0)
    def _():
        carry_ref[...] = jnp.zeros_like(carry_ref)

    tt = lg_ref.shape[0]
    shape3 = (N_GROUPS, GROUP_SIZE, tt)
    neg = -jnp.inf
    logits_t = lg_ref[...].T[:N_EXPERTS]
    scores = jax.nn.sigmoid(logits_t).reshape(shape3)
    sel = scores + bias_ref[...].reshape(N_GROUPS, GROUP_SIZE, 1)
    gi = lax.broadcasted_iota(I32, shape3, 0)
    ji = lax.broadcasted_iota(I32, shape3, 1)
    ei = gi * GROUP_SIZE + ji
    m1 = jnp.max(sel, axis=1, keepdims=True)
    first = jnp.min(jnp.where(sel == m1, ji, GROUP_SIZE), axis=1, keepdims=True)
    m2 = jnp.max(jnp.where(ji == first, neg, sel), axis=1, keepdims=True)
    cur = m1 + m2
    g1 = lax.broadcasted_iota(I32, cur.shape, 0)
    gmask = jnp.zeros(cur.shape, jnp.bool_)
    for _ in range(TOPK_GROUPS):
        mx = jnp.max(cur, axis=0, keepdims=True)
        idx = jnp.min(jnp.where(cur == mx, g1, N_GROUPS), axis=0, keepdims=True)
        hit = g1 == idx
        gmask = jnp.logical_or(gmask, hit)
        cur = jnp.where(hit, neg, cur)
    masked = jnp.where(gmask, sel, neg)

    def all_max(a):
        return jnp.max(jnp.max(a, axis=1, keepdims=True), axis=0, keepdims=True)

    def all_min(a):
        return jnp.min(jnp.min(a, axis=1, keepdims=True), axis=0, keepdims=True)

    def all_sum(a):
        return jnp.sum(jnp.sum(a, axis=1, keepdims=True), axis=0, keepdims=True)

    hits, idxs, ws = [], [], []
    for _ in range(TOP_K):
        mx = all_max(masked)
        idx = all_min(jnp.where(masked == mx, ei, N_EXPERTS))
        hit = ei == idx
        hits.append(hit)
        idxs.append(idx)
        ws.append(all_sum(jnp.where(hit, scores, 0.0)))
        masked = jnp.where(hit, neg, masked)
    wsum = ws[0]
    for wk in ws[1:]:
        wsum = wsum + wk
    onehot = jnp.zeros(shape3, F32)
    for hit in hits:
        onehot = onehot + hit.astype(F32)
    oh2 = onehot.reshape(N_EXPERTS, tt)
    before = jnp.dot(oh2.astype(BF16), tri_ref[...], preferred_element_type=F32) + carry_ref[:, 0:1]
    before3 = before.reshape(shape3)
    for k in range(TOP_K):
        eidx_ref[k:k + 1, :] = idxs[k].reshape(1, tt)
        w_ref[k:k + 1, :] = (ws[k] / wsum * ROUTED_SCALE).reshape(1, tt)
        rank_ref[k:k + 1, :] = all_sum(jnp.where(hits[k], before3, 0.0)).reshape(1, tt).astype(I32)
    eidx_ref[TOP_K:, :] = jnp.zeros((8 - TOP_K, tt), I32)
    w_ref[TOP_K:, :] = jnp.zeros((8 - TOP_K, tt), F32)
    rank_ref[TOP_K:, :] = jnp.zeros((8 - TOP_K, tt), I32)
    carry_ref[...] = carry_ref[...] + jnp.sum(oh2, axis=1, keepdims=True)
    cnt_ref[...] = carry_ref[...]


def moe_route(logits_t, bias):
    tt = ROUTER_TT
    tri = jnp.asarray(np.triu(np.ones((tt, tt), np.float32), 1), BF16)
    row8 = pl.BlockSpec((8, tt), lambda i: (0, i))
    return pl.pallas_call(
        _router_kernel,
        grid=(T_ALL // tt,),
        in_specs=[pl.BlockSpec((tt, LANE), lambda i: (i, 0)),
                  pl.BlockSpec((N_EXPERTS, 1), lambda i: (0, 0)),
                  pl.BlockSpec((tt, tt), lambda i: (0, 0))],
        out_specs=[row8, row8, row8, pl.BlockSpec((N_EXPERTS, LANE), lambda i: (0, 0))],
        out_shape=[jax.ShapeDtypeStruct((8, T_ALL), I32), jax.ShapeDtypeStruct((8, T_ALL), F32),
                   jax.ShapeDtypeStruct((8, T_ALL), I32), jax.ShapeDtypeStruct((N_EXPERTS, LANE), F32)],
        scratch_shapes=[pltpu.VMEM((N_EXPERTS, LANE), F32)],
        compiler_params=_cp(("arbitrary",)),
        name="moe_route",
    )(logits_t, bias, tri)


N_SLOTS = MOE_BLOCKS * MOE_TM
D_PACK = D_MODEL // 2


def _pack_bf16_pairs(x):
    n = x.shape[1] // 2
    hi = pltpu.bitcast(x[:, :n].astype(BF16).astype(F32), I32)
    lo = pltpu.bitcast(x[:, n:].astype(BF16).astype(F32), I32)
    return hi | lax.shift_right_logical(lo, 16)


def _unpack_bf16_pairs(u):
    hi = pltpu.bitcast(u & jnp.int32(-65536), F32)
    lo = pltpu.bitcast(lax.shift_left(u, 16), F32)
    return jnp.concatenate([hi, lo], axis=1)


def _dispatch_kernel(slots_ref, h2p_ref, xs_hbm, sem, *, tm):
    i = pl.program_id(0)

    def body(r, _):
        for k in range(TOP_K):
            s = slots_ref[(i * tm + r) * TOP_K + k]
            pltpu.make_async_copy(h2p_ref.at[pl.ds(r, 1), :], xs_hbm.at[pl.ds(s, 1), :], sem).start()
        return 0

    lax.fori_loop(0, tm, body, 0)
    for k in range(TOP_K):
        pltpu.make_async_copy(h2p_ref, xs_hbm.at[pl.ds(0, tm), :], sem).wait()


def moe_dispatch(slots, h2p):
    tm = 256
    return pl.pallas_call(
        functools.partial(_dispatch_kernel, tm=tm),
        grid_spec=pltpu.PrefetchScalarGridSpec(
            num_scalar_prefetch=1,
            grid=(T_ALL // tm,),
            in_specs=[pl.BlockSpec((tm, D_PACK), lambda i, s: (i, 0))],
            out_specs=pl.BlockSpec(memory_space=pl.ANY),
            scratch_shapes=[pltpu.SemaphoreType.DMA(())]),
        out_shape=jax.ShapeDtypeStruct((N_SLOTS, D_PACK), I32),
        compiler_params=_cp(("arbitrary",), unchecked_dma=True),
        name="moe_dispatch",
    )(slots, h2p)


def _gmm_kernel(be_ref, bv_ref, first_ref, par_ref, nxt_ref, x_ref, wg_hbm, wu_hbm, wd_hbm, y_ref,
                wgf, wuf, wdf, wgb, wub, wdb, wsem, *, layer):
    i = pl.program_id(0)

    def fetch(e, slot):
        return (pltpu.make_async_copy(wg_hbm.at[layer, e], wgf.at[slot], wsem.at[slot]),
                pltpu.make_async_copy(wu_hbm.at[layer, e], wuf.at[slot], wsem.at[slot]),
                pltpu.make_async_copy(wd_hbm.at[layer, e], wdf.at[slot], wsem.at[slot]))

    @pl.when(i == 0)
    def _():
        for cp in fetch(be_ref[0], 0):
            cp.start()

    @pl.when(first_ref[i] == 1)
    def _():
        slot = par_ref[i]
        for cp in fetch(be_ref[i], slot):
            cp.wait()
        nxt = nxt_ref[i]

        @pl.when(nxt >= 0)
        def _():
            for cp in fetch(nxt, 1 - slot):
                cp.start()

        step = 512
        for r in range(0, D_MODEL, step):
            wgb[r:r + step, :] = wgf[slot, r:r + step, :].astype(BF16)
            wub[r:r + step, :] = wuf[slot, r:r + step, :].astype(BF16)
        for r in range(0, D_EXPERT, 128):
            wdb[r:r + 128, :] = wdf[slot, r:r + 128, :].astype(BF16)

    nv = bv_ref[i]

    @pl.when(nv == 0)
    def _():
        y_ref[...] = jnp.zeros_like(y_ref)

    @pl.when(nv > 0)
    def _():
        row = lax.broadcasted_iota(I32, (x_ref.shape[0], 1), 0)
        xin = jnp.where(row < nv, x_ref[...], 0)
        x = _unpack_bf16_pairs(xin).astype(BF16)
        g = jnp.dot(x, wgb[...], preferred_element_type=F32)
        u = jnp.dot(x, wub[...], preferred_element_type=F32)
        a = (g * jax.nn.sigmoid(g) * u).astype(BF16)
        y_ref[...] = _pack_bf16_pairs(jnp.dot(a, wdb[...], preferred_element_type=F32))


def moe_experts(block_e, block_valid, block_first, block_par, block_next, xs, w_gate, w_up, w_down, layer):
    tm = MOE_TM
    row = pl.BlockSpec((tm, D_PACK), lambda i, *_: (i, 0))
    anyspace = pl.BlockSpec(memory_space=pl.ANY)
    return pl.pallas_call(
        functools.partial(_gmm_kernel, layer=layer),
        grid_spec=pltpu.PrefetchScalarGridSpec(
            num_scalar_prefetch=5,
            grid=(MOE_BLOCKS,),
            in_specs=[row, anyspace, anyspace, anyspace],
            out_specs=row,
            scratch_shapes=[pltpu.VMEM((2, D_MODEL, D_EXPERT), F32), pltpu.VMEM((2, D_MODEL, D_EXPERT), F32),
                            pltpu.VMEM((2, D_EXPERT, D_MODEL), F32),
                            pltpu.VMEM((D_MODEL, D_EXPERT), BF16), pltpu.VMEM((D_MODEL, D_EXPERT), BF16),
                            pltpu.VMEM((D_EXPERT, D_MODEL), BF16), pltpu.SemaphoreType.DMA((2,))]),
        out_shape=jax.ShapeDtypeStruct((N_SLOTS, D_PACK), I32),
        compiler_params=_cp(("arbitrary",)),
        name="moe_experts",
    )(block_e, block_valid, block_first, block_par, block_next, xs, w_gate, w_up, w_down)


def _combine_kernel(slots_ref, x1_ref, h2p_ref, w_ref, g2_ref, sg_ref, su_ref, sd_ref, ys_hbm, out_ref, buf, sem,
                    *, tm, nsteps):
    i = pl.program_id(0)

    def issue(step, slot):
        def body(r, _):
            for k in range(TOP_K):
                s = slots_ref[(step * tm + r) * TOP_K + k]
                pltpu.make_async_copy(ys_hbm.at[pl.ds(s, 1), :], buf.at[slot, pl.ds(k * tm + r, 1), :],
                                      sem.at[slot]).start()
            return 0

        lax.fori_loop(0, tm, body, 0)

    @pl.when(i == 0)
    def _():
        issue(0, 0)

    @pl.when(i + 1 < nsteps)
    def _():
        issue(i + 1, (i + 1) % 2)

    slot = i % 2
    pltpu.make_async_copy(ys_hbm.at[pl.ds(0, TOP_K * tm), :], buf.at[slot], sem.at[slot]).wait()
    hb = _unpack_bf16_pairs(h2p_ref[...]).astype(BF16)
    g = jnp.dot(hb, sg_ref[...], preferred_element_type=F32)
    u = jnp.dot(hb, su_ref[...], preferred_element_type=F32)
    acc = jnp.dot((g * jax.nn.sigmoid(g) * u).astype(BF16), sd_ref[...], preferred_element_type=F32)
    w = w_ref[...]
    for k in range(TOP_K):
        acc = acc + w[:, k:k + 1] * _unpack_bf16_pairs(buf[slot, k * tm:(k + 1) * tm, :])
    out_ref[...] = x1_ref[...] + g2_ref[...] * acc


def moe_combine(slots, x1, h2p, w_tok, mod4, layer, sg, su, sd, ys):
    tm = 256
    nsteps = T_ALL // tm
    kern = functools.partial(_combine_kernel, tm=tm, nsteps=nsteps)
    row = lambda n: pl.BlockSpec((tm, n), lambda i, s: (i, 0))
    return pl.pallas_call(
        kern,
        grid_spec=pltpu.PrefetchScalarGridSpec(
            num_scalar_prefetch=1,
            grid=(nsteps,),
            in_specs=[row(D_MODEL), row(D_PACK), row(8),
                      pl.BlockSpec((None, None, 1, D_MODEL),
                                   lambda i, s: (layer, _mod_row(i, tm) * 6 + 5, 0, 0)),
                      pl.BlockSpec((D_MODEL, D_EXPERT), lambda i, s: (0, 0)),
                      pl.BlockSpec((D_MODEL, D_EXPERT), lambda i, s: (0, 0)),
                      pl.BlockSpec((D_EXPERT, D_MODEL), lambda i, s: (0, 0)),
                      pl.BlockSpec(memory_space=pl.ANY)],
            out_specs=row(D_MODEL),
            scratch_shapes=[pltpu.VMEM((2, TOP_K * tm, D_PACK), I32), pltpu.SemaphoreType.DMA((2,))]),
        out_shape=jax.ShapeDtypeStruct((T_ALL, D_MODEL), F32),
        compiler_params=_cp(("arbitrary",), unchecked_dma=True),
        name="moe_combine",
    )(slots, x1, h2p, w_tok, mod4, sg, su, sd, ys)


def moe_ffn(x1, h2p, logits_t, mod4, layer, moe_bias, w_gate, w_up, w_down, sg, su, sd):
    eidx, w_t, rank, cnt = moe_route(logits_t, moe_bias)
    counts = cnt[:, 0].astype(I32)
    padded = (counts + MOE_TM - 1) // MOE_TM * MOE_TM
    pad_end = jnp.cumsum(padded)
    pad_start = pad_end - padded
    experts = jnp.arange(N_EXPERTS, dtype=I32)
    base = jnp.sum(jnp.where(eidx[:TOP_K, :, None] == experts, pad_start, 0), axis=-1)
    slots = (base + rank[:TOP_K]).T.reshape(-1)
    starts = jnp.arange(MOE_BLOCKS, dtype=I32) * MOE_TM
    block_e = jnp.minimum(jnp.sum((pad_end[None, :] <= starts[:, None]).astype(I32), axis=1), N_EXPERTS - 1)
    mine = block_e[:, None] == experts
    left = jnp.sum(jnp.where(mine, counts + pad_start, 0), axis=1) - starts
    block_valid = jnp.clip(left, 0, MOE_TM).astype(I32)
    nonempty = counts > 0
    pick = lambda v: jnp.sum(jnp.where(mine, v, 0), axis=1)
    block_first = jnp.logical_and(starts == pick(pad_start), block_valid > 0).astype(I32)
    block_par = pick(jnp.cumsum(nonempty.astype(I32)) - 1) % 2
    later = lax.cummin(jnp.where(nonempty, experts, N_EXPERTS), axis=0, reverse=True)
    nxt_e = jnp.concatenate([later[1:], jnp.full((1,), N_EXPERTS, I32)])
    block_next = pick(jnp.where(nxt_e >= N_EXPERTS, -1, nxt_e))
    xs = moe_dispatch(slots, h2p)
    ys = moe_experts(block_e, block_valid, block_first, block_par, block_next, xs, w_gate, w_up, w_down, layer)
    return moe_combine(slots, x1, h2p, w_t.T, mod4, layer, sg, su, sd, ys)


def _ml_gate_layouts(gp, gate_bias):
    nt = gp.shape[0]
    g16 = gp[:, :4 * ML_H].reshape(nt, 2, 2, ML_H)
    gr = jnp.transpose(g16, (3, 0, 1, 2)).reshape(ML_H, nt, 4)
    gc = jnp.transpose(gr, (0, 2, 1))
    b = jnp.transpose(gate_bias, (2, 0, 1)).reshape(ML_H, 4)
    return gr, gc, b[:, None, :], b[:, :, None]


def kernel(x_prompt, x_sample, c, cache_k, cache_v, state_mlstm_C, state_mlstm_n, state_mlstm_m, state_rglru,
           c_ctx, w_mod, b_mod, norm1, norm2, w_in, hy_short, hy_w1, hy_b1, hy_w2, hy_b2, hy_w3, hy_freq,
           hy_decay, hy_skip, ml_gate_bias, ml_out_norm, lru_conv_w, lru_conv_b, lru_wa, lru_ba, lru_wx, lru_bx,
           lru_lambda, da_q_norm, da_k_norm, da_lambda, da_sub_norm, w_branch, w_out, moe_router, moe_bias,
           moe_w_gate, moe_w_up, moe_w_down, sh_w_gate, sh_w_up, sh_w_down):
    x = jnp.concatenate([x_prompt.reshape(T_CTX, D_MODEL), x_sample.reshape(T_DEN, D_MODEL)], axis=0)
    cvecs = jnp.concatenate([c_ctx[None], c, jnp.zeros((N_MOD_ROWS - 1 - DEC_BATCH, D_MODEL), F32)], axis=0)
    mod4 = modulation_all(cvecs, w_mod, b_mod).reshape(DEPTH, N_MOD_ROWS * 6, 1, D_MODEL)
    norm1_3 = norm1.reshape(DEPTH, 1, D_MODEL)
    norm2_3 = norm2.reshape(DEPTH, 1, D_MODEL)
    cache_k4 = cache_k.reshape(DEC_BATCH, DEPTH, -1, MIX_W)
    cache_v4 = cache_v.reshape(DEC_BATCH, DEPTH, -1, MIX_W)
    den_off = T_CTX // DEC_SEQ
    new_k, new_v, new_c, new_n, new_m, new_h = [], [], [], [], [], []
    for l in range(DEPTH):
        lam_init = 0.8 - 0.6 * math.exp(-0.3 * l)
        wl = w_in[l]
        n_ml = 4 * ML_H
        w_main = jnp.concatenate([wl[:, :7 * MIX_W], wl[:, 7 * MIX_W + n_ml:12 * MIX_W + n_ml]], axis=1).astype(BF16)
        w_g16 = jnp.pad(wl[:, 7 * MIX_W:7 * MIX_W + n_ml], ((0, 0), (0, LANE - n_ml))).astype(BF16)
        w_gate = wl[:, 12 * MIX_W + n_ml:].astype(BF16)
        proj, h, gp = in_projection(x, norm1_3, mod4, l, w_main, w_g16)

        w1p = jnp.pad(hy_w1[l], ((0, LANE - HY_PE), (0, 0)))
        ws3 = jnp.transpose(hy_short[l].reshape(3, 3, MIX_W), (1, 0, 2))
        y_hy = []
        for off, nseq, sl in ((0, BATCH, SEQ), (den_off, DEC_BATCH, DEC_SEQ)):
            kfilt = hyena_filter(sl, w1p, hy_b1[l][None], hy_w2[l], hy_b2[l][None], hy_freq[l][None], hy_w3[l],
                                 hy_decay[l][None])
            if sl <= 512:
                y_hy.append(hyena_conv_direct(proj, off, nseq, sl, ws3, hyena_filter_fft_direct(sl, kfilt), hy_skip[l]))
            else:
                y_hy.append(hyena_conv(proj, off, nseq, sl, ws3, hyena_filter_fft(sl, kfilt), hy_skip[l]))

        gr, gc, b_r, b_c = _ml_gate_layouts(gp, ml_gate_bias[l])
        gain = ml_out_norm[l][None]
        y_ml0, c_new, n_new, m_new = mlstm_mixer(proj, gr, gc, b_r, b_c, gain, 0, BATCH, SEQ, None)
        st = (state_mlstm_C[:, l], state_mlstm_n[:, l][:, :, :, None, :],
              jnp.broadcast_to(state_mlstm_m[:, l][:, :, :, None, None], (DEC_BATCH, 2, ML_H, 1, LANE)))
        y_ml1 = mlstm_mixer(proj, gr, gc, b_r, b_c, gain, den_off, DEC_BATCH, DEC_SEQ, st)[0]

        lru_args = (lru_conv_w[l], lru_conv_b[l][None], lru_wa[l], lru_ba[l][:, None, :], lru_wx[l],
                    lru_bx[l][:, None, :], lru_lambda[l][:, None, :])
        y_lr0, h_new = rglru_mixer(proj, *lru_args, 0, BATCH, SEQ, None)
        y_lr1 = rglru_mixer(proj, *lru_args, den_off, DEC_BATCH, DEC_SEQ, state_rglru[:, l][:, :, None, :])[0]

        q_rot, k_rot, k_norm = qk_prepare(proj, jnp.tile(da_q_norm[l], 2)[None], jnp.tile(da_k_norm[l], 2)[None])
        sub_gain = da_sub_norm[l][None]
        y_da0 = diff_attention(q_rot, k_rot, proj, da_lambda[l], sub_gain, lam_init, 0, BATCH, SEQ, None, l)
        y_da1 = diff_attention(q_rot, k_rot, proj, da_lambda[l], sub_gain, lam_init, den_off, DEC_BATCH, DEC_SEQ,
                               (cache_k4, cache_v4), l)

        ys = [jnp.concatenate(p, axis=0) for p in (y_hy, (y_ml0, y_ml1), (y_lr0, y_lr1), (y_da0, y_da1))]
        merged = branch_merge(h, ys, w_gate, w_branch[l].astype(BF16))
        x1, h2, logits_t = out_projection(merged, x, mod4, l, w_out[l].astype(BF16), norm2_3,
                                          jnp.pad(moe_router[l], ((0, 0), (0, LANE - N_EXPERTS))))
        x = moe_ffn(x1, h2, logits_t, mod4, l, moe_bias[l][:, None], moe_w_gate, moe_w_up, moe_w_down,
                    sh_w_gate[l].astype(BF16), sh_w_up[l].astype(BF16), sh_w_down[l].astype(BF16))

        new_k.append(k_norm[:T_CTX].reshape(BATCH, SEQ, DA_H, 2, DA_DK))
        new_v.append(proj[:T_CTX, CB_DAV * LANE:(CB_DAV + DA_H) * LANE].astype(F32).reshape(BATCH, SEQ, DA_H, -1))
        new_c.append(c_new)
        new_n.append(n_new[:, :, :, 0, :])
        new_m.append(m_new[:, :, :, 0, 0])
        new_h.append(h_new[:, :, 0, :])
    y_prompt = x[:T_CTX].reshape(BATCH, SEQ, D_MODEL)
    y_sample = x[T_CTX:].reshape(DEC_BATCH, DEC_SEQ, D_MODEL)
    stack = lambda parts: jnp.stack(parts, axis=1)
    return (y_prompt, y_sample, stack(new_k), stack(new_v), stack(new_c), stack(new_n), stack(new_m), stack(new_h))
```

```python
import functools
import math

import numpy as np
import jax
import jax.numpy as jnp
from jax import lax
from jax.experimental import pallas as pl
from jax.experimental.pallas import tpu as pltpu

F32 = jnp.float32
BF16 = jnp.bfloat16
I32 = jnp.int32
HIGHEST = lax.Precision.HIGHEST

D_MODEL = 2048
BATCH = 16
SEQ = 256
DEPTH = 2
DEC_BATCH = 4
DEC_SEQ = 4096
GRID_W = 64
N_BRANCH = 4
MIX_W = 512
HY_BANDS = 16
HY_PE = 1 + 2 * HY_BANDS
HY_FFN = 64
ML_H = 4
ML_DK = 128
LRU_H = 4
LRU_C = 8.0
DA_H = 4
DA_DK = 64
ROPE_BASE = 10000.0
N_EXPERTS = 64
TOP_K = 6
N_GROUPS = 8
TOPK_GROUPS = 4
D_EXPERT = 512
ROUTED_SCALE = 2.5
EPS = 1e-6

LANE = 128
T_CTX = BATCH * SEQ
T_DEN = DEC_BATCH * DEC_SEQ
T_ALL = T_CTX + T_DEN
N_MOD_ROWS = 8
VMEM_LIMIT = 56 * 1024 * 1024

CB_HY = 0
CB_MLQ, CB_MLK, CB_MLV, CB_MLO = 12, 16, 20, 24
CB_LRX, CB_LRY = 28, 32
CB_DAQ, CB_DAK, CB_DAV = 36, 40, 44
N_MAIN = 6144


def _cp(sem, vmem=VMEM_LIMIT, unchecked_dma=False):
    return pltpu.CompilerParams(dimension_semantics=sem, vmem_limit_bytes=vmem, disable_bounds_checks=unchecked_dma)


def _mod_row(i, rows_per_block):
    nctx = T_CTX // rows_per_block
    per = DEC_SEQ // rows_per_block
    return jnp.where(i < nctx, 0, 1 + (i - nctx) // per)


def _mod_spec(layer, seg, rows_per_block, ngrid):
    if ngrid == 1:
        return pl.BlockSpec((None, None, 1, D_MODEL), lambda i: (layer, _mod_row(i, rows_per_block) * 6 + seg, 0, 0))
    return pl.BlockSpec((None, None, 1, D_MODEL), lambda i, j: (layer, _mod_row(i, rows_per_block) * 6 + seg, 0, 0))


def _mod_kernel(c_ref, w_ref, b_ref, o_ref):
    c = c_ref[...]
    a = (c * jax.nn.sigmoid(c)).astype(BF16)
    o_ref[...] = jnp.dot(a, w_ref[...].astype(BF16), preferred_element_type=F32) + b_ref[...]


def modulation_all(cvecs, w_mod, b_mod):
    tn = 1024
    n = 6 * D_MODEL
    return pl.pallas_call(
        _mod_kernel,
        grid=(DEPTH, n // tn),
        in_specs=[pl.BlockSpec((N_MOD_ROWS, D_MODEL), lambda l, j: (0, 0)),
                  pl.BlockSpec((None, D_MODEL, tn), lambda l, j: (l, 0, j)),
                  pl.BlockSpec((None, 1, tn), lambda l, j: (l, 0, j))],
        out_specs=pl.BlockSpec((None, N_MOD_ROWS, tn), lambda l, j: (l, 0, j)),
        out_shape=jax.ShapeDtypeStruct((DEPTH, N_MOD_ROWS, n), F32),
        compiler_params=_cp(("parallel", "parallel")),
        name="modulation",
    )(cvecs, w_mod, b_mod.reshape(DEPTH, 1, n))


def _norm_mod(x, g, sc, sh):
    ms = jnp.mean(x * x, axis=-1, keepdims=True)
    return (x * lax.rsqrt(ms + EPS) * g) * (1.0 + sc) + sh


def _inproj_kernel(x_ref, g_ref, sh_ref, sc_ref, w_ref, wg_ref, proj_ref, h_ref, gp_ref, *, tm, sub):
    j = pl.program_id(1)

    @pl.when(j == 0)
    def _():
        g = g_ref[...]
        sc = sc_ref[...]
        sh = sh_ref[...]

        def body(r, _):
            rows = pl.ds(pl.multiple_of(r * sub, sub), sub)
            hb = _norm_mod(x_ref[rows, :], g, sc, sh).astype(BF16)
            h_ref[rows, :] = hb
            gp_ref[rows, :] = jnp.dot(hb, wg_ref[...], preferred_element_type=F32)
            return 0

        lax.fori_loop(0, tm // sub, body, 0)

    proj_ref[...] = jnp.dot(h_ref[...], w_ref[...], preferred_element_type=F32).astype(BF16)


def in_projection(x, norm_g, mod4, layer, w_main, w_gate16):
    tm, tn = 1024, 1024
    kern = functools.partial(_inproj_kernel, tm=tm, sub=256)
    return pl.pallas_call(
        kern,
        grid=(T_ALL // tm, N_MAIN // tn),
        in_specs=[pl.BlockSpec((tm, D_MODEL), lambda i, j: (i, 0)),
                  pl.BlockSpec((None, 1, D_MODEL), lambda i, j: (layer, 0, 0)),
                  _mod_spec(layer, 0, tm, 2),
                  _mod_spec(layer, 1, tm, 2),
                  pl.BlockSpec((D_MODEL, tn), lambda i, j: (0, j)),
                  pl.BlockSpec((D_MODEL, LANE), lambda i, j: (0, 0))],
        out_specs=[pl.BlockSpec((tm, tn), lambda i, j: (i, j)),
                   pl.BlockSpec((tm, D_MODEL), lambda i, j: (i, 0)),
                   pl.BlockSpec((tm, LANE), lambda i, j: (i, 0))],
        out_shape=[jax.ShapeDtypeStruct((T_ALL, N_MAIN), BF16),
                   jax.ShapeDtypeStruct((T_ALL, D_MODEL), BF16),
                   jax.ShapeDtypeStruct((T_ALL, LANE), F32)],
        compiler_params=_cp(("parallel", "arbitrary")),
        name="in_projection",
    )(x, norm_g, mod4, mod4, w_main, w_gate16)


def _fft_factors(seq_len):
    n2 = 64
    return 2 * seq_len // n2, n2


def _fft_pages(seq_len):
    n1, _ = _fft_factors(seq_len)
    used = n1 // 2 + 1
    return used, -(-used // 8) * 8


@functools.lru_cache(maxsize=None)
def _dft_tables(seq_len):
    n1, n2 = _fft_factors(seq_len)
    used, npg = _fft_pages(seq_len)
    n = n1 * n2
    th1 = 2.0 * np.pi * np.outer(np.arange(npg, dtype=np.float64), np.arange(n1, dtype=np.float64)) / n1
    w1 = np.concatenate([np.cos(th1), -np.sin(th1)], axis=0)
    w1h = w1[:, : n1 // 2]
    wt = np.where(np.arange(npg) < used, 2.0, 0.0)
    wt[0] = 1.0
    wt[n1 // 2] = 1.0
    th4 = th1[:, : n1 // 2].T
    w4 = np.concatenate([np.cos(th4) * wt, -np.sin(th4) * wt], axis=1) / n
    k1 = np.arange(npg, dtype=np.float64)[:, None, None]
    k2 = np.arange(n2, dtype=np.float64)[None, :, None]
    m2 = np.arange(n2, dtype=np.float64)[None, None, :]
    ph = -2.0 * np.pi * (m2 * k1 / n + m2 * k2 / n2)
    gr, gi = np.cos(ph), np.sin(ph)
    g = np.concatenate([np.concatenate([gr, -gi], axis=2), np.concatenate([gi, gr], axis=2)], axis=1)
    gt = np.transpose(g, (0, 2, 1))
    return tuple(np.asarray(t, np.float32) for t in (w1, w1h, g, gt, w4))


@functools.lru_cache(maxsize=None)
def _direct_dft_tables(seq_len):
    n = 2 * seq_len
    th = 2.0 * np.pi * np.outer(np.arange(n, dtype=np.float64), np.arange(n, dtype=np.float64)) / n
    wf = np.concatenate([np.cos(th), -np.sin(th)], axis=0)
    wi = np.concatenate([np.cos(th), -np.sin(th)], axis=1)[:seq_len] / n
    return np.asarray(wf, np.float32), np.asarray(wi, np.float32)


def _dft_tables_bf16(seq_len):
    return tuple(jnp.asarray(t, BF16) for t in _dft_tables(seq_len))


@functools.lru_cache(maxsize=None)
def _filter_positions(seq_len):
    pos = np.concatenate([np.arange(seq_len), [0], np.arange(seq_len - 1, 0, -1)]).astype(np.float64)
    tn = pos / seq_len
    bands = np.linspace(1e-4, HY_BANDS - 1, HY_BANDS)
    ang = (2.0 * math.pi / seq_len) * pos[:, None] * bands
    pe = np.zeros((2 * seq_len, LANE), np.float64)
    pe[:, 0] = tn
    pe[:, 1:1 + HY_BANDS] = np.cos(ang)
    pe[:, 1 + HY_BANDS:HY_PE] = np.sin(ang)
    return np.asarray(pe, np.float32), np.asarray(tn[:, None], np.float32)


def _hy_mlp_kernel(pe_ref, w1_ref, b1_ref, w2_ref, b2_ref, fr_ref, z_ref):
    fr = fr_ref[...]
    z = jnp.sin(fr * (jnp.dot(pe_ref[...], w1_ref[...], precision=HIGHEST, preferred_element_type=F32) + b1_ref[...]))
    z_ref[...] = jnp.sin(fr * (jnp.dot(z, w2_ref[...], precision=HIGHEST, preferred_element_type=F32) + b2_ref[...]))


def _hy_filter_kernel(z_ref, tn_ref, w3a_ref, w3b_ref, dca_ref, dcb_ref, k_ref, *, seq_len):
    def taps(rows, w3_ref, dc_ref):
        f = jnp.dot(z_ref[rows, :].astype(BF16), w3_ref[...].astype(BF16), preferred_element_type=F32)
        return f * jnp.exp(-tn_ref[rows, :] * jnp.abs(dc_ref[...]))

    ka = taps(pl.ds(0, seq_len), w3a_ref, dca_ref)
    kb = taps(pl.ds(seq_len, seq_len), w3b_ref, dcb_ref)
    row = lax.broadcasted_iota(I32, ka.shape, 0)
    ka = ka + jnp.where(row == 0, taps(pl.ds(0, 8), w3b_ref, dcb_ref)[0:1], 0.0)
    kb = jnp.where(row == 0, 0.0, kb)
    inv = 1.0 / (jnp.sum(jnp.abs(ka), axis=0, keepdims=True) + jnp.sum(jnp.abs(kb), axis=0, keepdims=True))
    k_ref[0:seq_len, :] = ka * inv
    k_ref[seq_len:, :] = kb * inv


def hyena_filter(seq_len, w1p, b1, w2, b2, freq, w3, decay):
    pe, tn = (jnp.asarray(t) for t in _filter_positions(seq_len))
    n = 2 * seq_len
    tr = min(n, 1024)
    full = lambda shape: pl.BlockSpec(shape, lambda j: (0,) * len(shape))
    z = pl.pallas_call(
        _hy_mlp_kernel,
        grid=(n // tr,),
        in_specs=[pl.BlockSpec((tr, LANE), lambda j: (j, 0)), full((LANE, HY_FFN)), full((1, HY_FFN)),
                  full((HY_FFN, HY_FFN)), full((1, HY_FFN)), full((1, HY_FFN))],
        out_specs=pl.BlockSpec((tr, HY_FFN), lambda j: (j, 0)),
        out_shape=jax.ShapeDtypeStruct((n, HY_FFN), F32),
        compiler_params=_cp(("parallel",)),
        name=f"hyena_filter_mlp_{seq_len}",
    )(pe, w1p, b1, w2, b2, freq)
    tc = 128
    nc = (2 * MIX_W) // tc
    kern = functools.partial(_hy_filter_kernel, seq_len=seq_len)
    return pl.pallas_call(
        kern,
        grid=(nc,),
        in_specs=[full((n, HY_FFN)), full((n, 1)),
                  pl.BlockSpec((HY_FFN, tc), lambda j: (0, j)),
                  pl.BlockSpec((HY_FFN, tc), lambda j: (0, nc + j)),
                  pl.BlockSpec((1, tc), lambda j: (0, j)),
                  pl.BlockSpec((1, tc), lambda j: (0, nc + j))],
        out_specs=pl.BlockSpec((n, tc), lambda j: (0, j)),
        out_shape=jax.ShapeDtypeStruct((n, 2 * MIX_W), F32),
        compiler_params=_cp(("parallel",)),
        name=f"hyena_filter_{seq_len}",
    )(z, tn, w3, w3, decay, decay)


def _split_bf16(x):
    hi = x.astype(BF16)
    lo = (x - hi.astype(F32)).astype(BF16)
    return hi, lo


def _fft_stage1(load_rows, w1, a_ref, npg, n2, split):
    def body(m, _):
        xs = load_rows(m)
        if split:
            hi, lo = _split_bf16(xs)
            r = jnp.dot(w1, hi, preferred_element_type=F32) + jnp.dot(w1, lo, preferred_element_type=F32)
        else:
            r = jnp.dot(w1, xs.astype(BF16), preferred_element_type=F32)
        a_ref[pl.ds(m, npg, stride=2 * n2), :] = r[:npg]
        a_ref[pl.ds(n2 + m, npg, stride=2 * n2), :] = r[npg:]
        return 0

    lax.fori_loop(0, n2, body, 0, unroll=2)


def _hy_fft_kernel(k_ref, w1_ref, g_ref, kf_ref, a_ref, *, n1, n2, npg):
    w1 = w1_ref[...]
    _fft_stage1(lambda m: k_ref[pl.ds(m, n1, stride=n2), :], w1, a_ref, npg, n2, True)

    def body(p, _):
        rows = pl.ds(pl.multiple_of(p * 2 * n2, 2 * n2), 2 * n2)
        hi, lo = _split_bf16(a_ref[rows, :])
        g = g_ref[p]
        kf_ref[rows, :] = jnp.dot(g, hi, preferred_element_type=F32) + jnp.dot(g, lo, preferred_element_type=F32)
        return 0

    lax.fori_loop(0, npg, body, 0, unroll=4)


def hyena_filter_fft(seq_len, kfilt):
    n1, n2 = _fft_factors(seq_len)
    _, npg = _fft_pages(seq_len)
    w1, _, g, _, _ = _dft_tables_bf16(seq_len)
    tc = 128
    kern = functools.partial(_hy_fft_kernel, n1=n1, n2=n2, npg=npg)
    return pl.pallas_call(
        kern,
        grid=((2 * MIX_W) // tc,),
        in_specs=[pl.BlockSpec((n1 * n2, tc), lambda j: (0, j)),
                  pl.BlockSpec((2 * npg, n1), lambda j: (0, 0)),
                  pl.BlockSpec((npg, 2 * n2, 2 * n2), lambda j: (0, 0, 0))],
        out_specs=pl.BlockSpec((npg * 2 * n2, tc), lambda j: (0, j)),
        out_shape=jax.ShapeDtypeStruct((npg * 2 * n2, 2 * MIX_W), F32),
        scratch_shapes=[pltpu.VMEM((npg * 2 * n2, tc), F32)],
        compiler_params=_cp(("parallel",)),
        name=f"hyena_filter_fft_{seq_len}",
    )(kfilt, w1, g)


def _hy_fft_direct_kernel(k_ref, wf_ref, kf_ref):
    hi, lo = _split_bf16(k_ref[...])
    wf = wf_ref[...]
    kf_ref[...] = jnp.dot(wf, hi, preferred_element_type=F32) + jnp.dot(wf, lo, preferred_element_type=F32)


def hyena_filter_fft_direct(seq_len, kfilt):
    n = 2 * seq_len
    wf = jnp.asarray(_direct_dft_tables(seq_len)[0], BF16)
    tc = 256
    return pl.pallas_call(
        _hy_fft_direct_kernel,
        grid=((2 * MIX_W) // tc,),
        in_specs=[pl.BlockSpec((n, tc), lambda j: (0, j)), pl.BlockSpec((2 * n, n), lambda j: (0, 0))],
        out_specs=pl.BlockSpec((2 * n, tc), lambda j: (0, j)),
        out_shape=jax.ShapeDtypeStruct((2 * n, 2 * MIX_W), F32),
        compiler_params=_cp(("parallel",)),
        name=f"hyena_filter_fft_{seq_len}",
    )(kfilt, wf)


def _shift_rows(x, s, row):
    n = x.shape[0]
    y = pltpu.roll(x, s % n, 0)
    if s > 0:
        return jnp.where(row >= s, y, 0.0)
    return jnp.where(row < n + s, y, 0.0)


def _short_conv3(x_ref, w_ref):
    x = x_ref[...].astype(F32)
    row = lax.broadcasted_iota(I32, x.shape, 0)
    w = w_ref[...]
    return w[0:1] * _shift_rows(x, 1, row) + w[1:2] * x + w[2:3] * _shift_rows(x, -1, row)


def _hy_conv_kernel(v_ref, x1_ref, x2_ref, ws_ref, kf_ref, skip_ref, w1_ref, g_ref, gt_ref, w4_ref, y_ref,
                    z_ref, c_ref, a_ref, s_ref, *, n1, n2, used, npg):
    o = pl.program_id(2)
    nh = n1 // 2

    @pl.when(o == 0)
    def _():
        z_ref[...] = _short_conv3(v_ref, ws_ref.at[0])

    w1 = w1_ref[...]

    def stage1(m, _):
        xs = z_ref[pl.ds(m, nh, stride=n2), :].astype(BF16)
        s_ref[pl.ds(pl.multiple_of(m * 2 * npg, 2 * npg), 2 * npg), :] = jnp.dot(w1, xs, preferred_element_type=F32)
        return 0

    lax.fori_loop(0, n2, stage1, 0, unroll=2)
    a_ref[used * 2 * n2:, :] = jnp.zeros(((npg - used) * 2 * n2, a_ref.shape[1]), F32)

    def page(p, _):
        rows = pl.ds(pl.multiple_of(p * 2 * n2, 2 * n2), 2 * n2)
        ap = jnp.concatenate([s_ref[pl.ds(p, n2, stride=2 * npg), :],
                              s_ref[pl.ds(npg + p, n2, stride=2 * npg), :]], axis=0)
        x = jnp.dot(g_ref[p], ap.astype(BF16), preferred_element_type=F32)
        kf = kf_ref[rows, :]
        xr, xi = x[:n2], x[n2:]
        kr, ki = kf[:n2], kf[n2:]
        yc = jnp.concatenate([xr * kr - xi * ki, xr * ki + xi * kr], axis=0).astype(BF16)
        a_ref[rows, :] = jnp.dot(gt_ref[p], yc, preferred_element_type=F32)
        return 0

    lax.fori_loop(0, used, page, 0, unroll=5)

    w4 = w4_ref[...]

    def back(m, _):
        dr = a_ref[pl.ds(m, npg, stride=2 * n2), :]
        di = a_ref[pl.ds(n2 + m, npg, stride=2 * n2), :]
        d = jnp.concatenate([dr, di], axis=0).astype(BF16)
        c_ref[pl.ds(m, nh, stride=n2), :] = jnp.dot(w4, d, preferred_element_type=F32)
        return 0

    lax.fori_loop(0, n2, back, 0, unroll=2)

    @pl.when(o == 0)
    def _():
        z = z_ref[...]
        z_ref[...] = _short_conv3(x1_ref, ws_ref.at[1]) * (c_ref[...] + z * skip_ref[0:1])

    @pl.when(o == 1)
    def _():
        z = z_ref[...]
        y_ref[...] = (_short_conv3(x2_ref, ws_ref.at[2]) * (c_ref[...] + z * skip_ref[1:2])).astype(BF16)


def hyena_conv(proj, row_block_off, nseq, seq_len, w_short3, kf, skip):
    n1, n2 = _fft_factors(seq_len)
    used, npg = _fft_pages(seq_len)
    _, w1h, g, gt, w4 = _dft_tables_bf16(seq_len)
    rows_a = npg * 2 * n2
    tc = 128
    ncb = MIX_W // tc
    kern = functools.partial(_hy_conv_kernel, n1=n1, n2=n2, used=used, npg=npg)
    const3 = lambda c, b, o: (0, 0, 0)
    seg = lambda s: pl.BlockSpec((seq_len, tc), lambda c, b, o: (row_block_off + b, CB_HY + s * ncb + c))
    return pl.pallas_call(
        kern,
        grid=(ncb, nseq, 2),
        in_specs=[seg(0), seg(1), seg(2),
                  pl.BlockSpec((3, 3, tc), lambda c, b, o: (0, 0, c)),
                  pl.BlockSpec((rows_a, tc), lambda c, b, o: (0, o * ncb + c)),
                  pl.BlockSpec((2, tc), lambda c, b, o: (0, c)),
                  pl.BlockSpec((2 * npg, n1 // 2), lambda c, b, o: (0, 0)),
                  pl.BlockSpec((npg, 2 * n2, 2 * n2), const3, pipeline_mode=pl.Buffered(1)),
                  pl.BlockSpec((npg, 2 * n2, 2 * n2), const3, pipeline_mode=pl.Buffered(1)),
                  pl.BlockSpec((n1 // 2, 2 * npg), lambda c, b, o: (0, 0))],
        out_specs=pl.BlockSpec((seq_len, tc), lambda c, b, o: (b, c)),
        out_shape=jax.ShapeDtypeStruct((nseq * seq_len, MIX_W), BF16),
        scratch_shapes=[pltpu.VMEM((seq_len, tc), F32), pltpu.VMEM((seq_len, tc), F32),
                        pltpu.VMEM((rows_a, tc), F32), pltpu.VMEM((rows_a, tc), F32)],
        compiler_params=_cp(("parallel", "parallel", "arbitrary")),
        name=f"hyena_conv_{seq_len}",
    )(proj, proj, proj, w_short3, kf, skip, w1h, g, gt, w4)


def _hy_direct_kernel(u_ref, ws_ref, kf_ref, skip_ref, wf_ref, wi_ref, y_ref, *, n):
    w = MIX_W
    wf = wf_ref[...]
    wi = wi_ref[...]
    z = _short_conv3(u_ref.at[:, pl.ds(0, w)], ws_ref.at[0])
    for o in range(2):
        x = jnp.dot(wf, z.astype(BF16), preferred_element_type=F32)
        kf = kf_ref[:, o * w:(o + 1) * w]
        xr, xi = x[:n], x[n:]
        kr, ki = kf[:n], kf[n:]
        yc = jnp.concatenate([xr * kr - xi * ki, xr * ki + xi * kr], axis=0).astype(BF16)
        conv = jnp.dot(wi, yc, preferred_element_type=F32)
        gate = _short_conv3(u_ref.at[:, pl.ds((o + 1) * w, w)], ws_ref.at[o + 1])
        z = gate * (conv + z * skip_ref[o:o + 1])
    y_ref[...] = z.astype(BF16)


def hyena_conv_direct(proj, row_block_off, nseq, seq_len, w_short3, kf, skip):
    n = 2 * seq_len
    wf_np, wi_np = _direct_dft_tables(seq_len)
    wf = jnp.asarray(wf_np[:, :seq_len], BF16)
    wi = jnp.asarray(wi_np, BF16)
    kern = functools.partial(_hy_direct_kernel, n=n)
    return pl.pallas_call(
        kern,
        grid=(nseq,),
        in_specs=[pl.BlockSpec((seq_len, 3 * MIX_W), lambda b: (row_block_off + b, 0)),
                  pl.BlockSpec((3, 3, MIX_W), lambda b: (0, 0, 0)),
                  pl.BlockSpec((2 * n, 2 * MIX_W), lambda b: (0, 0)),
                  pl.BlockSpec((2, MIX_W), lambda b: (0, 0)),
                  pl.BlockSpec((2 * n, seq_len), lambda b: (0, 0)),
                  pl.BlockSpec((seq_len, 2 * n), lambda b: (0, 0))],
        out_specs=pl.BlockSpec((seq_len, MIX_W), lambda b: (b, 0)),
        out_shape=jax.ShapeDtypeStruct((nseq * seq_len, MIX_W), BF16),
        compiler_params=_cp(("parallel",)),
        name=f"hyena_conv_{seq_len}",
    )(proj, w_short3, kf, skip, wf, wi)


def _dot_nt(a, b):
    return lax.dot_general(a, b, (((1,), (1,)), ((), ())), preferred_element_type=F32)


ML_CHUNK = 256


def _log_sigmoid(x):
    return jnp.minimum(x, 0.0) - jnp.log1p(jnp.exp(-jnp.abs(x)))


def _mlstm_kernel(*refs, seq_len, has_state):
    if has_state:
        (q_ref, k_ref, v_ref, o_ref, gr_ref, gc_ref, br_ref, bc_ref, gain_ref, c0_ref, n0_ref, m0_ref,
         y_ref, cout_ref, nout_ref, mout_ref, hs0, hs1, cs0, cs1, ns0, ns1, ms0, ms1) = refs
    else:
        (q_ref, k_ref, v_ref, o_ref, gr_ref, gc_ref, br_ref, bc_ref, gain_ref,
         y_ref, cout_ref, nout_ref, mout_ref, hs0, hs1, cs0, cs1, ns0, ns1, ms0, ms1) = refs
    hs, cs, ns, ms = (hs0, hs1), (cs0, cs1), (ns0, ns1), (ms0, ms1)
    t = min(ML_CHUNK, seq_len)
    nc = seq_len // t
    scale = ML_DK ** -0.5
    for d in (0, 1):
        if has_state:
            cs[d][...] = c0_ref[d]
            ns[d][...] = n0_ref[d]
            ms[d][...] = m0_ref[d]
        else:
            cs[d][...] = jnp.zeros_like(cs[d])
            ns[d][...] = jnp.zeros_like(ns[d])
            ms[d][...] = jnp.zeros_like(ms[d])
    row = lax.broadcasted_iota(I32, (t, t), 0)
    col = lax.broadcasted_iota(I32, (t, t), 1)
    br = br_ref[...]
    bc = bc_ref[...]
    tris = (jnp.where(row <= col, 1.0, 0.0).astype(BF16), jnp.where(row >= col, 1.0, 0.0).astype(BF16))

    def cumsum_row(f_r, tri):
        f8 = jnp.broadcast_to(f_r, (8, t))
        hi = f8.astype(BF16)
        r1 = f8 - hi.astype(F32)
        mid = r1.astype(BF16)
        lo = (r1 - mid.astype(F32)).astype(BF16)
        out = (jnp.dot(hi, tri, preferred_element_type=F32) + jnp.dot(mid, tri, preferred_element_type=F32)
               + jnp.dot(lo, tri, preferred_element_type=F32))
        return out[0:1]

    def step(j, _):
        for d in (0, 1):
            jj = j if d == 0 else nc - 1 - j
            rows = pl.ds(pl.multiple_of(jj * t, t), t)
            qf = q_ref[rows, :].astype(F32) * scale
            qb = qf.astype(BF16)
            kb = k_ref[rows, :]
            vb = v_ref[rows, :]
            gr = gr_ref[rows, :] + br
            gc = gc_ref[:, rows] + bc
            i_c = gr[:, 2 * d:2 * d + 1]
            i_r = gc[2 * d:2 * d + 1, :]
            f_r = _log_sigmoid(gc[2 * d + 1:2 * d + 2, :])
            mask = (col <= row) if d == 0 else (col >= row)
            b_c = jnp.sum(jnp.where(mask, f_r, 0.0), axis=1, keepdims=True)
            b_r = cumsum_row(f_r, tris[d])
            logw = jnp.where(mask, b_c - b_r + i_r, -jnp.inf)
            m = ms[d][:, 0:1]
            g = b_c + m
            mt = jnp.maximum(g, jnp.max(logw, axis=1, keepdims=True))
            s = _dot_nt(qb, kb) * jnp.exp(logw - mt)
            inter = jnp.exp(g - mt)
            cm = cs[d][...]
            nv = ns[d][...]
            num = (jnp.dot(s.astype(BF16), vb, preferred_element_type=F32)
                   + inter * jnp.dot(qb, cm.astype(BF16), preferred_element_type=F32))
            den = jnp.sum(s, axis=1, keepdims=True) + inter * jnp.sum(qf * nv, axis=1, keepdims=True)
            hs[d][rows, :] = num / jnp.maximum(jnp.abs(den), jnp.exp(-mt))
            bl = jnp.sum(f_r, axis=1, keepdims=True)
            wlog_r = bl - b_r + i_r
            wlog_c = bl - b_c + i_c
            m_new = jnp.maximum(bl + m, jnp.max(wlog_r, axis=1, keepdims=True))
            dec = jnp.exp(bl + m - m_new)
            kw = jnp.exp(wlog_c - m_new) * kb.astype(F32)
            cs[d][...] = dec * cm + jnp.dot(kw.T.astype(BF16), vb, preferred_element_type=F32)
            ns[d][...] = dec * nv + jnp.sum(kw, axis=0, keepdims=True)
            ms[d][...] = jnp.broadcast_to(m_new, (1, LANE))
        return 0

    lax.fori_loop(0, nc, step, 0, unroll=2)
    hsum = hs0[...] + hs1[...]
    hn = hsum * lax.rsqrt(jnp.mean(hsum * hsum, axis=-1, keepdims=True) + EPS) * gain_ref[...]
    y_ref[...] = (jax.nn.sigmoid(o_ref[...].astype(F32)) * hn).astype(BF16)
    for d in (0, 1):
        cout_ref[d] = cs[d][...]
        nout_ref[d] = ns[d][...]
        mout_ref[d] = ms[d][...]


def mlstm_mixer(proj, gates_r, gates_c, bias_r, bias_c, gain, row_block_off, nseq, seq_len, state):
    has_state = state is not None
    kern = functools.partial(_mlstm_kernel, seq_len=seq_len, has_state=has_state)
    col = lambda cb: pl.BlockSpec((seq_len, LANE), lambda b, h: (row_block_off + b, cb + h))
    c_spec = pl.BlockSpec((None, 2, None, ML_DK, ML_DK), lambda b, h: (b, 0, h, 0, 0))
    v_spec = pl.BlockSpec((None, 2, None, 1, LANE), lambda b, h: (b, 0, h, 0, 0))
    in_specs = [col(CB_MLQ), col(CB_MLK), col(CB_MLV), col(CB_MLO),
                pl.BlockSpec((None, seq_len, 4), lambda b, h: (h, row_block_off + b, 0)),
                pl.BlockSpec((None, 4, seq_len), lambda b, h: (h, 0, row_block_off + b)),
                pl.BlockSpec((None, 1, 4), lambda b, h: (h, 0, 0)),
                pl.BlockSpec((None, 4, 1), lambda b, h: (h, 0, 0)),
                pl.BlockSpec((1, LANE), lambda b, h: (0, h))]
    args = [proj, proj, proj, proj, gates_r, gates_c, bias_r, bias_c, gain]
    if has_state:
        in_specs += [c_spec, v_spec, v_spec]
        args += list(state)
    return pl.pallas_call(
        kern,
        grid=(nseq, ML_H),
        in_specs=in_specs,
        out_specs=[pl.BlockSpec((seq_len, LANE), lambda b, h: (b, h)), c_spec, v_spec, v_spec],
        out_shape=[jax.ShapeDtypeStruct((nseq * seq_len, MIX_W), BF16),
                   jax.ShapeDtypeStruct((nseq, 2, ML_H, ML_DK, ML_DK), F32),
                   jax.ShapeDtypeStruct((nseq, 2, ML_H, 1, LANE), F32),
                   jax.ShapeDtypeStruct((nseq, 2, ML_H, 1, LANE), F32)],
        scratch_shapes=([pltpu.VMEM((seq_len, LANE), F32)] * 2 + [pltpu.VMEM((ML_DK, ML_DK), F32)] * 2
                        + [pltpu.VMEM((1, LANE), F32)] * 4),
        compiler_params=_cp(("parallel", "parallel")),
        name=f"mlstm_{seq_len}",
    )(*args)


def _softplus(x):
    return jnp.maximum(x, 0.0) + jnp.log1p(jnp.exp(-jnp.abs(x)))


def _gelu_tanh(x):
    return 0.5 * x * (1.0 + jnp.tanh(math.sqrt(2.0 / math.pi) * (x + 0.044715 * (x * x * x))))


def _lru_kernel(*refs, seq_len, has_state):
    if has_state:
        (x_ref, y_ref, cw_ref, cb_ref, wa_ref, ba_ref, wx_ref, bx_ref, lam_ref, h0_ref,
         out_ref, hl_ref, a_s, b_s, hsum, hsum_b) = refs
    else:
        (x_ref, y_ref, cw_ref, cb_ref, wa_ref, ba_ref, wx_ref, bx_ref, lam_ref,
         out_ref, hl_ref, a_s, b_s, hsum, hsum_b) = refs
    xf = x_ref[...].astype(F32)
    row = lax.broadcasted_iota(I32, xf.shape, 0)
    w = cw_ref[...]
    x = (w[0:1] * _shift_rows(xf, 2, row) + w[1:2] * _shift_rows(xf, 1, row) + w[2:3] * xf
         + w[3:4] * _shift_rows(xf, -1, row) + cb_ref[...])
    xb = x.astype(BF16)
    for d in (0, 1):
        r = jax.nn.sigmoid(jnp.dot(xb, wa_ref[d].astype(BF16), preferred_element_type=F32) + ba_ref[d])
        i = jax.nn.sigmoid(jnp.dot(xb, wx_ref[d].astype(BF16), preferred_element_type=F32) + bx_ref[d])
        log_a = (-LRU_C) * r * _softplus(-lam_ref[d])
        a_s[d] = jnp.exp(log_a)
        b_s[d] = jnp.sqrt(jnp.maximum(1.0 - jnp.exp(2.0 * log_a), 0.0)) * (i * x)

    nb = seq_len // 8
    sub = lax.broadcasted_iota(I32, (8, LANE), 0)

    def block_scan(a, b, reverse):
        for s in (1, 2, 4):
            sh = 8 - s if reverse else s
            ok = (sub < 8 - s) if reverse else (sub >= s)
            b = jnp.where(ok, a * pltpu.roll(b, sh, 0) + b, b)
            a = jnp.where(ok, a * pltpu.roll(a, sh, 0), a)
        return a, b

    def both(i, carry):
        cf, cb = carry
        rows_f = pl.ds(pl.multiple_of(i * 8, 8), 8)
        rows_b = pl.ds(pl.multiple_of((nb - 1 - i) * 8, 8), 8)
        a, b = block_scan(a_s[0, rows_f, :], b_s[0, rows_f, :], False)
        hf = a * cf + b
        hsum[rows_f, :] = hf
        a, b = block_scan(a_s[1, rows_b, :], b_s[1, rows_b, :], True)
        hb = a * cb + b
        hsum_b[rows_b, :] = hb
        return hf[7:8, :], hb[0:1, :]

    zero = jnp.zeros((1, LANE), F32)
    init = (h0_ref[0], h0_ref[1]) if has_state else (zero, zero)
    hf, hb = lax.fori_loop(0, nb, both, init, unroll=4)
    out_ref[...] = ((hsum[...] + hsum_b[...]) * _gelu_tanh(y_ref[...].astype(F32))).astype(BF16)
    hl_ref[0] = hf
    hl_ref[1] = hb


def rglru_mixer(proj, conv_w, conv_b, wa, ba, wx, bx, lam, row_block_off, nseq, seq_len, h0):
    has_state = h0 is not None
    kern = functools.partial(_lru_kernel, seq_len=seq_len, has_state=has_state)
    col = lambda cb: pl.BlockSpec((seq_len, LANE), lambda b, h: (row_block_off + b, cb + h))
    w_spec = pl.BlockSpec((2, None, LANE, LANE), lambda b, h: (0, h, 0, 0))
    v_spec = pl.BlockSpec((2, 1, LANE), lambda b, h: (0, 0, h))
    s_spec = pl.BlockSpec((None, 2, 1, LANE), lambda b, h: (b, 0, 0, h))
    in_specs = [col(CB_LRX), col(CB_LRY),
                pl.BlockSpec((4, LANE), lambda b, h: (0, h)),
                pl.BlockSpec((1, LANE), lambda b, h: (0, h)),
                w_spec, v_spec, w_spec, v_spec, v_spec]
    args = [proj, proj, conv_w, conv_b, wa, ba, wx, bx, lam]
    if has_state:
        in_specs.append(s_spec)
        args.append(h0)
    return pl.pallas_call(
        kern,
        grid=(nseq, LRU_H),
        in_specs=in_specs,
        out_specs=[pl.BlockSpec((seq_len, LANE), lambda b, h: (b, h)), s_spec],
        out_shape=[jax.ShapeDtypeStruct((nseq * seq_len, MIX_W), BF16),
                   jax.ShapeDtypeStruct((nseq, 2, 1, MIX_W), F32)],
        scratch_shapes=[pltpu.VMEM((2, seq_len, LANE), F32), pltpu.VMEM((2, seq_len, LANE), F32),
                        pltpu.VMEM((seq_len, LANE), F32), pltpu.VMEM((seq_len, LANE), F32)],
        compiler_params=_cp(("parallel", "parallel")),
        name=f"rglru_{seq_len}",
    )(*args)


@functools.lru_cache(maxsize=None)
def _rope_tables():
    t = np.arange(DEC_SEQ)
    pos_row, pos_col = (t // GRID_W).astype(np.float64), (t % GRID_W).astype(np.float64)
    half = DA_DK // 2
    inv = ROPE_BASE ** (-np.arange(0, half, 2, dtype=np.float64) / half)
    lane = np.arange(LANE)
    sub = lane % DA_DK
    pos = np.where((sub < half)[None, :], pos_row[:, None], pos_col[:, None])
    ang = pos * inv[sub % (half // 2)][None, :]
    first = (sub % half) < (half // 2)
    cos = np.cos(ang)
    sin = np.where(first[None, :], -np.sin(ang), np.sin(ang))
    cos_all = np.concatenate([np.ones((T_CTX, LANE))] + [cos] * DEC_BATCH, axis=0)
    sin_all = np.concatenate([np.zeros((T_CTX, LANE))] + [sin] * DEC_BATCH, axis=0)
    pm = np.kron(np.eye(2), np.full((DA_DK, DA_DK), 1.0 / DA_DK))
    return np.asarray(cos_all, np.float32), np.asarray(sin_all, np.float32), np.asarray(pm, np.float32)


def _qkprep_kernel(q_ref, k_ref, cos_ref, sin_ref, qg_ref, kg_ref, pm_ref, qo_ref, ko_ref, kn_ref):
    pmb = pm_ref[...].astype(BF16)
    cos = cos_ref[...]
    sin = sin_ref[...]
    lane = lax.broadcasted_iota(I32, cos.shape, 1)
    first = (lane % (DA_DK // 2)) < (DA_DK // 4)

    def norm(x, g):
        hi, lo = _split_bf16(x * x)
        ms = jnp.dot(hi, pmb, preferred_element_type=F32) + jnp.dot(lo, pmb, preferred_element_type=F32)
        return x * lax.rsqrt(ms + EPS) * g

    def rope(x):
        partner = jnp.where(first, pltpu.roll(x, LANE - DA_DK // 4, 1), pltpu.roll(x, DA_DK // 4, 1))
        return x * cos + partner * sin

    q = norm(q_ref[...].astype(F32), qg_ref[...])
    k = norm(k_ref[...].astype(F32), kg_ref[...])
    kn_ref[...] = k
    qo_ref[...] = (rope(q) * (DA_DK ** -0.5)).astype(BF16)
    ko_ref[...] = rope(k).astype(BF16)


def qk_prepare(proj, q_gain, k_gain):
    cos, sin, pm = (jnp.asarray(t) for t in _rope_tables())
    tm = 512
    blk = lambda cb: pl.BlockSpec((tm, LANE), lambda i, h: (i, cb + h))
    tab = pl.BlockSpec((tm, LANE), lambda i, h: (i, 0))
    one = pl.BlockSpec((1, LANE), lambda i, h: (0, 0))
    out = pl.BlockSpec((tm, LANE), lambda i, h: (i, h))
    return pl.pallas_call(
        _qkprep_kernel,
        grid=(T_ALL // tm, DA_H),
        in_specs=[blk(CB_DAQ), blk(CB_DAK), tab, tab, one, one, pl.BlockSpec((LANE, LANE), lambda i, h: (0, 0))],
        out_specs=[out, out, out],
        out_shape=[jax.ShapeDtypeStruct((T_ALL, MIX_W), BF16), jax.ShapeDtypeStruct((T_ALL, MIX_W), BF16),
                   jax.ShapeDtypeStruct((T_ALL, MIX_W), F32)],
        compiler_params=_cp(("parallel", "parallel")),
        name="qk_prepare",
    )(proj, proj, cos, sin, q_gain, k_gain, pm)


def _attn_kernel(*refs, has_cache, lam_init, seq_len):
    if has_cache:
        q_ref, k_ref, v_ref, kc_ref, vc_ref, lp_ref, sg_ref, o_ref, ka_ref, va_ref = refs
    else:
        q_ref, k_ref, v_ref, lp_ref, sg_ref, o_ref, ka_ref, va_ref = refs

    @pl.when(pl.program_id(2) == 0)
    def _():
        ka_ref[0:seq_len, :] = k_ref[...]
        va_ref[0:seq_len, 0:LANE] = v_ref[...]
        if has_cache:
            ka_ref[seq_len:, :] = kc_ref[...].astype(BF16)
            va_ref[seq_len:, 0:LANE] = vc_ref[...].astype(BF16)
        va_ref[:, LANE:] = jnp.ones((va_ref.shape[0], LANE), BF16)

    q = q_ref[...]
    lane = lax.broadcasted_iota(I32, q.shape, 1)
    zero = jnp.zeros_like(q)
    lp = lp_ref[...]
    lam = (jnp.exp(jnp.sum(lp[0:1] * lp[1:2], axis=1, keepdims=True))
           - jnp.exp(jnp.sum(lp[2:3] * lp[3:4], axis=1, keepdims=True)) + lam_init)

    def softmax_half(qc):
        s = _dot_nt(qc, ka_ref[...])
        p = jnp.exp(s - jnp.max(s, axis=-1, keepdims=True)).astype(BF16)
        r = jnp.dot(p, va_ref[...], preferred_element_type=F32)
        return r[:, :LANE] / r[:, LANE:]

    o = softmax_half(jnp.where(lane < DA_DK, q, zero)) - lam * softmax_half(jnp.where(lane >= DA_DK, q, zero))
    o = o * lax.rsqrt(jnp.mean(o * o, axis=-1, keepdims=True) + EPS) * sg_ref[...]
    o_ref[...] = (o * (1.0 - lam_init)).astype(BF16)


def diff_attention(q_rot, k_rot, proj, lam_p, sub_gain, lam_init, row_block_off, nseq, seq_len, cache, layer):
    has_cache = cache is not None
    tq = 256
    kern = functools.partial(_attn_kernel, has_cache=has_cache, lam_init=lam_init, seq_len=seq_len)
    nq = seq_len // tq
    n_keys = seq_len + (cache[0].shape[2] if has_cache else 0)
    kv = lambda cb: pl.BlockSpec((seq_len, LANE), lambda b, h, i: (row_block_off + b, cb + h))
    in_specs = [pl.BlockSpec((tq, LANE), lambda b, h, i: ((row_block_off + b) * nq + i, h)), kv(0), kv(CB_DAV)]
    args = [q_rot, k_rot, proj]
    if has_cache:
        past = cache[0].shape[2]
        cspec = pl.BlockSpec((None, None, past, LANE), lambda b, h, i: (b, layer, 0, h))
        in_specs += [cspec, cspec]
        args += list(cache)
    in_specs += [pl.BlockSpec((4, DA_DK), lambda b, h, i: (0, 0)), pl.BlockSpec((1, LANE), lambda b, h, i: (0, 0))]
    args += [lam_p, sub_gain]
    return pl.pallas_call(
        kern,
        grid=(nseq, DA_H, nq),
        in_specs=in_specs,
        out_specs=pl.BlockSpec((tq, LANE), lambda b, h, i: (b * nq + i, h)),
        out_shape=jax.ShapeDtypeStruct((nseq * seq_len, MIX_W), BF16),
        scratch_shapes=[pltpu.VMEM((n_keys, LANE), BF16), pltpu.VMEM((n_keys, 2 * LANE), BF16)],
        compiler_params=_cp(("parallel", "parallel", "arbitrary")),
        name=f"diff_attention_{seq_len}",
    )(*args)


def _merge_kernel(h_ref, y0_ref, y1_ref, y2_ref, y3_ref, wg0_ref, wg1_ref, wg2_ref, wg3_ref, wb_ref, o_ref):
    h = h_ref[...]
    acc = None
    for bi, (y_ref, wg_ref) in enumerate(((y0_ref, wg0_ref), (y1_ref, wg1_ref), (y2_ref, wg2_ref), (y3_ref, wg3_ref))):
        gate = jax.nn.sigmoid(jnp.dot(h, wg_ref[...], preferred_element_type=F32))
        term = gate * jnp.dot(y_ref[...], wb_ref[bi], preferred_element_type=F32)
        acc = term if acc is None else acc + term
    o_ref[...] = acc.astype(BF16)


def branch_merge(h, ys, w_gate, w_branch):
    tm, tn = 1024, 256
    nj = D_MODEL // tn
    yspec = pl.BlockSpec((tm, MIX_W), lambda i, j: (i, 0))
    gspec = lambda bi: pl.BlockSpec((D_MODEL, tn), lambda i, j: (0, bi * nj + j))
    return pl.pallas_call(
        _merge_kernel,
        grid=(T_ALL // tm, nj),
        in_specs=[pl.BlockSpec((tm, D_MODEL), lambda i, j: (i, 0)), yspec, yspec, yspec, yspec,
                  gspec(0), gspec(1), gspec(2), gspec(3),
                  pl.BlockSpec((N_BRANCH, MIX_W, tn), lambda i, j: (0, 0, j))],
        out_specs=pl.BlockSpec((tm, tn), lambda i, j: (i, j)),
        out_shape=jax.ShapeDtypeStruct((T_ALL, D_MODEL), BF16),
        compiler_params=_cp(("parallel", "parallel")),
        name="branch_merge",
    )(h, *ys, w_gate, w_gate, w_gate, w_gate, w_branch)


def _outproj_kernel(m_ref, x_ref, g1_ref, w_ref, n2_ref, sh2_ref, sc2_ref, rt_ref, rl_ref, x1_ref, h2_ref, lg_ref):
    half = m_ref.shape[0] // 2
    for r in (0, half):
        rows = slice(r, r + half)
        y = jnp.dot(m_ref[rows, :], w_ref[...], preferred_element_type=F32)
        x1 = x_ref[rows, :] + g1_ref[...] * y
        x1_ref[rows, :] = x1
        h2 = _norm_mod(x1, n2_ref[...], sc2_ref[...], sh2_ref[...])
        h2_ref[rows, :] = _pack_bf16_pairs(h2)
        hi, lo = _split_bf16(h2)
        lg_ref[rows, :] = (jnp.dot(hi, rt_ref[...], preferred_element_type=F32)
                           + jnp.dot(lo, rt_ref[...], preferred_element_type=F32)
                           + jnp.dot(hi, rl_ref[...], preferred_element_type=F32))


def out_projection(merged, x, mod4, layer, w_out, norm2_g, router):
    tm = 256
    router_hi = router.astype(BF16)
    router_lo = (router - router_hi.astype(F32)).astype(BF16)
    return pl.pallas_call(
        _outproj_kernel,
        grid=(T_ALL // tm,),
        in_specs=[pl.BlockSpec((tm, D_MODEL), lambda i: (i, 0)),
                  pl.BlockSpec((tm, D_MODEL), lambda i: (i, 0)),
                  _mod_spec(layer, 2, tm, 1),
                  pl.BlockSpec((D_MODEL, D_MODEL), lambda i: (0, 0)),
                  pl.BlockSpec((None, 1, D_MODEL), lambda i: (layer, 0, 0)),
                  _mod_spec(layer, 3, tm, 1),
                  _mod_spec(layer, 4, tm, 1),
                  pl.BlockSpec((D_MODEL, LANE), lambda i: (0, 0)),
                  pl.BlockSpec((D_MODEL, LANE), lambda i: (0, 0))],
        out_specs=[pl.BlockSpec((tm, D_MODEL), lambda i: (i, 0)),
                   pl.BlockSpec((tm, D_MODEL // 2), lambda i: (i, 0)),
                   pl.BlockSpec((tm, LANE), lambda i: (i, 0))],
        out_shape=[jax.ShapeDtypeStruct((T_ALL, D_MODEL), F32),
                   jax.ShapeDtypeStruct((T_ALL, D_MODEL // 2), I32),
                   jax.ShapeDtypeStruct((T_ALL, LANE), F32)],
        compiler_params=_cp(("parallel",)),
        name="out_projection",
    )(merged, x, mod4, w_out, norm2_g, mod4, mod4, router_hi, router_lo)


MOE_TM = 256
N_ASSIGN = T_ALL * TOP_K
MOE_BLOCKS = (N_ASSIGN + N_EXPERTS * (MOE_TM - 1)) // MOE_TM + 1
ROUTER_TT = 512
GROUP_SIZE = N_EXPERTS // N_GROUPS


def _router_kernel(lg_ref, bias_ref, tri_ref, eidx_ref, w_ref, rank_ref, cnt_ref, carry_ref):
    i = pl.program_id(0)

    @pl.when(i == 0)
    def _():
        carry_ref[...] = jnp.zeros_like(carry_ref)

    tt = lg_ref.shape[0]
    shape3 = (N_GROUPS, GROUP_SIZE, tt)
    neg = -jnp.inf
    logits_t = lg_ref[...].T[:N_EXPERTS]
    scores = jax.nn.sigmoid(logits_t).reshape(shape3)
    sel = scores + bias_ref[...].reshape(N_GROUPS, GROUP_SIZE, 1)
    gi = lax.broadcasted_iota(I32, shape3, 0)
    ji = lax.broadcasted_iota(I32, shape3, 1)
    ei = gi * GROUP_SIZE + ji
    m1 = jnp.max(sel, axis=1, keepdims=True)
    first = jnp.min(jnp.where(sel == m1, ji, GROUP_SIZE), axis=1, keepdims=True)
    m2 = jnp.max(jnp.where(ji == first, neg, sel), axis=1, keepdims=True)
    cur = m1 + m2
    g1 = lax.broadcasted_iota(I32, cur.shape, 0)
    gmask = jnp.zeros(cur.shape, jnp.bool_)
    for _ in range(TOPK_GROUPS):
        mx = jnp.max(cur, axis=0, keepdims=True)
        idx = jnp.min(jnp.where(cur == mx, g1, N_GROUPS), axis=0, keepdims=True)
        hit = g1 == idx
        gmask = jnp.logical_or(gmask, hit)
        cur = jnp.where(hit, neg, cur)
    masked = jnp.where(gmask, sel, neg)

    def all_max(a):
        return jnp.max(jnp.max(a, axis=1, keepdims=True), axis=0, keepdims=True)

    def all_min(a):
        return jnp.min(jnp.min(a, axis=1, keepdims=True), axis=0, keepdims=True)

    def all_sum(a):
        return jnp.sum(jnp.sum(a, axis=1, keepdims=True), axis=0, keepdims=True)

    hits, idxs, ws = [], [], []
    for _ in range(TOP_K):
        mx = all_max(masked)
        idx = all_min(jnp.where(masked == mx, ei, N_EXPERTS))
        hit = ei == idx
        hits.append(hit)
        idxs.append(idx)
        ws.append(all_sum(jnp.where(hit, scores, 0.0)))
        masked = jnp.where(hit, neg, masked)
    wsum = ws[0]
    for wk in ws[1:]:
        wsum = wsum + wk
    onehot = jnp.zeros(shape3, F32)
    for hit in hits:
        onehot = onehot + hit.astype(F32)
    oh2 = onehot.reshape(N_EXPERTS, tt)
    before = jnp.dot(oh2.astype(BF16), tri_ref[...], preferred_element_type=F32) + carry_ref[:, 0:1]
    before3 = before.reshape(shape3)
    for k in range(TOP_K):
        eidx_ref[k:k + 1, :] = idxs[k].reshape(1, tt)
        w_ref[k:k + 1, :] = (ws[k] / wsum * ROUTED_SCALE).reshape(1, tt)
        rank_ref[k:k + 1, :] = all_sum(jnp.where(hits[k], before3, 0.0)).reshape(1, tt).astype(I32)
    eidx_ref[TOP_K:, :] = jnp.zeros((8 - TOP_K, tt), I32)
    w_ref[TOP_K:, :] = jnp.zeros((8 - TOP_K, tt), F32)
    rank_ref[TOP_K:, :] = jnp.zeros((8 - TOP_K, tt), I32)
    carry_ref[...] = carry_ref[...] + jnp.sum(oh2, axis=1, keepdims=True)
    cnt_ref[...] = carry_ref[...]


def moe_route(logits_t, bias):
    tt = ROUTER_TT
    tri = jnp.asarray(np.triu(np.ones((tt, tt), np.float32), 1), BF16)
    row8 = pl.BlockSpec((8, tt), lambda i: (0, i))
    return pl.pallas_call(
        _router_kernel,
        grid=(T_ALL // tt,),
        in_specs=[pl.BlockSpec((tt, LANE), lambda i: (i, 0)),
                  pl.BlockSpec((N_EXPERTS, 1), lambda i: (0, 0)),
                  pl.BlockSpec((tt, tt), lambda i: (0, 0))],
        out_specs=[row8, row8, row8, pl.BlockSpec((N_EXPERTS, LANE), lambda i: (0, 0))],
        out_shape=[jax.ShapeDtypeStruct((8, T_ALL), I32), jax.ShapeDtypeStruct((8, T_ALL), F32),
                   jax.ShapeDtypeStruct((8, T_ALL), I32), jax.ShapeDtypeStruct((N_EXPERTS, LANE), F32)],
        scratch_shapes=[pltpu.VMEM((N_EXPERTS, LANE), F32)],
        compiler_params=_cp(("arbitrary",)),
        name="moe_route",
    )(logits_t, bias, tri)


N_SLOTS = MOE_BLOCKS * MOE_TM
D_PACK = D_MODEL // 2


def _pack_bf16_pairs(x):
    n = x.shape[1] // 2
    hi = pltpu.bitcast(x[:, :n].astype(BF16).astype(F32), I32)
    lo = pltpu.bitcast(x[:, n:].astype(BF16).astype(F32), I32)
    return hi | lax.shift_right_logical(lo, 16)


def _unpack_bf16_pairs(u):
    hi = pltpu.bitcast(u & jnp.int32(-65536), F32)
    lo = pltpu.bitcast(lax.shift_left(u, 16), F32)
    return jnp.concatenate([hi, lo], axis=1)


def _dispatch_kernel(slots_ref, h2p_ref, xs_hbm, sem, *, tm):
    i = pl.program_id(0)

    def body(r, _):
        for k in range(TOP_K):
            s = slots_ref[(i * tm + r) * TOP_K + k]
            pltpu.make_async_copy(h2p_ref.at[pl.ds(r, 1), :], xs_hbm.at[pl.ds(s, 1), :], sem).start()
        return 0

    lax.fori_loop(0, tm, body, 0)
    for k in range(TOP_K):
        pltpu.make_async_copy(h2p_ref, xs_hbm.at[pl.ds(0, tm), :], sem).wait()


def moe_dispatch(slots, h2p):
    tm = 256
    return pl.pallas_call(
        functools.partial(_dispatch_kernel, tm=tm),
        grid_spec=pltpu.PrefetchScalarGridSpec(
            num_scalar_prefetch=1,
            grid=(T_ALL // tm,),
            in_specs=[pl.BlockSpec((tm, D_PACK), lambda i, s: (i, 0))],
            out_specs=pl.BlockSpec(memory_space=pl.ANY),
            scratch_shapes=[pltpu.SemaphoreType.DMA(())]),
        out_shape=jax.ShapeDtypeStruct((N_SLOTS, D_PACK), I32),
        compiler_params=_cp(("arbitrary",), unchecked_dma=True),
        name="moe_dispatch",
    )(slots, h2p)


def _gmm_kernel(be_ref, bv_ref, first_ref, par_ref, nxt_ref, x_ref, wg_hbm, wu_hbm, wd_hbm, y_ref,
                wgf, wuf, wdf, wgb, wub, wdb, wsem, *, layer):
    i = pl.program_id(0)

    def fetch(e, slot):
        return (pltpu.make_async_copy(wg_hbm.at[layer, e], wgf.at[slot], wsem.at[slot]),
                pltpu.make_async_copy(wu_hbm.at[layer, e], wuf.at[slot], wsem.at[slot]),
                pltpu.make_async_copy(wd_hbm.at[layer, e], wdf.at[slot], wsem.at[slot]))

    @pl.when(i == 0)
    def _():
        for cp in fetch(be_ref[0], 0):
            cp.start()

    @pl.when(first_ref[i] == 1)
    def _():
        slot = par_ref[i]
        for cp in fetch(be_ref[i], slot):
            cp.wait()
        nxt = nxt_ref[i]

        @pl.when(nxt >= 0)
        def _():
            for cp in fetch(nxt, 1 - slot):
                cp.start()

        step = 512
        for r in range(0, D_MODEL, step):
            wgb[r:r + step, :] = wgf[slot, r:r + step, :].astype(BF16)
            wub[r:r + step, :] = wuf[slot, r:r + step, :].astype(BF16)
        for r in range(0, D_EXPERT, 128):
            wdb[r:r + 128, :] = wdf[slot, r:r + 128, :].astype(BF16)

    nv = bv_ref[i]

    @pl.when(nv == 0)
    def _():
        y_ref[...] = jnp.zeros_like(y_ref)

    @pl.when(nv > 0)
    def _():
        row = lax.broadcasted_iota(I32, (x_ref.shape[0], 1), 0)
        xin = jnp.where(row < nv, x_ref[...], 0)
        x = _unpack_bf16_pairs(xin).astype(BF16)
        g = jnp.dot(x, wgb[...], preferred_element_type=F32)
        u = jnp.dot(x, wub[...], preferred_element_type=F32)
        a = (g * jax.nn.sigmoid(g) * u).astype(BF16)
        y_ref[...] = _pack_bf16_pairs(jnp.dot(a, wdb[...], preferred_element_type=F32))


def moe_experts(block_e, block_valid, block_first, block_par, block_next, xs, w_gate, w_up, w_down, layer):
    tm = MOE_TM
    row = pl.BlockSpec((tm, D_PACK), lambda i, *_: (i, 0))
    anyspace = pl.BlockSpec(memory_space=pl.ANY)
    return pl.pallas_call(
        functools.partial(_gmm_kernel, layer=layer),
        grid_spec=pltpu.PrefetchScalarGridSpec(
            num_scalar_prefetch=5,
            grid=(MOE_BLOCKS,),
            in_specs=[row, anyspace, anyspace, anyspace],
            out_specs=row,
            scratch_shapes=[pltpu.VMEM((2, D_MODEL, D_EXPERT), F32), pltpu.VMEM((2, D_MODEL, D_EXPERT), F32),
                            pltpu.VMEM((2, D_EXPERT, D_MODEL), F32),
                            pltpu.VMEM((D_MODEL, D_EXPERT), BF16), pltpu.VMEM((D_MODEL, D_EXPERT), BF16),
                            pltpu.VMEM((D_EXPERT, D_MODEL), BF16), pltpu.SemaphoreType.DMA((2,))]),
        out_shape=jax.ShapeDtypeStruct((N_SLOTS, D_PACK), I32),
        compiler_params=_cp(("arbitrary",)),
        name="moe_experts",
    )(block_e, block_valid, block_first, block_par, block_next, xs, w_gate, w_up, w_down)


def _combine_kernel(slots_ref, x1_ref, h2p_ref, w_ref, g2_ref, sg_ref, su_ref, sd_ref, ys_hbm, out_ref, buf, sem,
                    *, tm, nsteps):
    i = pl.program_id(0)

    def issue(step, slot):
        def body(r, _):
            for k in range(TOP_K):
                s = slots_ref[(step * tm + r) * TOP_K + k]
                pltpu.make_async_copy(ys_hbm.at[pl.ds(s, 1), :], buf.at[slot, pl.ds(k * tm + r, 1), :],
                                      sem.at[slot]).start()
            return 0

        lax.fori_loop(0, tm, body, 0)

    @pl.when(i == 0)
    def _():
        issue(0, 0)

    @pl.when(i + 1 < nsteps)
    def _():
        issue(i + 1, (i + 1) % 2)

    slot = i % 2
    pltpu.make_async_copy(ys_hbm.at[pl.ds(0, TOP_K * tm), :], buf.at[slot], sem.at[slot]).wait()
    hb = _unpack_bf16_pairs(h2p_ref[...]).astype(BF16)
    g = jnp.dot(hb, sg_ref[...], preferred_element_type=F32)
    u = jnp.dot(hb, su_ref[...], preferred_element_type=F32)
    acc = jnp.dot((g * jax.nn.sigmoid(g) * u).astype(BF16), sd_ref[...], preferred_element_type=F32)
    w = w_ref[...]
    for k in range(TOP_K):
        acc = acc + w[:, k:k + 1] * _unpack_bf16_pairs(buf[slot, k * tm:(k + 1) * tm, :])
    out_ref[...] = x1_ref[...] + g2_ref[...] * acc


def moe_combine(slots, x1, h2p, w_tok, mod4, layer, sg, su, sd, ys):
    tm = 256
    nsteps = T_ALL // tm
    kern = functools.partial(_combine_kernel, tm=tm, nsteps=nsteps)
    row = lambda n: pl.BlockSpec((tm, n), lambda i, s: (i, 0))
    return pl.pallas_call(
        kern,
        grid_spec=pltpu.PrefetchScalarGridSpec(
            num_scalar_prefetch=1,
            grid=(nsteps,),
            in_specs=[row(D_MODEL), row(D_PACK), row(8),
                      pl.BlockSpec((None, None, 1, D_MODEL),
                                   lambda i, s: (layer, _mod_row(i, tm) * 6 + 5, 0, 0)),
                      pl.BlockSpec((D_MODEL, D_EXPERT), lambda i, s: (0, 0)),
                      pl.BlockSpec((D_MODEL, D_EXPERT), lambda i, s: (0, 0)),
                      pl.BlockSpec((D_EXPERT, D_MODEL), lambda i, s: (0, 0)),
                      pl.BlockSpec(memory_space=pl.ANY)],
            out_specs=row(D_MODEL),
            scratch_shapes=[pltpu.VMEM((2, TOP_K * tm, D_PACK), I32), pltpu.SemaphoreType.DMA((2,))]),
        out_shape=jax.ShapeDtypeStruct((T_ALL, D_MODEL), F32),
        compiler_params=_cp(("arbitrary",), unchecked_dma=True),
        name="moe_combine",
    )(slots, x1, h2p, w_tok, mod4, sg, su, sd, ys)


def moe_ffn(x1, h2p, logits_t, mod4, layer, moe_bias, w_gate, w_up, w_down, sg, su, sd):
    eidx, w_t, rank, cnt = moe_route(logits_t, moe_bias)
    counts = cnt[:, 0].astype(I32)
    padded = (counts + MOE_TM - 1) // MOE_TM * MOE_TM
    pad_end = jnp.cumsum(padded)
    pad_start = pad_end - padded
    experts = jnp.arange(N_EXPERTS, dtype=I32)
    base = jnp.sum(jnp.where(eidx[:TOP_K, :, None] == experts, pad_start, 0), axis=-1)
    slots = (base + rank[:TOP_K]).T.reshape(-1)
    starts = jnp.arange(MOE_BLOCKS, dtype=I32) * MOE_TM
    block_e = jnp.minimum(jnp.sum((pad_end[None, :] <= starts[:, None]).astype(I32), axis=1), N_EXPERTS - 1)
    mine = block_e[:, None] == experts
    left = jnp.sum(jnp.where(mine, counts + pad_start, 0), axis=1) - starts
    block_valid = jnp.clip(left, 0, MOE_TM).astype(I32)
    nonempty = counts > 0
    pick = lambda v: jnp.sum(jnp.where(mine, v, 0), axis=1)
    block_first = jnp.logical_and(starts == pick(pad_start), block_valid > 0).astype(I32)
    block_par = pick(jnp.cumsum(nonempty.astype(I32)) - 1) % 2
    later = lax.cummin(jnp.where(nonempty, experts, N_EXPERTS), axis=0, reverse=True)
    nxt_e = jnp.concatenate([later[1:], jnp.full((1,), N_EXPERTS, I32)])
    block_next = pick(jnp.where(nxt_e >= N_EXPERTS, -1, nxt_e))
    xs = moe_dispatch(slots, h2p)
    ys = moe_experts(block_e, block_valid, block_first, block_par, block_next, xs, w_gate, w_up, w_down, layer)
    return moe_combine(slots, x1, h2p, w_t.T, mod4, layer, sg, su, sd, ys)


def _ml_gate_layouts(gp, gate_bias):
    nt = gp.shape[0]
    g16 = gp[:, :4 * ML_H].reshape(nt, 2, 2, ML_H)
    gr = jnp.transpose(g16, (3, 0, 1, 2)).reshape(ML_H, nt, 4)
    gc = jnp.transpose(gr, (0, 2, 1))
    b = jnp.transpose(gate_bias, (2, 0, 1)).reshape(ML_H, 4)
    return gr, gc, b[:, None, :], b[:, :, None]


def kernel(x_prompt, x_sample, c, cache_k, cache_v, state_mlstm_C, state_mlstm_n, state_mlstm_m, state_rglru,
           c_ctx, w_mod, b_mod, norm1, norm2, w_in, hy_short, hy_w1, hy_b1, hy_w2, hy_b2, hy_w3, hy_freq,
           hy_decay, hy_skip, ml_gate_bias, ml_out_norm, lru_conv_w, lru_conv_b, lru_wa, lru_ba, lru_wx, lru_bx,
           lru_lambda, da_q_norm, da_k_norm, da_lambda, da_sub_norm, w_branch, w_out, moe_router, moe_bias,
           moe_w_gate, moe_w_up, moe_w_down, sh_w_gate, sh_w_up, sh_w_down):
    x = jnp.concatenate([x_prompt.reshape(T_CTX, D_MODEL), x_sample.reshape(T_DEN, D_MODEL)], axis=0)
    cvecs = jnp.concatenate([c_ctx[None], c, jnp.zeros((N_MOD_ROWS - 1 - DEC_BATCH, D_MODEL), F32)], axis=0)
    mod4 = modulation_all(cvecs, w_mod, b_mod).reshape(DEPTH, N_MOD_ROWS * 6, 1, D_MODEL)
    norm1_3 = norm1.reshape(DEPTH, 1, D_MODEL)
    norm2_3 = norm2.reshape(DEPTH, 1, D_MODEL)
    cache_k4 = cache_k.reshape(DEC_BATCH, DEPTH, -1, MIX_W)
    cache_v4 = cache_v.reshape(DEC_BATCH, DEPTH, -1, MIX_W)
    den_off = T_CTX // DEC_SEQ
    new_k, new_v, new_c, new_n, new_m, new_h = [], [], [], [], [], []
    for l in range(DEPTH):
        lam_init = 0.8 - 0.6 * math.exp(-0.3 * l)
        wl = w_in[l]
        n_ml = 4 * ML_H
        w_main = jnp.concatenate([wl[:, :7 * MIX_W], wl[:, 7 * MIX_W + n_ml:12 * MIX_W + n_ml]], axis=1).astype(BF16)
        w_g16 = jnp.pad(wl[:, 7 * MIX_W:7 * MIX_W + n_ml], ((0, 0), (0, LANE - n_ml))).astype(BF16)
        w_gate = wl[:, 12 * MIX_W + n_ml:].astype(BF16)
        proj, h, gp = in_projection(x, norm1_3, mod4, l, w_main, w_g16)

        w1p = jnp.pad(hy_w1[l], ((0, LANE - HY_PE), (0, 0)))
        ws3 = jnp.transpose(hy_short[l].reshape(3, 3, MIX_W), (1, 0, 2))
        y_hy = []
        for off, nseq, sl in ((0, BATCH, SEQ), (den_off, DEC_BATCH, DEC_SEQ)):
            kfilt = hyena_filter(sl, w1p, hy_b1[l][None], hy_w2[l], hy_b2[l][None], hy_freq[l][None], hy_w3[l],
                                 hy_decay[l][None])
            if sl <= 512:
                y_hy.append(hyena_conv_direct(proj, off, nseq, sl, ws3, hyena_filter_fft_direct(sl, kfilt), hy_skip[l]))
            else:
                y_hy.append(hyena_conv(proj, off, nseq, sl, ws3, hyena_filter_fft(sl, kfilt), hy_skip[l]))

        gr, gc, b_r, b_c = _ml_gate_layouts(gp, ml_gate_bias[l])
        gain = ml_out_norm[l][None]
        y_ml0, c_new, n_new, m_new = mlstm_mixer(proj, gr, gc, b_r, b_c, gain, 0, BATCH, SEQ, None)
        st = (state_mlstm_C[:, l], state_mlstm_n[:, l][:, :, :, None, :],
              jnp.broadcast_to(state_mlstm_m[:, l][:, :, :, None, None], (DEC_BATCH, 2, ML_H, 1, LANE)))
        y_ml1 = mlstm_mixer(proj, gr, gc, b_r, b_c, gain, den_off, DEC_BATCH, DEC_SEQ, st)[0]

        lru_args = (lru_conv_w[l], lru_conv_b[l][None], lru_wa[l], lru_ba[l][:, None, :], lru_wx[l],
                    lru_bx[l][:, None, :], lru_lambda[l][:, None, :])
        y_lr0, h_new = rglru_mixer(proj, *lru_args, 0, BATCH, SEQ, None)
        y_lr1 = rglru_mixer(proj, *lru_args, den_off, DEC_BATCH, DEC_SEQ, state_rglru[:, l][:, :, None, :])[0]

        q_rot, k_rot, k_norm = qk_prepare(proj, jnp.tile(da_q_norm[l], 2)[None], jnp.tile(da_k_norm[l], 2)[None])
        sub_gain = da_sub_norm[l][None]
        y_da0 = diff_attention(q_rot, k_rot, proj, da_lambda[l], sub_gain, lam_init, 0, BATCH, SEQ, None, l)
        y_da1 = diff_attention(q_rot, k_rot, proj, da_lambda[l], sub_gain, lam_init, den_off, DEC_BATCH, DEC_SEQ,
                               (cache_k4, cache_v4), l)

        ys = [jnp.concatenate(p, axis=0) for p in (y_hy, (y_ml0, y_ml1), (y_lr0, y_lr1), (y_da0, y_da1))]
        merged = branch_merge(h, ys, w_gate, w_branch[l].astype(BF16))
        x1, h2, logits_t = out_projection(merged, x, mod4, l, w_out[l].astype(BF16), norm2_3,
                                          jnp.pad(moe_router[l], ((0, 0), (0, LANE - N_EXPERTS))))
        x = moe_ffn(x1, h2, logits_t, mod4, l, moe_bias[l][:, None], moe_w_gate, moe_w_up, moe_w_down,
                    sh_w_gate[l].astype(BF16), sh_w_up[l].astype(BF16), sh_w_down[l].astype(BF16))

        new_k.append(k_norm[:T_CTX].reshape(BATCH, SEQ, DA_H, 2, DA_DK))
        new_v.append(proj[:T_CTX, CB_DAV * LANE:(CB_DAV + DA_H) * LANE].astype(F32).reshape(BATCH, SEQ, DA_H, -1))
        new_c.append(c_new)
        new_n.append(n_new[:, :, :, 0, :])
        new_m.append(m_new[:, :, :, 0, 0])
        new_h.append(h_new[:, :, 0, :])
    y_prompt = x[:T_CTX].reshape(BATCH, SEQ, D_MODEL)
    y_sample = x[T_CTX:].reshape(DEC_BATCH, DEC_SEQ, D_MODEL)
    stack = lambda parts: jnp.stack(parts, axis=1)
    return (y_prompt, y_sample, stack(new_k), stack(new_v), stack(new_c), stack(new_n), stack(new_m), stack(new_h))
```

```python
import functools
import math

import numpy as np
import jax
import jax.numpy as jnp
from jax import lax
from jax.experimental import pallas as pl
from jax.experimental.pallas import tpu as pltpu

F32 = jnp.float32
BF16 = jnp.bfloat16
I32 = jnp.int32
HIGHEST = lax.Precision.HIGHEST

D_MODEL = 2048
BATCH = 16
SEQ = 256
DEPTH = 2
DEC_BATCH = 4
DEC_SEQ = 4096
GRID_W = 64
N_BRANCH = 4
MIX_W = 512
HY_BANDS = 16
HY_PE = 1 + 2 * HY_BANDS
HY_FFN = 64
ML_H = 4
ML_DK = 128
LRU_H = 4
LRU_C = 8.0
DA_H = 4
DA_DK = 64
ROPE_BASE = 10000.0
N_EXPERTS = 64
TOP_K = 6
N_GROUPS = 8
TOPK_GROUPS = 4
D_EXPERT = 512
ROUTED_SCALE = 2.5
EPS = 1e-6

LANE = 128
T_CTX = BATCH * SEQ
T_DEN = DEC_BATCH * DEC_SEQ
T_ALL = T_CTX + T_DEN
N_MOD_ROWS = 8
VMEM_LIMIT = 56 * 1024 * 1024

CB_HY = 0
CB_MLQ, CB_MLK, CB_MLV, CB_MLO = 12, 16, 20, 24
CB_LRX, CB_LRY = 28, 32
CB_DAQ, CB_DAK, CB_DAV = 36, 40, 44
N_MAIN = 6144


def _cp(sem, vmem=VMEM_LIMIT, unchecked_dma=False):
    return pltpu.CompilerParams(dimension_semantics=sem, vmem_limit_bytes=vmem, disable_bounds_checks=unchecked_dma)


def _mod_row(i, rows_per_block):
    nctx = T_CTX // rows_per_block
    per = DEC_SEQ // rows_per_block
    return jnp.where(i < nctx, 0, 1 + (i - nctx) // per)


def _mod_spec(layer, seg, rows_per_block, ngrid):
    if ngrid == 1:
        return pl.BlockSpec((None, None, 1, D_MODEL), lambda i: (layer, _mod_row(i, rows_per_block) * 6 + seg, 0, 0))
    return pl.BlockSpec((None, None, 1, D_MODEL), lambda i, j: (layer, _mod_row(i, rows_per_block) * 6 + seg, 0, 0))


def _mod_kernel(c_ref, w_ref, b_ref, o_ref):
    c = c_ref[...]
    a = (c * jax.nn.sigmoid(c)).astype(BF16)
    o_ref[...] = jnp.dot(a, w_ref[...].astype(BF16), preferred_element_type=F32) + b_ref[...]


def modulation_all(cvecs, w_mod, b_mod):
    tn = 1024
    n = 6 * D_MODEL
    return pl.pallas_call(
        _mod_kernel,
        grid=(DEPTH, n // tn),
        in_specs=[pl.BlockSpec((N_MOD_ROWS, D_MODEL), lambda l, j: (0, 0)),
                  pl.BlockSpec((None, D_MODEL, tn), lambda l, j: (l, 0, j)),
                  pl.BlockSpec((None, 1, tn), lambda l, j: (l, 0, j))],
        out_specs=pl.BlockSpec((None, N_MOD_ROWS, tn), lambda l, j: (l, 0, j)),
        out_shape=jax.ShapeDtypeStruct((DEPTH, N_MOD_ROWS, n), F32),
        compiler_params=_cp(("parallel", "parallel")),
        name="modulation",
    )(cvecs, w_mod, b_mod.reshape(DEPTH, 1, n))


def _norm_mod(x, g, sc, sh):
    ms = jnp.mean(x * x, axis=-1, keepdims=True)
    return (x * lax.rsqrt(ms + EPS) * g) * (1.0 + sc) + sh


def _inproj_kernel(x_ref, g_ref, sh_ref, sc_ref, w_ref, wg_ref, proj_ref, h_ref, gp_ref, *, tm, sub):
    j = pl.program_id(1)

    @pl.when(j == 0)
    def _():
        g = g_ref[...]
        sc = sc_ref[...]
        sh = sh_ref[...]

        def body(r, _):
            rows = pl.ds(pl.multiple_of(r * sub, sub), sub)
            hb = _norm_mod(x_ref[rows, :], g, sc, sh).astype(BF16)
            h_ref[rows, :] = hb
            gp_ref[rows, :] = jnp.dot(hb, wg_ref[...], preferred_element_type=F32)
            return 0

        lax.fori_loop(0, tm // sub, body, 0)

    proj_ref[...] = jnp.dot(h_ref[...], w_ref[...], preferred_element_type=F32).astype(BF16)


def in_projection(x, norm_g, mod4, layer, w_main, w_gate16):
    tm, tn = 1024, 1024
    kern = functools.partial(_inproj_kernel, tm=tm, sub=256)
    return pl.pallas_call(
        kern,
        grid=(T_ALL // tm, N_MAIN // tn),
        in_specs=[pl.BlockSpec((tm, D_MODEL), lambda i, j: (i, 0)),
                  pl.BlockSpec((None, 1, D_MODEL), lambda i, j: (layer, 0, 0)),
                  _mod_spec(layer, 0, tm, 2),
                  _mod_spec(layer, 1, tm, 2),
                  pl.BlockSpec((D_MODEL, tn), lambda i, j: (0, j)),
                  pl.BlockSpec((D_MODEL, LANE), lambda i, j: (0, 0))],
        out_specs=[pl.BlockSpec((tm, tn), lambda i, j: (i, j)),
                   pl.BlockSpec((tm, D_MODEL), lambda i, j: (i, 0)),
                   pl.BlockSpec((tm, LANE), lambda i, j: (i, 0))],
        out_shape=[jax.ShapeDtypeStruct((T_ALL, N_MAIN), BF16),
                   jax.ShapeDtypeStruct((T_ALL, D_MODEL), BF16),
                   jax.ShapeDtypeStruct((T_ALL, LANE), F32)],
        compiler_params=_cp(("parallel", "arbitrary")),
        name="in_projection",
    )(x, norm_g, mod4, mod4, w_main, w_gate16)


def _fft_factors(seq_len):
    n2 = 64
    return 2 * seq_len // n2, n2


def _fft_pages(seq_len):
    n1, _ = _fft_factors(seq_len)
    used = n1 // 2 + 1
    return used, -(-used // 8) * 8


@functools.lru_cache(maxsize=None)
def _dft_tables(seq_len):
    n1, n2 = _fft_factors(seq_len)
    used, npg = _fft_pages(seq_len)
    n = n1 * n2
    th1 = 2.0 * np.pi * np.outer(np.arange(npg, dtype=np.float64), np.arange(n1, dtype=np.float64)) / n1
    w1 = np.concatenate([np.cos(th1), -np.sin(th1)], axis=0)
    w1h = w1[:, : n1 // 2]
    wt = np.where(np.arange(npg) < used, 2.0, 0.0)
    wt[0] = 1.0
    wt[n1 // 2] = 1.0
    th4 = th1[:, : n1 // 2].T
    w4 = np.concatenate([np.cos(th4) * wt, -np.sin(th4) * wt], axis=1) / n
    k1 = np.arange(npg, dtype=np.float64)[:, None, None]
    k2 = np.arange(n2, dtype=np.float64)[None, :, None]
    m2 = np.arange(n2, dtype=np.float64)[None, None, :]
    ph = -2.0 * np.pi * (m2 * k1 / n + m2 * k2 / n2)
    gr, gi = np.cos(ph), np.sin(ph)
    g = np.concatenate([np.concatenate([gr, -gi], axis=2), np.concatenate([gi, gr], axis=2)], axis=1)
    gt = np.transpose(g, (0, 2, 1))
    return tuple(np.asarray(t, np.float32) for t in (w1, w1h, g, gt, w4))


@functools.lru_cache(maxsize=None)
def _direct_dft_tables(seq_len):
    n = 2 * seq_len
    th = 2.0 * np.pi * np.outer(np.arange(n, dtype=np.float64), np.arange(n, dtype=np.float64)) / n
    wf = np.concatenate([np.cos(th), -np.sin(th)], axis=0)
    wi = np.concatenate([np.cos(th), -np.sin(th)], axis=1)[:seq_len] / n
    return np.asarray(wf, np.float32), np.asarray(wi, np.float32)


def _dft_tables_bf16(seq_len):
    return tuple(jnp.asarray(t, BF16) for t in _dft_tables(seq_len))


@functools.lru_cache(maxsize=None)
def _filter_positions(seq_len):
    pos = np.concatenate([np.arange(seq_len), [0], np.arange(seq_len - 1, 0, -1)]).astype(np.float64)
    tn = pos / seq_len
    bands = np.linspace(1e-4, HY_BANDS - 1, HY_BANDS)
    ang = (2.0 * math.pi / seq_len) * pos[:, None] * bands
    pe = np.zeros((2 * seq_len, LANE), np.float64)
    pe[:, 0] = tn
    pe[:, 1:1 + HY_BANDS] = np.cos(ang)
    pe[:, 1 + HY_BANDS:HY_PE] = np.sin(ang)
    return np.asarray(pe, np.float32), np.asarray(tn[:, None], np.float32)


def _hy_mlp_kernel(pe_ref, w1_ref, b1_ref, w2_ref, b2_ref, fr_ref, z_ref):
    fr = fr_ref[...]
    z = jnp.sin(fr * (jnp.dot(pe_ref[...], w1_ref[...], precision=HIGHEST, preferred_element_type=F32) + b1_ref[...]))
    z_ref[...] = jnp.sin(fr * (jnp.dot(z, w2_ref[...], precision=HIGHEST, preferred_element_type=F32) + b2_ref[...]))


def _hy_filter_kernel(z_ref, tn_ref, w3a_ref, w3b_ref, dca_ref, dcb_ref, k_ref, *, seq_len):
    def taps(rows, w3_ref, dc_ref):
        f = jnp.dot(z_ref[rows, :].astype(BF16), w3_ref[...].astype(BF16), preferred_element_type=F32)
        return f * jnp.exp(-tn_ref[rows, :] * jnp.abs(dc_ref[...]))

    ka = taps(pl.ds(0, seq_len), w3a_ref, dca_ref)
    kb = taps(pl.ds(seq_len, seq_len), w3b_ref, dcb_ref)
    row = lax.broadcasted_iota(I32, ka.shape, 0)
    ka = ka + jnp.where(row == 0, taps(pl.ds(0, 8), w3b_ref, dcb_ref)[0:1], 0.0)
    kb = jnp.where(row == 0, 0.0, kb)
    inv = 1.0 / (jnp.sum(jnp.abs(ka), axis=0, keepdims=True) + jnp.sum(jnp.abs(kb), axis=0, keepdims=True))
    k_ref[0:seq_len, :] = ka * inv
    k_ref[seq_len:, :] = kb * inv


def hyena_filter(seq_len, w1p, b1, w2, b2, freq, w3, decay):
    pe, tn = (jnp.asarray(t) for t in _filter_positions(seq_len))
    n = 2 * seq_len
    tr = min(n, 1024)
    full = lambda shape: pl.BlockSpec(shape, lambda j: (0,) * len(shape))
    z = pl.pallas_call(
        _hy_mlp_kernel,
        grid=(n // tr,),
        in_specs=[pl.BlockSpec((tr, LANE), lambda j: (j, 0)), full((LANE, HY_FFN)), full((1, HY_FFN)),
                  full((HY_FFN, HY_FFN)), full((1, HY_FFN)), full((1, HY_FFN))],
        out_specs=pl.BlockSpec((tr, HY_FFN), lambda j: (j, 0)),
        out_shape=jax.ShapeDtypeStruct((n, HY_FFN), F32),
        compiler_params=_cp(("parallel",)),
        name=f"hyena_filter_mlp_{seq_len}",
    )(pe, w1p, b1, w2, b2, freq)
    tc = 128
    nc = (2 * MIX_W) // tc
    kern = functools.partial(_hy_filter_kernel, seq_len=seq_len)
    return pl.pallas_call(
        kern,
        grid=(nc,),
        in_specs=[full((n, HY_FFN)), full((n, 1)),
                  pl.BlockSpec((HY_FFN, tc), lambda j: (0, j)),
                  pl.BlockSpec((HY_FFN, tc), lambda j: (0, nc + j)),
                  pl.BlockSpec((1, tc), lambda j: (0, j)),
                  pl.BlockSpec((1, tc), lambda j: (0, nc + j))],
        out_specs=pl.BlockSpec((n, tc), lambda j: (0, j)),
        out_shape=jax.ShapeDtypeStruct((n, 2 * MIX_W), F32),
        compiler_params=_cp(("parallel",)),
        name=f"hyena_filter_{seq_len}",
    )(z, tn, w3, w3, decay, decay)


def _split_bf16(x):
    hi = x.astype(BF16)
    lo = (x - hi.astype(F32)).astype(BF16)
    return hi, lo


def _fft_stage1(load_rows, w1, a_ref, npg, n2, split):
    def body(m, _):
        xs = load_rows(m)
        if split:
            hi, lo = _split_bf16(xs)
            r = jnp.dot(w1, hi, preferred_element_type=F32) + jnp.dot(w1, lo, preferred_element_type=F32)
        else:
            r = jnp.dot(w1, xs.astype(BF16), preferred_element_type=F32)
        a_ref[pl.ds(m, npg, stride=2 * n2), :] = r[:npg]
        a_ref[pl.ds(n2 + m, npg, stride=2 * n2), :] = r[npg:]
        return 0

    lax.fori_loop(0, n2, body, 0, unroll=8)


def _hy_fft_kernel(k_ref, w1_ref, g_ref, kf_ref, a_ref, *, n1, n2, npg):
    w1 = w1_ref[...]
    _fft_stage1(lambda m: k_ref[pl.ds(m, n1, stride=n2), :], w1, a_ref, npg, n2, True)

    def body(p, _):
        rows = pl.ds(pl.multiple_of(p * 2 * n2, 2 * n2), 2 * n2)
        hi, lo = _split_bf16(a_ref[rows, :])
        g = g_ref[p]
        kf_ref[rows, :] = jnp.dot(g, hi, preferred_element_type=F32) + jnp.dot(g, lo, preferred_element_type=F32)
        return 0

    lax.fori_loop(0, npg, body, 0, unroll=8)


def hyena_filter_fft(seq_len, kfilt):
    n1, n2 = _fft_factors(seq_len)
    _, npg = _fft_pages(seq_len)
    w1, _, g, _, _ = _dft_tables_bf16(seq_len)
    tc = 128
    kern = functools.partial(_hy_fft_kernel, n1=n1, n2=n2, npg=npg)
    return pl.pallas_call(
        kern,
        grid=((2 * MIX_W) // tc,),
        in_specs=[pl.BlockSpec((n1 * n2, tc), lambda j: (0, j)),
                  pl.BlockSpec((2 * npg, n1), lambda j: (0, 0)),
                  pl.BlockSpec((npg, 2 * n2, 2 * n2), lambda j: (0, 0, 0))],
        out_specs=pl.BlockSpec((npg * 2 * n2, tc), lambda j: (0, j)),
        out_shape=jax.ShapeDtypeStruct((npg * 2 * n2, 2 * MIX_W), F32),
        scratch_shapes=[pltpu.VMEM((npg * 2 * n2, tc), F32)],
        compiler_params=_cp(("parallel",)),
        name=f"hyena_filter_fft_{seq_len}",
    )(kfilt, w1, g)


def _hy_fft_direct_kernel(k_ref, wf_ref, kf_ref):
    hi, lo = _split_bf16(k_ref[...])
    wf = wf_ref[...]
    kf_ref[...] = jnp.dot(wf, hi, preferred_element_type=F32) + jnp.dot(wf, lo, preferred_element_type=F32)


def hyena_filter_fft_direct(seq_len, kfilt):
    n = 2 * seq_len
    wf = jnp.asarray(_direct_dft_tables(seq_len)[0], BF16)
    tc = 256
    return pl.pallas_call(
        _hy_fft_direct_kernel,
        grid=((2 * MIX_W) // tc,),
        in_specs=[pl.BlockSpec((n, tc), lambda j: (0, j)), pl.BlockSpec((2 * n, n), lambda j: (0, 0))],
        out_specs=pl.BlockSpec((2 * n, tc), lambda j: (0, j)),
        out_shape=jax.ShapeDtypeStruct((2 * n, 2 * MIX_W), F32),
        compiler_params=_cp(("parallel",)),
        name=f"hyena_filter_fft_{seq_len}",
    )(kfilt, wf)


def _shift_rows(x, s, row):
    n = x.shape[0]
    y = pltpu.roll(x, s % n, 0)
    if s > 0:
        return jnp.where(row >= s, y, 0.0)
    return jnp.where(row < n + s, y, 0.0)


def _short_conv3(x_ref, w_ref):
    x = x_ref[...].astype(F32)
    row = lax.broadcasted_iota(I32, x.shape, 0)
    w = w_ref[...]
    return w[0:1] * _shift_rows(x, 1, row) + w[1:2] * x + w[2:3] * _shift_rows(x, -1, row)


def _hy_conv_kernel(v_ref, x1_ref, x2_ref, ws_ref, kf_ref, skip_ref, w1_ref, g_ref, gt_ref, w4_ref, y_ref,
                    z_ref, c_ref, a_ref, s_ref, *, n1, n2, used, npg):
    o = pl.program_id(2)
    nh = n1 // 2

    @pl.when(o == 0)
    def _():
        z_ref[...] = _short_conv3(v_ref, ws_ref.at[0])

    w1 = w1_ref[...]

    def stage1(m, _):
        xs = z_ref[pl.ds(m, nh, stride=n2), :].astype(BF16)
        s_ref[pl.ds(pl.multiple_of(m * 2 * npg, 2 * npg), 2 * npg), :] = jnp.dot(w1, xs, preferred_element_type=F32)
        return 0

    lax.fori_loop(0, n2, stage1, 0, unroll=16)
    a_ref[used * 2 * n2:, :] = jnp.zeros(((npg - used) * 2 * n2, a_ref.shape[1]), F32)

    def page(p, _):
        rows = pl.ds(pl.multiple_of(p * 2 * n2, 2 * n2), 2 * n2)
        ap = jnp.concatenate([s_ref[pl.ds(p, n2, stride=2 * npg), :],
                              s_ref[pl.ds(npg + p, n2, stride=2 * npg), :]], axis=0)
        x = jnp.dot(g_ref[p], ap.astype(BF16), preferred_element_type=F32)
        kf = kf_ref[rows, :]
        xr, xi = x[:n2], x[n2:]
        kr, ki = kf[:n2], kf[n2:]
        yc = jnp.concatenate([xr * kr - xi * ki, xr * ki + xi * kr], axis=0).astype(BF16)
        a_ref[rows, :] = jnp.dot(gt_ref[p], yc, preferred_element_type=F32)
        return 0

    lax.fori_loop(0, used, page, 0, unroll=13)

    w4 = w4_ref[...]

    def back(m, _):
        dr = a_ref[pl.ds(m, npg, stride=2 * n2), :]
        di = a_ref[pl.ds(n2 + m, npg, stride=2 * n2), :]
        d = jnp.concatenate([dr, di], axis=0).astype(BF16)
        c_ref[pl.ds(m, nh, stride=n2), :] = jnp.dot(w4, d, preferred_element_type=F32)
        return 0

    lax.fori_loop(0, n2, back, 0, unroll=16)

    @pl.when(o == 0)
    def _():
        z = z_ref[...]
        z_ref[...] = _short_conv3(x1_ref, ws_ref.at[1]) * (c_ref[...] + z * skip_ref[0:1])

    @pl.when(o == 1)
    def _():
        z = z_ref[...]
        y_ref[...] = (_short_conv3(x2_ref, ws_ref.at[2]) * (c_ref[...] + z * skip_ref[1:2])).astype(BF16)


def hyena_conv(proj, row_block_off, nseq, seq_len, w_short3, kf, skip):
    n1, n2 = _fft_factors(seq_len)
    used, npg = _fft_pages(seq_len)
    _, w1h, g, gt, w4 = _dft_tables_bf16(seq_len)
    rows_a = npg * 2 * n2
    tc = 128
    ncb = MIX_W // tc
    kern = functools.partial(_hy_conv_kernel, n1=n1, n2=n2, used=used, npg=npg)
    const3 = lambda c, b, o: (0, 0, 0)
    seg = lambda s: pl.BlockSpec((seq_len, tc), lambda c, b, o: (row_block_off + b, CB_HY + s * ncb + c))
    return pl.pallas_call(
        kern,
        grid=(ncb, nseq, 2),
        in_specs=[seg(0), seg(1), seg(2),
                  pl.BlockSpec((3, 3, tc), lambda c, b, o: (0, 0, c)),
                  pl.BlockSpec((rows_a, tc), lambda c, b, o: (0, o * ncb + c)),
                  pl.BlockSpec((2, tc), lambda c, b, o: (0, c)),
                  pl.BlockSpec((2 * npg, n1 // 2), lambda c, b, o: (0, 0)),
                  pl.BlockSpec((npg, 2 * n2, 2 * n2), const3, pipeline_mode=pl.Buffered(1)),
                  pl.BlockSpec((npg, 2 * n2, 2 * n2), const3, pipeline_mode=pl.Buffered(1)),
                  pl.BlockSpec((n1 // 2, 2 * npg), lambda c, b, o: (0, 0))],
        out_specs=pl.BlockSpec((seq_len, tc), lambda c, b, o: (b, c)),
        out_shape=jax.ShapeDtypeStruct((nseq * seq_len, MIX_W), BF16),
        scratch_shapes=[pltpu.VMEM((seq_len, tc), F32), pltpu.VMEM((seq_len, tc), F32),
                        pltpu.VMEM((rows_a, tc), F32), pltpu.VMEM((rows_a, tc), F32)],
        compiler_params=_cp(("parallel", "parallel", "arbitrary")),
        name=f"hyena_conv_{seq_len}",
    )(proj, proj, proj, w_short3, kf, skip, w1h, g, gt, w4)


def _hy_direct_kernel(u_ref, ws_ref, kf_ref, skip_ref, wf_ref, wi_ref, y_ref, *, n):
    w = MIX_W
    wf = wf_ref[...]
    wi = wi_ref[...]
    z = _short_conv3(u_ref.at[:, pl.ds(0, w)], ws_ref.at[0])
    for o in range(2):
        x = jnp.dot(wf, z.astype(BF16), preferred_element_type=F32)
        kf = kf_ref[:, o * w:(o + 1) * w]
        xr, xi = x[:n], x[n:]
        kr, ki = kf[:n], kf[n:]
        yc = jnp.concatenate([xr * kr - xi * ki, xr * ki + xi * kr], axis=0).astype(BF16)
        conv = jnp.dot(wi, yc, preferred_element_type=F32)
        gate = _short_conv3(u_ref.at[:, pl.ds((o + 1) * w, w)], ws_ref.at[o + 1])
        z = gate * (conv + z * skip_ref[o:o + 1])
    y_ref[...] = z.astype(BF16)


def hyena_conv_direct(proj, row_block_off, nseq, seq_len, w_short3, kf, skip):
    n = 2 * seq_len
    wf_np, wi_np = _direct_dft_tables(seq_len)
    wf = jnp.asarray(wf_np[:, :seq_len], BF16)
    wi = jnp.asarray(wi_np, BF16)
    kern = functools.partial(_hy_direct_kernel, n=n)
    return pl.pallas_call(
        kern,
        grid=(nseq,),
        in_specs=[pl.BlockSpec((seq_len, 3 * MIX_W), lambda b: (row_block_off + b, 0)),
                  pl.BlockSpec((3, 3, MIX_W), lambda b: (0, 0, 0)),
                  pl.BlockSpec((2 * n, 2 * MIX_W), lambda b: (0, 0)),
                  pl.BlockSpec((2, MIX_W), lambda b: (0, 0)),
                  pl.BlockSpec((2 * n, seq_len), lambda b: (0, 0)),
                  pl.BlockSpec((seq_len, 2 * n), lambda b: (0, 0))],
        out_specs=pl.BlockSpec((seq_len, MIX_W), lambda b: (b, 0)),
        out_shape=jax.ShapeDtypeStruct((nseq * seq_len, MIX_W), BF16),
        compiler_params=_cp(("parallel",)),
        name=f"hyena_conv_{seq_len}",
    )(proj, w_short3, kf, skip, wf, wi)


def _dot_nt(a, b):
    return lax.dot_general(a, b, (((1,), (1,)), ((), ())), preferred_element_type=F32)


ML_CHUNK = 256


def _log_sigmoid(x):
    return jnp.minimum(x, 0.0) - jnp.log1p(jnp.exp(-jnp.abs(x)))


def _mlstm_kernel(*refs, seq_len, has_state):
    if has_state:
        (q_ref, k_ref, v_ref, o_ref, gr_ref, gc_ref, br_ref, bc_ref, gain_ref, c0_ref, n0_ref, m0_ref,
         y_ref, cout_ref, nout_ref, mout_ref, hs0, hs1, cs0, cs1, ns0, ns1, ms0, ms1) = refs
    else:
        (q_ref, k_ref, v_ref, o_ref, gr_ref, gc_ref, br_ref, bc_ref, gain_ref,
         y_ref, cout_ref, nout_ref, mout_ref, hs0, hs1, cs0, cs1, ns0, ns1, ms0, ms1) = refs
    hs, cs, ns, ms = (hs0, hs1), (cs0, cs1), (ns0, ns1), (ms0, ms1)
    t = min(ML_CHUNK, seq_len)
    nc = seq_len // t
    scale = ML_DK ** -0.5
    for d in (0, 1):
        if has_state:
            cs[d][...] = c0_ref[d]
            ns[d][...] = n0_ref[d]
            ms[d][...] = m0_ref[d]
        else:
            cs[d][...] = jnp.zeros_like(cs[d])
            ns[d][...] = jnp.zeros_like(ns[d])
            ms[d][...] = jnp.zeros_like(ms[d])
    row = lax.broadcasted_iota(I32, (t, t), 0)
    col = lax.broadcasted_iota(I32, (t, t), 1)
    br = br_ref[...]
    bc = bc_ref[...]
    tris = (jnp.where(row <= col, 1.0, 0.0).astype(BF16), jnp.where(row >= col, 1.0, 0.0).astype(BF16))

    def cumsum_row(f_r, tri):
        f8 = jnp.broadcast_to(f_r, (8, t))
        hi = f8.astype(BF16)
        r1 = f8 - hi.astype(F32)
        mid = r1.astype(BF16)
        lo = (r1 - mid.astype(F32)).astype(BF16)
        out = (jnp.dot(hi, tri, preferred_element_type=F32) + jnp.dot(mid, tri, preferred_element_type=F32)
               + jnp.dot(lo, tri, preferred_element_type=F32))
        return out[0:1]

    def step(j, _):
        for d in (0, 1):
            jj = j if d == 0 else nc - 1 - j
            rows = pl.ds(pl.multiple_of(jj * t, t), t)
            qf = q_ref[rows, :].astype(F32) * scale
            qb = qf.astype(BF16)
            kb = k_ref[rows, :]
            vb = v_ref[rows, :]
            gr = gr_ref[rows, :] + br
            gc = gc_ref[:, rows] + bc
            i_c = gr[:, 2 * d:2 * d + 1]
            i_r = gc[2 * d:2 * d + 1, :]
            f_r = _log_sigmoid(gc[2 * d + 1:2 * d + 2, :])
            mask = (col <= row) if d == 0 else (col >= row)
            b_c = jnp.sum(jnp.where(mask, f_r, 0.0), axis=1, keepdims=True)
            b_r = cumsum_row(f_r, tris[d])
            logw = jnp.where(mask, b_c - b_r + i_r, -jnp.inf)
            m = ms[d][:, 0:1]
            g = b_c + m
            mt = jnp.maximum(g, jnp.max(logw, axis=1, keepdims=True))
            s = _dot_nt(qb, kb) * jnp.exp(logw - mt)
            inter = jnp.exp(g - mt)
            cm = cs[d][...]
            nv = ns[d][...]
            num = (jnp.dot(s.astype(BF16), vb, preferred_element_type=F32)
                   + inter * jnp.dot(qb, cm.astype(BF16), preferred_element_type=F32))
            den = jnp.sum(s, axis=1, keepdims=True) + inter * jnp.sum(qf * nv, axis=1, keepdims=True)
            hs[d][rows, :] = num / jnp.maximum(jnp.abs(den), jnp.exp(-mt))
            bl = jnp.sum(f_r, axis=1, keepdims=True)
            wlog_r = bl - b_r + i_r
            wlog_c = bl - b_c + i_c
            m_new = jnp.maximum(bl + m, jnp.max(wlog_r, axis=1, keepdims=True))
            dec = jnp.exp(bl + m - m_new)
            kw = jnp.exp(wlog_c - m_new) * kb.astype(F32)
            cs[d][...] = dec * cm + jnp.dot(kw.T.astype(BF16), vb, preferred_element_type=F32)
            ns[d][...] = dec * nv + jnp.sum(kw, axis=0, keepdims=True)
            ms[d][...] = jnp.broadcast_to(m_new, (1, LANE))
        return 0

    lax.fori_loop(0, nc, step, 0, unroll=2)
    hsum = hs0[...] + hs1[...]
    hn = hsum * lax.rsqrt(jnp.mean(hsum * hsum, axis=-1, keepdims=True) + EPS) * gain_ref[...]
    y_ref[...] = (jax.nn.sigmoid(o_ref[...].astype(F32)) * hn).astype(BF16)
    for d in (0, 1):
        cout_ref[d] = cs[d][...]
        nout_ref[d] = ns[d][...]
        mout_ref[d] = ms[d][...]


def mlstm_mixer(proj, gates_r, gates_c, bias_r, bias_c, gain, row_block_off, nseq, seq_len, state):
    has_state = state is not None
    kern = functools.partial(_mlstm_kernel, seq_len=seq_len, has_state=has_state)
    col = lambda cb: pl.BlockSpec((seq_len, LANE), lambda b, h: (row_block_off + b, cb + h))
    c_spec = pl.BlockSpec((None, 2, None, ML_DK, ML_DK), lambda b, h: (b, 0, h, 0, 0))
    v_spec = pl.BlockSpec((None, 2, None, 1, LANE), lambda b, h: (b, 0, h, 0, 0))
    in_specs = [col(CB_MLQ), col(CB_MLK), col(CB_MLV), col(CB_MLO),
                pl.BlockSpec((None, seq_len, 4), lambda b, h: (h, row_block_off + b, 0)),
                pl.BlockSpec((None, 4, seq_len), lambda b, h: (h, 0, row_block_off + b)),
                pl.BlockSpec((None, 1, 4), lambda b, h: (h, 0, 0)),
                pl.BlockSpec((None, 4, 1), lambda b, h: (h, 0, 0)),
                pl.BlockSpec((1, LANE), lambda b, h: (0, h))]
    args = [proj, proj, proj, proj, gates_r, gates_c, bias_r, bias_c, gain]
    if has_state:
        in_specs += [c_spec, v_spec, v_spec]
        args += list(state)
    return pl.pallas_call(
        kern,
        grid=(nseq, ML_H),
        in_specs=in_specs,
        out_specs=[pl.BlockSpec((seq_len, LANE), lambda b, h: (b, h)), c_spec, v_spec, v_spec],
        out_shape=[jax.ShapeDtypeStruct((nseq * seq_len, MIX_W), BF16),
                   jax.ShapeDtypeStruct((nseq, 2, ML_H, ML_DK, ML_DK), F32),
                   jax.ShapeDtypeStruct((nseq, 2, ML_H, 1, LANE), F32),
                   jax.ShapeDtypeStruct((nseq, 2, ML_H, 1, LANE), F32)],
        scratch_shapes=([pltpu.VMEM((seq_len, LANE), F32)] * 2 + [pltpu.VMEM((ML_DK, ML_DK), F32)] * 2
                        + [pltpu.VMEM((1, LANE), F32)] * 4),
        compiler_params=_cp(("parallel", "parallel")),
        name=f"mlstm_{seq_len}",
    )(*args)


def _softplus(x):
    return jnp.maximum(x, 0.0) + jnp.log1p(jnp.exp(-jnp.abs(x)))


def _gelu_tanh(x):
    return 0.5 * x * (1.0 + jnp.tanh(math.sqrt(2.0 / math.pi) * (x + 0.044715 * (x * x * x))))


def _lru_kernel(*refs, seq_len, has_state):
    if has_state:
        (x_ref, y_ref, cw_ref, cb_ref, wa_ref, ba_ref, wx_ref, bx_ref, lam_ref, h0_ref,
         out_ref, hl_ref, a_s, b_s, hsum, hsum_b) = refs
    else:
        (x_ref, y_ref, cw_ref, cb_ref, wa_ref, ba_ref, wx_ref, bx_ref, lam_ref,
         out_ref, hl_ref, a_s, b_s, hsum, hsum_b) = refs
    xf = x_ref[...].astype(F32)
    row = lax.broadcasted_iota(I32, xf.shape, 0)
    w = cw_ref[...]
    x = (w[0:1] * _shift_rows(xf, 2, row) + w[1:2] * _shift_rows(xf, 1, row) + w[2:3] * xf
         + w[3:4] * _shift_rows(xf, -1, row) + cb_ref[...])
    xb = x.astype(BF16)
    for d in (0, 1):
        r = jax.nn.sigmoid(jnp.dot(xb, wa_ref[d].astype(BF16), preferred_element_type=F32) + ba_ref[d])
        i = jax.nn.sigmoid(jnp.dot(xb, wx_ref[d].astype(BF16), preferred_element_type=F32) + bx_ref[d])
        log_a = (-LRU_C) * r * _softplus(-lam_ref[d])
        a_s[d] = jnp.exp(log_a)
        b_s[d] = jnp.sqrt(jnp.maximum(1.0 - jnp.exp(2.0 * log_a), 0.0)) * (i * x)

    nb = seq_len // 8
    sub = lax.broadcasted_iota(I32, (8, LANE), 0)

    def block_scan(a, b, reverse):
        for s in (1, 2, 4):
            sh = 8 - s if reverse else s
            ok = (sub < 8 - s) if reverse else (sub >= s)
            b = jnp.where(ok, a * pltpu.roll(b, sh, 0) + b, b)
            a = jnp.where(ok, a * pltpu.roll(a, sh, 0), a)
        return a, b

    def both(i, carry):
        cf, cb = carry
        rows_f = pl.ds(pl.multiple_of(i * 8, 8), 8)
        rows_b = pl.ds(pl.multiple_of((nb - 1 - i) * 8, 8), 8)
        a, b = block_scan(a_s[0, rows_f, :], b_s[0, rows_f, :], False)
        hf = a * cf + b
        hsum[rows_f, :] = hf
        a, b = block_scan(a_s[1, rows_b, :], b_s[1, rows_b, :], True)
        hb = a * cb + b
        hsum_b[rows_b, :] = hb
        return hf[7:8, :], hb[0:1, :]

    zero = jnp.zeros((1, LANE), F32)
    init = (h0_ref[0], h0_ref[1]) if has_state else (zero, zero)
    hf, hb = lax.fori_loop(0, nb, both, init, unroll=4)
    out_ref[...] = ((hsum[...] + hsum_b[...]) * _gelu_tanh(y_ref[...].astype(F32))).astype(BF16)
    hl_ref[0] = hf
    hl_ref[1] = hb


def rglru_mixer(proj, conv_w, conv_b, wa, ba, wx, bx, lam, row_block_off, nseq, seq_len, h0):
    has_state = h0 is not None
    kern = functools.partial(_lru_kernel, seq_len=seq_len, has_state=has_state)
    col = lambda cb: pl.BlockSpec((seq_len, LANE), lambda b, h: (row_block_off + b, cb + h))
    w_spec = pl.BlockSpec((2, None, LANE, LANE), lambda b, h: (0, h, 0, 0))
    v_spec = pl.BlockSpec((2, 1, LANE), lambda b, h: (0, 0, h))
    s_spec = pl.BlockSpec((None, 2, 1, LANE), lambda b, h: (b, 0, 0, h))
    in_specs = [col(CB_LRX), col(CB_LRY),
                pl.BlockSpec((4, LANE), lambda b, h: (0, h)),
                pl.BlockSpec((1, LANE), lambda b, h: (0, h)),
                w_spec, v_spec, w_spec, v_spec, v_spec]
    args = [proj, proj, conv_w, conv_b, wa, ba, wx, bx, lam]
    if has_state:
        in_specs.append(s_spec)
        args.append(h0)
    return pl.pallas_call(
        kern,
        grid=(nseq, LRU_H),
        in_specs=in_specs,
        out_specs=[pl.BlockSpec((seq_len, LANE), lambda b, h: (b, h)), s_spec],
        out_shape=[jax.ShapeDtypeStruct((nseq * seq_len, MIX_W), BF16),
                   jax.ShapeDtypeStruct((nseq, 2, 1, MIX_W), F32)],
        scratch_shapes=[pltpu.VMEM((2, seq_len, LANE), F32), pltpu.VMEM((2, seq_len, LANE), F32),
                        pltpu.VMEM((seq_len, LANE), F32), pltpu.VMEM((seq_len, LANE), F32)],
        compiler_params=_cp(("parallel", "parallel")),
        name=f"rglru_{seq_len}",
    )(*args)


@functools.lru_cache(maxsize=None)
def _rope_tables():
    t = np.arange(DEC_SEQ)
    pos_row, pos_col = (t // GRID_W).astype(np.float64), (t % GRID_W).astype(np.float64)
    half = DA_DK // 2
    inv = ROPE_BASE ** (-np.arange(0, half, 2, dtype=np.float64) / half)
    lane = np.arange(LANE)
    sub = lane % DA_DK
    pos = np.where((sub < half)[None, :], pos_row[:, None], pos_col[:, None])
    ang = pos * inv[sub % (half // 2)][None, :]
    first = (sub % half) < (half // 2)
    cos = np.cos(ang)
    sin = np.where(first[None, :], -np.sin(ang), np.sin(ang))
    cos_all = np.concatenate([np.ones((T_CTX, LANE))] + [cos] * DEC_BATCH, axis=0)
    sin_all = np.concatenate([np.zeros((T_CTX, LANE))] + [sin] * DEC_BATCH, axis=0)
    pm = np.kron(np.eye(2), np.full((DA_DK, DA_DK), 1.0 / DA_DK))
    return np.asarray(cos_all, np.float32), np.asarray(sin_all, np.float32), np.asarray(pm, np.float32)


def _qkprep_kernel(q_ref, k_ref, cos_ref, sin_ref, qg_ref, kg_ref, pm_ref, qo_ref, ko_ref, kn_ref):
    pmb = pm_ref[...].astype(BF16)
    cos = cos_ref[...]
    sin = sin_ref[...]
    lane = lax.broadcasted_iota(I32, cos.shape, 1)
    first = (lane % (DA_DK // 2)) < (DA_DK // 4)

    def norm(x, g):
        hi, lo = _split_bf16(x * x)
        ms = jnp.dot(hi, pmb, preferred_element_type=F32) + jnp.dot(lo, pmb, preferred_element_type=F32)
        return x * lax.rsqrt(ms + EPS) * g

    def rope(x):
        partner = jnp.where(first, pltpu.roll(x, LANE - DA_DK // 4, 1), pltpu.roll(x, DA_DK // 4, 1))
        return x * cos + partner * sin

    q = norm(q_ref[...].astype(F32), qg_ref[...])
    k = norm(k_ref[...].astype(F32), kg_ref[...])
    kn_ref[...] = k
    qo_ref[...] = (rope(q) * (DA_DK ** -0.5)).astype(BF16)
    ko_ref[...] = rope(k).astype(BF16)


def qk_prepare(proj, q_gain, k_gain):
    cos, sin, pm = (jnp.asarray(t) for t in _rope_tables())
    tm = 512
    blk = lambda cb: pl.BlockSpec((tm, LANE), lambda i, h: (i, cb + h))
    tab = pl.BlockSpec((tm, LANE), lambda i, h: (i, 0))
    one = pl.BlockSpec((1, LANE), lambda i, h: (0, 0))
    out = pl.BlockSpec((tm, LANE), lambda i, h: (i, h))
    return pl.pallas_call(
        _qkprep_kernel,
        grid=(T_ALL // tm, DA_H),
        in_specs=[blk(CB_DAQ), blk(CB_DAK), tab, tab, one, one, pl.BlockSpec((LANE, LANE), lambda i, h: (0, 0))],
        out_specs=[out, out, out],
        out_shape=[jax.ShapeDtypeStruct((T_ALL, MIX_W), BF16), jax.ShapeDtypeStruct((T_ALL, MIX_W), BF16),
                   jax.ShapeDtypeStruct((T_ALL, MIX_W), F32)],
        compiler_params=_cp(("parallel", "parallel")),
        name="qk_prepare",
    )(proj, proj, cos, sin, q_gain, k_gain, pm)


def _attn_kernel(*refs, has_cache, lam_init, seq_len):
    if has_cache:
        q_ref, k_ref, v_ref, kc_ref, vc_ref, lp_ref, sg_ref, o_ref, ka_ref, va_ref = refs
    else:
        q_ref, k_ref, v_ref, lp_ref, sg_ref, o_ref, ka_ref, va_ref = refs

    @pl.when(pl.program_id(2) == 0)
    def _():
        ka_ref[0:seq_len, :] = k_ref[...]
        va_ref[0:seq_len, 0:LANE] = v_ref[...]
        if has_cache:
            ka_ref[seq_len:, :] = kc_ref[...].astype(BF16)
            va_ref[seq_len:, 0:LANE] = vc_ref[...].astype(BF16)
        va_ref[:, LANE:] = jnp.ones((va_ref.shape[0], LANE), BF16)

    q = q_ref[...]
    lane = lax.broadcasted_iota(I32, q.shape, 1)
    zero = jnp.zeros_like(q)
    lp = lp_ref[...]
    lam = (jnp.exp(jnp.sum(lp[0:1] * lp[1:2], axis=1, keepdims=True))
           - jnp.exp(jnp.sum(lp[2:3] * lp[3:4], axis=1, keepdims=True)) + lam_init)

    def softmax_half(qc):
        s = _dot_nt(qc, ka_ref[...])
        p = jnp.exp(s - jnp.max(s, axis=-1, keepdims=True)).astype(BF16)
        r = jnp.dot(p, va_ref[...], preferred_element_type=F32)
        return r[:, :LANE] / r[:, LANE:]

    o = softmax_half(jnp.where(lane < DA_DK, q, zero)) - lam * softmax_half(jnp.where(lane >= DA_DK, q, zero))
    o = o * lax.rsqrt(jnp.mean(o * o, axis=-1, keepdims=True) + EPS) * sg_ref[...]
    o_ref[...] = (o * (1.0 - lam_init)).astype(BF16)


def diff_attention(q_rot, k_rot, proj, lam_p, sub_gain, lam_init, row_block_off, nseq, seq_len, cache, layer):
    has_cache = cache is not None
    tq = 256
    kern = functools.partial(_attn_kernel, has_cache=has_cache, lam_init=lam_init, seq_len=seq_len)
    nq = seq_len // tq
    n_keys = seq_len + (cache[0].shape[2] if has_cache else 0)
    kv = lambda cb: pl.BlockSpec((seq_len, LANE), lambda b, h, i: (row_block_off + b, cb + h))
    in_specs = [pl.BlockSpec((tq, LANE), lambda b, h, i: ((row_block_off + b) * nq + i, h)), kv(0), kv(CB_DAV)]
    args = [q_rot, k_rot, proj]
    if has_cache:
        past = cache[0].shape[2]
        cspec = pl.BlockSpec((None, None, past, LANE), lambda b, h, i: (b, layer, 0, h))
        in_specs += [cspec, cspec]
        args += list(cache)
    in_specs += [pl.BlockSpec((4, DA_DK), lambda b, h, i: (0, 0)), pl.BlockSpec((1, LANE), lambda b, h, i: (0, 0))]
    args += [lam_p, sub_gain]
    return pl.pallas_call(
        kern,
        grid=(nseq, DA_H, nq),
        in_specs=in_specs,
        out_specs=pl.BlockSpec((tq, LANE), lambda b, h, i: (b * nq + i, h)),
        out_shape=jax.ShapeDtypeStruct((nseq * seq_len, MIX_W), BF16),
        scratch_shapes=[pltpu.VMEM((n_keys, LANE), BF16), pltpu.VMEM((n_keys, 2 * LANE), BF16)],
        compiler_params=_cp(("parallel", "parallel", "arbitrary")),
        name=f"diff_attention_{seq_len}",
    )(*args)


def _merge_kernel(h_ref, y0_ref, y1_ref, y2_ref, y3_ref, wg0_ref, wg1_ref, wg2_ref, wg3_ref, wb_ref, o_ref):
    h = h_ref[...]
    acc = None
    for bi, (y_ref, wg_ref) in enumerate(((y0_ref, wg0_ref), (y1_ref, wg1_ref), (y2_ref, wg2_ref), (y3_ref, wg3_ref))):
        gate = jax.nn.sigmoid(jnp.dot(h, wg_ref[...], preferred_element_type=F32))
        term = gate * jnp.dot(y_ref[...], wb_ref[bi], preferred_element_type=F32)
        acc = term if acc is None else acc + term
    o_ref[...] = acc.astype(BF16)


def branch_merge(h, ys, w_gate, w_branch):
    tm, tn = 1024, 256
    nj = D_MODEL // tn
    yspec = pl.BlockSpec((tm, MIX_W), lambda i, j: (i, 0))
    gspec = lambda bi: pl.BlockSpec((D_MODEL, tn), lambda i, j: (0, bi * nj + j))
    return pl.pallas_call(
        _merge_kernel,
        grid=(T_ALL // tm, nj),
        in_specs=[pl.BlockSpec((tm, D_MODEL), lambda i, j: (i, 0)), yspec, yspec, yspec, yspec,
                  gspec(0), gspec(1), gspec(2), gspec(3),
                  pl.BlockSpec((N_BRANCH, MIX_W, tn), lambda i, j: (0, 0, j))],
        out_specs=pl.BlockSpec((tm, tn), lambda i, j: (i, j)),
        out_shape=jax.ShapeDtypeStruct((T_ALL, D_MODEL), BF16),
        compiler_params=_cp(("parallel", "parallel")),
        name="branch_merge",
    )(h, *ys, w_gate, w_gate, w_gate, w_gate, w_branch)


def _outproj_kernel(m_ref, x_ref, g1_ref, w_ref, n2_ref, sh2_ref, sc2_ref, rt_ref, rl_ref, x1_ref, h2_ref, lg_ref):
    half = m_ref.shape[0] // 2
    for r in (0, half):
        rows = slice(r, r + half)
        y = jnp.dot(m_ref[rows, :], w_ref[...], preferred_element_type=F32)
        x1 = x_ref[rows, :] + g1_ref[...] * y
        x1_ref[rows, :] = x1
        h2 = _norm_mod(x1, n2_ref[...], sc2_ref[...], sh2_ref[...])
        h2_ref[rows, :] = _pack_bf16_pairs(h2)
        hi, lo = _split_bf16(h2)
        lg_ref[rows, :] = (jnp.dot(hi, rt_ref[...], preferred_element_type=F32)
                           + jnp.dot(lo, rt_ref[...], preferred_element_type=F32)
                           + jnp.dot(hi, rl_ref[...], preferred_element_type=F32))


def out_projection(merged, x, mod4, layer, w_out, norm2_g, router):
    tm = 256
    router_hi = router.astype(BF16)
    router_lo = (router - router_hi.astype(F32)).astype(BF16)
    return pl.pallas_call(
        _outproj_kernel,
        grid=(T_ALL // tm,),
        in_specs=[pl.BlockSpec((tm, D_MODEL), lambda i: (i, 0)),
                  pl.BlockSpec((tm, D_MODEL), lambda i: (i, 0)),
                  _mod_spec(layer, 2, tm, 1),
                  pl.BlockSpec((D_MODEL, D_MODEL), lambda i: (0, 0)),
                  pl.BlockSpec((None, 1, D_MODEL), lambda i: (layer, 0, 0)),
                  _mod_spec(layer, 3, tm, 1),
                  _mod_spec(layer, 4, tm, 1),
                  pl.BlockSpec((D_MODEL, LANE), lambda i: (0, 0)),
                  pl.BlockSpec((D_MODEL, LANE), lambda i: (0, 0))],
        out_specs=[pl.BlockSpec((tm, D_MODEL), lambda i: (i, 0)),
                   pl.BlockSpec((tm, D_MODEL // 2), lambda i: (i, 0)),
                   pl.BlockSpec((tm, LANE), lambda i: (i, 0))],
        out_shape=[jax.ShapeDtypeStruct((T_ALL, D_MODEL), F32),
                   jax.ShapeDtypeStruct((T_ALL, D_MODEL // 2), I32),
                   jax.ShapeDtypeStruct((T_ALL, LANE), F32)],
        compiler_params=_cp(("parallel",)),
        name="out_projection",
    )(merged, x, mod4, w_out, norm2_g, mod4, mod4, router_hi, router_lo)


MOE_TM = 256
N_ASSIGN = T_ALL * TOP_K
MOE_BLOCKS = (N_ASSIGN + N_EXPERTS * (MOE_TM - 1)) // MOE_TM + 1
ROUTER_TT = 512
GROUP_SIZE = N_EXPERTS // N_GROUPS


def _router_kernel(lg_ref, bias_ref, tri_ref, eidx_ref, w_ref, rank_ref, cnt_ref, carry_ref):
    i = pl.program_id(0)

    @pl.when(i == 0)
    def _():
        carry_ref[...] = jnp.zeros_like(carry_ref)

    tt = lg_ref.shape[0]
    shape3 = (N_GROUPS, GROUP_SIZE, tt)
    neg = -jnp.inf
    logits_t = lg_ref[...].T[:N_EXPERTS]
    scores = jax.nn.sigmoid(logits_t).reshape(shape3)
    sel = scores + bias_ref[...].reshape(N_GROUPS, GROUP_SIZE, 1)
    gi = lax.broadcasted_iota(I32, shape3, 0)
    ji = lax.broadcasted_iota(I32, shape3, 1)
    ei = gi * GROUP_SIZE + ji
    m1 = jnp.max(sel, axis=1, keepdims=True)
    first = jnp.min(jnp.where(sel == m1, ji, GROUP_SIZE), axis=1, keepdims=True)
    m2 = jnp.max(jnp.where(ji == first, neg, sel), axis=1, keepdims=True)
    cur = m1 + m2
    g1 = lax.broadcasted_iota(I32, cur.shape, 0)
    gmask = jnp.zeros(cur.shape, jnp.bool_)
    for _ in range(TOPK_GROUPS):
        mx = jnp.max(cur, axis=0, keepdims=True)
        idx = jnp.min(jnp.where(cur == mx, g1, N_GROUPS), axis=0, keepdims=True)
        hit = g1 == idx
        gmask = jnp.logical_or(gmask, hit)
        cur = jnp.where(hit, neg, cur)
    masked = jnp.where(gmask, sel, neg)

    def all_max(a):
        return jnp.max(jnp.max(a, axis=1, keepdims=True), axis=0, keepdims=True)

    def all_min(a):
        return jnp.min(jnp.min(a, axis=1, keepdims=True), axis=0, keepdims=True)

    def all_sum(a):
        return jnp.sum(jnp.sum(a, axis=1, keepdims=True), axis=0, keepdims=True)

    hits, idxs, ws = [], [], []
    for _ in range(TOP_K):
        mx = all_max(masked)
        idx = all_min(jnp.where(masked == mx, ei, N_EXPERTS))
        hit = ei == idx
        hits.append(hit)
        idxs.append(idx)
        ws.append(all_sum(jnp.where(hit, scores, 0.0)))
        masked = jnp.where(hit, neg, masked)
    wsum = ws[0]
    for wk in ws[1:]:
        wsum = wsum + wk
    onehot = jnp.zeros(shape3, F32)
    for hit in hits:
        onehot = onehot + hit.astype(F32)
    oh2 = onehot.reshape(N_EXPERTS, tt)
    before = jnp.dot(oh2.astype(BF16), tri_ref[...], preferred_element_type=F32) + carry_ref[:, 0:1]
    before3 = before.reshape(shape3)
    for k in range(TOP_K):
        eidx_ref[k:k + 1, :] = idxs[k].reshape(1, tt)
        w_ref[k:k + 1, :] = (ws[k] / wsum * ROUTED_SCALE).reshape(1, tt)
        rank_ref[k:k + 1, :] = all_sum(jnp.where(hits[k], before3, 0.0)).reshape(1, tt).astype(I32)
    eidx_ref[TOP_K:, :] = jnp.zeros((8 - TOP_K, tt), I32)
    w_ref[TOP_K:, :] = jnp.zeros((8 - TOP_K, tt), F32)
    rank_ref[TOP_K:, :] = jnp.zeros((8 - TOP_K, tt), I32)
    carry_ref[...] = carry_ref[...] + jnp.sum(oh2, axis=1, keepdims=True)
    cnt_ref[...] = carry_ref[...]


def moe_route(logits_t, bias):
    tt = ROUTER_TT
    tri = jnp.asarray(np.triu(np.ones((tt, tt), np.float32), 1), BF16)
    row8 = pl.BlockSpec((8, tt), lambda i: (0, i))
    return pl.pallas_call(
        _router_kernel,
        grid=(T_ALL // tt,),
        in_specs=[pl.BlockSpec((tt, LANE), lambda i: (i, 0)),
                  pl.BlockSpec((N_EXPERTS, 1), lambda i: (0, 0)),
                  pl.BlockSpec((tt, tt), lambda i: (0, 0))],
        out_specs=[row8, row8, row8, pl.BlockSpec((N_EXPERTS, LANE), lambda i: (0, 0))],
        out_shape=[jax.ShapeDtypeStruct((8, T_ALL), I32), jax.ShapeDtypeStruct((8, T_ALL), F32),
                   jax.ShapeDtypeStruct((8, T_ALL), I32), jax.ShapeDtypeStruct((N_EXPERTS, LANE), F32)],
        scratch_shapes=[pltpu.VMEM((N_EXPERTS, LANE), F32)],
        compiler_params=_cp(("arbitrary",)),
        name="moe_route",
    )(logits_t, bias, tri)


N_SLOTS = MOE_BLOCKS * MOE_TM
D_PACK = D_MODEL // 2


def _pack_bf16_pairs(x):
    n = x.shape[1] // 2
    hi = pltpu.bitcast(x[:, :n].astype(BF16).astype(F32), I32)
    lo = pltpu.bitcast(x[:, n:].astype(BF16).astype(F32), I32)
    return hi | lax.shift_right_logical(lo, 16)


def _unpack_bf16_pairs(u):
    hi = pltpu.bitcast(u & jnp.int32(-65536), F32)
    lo = pltpu.bitcast(lax.shift_left(u, 16), F32)
    return jnp.concatenate([hi, lo], axis=1)


def _dispatch_kernel(slots_ref, h2p_ref, xs_hbm, sem, *, tm):
    i = pl.program_id(0)

    def body(r, _):
        for k in range(TOP_K):
            s = slots_ref[(i * tm + r) * TOP_K + k]
            pltpu.make_async_copy(h2p_ref.at[pl.ds(r, 1), :], xs_hbm.at[pl.ds(s, 1), :], sem).start()
        return 0

    lax.fori_loop(0, tm, body, 0)
    for k in range(TOP_K):
        pltpu.make_async_copy(h2p_ref, xs_hbm.at[pl.ds(0, tm), :], sem).wait()


def moe_dispatch(slots, h2p):
    tm = 256
    return pl.pallas_call(
        functools.partial(_dispatch_kernel, tm=tm),
        grid_spec=pltpu.PrefetchScalarGridSpec(
            num_scalar_prefetch=1,
            grid=(T_ALL // tm,),
            in_specs=[pl.BlockSpec((tm, D_PACK), lambda i, s: (i, 0))],
            out_specs=pl.BlockSpec(memory_space=pl.ANY),
            scratch_shapes=[pltpu.SemaphoreType.DMA(())]),
        out_shape=jax.ShapeDtypeStruct((N_SLOTS, D_PACK), I32),
        compiler_params=_cp(("arbitrary",), unchecked_dma=True),
        name="moe_dispatch",
    )(slots, h2p)


def _gmm_kernel(be_ref, bv_ref, first_ref, par_ref, nxt_ref, x_ref, wg_hbm, wu_hbm, wd_hbm, y_ref,
                wgf, wuf, wdf, wgb, wub, wdb, wsem, *, layer):
    i = pl.program_id(0)

    def fetch(e, slot):
        return (pltpu.make_async_copy(wg_hbm.at[layer, e], wgf.at[slot], wsem.at[slot]),
                pltpu.make_async_copy(wu_hbm.at[layer, e], wuf.at[slot], wsem.at[slot]),
                pltpu.make_async_copy(wd_hbm.at[layer, e], wdf.at[slot], wsem.at[slot]))

    @pl.when(i == 0)
    def _():
        for cp in fetch(be_ref[0], 0):
            cp.start()

    @pl.when(first_ref[i] == 1)
    def _():
        slot = par_ref[i]
        for cp in fetch(be_ref[i], slot):
            cp.wait()
        nxt = nxt_ref[i]

        @pl.when(nxt >= 0)
        def _():
            for cp in fetch(nxt, 1 - slot):
                cp.start()

        step = 512
        for r in range(0, D_MODEL, step):
            wgb[r:r + step, :] = wgf[slot, r:r + step, :].astype(BF16)
            wub[r:r + step, :] = wuf[slot, r:r + step, :].astype(BF16)
        for r in range(0, D_EXPERT, 128):
            wdb[r:r + 128, :] = wdf[slot, r:r + 128, :].astype(BF16)

    nv = bv_ref[i]

    @pl.when(nv == 0)
    def _():
        y_ref[...] = jnp.zeros_like(y_ref)

    @pl.when(nv > 0)
    def _():
        row = lax.broadcasted_iota(I32, (x_ref.shape[0], 1), 0)
        xin = jnp.where(row < nv, x_ref[...], 0)
        x = _unpack_bf16_pairs(xin).astype(BF16)
        g = jnp.dot(x, wgb[...], preferred_element_type=F32)
        u = jnp.dot(x, wub[...], preferred_element_type=F32)
        a = (g * jax.nn.sigmoid(g) * u).astype(BF16)
        y_ref[...] = _pack_bf16_pairs(jnp.dot(a, wdb[...], preferred_element_type=F32))


def moe_experts(block_e, block_valid, block_first, block_par, block_next, xs, w_gate, w_up, w_down, layer):
    tm = MOE_TM
    row = pl.BlockSpec((tm, D_PACK), lambda i, *_: (i, 0))
    anyspace = pl.BlockSpec(memory_space=pl.ANY)
    return pl.pallas_call(
        functools.partial(_gmm_kernel, layer=layer),
        grid_spec=pltpu.PrefetchScalarGridSpec(
            num_scalar_prefetch=5,
            grid=(MOE_BLOCKS,),
            in_specs=[row, anyspace, anyspace, anyspace],
            out_specs=row,
            scratch_shapes=[pltpu.VMEM((2, D_MODEL, D_EXPERT), F32), pltpu.VMEM((2, D_MODEL, D_EXPERT), F32),
                            pltpu.VMEM((2, D_EXPERT, D_MODEL), F32),
                            pltpu.VMEM((D_MODEL, D_EXPERT), BF16), pltpu.VMEM((D_MODEL, D_EXPERT), BF16),
                            pltpu.VMEM((D_EXPERT, D_MODEL), BF16), pltpu.SemaphoreType.DMA((2,))]),
        out_shape=jax.ShapeDtypeStruct((N_SLOTS, D_PACK), I32),
        compiler_params=_cp(("arbitrary",)),
        name="moe_experts",
    )(block_e, block_valid, block_first, block_par, block_next, xs, w_gate, w_up, w_down)


def _combine_kernel(slots_ref, x1_ref, h2p_ref, w_ref, g2_ref, sg_ref, su_ref, sd_ref, ys_hbm, out_ref, buf, sem,
                    *, tm, nsteps):
    i = pl.program_id(0)

    def issue(step, slot):
        def body(r, _):
            for k in range(TOP_K):
                s = slots_ref[(step * tm + r) * TOP_K + k]
                pltpu.make_async_copy(ys_hbm.at[pl.ds(s, 1), :], buf.at[slot, pl.ds(k * tm + r, 1), :],
                                      sem.at[slot]).start()
            return 0

        lax.fori_loop(0, tm, body, 0)

    @pl.when(i == 0)
    def _():
        issue(0, 0)

    @pl.when(i + 1 < nsteps)
    def _():
        issue(i + 1, (i + 1) % 2)

    slot = i % 2
    pltpu.make_async_copy(ys_hbm.at[pl.ds(0, TOP_K * tm), :], buf.at[slot], sem.at[slot]).wait()
    hb = _unpack_bf16_pairs(h2p_ref[...]).astype(BF16)
    g = jnp.dot(hb, sg_ref[...], preferred_element_type=F32)
    u = jnp.dot(hb, su_ref[...], preferred_element_type=F32)
    acc = jnp.dot((g * jax.nn.sigmoid(g) * u).astype(BF16), sd_ref[...], preferred_element_type=F32)
    w = w_ref[...]
    for k in range(TOP_K):
        acc = acc + w[:, k:k + 1] * _unpack_bf16_pairs(buf[slot, k * tm:(k + 1) * tm, :])
    out_ref[...] = x1_ref[...] + g2_ref[...] * acc


def moe_combine(slots, x1, h2p, w_tok, mod4, layer, sg, su, sd, ys):
    tm = 256
    nsteps = T_ALL // tm
    kern = functools.partial(_combine_kernel, tm=tm, nsteps=nsteps)
    row = lambda n: pl.BlockSpec((tm, n), lambda i, s: (i, 0))
    return pl.pallas_call(
        kern,
        grid_spec=pltpu.PrefetchScalarGridSpec(
            num_scalar_prefetch=1,
            grid=(nsteps,),
            in_specs=[row(D_MODEL), row(D_PACK), row(8),
                      pl.BlockSpec((None, None, 1, D_MODEL),
                                   lambda i, s: (layer, _mod_row(i, tm) * 6 + 5, 0, 0)),
                      pl.BlockSpec((D_MODEL, D_EXPERT), lambda i, s: (0, 0)),
                      pl.BlockSpec((D_MODEL, D_EXPERT), lambda i, s: (0, 0)),
                      pl.BlockSpec((D_EXPERT, D_MODEL), lambda i, s: (0, 0)),
                      pl.BlockSpec(memory_space=pl.ANY)],
            out_specs=row(D_MODEL),
            scratch_shapes=[pltpu.VMEM((2, TOP_K * tm, D_PACK), I32), pltpu.SemaphoreType.DMA((2,))]),
        out_shape=jax.ShapeDtypeStruct((T_ALL, D_MODEL), F32),
        compiler_params=_cp(("arbitrary",), unchecked_dma=True),
        name="moe_combine",
    )(slots, x1, h2p, w_tok, mod4, sg, su, sd, ys)


def moe_ffn(x1, h2p, logits_t, mod4, layer, moe_bias, w_gate, w_up, w_down, sg, su, sd):
    eidx, w_t, rank, cnt = moe_route(logits_t, moe_bias)
    counts = cnt[:, 0].astype(I32)
    padded = (counts + MOE_TM - 1) // MOE_TM * MOE_TM
    pad_end = jnp.cumsum(padded)
    pad_start = pad_end - padded
    experts = jnp.arange(N_EXPERTS, dtype=I32)
    base = jnp.sum(jnp.where(eidx[:TOP_K, :, None] == experts, pad_start, 0), axis=-1)
    slots = (base + rank[:TOP_K]).T.reshape(-1)
    starts = jnp.arange(MOE_BLOCKS, dtype=I32) * MOE_TM
    block_e = jnp.minimum(jnp.sum((pad_end[None, :] <= starts[:, None]).astype(I32), axis=1), N_EXPERTS - 1)
    mine = block_e[:, None] == experts
    left = jnp.sum(jnp.where(mine, counts + pad_start, 0), axis=1) - starts
    block_valid = jnp.clip(left, 0, MOE_TM).astype(I32)
    nonempty = counts > 0
    pick = lambda v: jnp.sum(jnp.where(mine, v, 0), axis=1)
    block_first = jnp.logical_and(starts == pick(pad_start), block_valid > 0).astype(I32)
    block_par = pick(jnp.cumsum(nonempty.astype(I32)) - 1) % 2
    later = lax.cummin(jnp.where(nonempty, experts, N_EXPERTS), axis=0, reverse=True)
    nxt_e = jnp.concatenate([later[1:], jnp.full((1,), N_EXPERTS, I32)])
    block_next = pick(jnp.where(nxt_e >= N_EXPERTS, -1, nxt_e))
    xs = moe_dispatch(slots, h2p)
    ys = moe_experts(block_e, block_valid, block_first, block_par, block_next, xs, w_gate, w_up, w_down, layer)
    return moe_combine(slots, x1, h2p, w_t.T, mod4, layer, sg, su, sd, ys)


def _ml_gate_layouts(gp, gate_bias):
    nt = gp.shape[0]
    g16 = gp[:, :4 * ML_H].reshape(nt, 2, 2, ML_H)
    gr = jnp.transpose(g16, (3, 0, 1, 2)).reshape(ML_H, nt, 4)
    gc = jnp.transpose(gr, (0, 2, 1))
    b = jnp.transpose(gate_bias, (2, 0, 1)).reshape(ML_H, 4)
    return gr, gc, b[:, None, :], b[:, :, None]


def kernel(x_prompt, x_sample, c, cache_k, cache_v, state_mlstm_C, state_mlstm_n, state_mlstm_m, state_rglru,
           c_ctx, w_mod, b_mod, norm1, norm2, w_in, hy_short, hy_w1, hy_b1, hy_w2, hy_b2, hy_w3, hy_freq,
           hy_decay, hy_skip, ml_gate_bias, ml_out_norm, lru_conv_w, lru_conv_b, lru_wa, lru_ba, lru_wx, lru_bx,
           lru_lambda, da_q_norm, da_k_norm, da_lambda, da_sub_norm, w_branch, w_out, moe_router, moe_bias,
           moe_w_gate, moe_w_up, moe_w_down, sh_w_gate, sh_w_up, sh_w_down):
    x = jnp.concatenate([x_prompt.reshape(T_CTX, D_MODEL), x_sample.reshape(T_DEN, D_MODEL)], axis=0)
    cvecs = jnp.concatenate([c_ctx[None], c, jnp.zeros((N_MOD_ROWS - 1 - DEC_BATCH, D_MODEL), F32)], axis=0)
    mod4 = modulation_all(cvecs, w_mod, b_mod).reshape(DEPTH, N_MOD_ROWS * 6, 1, D_MODEL)
    norm1_3 = norm1.reshape(DEPTH, 1, D_MODEL)
    norm2_3 = norm2.reshape(DEPTH, 1, D_MODEL)
    cache_k4 = cache_k.reshape(DEC_BATCH, DEPTH, -1, MIX_W)
    cache_v4 = cache_v.reshape(DEC_BATCH, DEPTH, -1, MIX_W)
    den_off = T_CTX // DEC_SEQ
    new_k, new_v, new_c, new_n, new_m, new_h = [], [], [], [], [], []
    for l in range(DEPTH):
        lam_init = 0.8 - 0.6 * math.exp(-0.3 * l)
        wl = w_in[l]
        n_ml = 4 * ML_H
        w_main = jnp.concatenate([wl[:, :7 * MIX_W], wl[:, 7 * MIX_W + n_ml:12 * MIX_W + n_ml]], axis=1).astype(BF16)
        w_g16 = jnp.pad(wl[:, 7 * MIX_W:7 * MIX_W + n_ml], ((0, 0), (0, LANE - n_ml))).astype(BF16)
        w_gate = wl[:, 12 * MIX_W + n_ml:].astype(BF16)
        proj, h, gp = in_projection(x, norm1_3, mod4, l, w_main, w_g16)

        w1p = jnp.pad(hy_w1[l], ((0, LANE - HY_PE), (0, 0)))
        ws3 = jnp.transpose(hy_short[l].reshape(3, 3, MIX_W), (1, 0, 2))
        y_hy = []
        for off, nseq, sl in ((0, BATCH, SEQ), (den_off, DEC_BATCH, DEC_SEQ)):
            kfilt = hyena_filter(sl, w1p, hy_b1[l][None], hy_w2[l], hy_b2[l][None], hy_freq[l][None], hy_w3[l],
                                 hy_decay[l][None])
            if sl <= 512:
                y_hy.append(hyena_conv_direct(proj, off, nseq, sl, ws3, hyena_filter_fft_direct(sl, kfilt), hy_skip[l]))
            else:
                y_hy.append(hyena_conv(proj, off, nseq, sl, ws3, hyena_filter_fft(sl, kfilt), hy_skip[l]))

        gr, gc, b_r, b_c = _ml_gate_layouts(gp, ml_gate_bias[l])
        gain = ml_out_norm[l][None]
        y_ml0, c_new, n_new, m_new = mlstm_mixer(proj, gr, gc, b_r, b_c, gain, 0, BATCH, SEQ, None)
        st = (state_mlstm_C[:, l], state_mlstm_n[:, l][:, :, :, None, :],
              jnp.broadcast_to(state_mlstm_m[:, l][:, :, :, None, None], (DEC_BATCH, 2, ML_H, 1, LANE)))
        y_ml1 = mlstm_mixer(proj, gr, gc, b_r, b_c, gain, den_off, DEC_BATCH, DEC_SEQ, st)[0]

        lru_args = (lru_conv_w[l], lru_conv_b[l][None], lru_wa[l], lru_ba[l][:, None, :], lru_wx[l],
                    lru_bx[l][:, None, :], lru_lambda[l][:, None, :])
        y_lr0, h_new = rglru_mixer(proj, *lru_args, 0, BATCH, SEQ, None)
        y_lr1 = rglru_mixer(proj, *lru_args, den_off, DEC_BATCH, DEC_SEQ, state_rglru[:, l][:, :, None, :])[0]

        q_rot, k_rot, k_norm = qk_prepare(proj, jnp.tile(da_q_norm[l], 2)[None], jnp.tile(da_k_norm[l], 2)[None])
        sub_gain = da_sub_norm[l][None]
        y_da0 = diff_attention(q_rot, k_rot, proj, da_lambda[l], sub_gain, lam_init, 0, BATCH, SEQ, None, l)
        y_da1 = diff_attention(q_rot, k_rot, proj, da_lambda[l], sub_gain, lam_init, den_off, DEC_BATCH, DEC_SEQ,
                               (cache_k4, cache_v4), l)

        ys = [jnp.concatenate(p, axis=0) for p in (y_hy, (y_ml0, y_ml1), (y_lr0, y_lr1), (y_da0, y_da1))]
        merged = branch_merge(h, ys, w_gate, w_branch[l].astype(BF16))
        x1, h2, logits_t = out_projection(merged, x, mod4, l, w_out[l].astype(BF16), norm2_3,
                                          jnp.pad(moe_router[l], ((0, 0), (0, LANE - N_EXPERTS))))
        x = moe_ffn(x1, h2, logits_t, mod4, l, moe_bias[l][:, None], moe_w_gate, moe_w_up, moe_w_down,
                    sh_w_gate[l].astype(BF16), sh_w_up[l].astype(BF16), sh_w_down[l].astype(BF16))

        new_k.append(k_norm[:T_CTX].reshape(BATCH, SEQ, DA_H, 2, DA_DK))
        new_v.append(proj[:T_CTX, CB_DAV * LANE:(CB_DAV + DA_H) * LANE].astype(F32).reshape(BATCH, SEQ, DA_H, -1))
        new_c.append(c_new)
        new_n.append(n_new[:, :, :, 0, :])
        new_m.append(m_new[:, :, :, 0, 0])
        new_h.append(h_new[:, :, 0, :])
    y_prompt = x[:T_CTX].reshape(BATCH, SEQ, D_MODEL)
    y_sample = x[T_CTX:].reshape(DEC_BATCH, DEC_SEQ, D_MODEL)
    stack = lambda parts: jnp.stack(parts, axis=1)
    return (y_prompt, y_sample, stack(new_k), stack(new_v), stack(new_c), stack(new_n), stack(new_m), stack(new_h))
```

```python
import functools
import math

import numpy as np
import jax
import jax.numpy as jnp
from jax import lax
from jax.experimental import pallas as pl
from jax.experimental.pallas import tpu as pltpu

F32 = jnp.float32
BF16 = jnp.bfloat16
I32 = jnp.int32
HIGHEST = lax.Precision.HIGHEST

D_MODEL = 2048
BATCH = 16
SEQ = 256
DEPTH = 2
DEC_BATCH = 4
DEC_SEQ = 4096
GRID_W = 64
N_BRANCH = 4
MIX_W = 512
HY_BANDS = 16
HY_PE = 1 + 2 * HY_BANDS
HY_FFN = 64
ML_H = 4
ML_DK = 128
LRU_H = 4
LRU_C = 8.0
DA_H = 4
DA_DK = 64
ROPE_BASE = 10000.0
N_EXPERTS = 64
TOP_K = 6
N_GROUPS = 8
TOPK_GROUPS = 4
D_EXPERT = 512
ROUTED_SCALE = 2.5
EPS = 1e-6

LANE = 128
T_CTX = BATCH * SEQ
T_DEN = DEC_BATCH * DEC_SEQ
T_ALL = T_CTX + T_DEN
N_MOD_ROWS = 8
VMEM_LIMIT = 56 * 1024 * 1024

CB_HY = 0
CB_MLQ, CB_MLK, CB_MLV, CB_MLO = 12, 16, 20, 24
CB_LRX, CB_LRY = 28, 32
CB_DAQ, CB_DAK, CB_DAV = 36, 40, 44
N_MAIN = 6144


def _cp(sem, vmem=VMEM_LIMIT, unchecked_dma=False):
    return pltpu.CompilerParams(dimension_semantics=sem, vmem_limit_bytes=vmem, disable_bounds_checks=unchecked_dma)


def _mod_row(i, rows_per_block):
    nctx = T_CTX // rows_per_block
    per = DEC_SEQ // rows_per_block
    return jnp.where(i < nctx, 0, 1 + (i - nctx) // per)


def _mod_spec(layer, seg, rows_per_block, ngrid):
    if ngrid == 1:
        return pl.BlockSpec((None, None, 1, D_MODEL), lambda i: (layer, _mod_row(i, rows_per_block) * 6 + seg, 0, 0))
    return pl.BlockSpec((None, None, 1, D_MODEL), lambda i, j: (layer, _mod_row(i, rows_per_block) * 6 + seg, 0, 0))


def _mod_kernel(c_ref, w_ref, b_ref, o_ref):
    c = c_ref[...]
    a = (c * jax.nn.sigmoid(c)).astype(BF16)
    o_ref[...] = jnp.dot(a, w_ref[...].astype(BF16), preferred_element_type=F32) + b_ref[...]


def modulation_all(cvecs, w_mod, b_mod):
    tn = 1024
    n = 6 * D_MODEL
    return pl.pallas_call(
        _mod_kernel,
        grid=(DEPTH, n // tn),
        in_specs=[pl.BlockSpec((N_MOD_ROWS, D_MODEL), lambda l, j: (0, 0)),
                  pl.BlockSpec((None, D_MODEL, tn), lambda l, j: (l, 0, j)),
                  pl.BlockSpec((None, 1, tn), lambda l, j: (l, 0, j))],
        out_specs=pl.BlockSpec((None, N_MOD_ROWS, tn), lambda l, j: (l, 0, j)),
        out_shape=jax.ShapeDtypeStruct((DEPTH, N_MOD_ROWS, n), F32),
        compiler_params=_cp(("parallel", "parallel")),
        name="modulation",
    )(cvecs, w_mod, b_mod.reshape(DEPTH, 1, n))


def _norm_mod(x, g, sc, sh):
    ms = jnp.mean(x * x, axis=-1, keepdims=True)
    return (x * lax.rsqrt(ms + EPS) * g) * (1.0 + sc) + sh


def _inproj_kernel(x_ref, g_ref, sh_ref, sc_ref, w_ref, wg_ref, proj_ref, h_ref, gp_ref, *, tm, sub):
    j = pl.program_id(1)

    @pl.when(j == 0)
    def _():
        g = g_ref[...]
        sc = sc_ref[...]
        sh = sh_ref[...]

        def body(r, _):
            rows = pl.ds(pl.multiple_of(r * sub, sub), sub)
            hb = _norm_mod(x_ref[rows, :], g, sc, sh).astype(BF16)
            h_ref[rows, :] = hb
            gp_ref[rows, :] = jnp.dot(hb, wg_ref[...], preferred_element_type=F32)
            return 0

        lax.fori_loop(0, tm // sub, body, 0)

    proj_ref[...] = jnp.dot(h_ref[...], w_ref[...], preferred_element_type=F32).astype(BF16)


def in_projection(x, norm_g, mod4, layer, w_main, w_gate16):
    tm, tn = 1024, 1024
    kern = functools.partial(_inproj_kernel, tm=tm, sub=256)
    return pl.pallas_call(
        kern,
        grid=(T_ALL // tm, N_MAIN // tn),
        in_specs=[pl.BlockSpec((tm, D_MODEL), lambda i, j: (i, 0)),
                  pl.BlockSpec((None, 1, D_MODEL), lambda i, j: (layer, 0, 0)),
                  _mod_spec(layer, 0, tm, 2),
                  _mod_spec(layer, 1, tm, 2),
                  pl.BlockSpec((D_MODEL, tn), lambda i, j: (0, j)),
                  pl.BlockSpec((D_MODEL, LANE), lambda i, j: (0, 0))],
        out_specs=[pl.BlockSpec((tm, tn), lambda i, j: (i, j)),
                   pl.BlockSpec((tm, D_MODEL), lambda i, j: (i, 0)),
                   pl.BlockSpec((tm, LANE), lambda i, j: (i, 0))],
        out_shape=[jax.ShapeDtypeStruct((T_ALL, N_MAIN), BF16),
                   jax.ShapeDtypeStruct((T_ALL, D_MODEL), BF16),
                   jax.ShapeDtypeStruct((T_ALL, LANE), F32)],
        compiler_params=_cp(("parallel", "arbitrary")),
        name="in_projection",
    )(x, norm_g, mod4, mod4, w_main, w_gate16)


def _fft_factors(seq_len):
    n2 = 64
    return 2 * seq_len // n2, n2


def _fft_pages(seq_len):
    n1, _ = _fft_factors(seq_len)
    used = n1 // 2 + 1
    return used, -(-used // 8) * 8


@functools.lru_cache(maxsize=None)
def _dft_tables(seq_len):
    n1, n2 = _fft_factors(seq_len)
    used, npg = _fft_pages(seq_len)
    n = n1 * n2
    th1 = 2.0 * np.pi * np.outer(np.arange(npg, dtype=np.float64), np.arange(n1, dtype=np.float64)) / n1
    w1 = np.concatenate([np.cos(th1), -np.sin(th1)], axis=0)
    w1h = w1[:, : n1 // 2]
    wt = np.where(np.arange(npg) < used, 2.0, 0.0)
    wt[0] = 1.0
    wt[n1 // 2] = 1.0
    th4 = th1[:, : n1 // 2].T
    w4 = np.concatenate([np.cos(th4) * wt, -np.sin(th4) * wt], axis=1) / n
    k1 = np.arange(npg, dtype=np.float64)[:, None, None]
    k2 = np.arange(n2, dtype=np.float64)[None, :, None]
    m2 = np.arange(n2, dtype=np.float64)[None, None, :]
    ph = -2.0 * np.pi * (m2 * k1 / n + m2 * k2 / n2)
    gr, gi = np.cos(ph), np.sin(ph)
    g = np.concatenate([np.concatenate([gr, -gi], axis=2), np.concatenate([gi, gr], axis=2)], axis=1)
    gt = np.transpose(g, (0, 2, 1))
    return tuple(np.asarray(t, np.float32) for t in (w1, w1h, g, gt, w4))


@functools.lru_cache(maxsize=None)
def _direct_dft_tables(seq_len):
    n = 2 * seq_len
    th = 2.0 * np.pi * np.outer(np.arange(n, dtype=np.float64), np.arange(n, dtype=np.float64)) / n
    wf = np.concatenate([np.cos(th), -np.sin(th)], axis=0)
    wi = np.concatenate([np.cos(th), -np.sin(th)], axis=1)[:seq_len] / n
    return np.asarray(wf, np.float32), np.asarray(wi, np.float32)


def _dft_tables_bf16(seq_len):
    return tuple(jnp.asarray(t, BF16) for t in _dft_tables(seq_len))


@functools.lru_cache(maxsize=None)
def _filter_positions(seq_len):
    pos = np.concatenate([np.arange(seq_len), [0], np.arange(seq_len - 1, 0, -1)]).astype(np.float64)
    tn = pos / seq_len
    bands = np.linspace(1e-4, HY_BANDS - 1, HY_BANDS)
    ang = (2.0 * math.pi / seq_len) * pos[:, None] * bands
    pe = np.zeros((2 * seq_len, LANE), np.float64)
    pe[:, 0] = tn
    pe[:, 1:1 + HY_BANDS] = np.cos(ang)
    pe[:, 1 + HY_BANDS:HY_PE] = np.sin(ang)
    return np.asarray(pe, np.float32), np.asarray(tn[:, None], np.float32)


def _hy_mlp_kernel(pe_ref, w1_ref, b1_ref, w2_ref, b2_ref, fr_ref, z_ref):
    fr = fr_ref[...]
    z = jnp.sin(fr * (jnp.dot(pe_ref[...], w1_ref[...], precision=HIGHEST, preferred_element_type=F32) + b1_ref[...]))
    z_ref[...] = jnp.sin(fr * (jnp.dot(z, w2_ref[...], precision=HIGHEST, preferred_element_type=F32) + b2_ref[...]))


def _hy_filter_kernel(z_ref, tn_ref, w3a_ref, w3b_ref, dca_ref, dcb_ref, k_ref, *, seq_len):
    def taps(rows, w3_ref, dc_ref):
        f = jnp.dot(z_ref[rows, :].astype(BF16), w3_ref[...].astype(BF16), preferred_element_type=F32)
        return f * jnp.exp(-tn_ref[rows, :] * jnp.abs(dc_ref[...]))

    ka = taps(pl.ds(0, seq_len), w3a_ref, dca_ref)
    kb = taps(pl.ds(seq_len, seq_len), w3b_ref, dcb_ref)
    row = lax.broadcasted_iota(I32, ka.shape, 0)
    ka = ka + jnp.where(row == 0, taps(pl.ds(0, 8), w3b_ref, dcb_ref)[0:1], 0.0)
    kb = jnp.where(row == 0, 0.0, kb)
    inv = 1.0 / (jnp.sum(jnp.abs(ka), axis=0, keepdims=True) + jnp.sum(jnp.abs(kb), axis=0, keepdims=True))
    k_ref[0:seq_len, :] = ka * inv
    k_ref[seq_len:, :] = kb * inv


def hyena_filter(seq_len, w1p, b1, w2, b2, freq, w3, decay):
    pe, tn = (jnp.asarray(t) for t in _filter_positions(seq_len))
    n = 2 * seq_len
    tr = min(n, 1024)
    full = lambda shape: pl.BlockSpec(shape, lambda j: (0,) * len(shape))
    z = pl.pallas_call(
        _hy_mlp_kernel,
        grid=(n // tr,),
        in_specs=[pl.BlockSpec((tr, LANE), lambda j: (j, 0)), full((LANE, HY_FFN)), full((1, HY_FFN)),
                  full((HY_FFN, HY_FFN)), full((1, HY_FFN)), full((1, HY_FFN))],
        out_specs=pl.BlockSpec((tr, HY_FFN), lambda j: (j, 0)),
        out_shape=jax.ShapeDtypeStruct((n, HY_FFN), F32),
        compiler_params=_cp(("parallel",)),
        name=f"hyena_filter_mlp_{seq_len}",
    )(pe, w1p, b1, w2, b2, freq)
    tc = 128
    nc = (2 * MIX_W) // tc
    kern = functools.partial(_hy_filter_kernel, seq_len=seq_len)
    return pl.pallas_call(
        kern,
        grid=(nc,),
        in_specs=[full((n, HY_FFN)), full((n, 1)),
                  pl.BlockSpec((HY_FFN, tc), lambda j: (0, j)),
                  pl.BlockSpec((HY_FFN, tc), lambda j: (0, nc + j)),
                  pl.BlockSpec((1, tc), lambda j: (0, j)),
                  pl.BlockSpec((1, tc), lambda j: (0, nc + j))],
        out_specs=pl.BlockSpec((n, tc), lambda j: (0, j)),
        out_shape=jax.ShapeDtypeStruct((n, 2 * MIX_W), F32),
        compiler_params=_cp(("parallel",)),
        name=f"hyena_filter_{seq_len}",
    )(z, tn, w3, w3, decay, decay)


def _split_bf16(x):
    hi = x.astype(BF16)
    lo = (x - hi.astype(F32)).astype(BF16)
    return hi, lo


def _fft_stage1(load_rows, w1, a_ref, npg, n2, split):
    def body(m, _):
        xs = load_rows(m)
        if split:
            hi, lo = _split_bf16(xs)
            r = jnp.dot(w1, hi, preferred_element_type=F32) + jnp.dot(w1, lo, preferred_element_type=F32)
        else:
            r = jnp.dot(w1, xs.astype(BF16), preferred_element_type=F32)
        a_ref[pl.ds(m, npg, stride=2 * n2), :] = r[:npg]
        a_ref[pl.ds(n2 + m, npg, stride=2 * n2), :] = r[npg:]
        return 0

    lax.fori_loop(0, n2, body, 0, unroll=8)


def _hy_fft_kernel(k_ref, w1_ref, g_ref, kf_ref, a_ref, *, n1, n2, npg):
    w1 = w1_ref[...]
    _fft_stage1(lambda m: k_ref[pl.ds(m, n1, stride=n2), :], w1, a_ref, npg, n2, True)

    def body(p, _):
        rows = pl.ds(pl.multiple_of(p * 2 * n2, 2 * n2), 2 * n2)
        hi, lo = _split_bf16(a_ref[rows, :])
        g = g_ref[p]
        kf_ref[rows, :] = jnp.dot(g, hi, preferred_element_type=F32) + jnp.dot(g, lo, preferred_element_type=F32)
        return 0

    lax.fori_loop(0, npg, body, 0, unroll=8)


def hyena_filter_fft(seq_len, kfilt):
    n1, n2 = _fft_factors(seq_len)
    _, npg = _fft_pages(seq_len)
    w1, _, g, _, _ = _dft_tables_bf16(seq_len)
    tc = 128
    kern = functools.partial(_hy_fft_kernel, n1=n1, n2=n2, npg=npg)
    return pl.pallas_call(
        kern,
        grid=((2 * MIX_W) // tc,),
        in_specs=[pl.BlockSpec((n1 * n2, tc), lambda j: (0, j)),
                  pl.BlockSpec((2 * npg, n1), lambda j: (0, 0)),
                  pl.BlockSpec((npg, 2 * n2, 2 * n2), lambda j: (0, 0, 0))],
        out_specs=pl.BlockSpec((npg * 2 * n2, tc), lambda j: (0, j)),
        out_shape=jax.ShapeDtypeStruct((npg * 2 * n2, 2 * MIX_W), F32),
        scratch_shapes=[pltpu.VMEM((npg * 2 * n2, tc), F32)],
        compiler_params=_cp(("parallel",)),
        name=f"hyena_filter_fft_{seq_len}",
    )(kfilt, w1, g)


def _hy_fft_direct_kernel(k_ref, wf_ref, kf_ref):
    hi, lo = _split_bf16(k_ref[...])
    wf = wf_ref[...]
    kf_ref[...] = jnp.dot(wf, hi, preferred_element_type=F32) + jnp.dot(wf, lo, preferred_element_type=F32)


def hyena_filter_fft_direct(seq_len, kfilt):
    n = 2 * seq_len
    wf = jnp.asarray(_direct_dft_tables(seq_len)[0], BF16)
    tc = 256
    return pl.pallas_call(
        _hy_fft_direct_kernel,
        grid=((2 * MIX_W) // tc,),
        in_specs=[pl.BlockSpec((n, tc), lambda j: (0, j)), pl.BlockSpec((2 * n, n), lambda j: (0, 0))],
        out_specs=pl.BlockSpec((2 * n, tc), lambda j: (0, j)),
        out_shape=jax.ShapeDtypeStruct((2 * n, 2 * MIX_W), F32),
        compiler_params=_cp(("parallel",)),
        name=f"hyena_filter_fft_{seq_len}",
    )(kfilt, wf)


def _shift_rows(x, s, row):
    n = x.shape[0]
    y = pltpu.roll(x, s % n, 0)
    if s > 0:
        return jnp.where(row >= s, y, 0.0)
    return jnp.where(row < n + s, y, 0.0)


def _short_conv3(x_ref, w_ref):
    x = x_ref[...].astype(F32)
    row = lax.broadcasted_iota(I32, x.shape, 0)
    w = w_ref[...]
    return w[0:1] * _shift_rows(x, 1, row) + w[1:2] * x + w[2:3] * _shift_rows(x, -1, row)


def _hy_conv_kernel(v_ref, x1_ref, x2_ref, ws_ref, kf_ref, skip_ref, w1_ref, g_ref, gt_ref, w4_ref, y_ref,
                    z_ref, c_ref, a_ref, s_ref, *, n1, n2, used, npg):
    o = pl.program_id(2)
    nh = n1 // 2

    @pl.when(o == 0)
    def _():
        z_ref[...] = _short_conv3(v_ref, ws_ref.at[0])

    w1 = w1_ref[...]

    def stage1(m, _):
        xs = z_ref[pl.ds(m, nh, stride=n2), :].astype(BF16)
        s_ref[pl.ds(pl.multiple_of(m * 2 * npg, 2 * npg), 2 * npg), :] = jnp.dot(w1, xs, preferred_element_type=F32)
        return 0

    lax.fori_loop(0, n2, stage1, 0, unroll=16)
    a_ref[used * 2 * n2:, :] = jnp.zeros(((npg - used) * 2 * n2, a_ref.shape[1]), F32)

    def page(p, _):
        rows = pl.ds(pl.multiple_of(p * 2 * n2, 2 * n2), 2 * n2)
        ap = jnp.concatenate([s_ref[pl.ds(p, n2, stride=2 * npg), :],
                              s_ref[pl.ds(npg + p, n2, stride=2 * npg), :]], axis=0)
        x = jnp.dot(g_ref[p], ap.astype(BF16), preferred_element_type=F32)
        kf = kf_ref[rows, :]
        xr, xi = x[:n2], x[n2:]
        kr, ki = kf[:n2], kf[n2:]
        yc = jnp.concatenate([xr * kr - xi * ki, xr * ki + xi * kr], axis=0).astype(BF16)
        a_ref[rows, :] = jnp.dot(gt_ref[p], yc, preferred_element_type=F32)
        return 0

    lax.fori_loop(0, used, page, 0, unroll=13)

    w4 = w4_ref[...]

    def back(m, _):
        dr = a_ref[pl.ds(m, npg, stride=2 * n2), :]
        di = a_ref[pl.ds(n2 + m, npg, stride=2 * n2), :]
        d = jnp.concatenate([dr, di], axis=0).astype(BF16)
        c_ref[pl.ds(m, nh, stride=n2), :] = jnp.dot(w4, d, preferred_element_type=F32)
        return 0

    lax.fori_loop(0, n2, back, 0, unroll=16)

    @pl.when(o == 0)
    def _():
        z = z_ref[...]
        z_ref[...] = _short_conv3(x1_ref, ws_ref.at[1]) * (c_ref[...] + z * skip_ref[0:1])

    @pl.when(o == 1)
    def _():
        z = z_ref[...]
        y_ref[...] = (_short_conv3(x2_ref, ws_ref.at[2]) * (c_ref[...] + z * skip_ref[1:2])).astype(BF16)


def hyena_conv(proj, row_block_off, nseq, seq_len, w_short3, kf, skip):
    n1, n2 = _fft_factors(seq_len)
    used, npg = _fft_pages(seq_len)
    _, w1h, g, gt, w4 = _dft_tables_bf16(seq_len)
    rows_a = npg * 2 * n2
    tc = 128
    ncb = MIX_W // tc
    kern = functools.partial(_hy_conv_kernel, n1=n1, n2=n2, used=used, npg=npg)
    const3 = lambda c, b, o: (0, 0, 0)
    seg = lambda s: pl.BlockSpec((seq_len, tc), lambda c, b, o: (row_block_off + b, CB_HY + s * ncb + c))
    return pl.pallas_call(
        kern,
        grid=(ncb, nseq, 2),
        in_specs=[seg(0), seg(1), seg(2),
                  pl.BlockSpec((3, 3, tc), lambda c, b, o: (0, 0, c)),
                  pl.BlockSpec((rows_a, tc), lambda c, b, o: (0, o * ncb + c)),
                  pl.BlockSpec((2, tc), lambda c, b, o: (0, c)),
                  pl.BlockSpec((2 * npg, n1 // 2), lambda c, b, o: (0, 0)),
                  pl.BlockSpec((npg, 2 * n2, 2 * n2), const3, pipeline_mode=pl.Buffered(1)),
                  pl.BlockSpec((npg, 2 * n2, 2 * n2), const3, pipeline_mode=pl.Buffered(1)),
                  pl.BlockSpec((n1 // 2, 2 * npg), lambda c, b, o: (0, 0))],
        out_specs=pl.BlockSpec((seq_len, tc), lambda c, b, o: (b, c)),
        out_shape=jax.ShapeDtypeStruct((nseq * seq_len, MIX_W), BF16),
        scratch_shapes=[pltpu.VMEM((seq_len, tc), F32), pltpu.VMEM((seq_len, tc), F32),
                        pltpu.VMEM((rows_a, tc), F32), pltpu.VMEM((rows_a, tc), F32)],
        compiler_params=_cp(("parallel", "parallel", "arbitrary")),
        name=f"hyena_conv_{seq_len}",
    )(proj, proj, proj, w_short3, kf, skip, w1h, g, gt, w4)


def _hy_direct_kernel(u_ref, ws_ref, kf_ref, skip_ref, wf_ref, wi_ref, y_ref, *, n):
    w = MIX_W
    wf = wf_ref[...]
    wi = wi_ref[...]
    z = _short_conv3(u_ref.at[:, pl.ds(0, w)], ws_ref.at[0])
    for o in range(2):
        x = jnp.dot(wf, z.astype(BF16), preferred_element_type=F32)
        kf = kf_ref[:, o * w:(o + 1) * w]
        xr, xi = x[:n], x[n:]
        kr, ki = kf[:n], kf[n:]
        yc = jnp.concatenate([xr * kr - xi * ki, xr * ki + xi * kr], axis=0).astype(BF16)
        conv = jnp.dot(wi, yc, preferred_element_type=F32)
        gate = _short_conv3(u_ref.at[:, pl.ds((o + 1) * w, w)], ws_ref.at[o + 1])
        z = gate * (conv + z * skip_ref[o:o + 1])
    y_ref[...] = z.astype(BF16)


def hyena_conv_direct(proj, row_block_off, nseq, seq_len, w_short3, kf, skip):
    n = 2 * seq_len
    wf_np, wi_np = _direct_dft_tables(seq_len)
    wf = jnp.asarray(wf_np[:, :seq_len], BF16)
    wi = jnp.asarray(wi_np, BF16)
    kern = functools.partial(_hy_direct_kernel, n=n)
    return pl.pallas_call(
        kern,
        grid=(nseq,),
        in_specs=[pl.BlockSpec((seq_len, 3 * MIX_W), lambda b: (row_block_off + b, 0)),
                  pl.BlockSpec((3, 3, MIX_W), lambda b: (0, 0, 0)),
                  pl.BlockSpec((2 * n, 2 * MIX_W), lambda b: (0, 0)),
                  pl.BlockSpec((2, MIX_W), lambda b: (0, 0)),
                  pl.BlockSpec((2 * n, seq_len), lambda b: (0, 0)),
                  pl.BlockSpec((seq_len, 2 * n), lambda b: (0, 0))],
        out_specs=pl.BlockSpec((seq_len, MIX_W), lambda b: (b, 0)),
        out_shape=jax.ShapeDtypeStruct((nseq * seq_len, MIX_W), BF16),
        compiler_params=_cp(("parallel",)),
        name=f"hyena_conv_{seq_len}",
    )(proj, w_short3, kf, skip, wf, wi)


def _dot_nt(a, b):
    return lax.dot_general(a, b, (((1,), (1,)), ((), ())), preferred_element_type=F32)


ML_CHUNK = 256


def _log_sigmoid(x):
    return jnp.minimum(x, 0.0) - jnp.log1p(jnp.exp(-jnp.abs(x)))


def _mlstm_kernel(*refs, seq_len, has_state):
    if has_state:
        (q_ref, k_ref, v_ref, o_ref, gr_ref, gc_ref, br_ref, bc_ref, gain_ref, c0_ref, n0_ref, m0_ref,
         y_ref, cout_ref, nout_ref, mout_ref, hs0, hs1, cs0, cs1, ns0, ns1, ms0, ms1) = refs
    else:
        (q_ref, k_ref, v_ref, o_ref, gr_ref, gc_ref, br_ref, bc_ref, gain_ref,
         y_ref, cout_ref, nout_ref, mout_ref, hs0, hs1, cs0, cs1, ns0, ns1, ms0, ms1) = refs
    hs, cs, ns, ms = (hs0, hs1), (cs0, cs1), (ns0, ns1), (ms0, ms1)
    t = min(ML_CHUNK, seq_len)
    nc = seq_len // t
    scale = ML_DK ** -0.5
    for d in (0, 1):
        if has_state:
            cs[d][...] = c0_ref[d]
            ns[d][...] = n0_ref[d]
            ms[d][...] = m0_ref[d]
        else:
            cs[d][...] = jnp.zeros_like(cs[d])
            ns[d][...] = jnp.zeros_like(ns[d])
            ms[d][...] = jnp.zeros_like(ms[d])
    row = lax.broadcasted_iota(I32, (t, t), 0)
    col = lax.broadcasted_iota(I32, (t, t), 1)
    br = br_ref[...]
    bc = bc_ref[...]
    tris = (jnp.where(row <= col, 1.0, 0.0).astype(BF16), jnp.where(row >= col, 1.0, 0.0).astype(BF16))

    def cumsum_row(f_r, tri):
        f8 = jnp.broadcast_to(f_r, (8, t))
        hi = f8.astype(BF16)
        r1 = f8 - hi.astype(F32)
        mid = r1.astype(BF16)
        lo = (r1 - mid.astype(F32)).astype(BF16)
        out = (jnp.dot(hi, tri, preferred_element_type=F32) + jnp.dot(mid, tri, preferred_element_type=F32)
               + jnp.dot(lo, tri, preferred_element_type=F32))
        return out[0:1]

    def step(j, _):
        for d in (0, 1):
            jj = j if d == 0 else nc - 1 - j
            rows = pl.ds(pl.multiple_of(jj * t, t), t)
            qf = q_ref[rows, :].astype(F32) * scale
            qb = qf.astype(BF16)
            kb = k_ref[rows, :]
            vb = v_ref[rows, :]
            gr = gr_ref[rows, :] + br
            gc = gc_ref[:, rows] + bc
            i_c = gr[:, 2 * d:2 * d + 1]
            i_r = gc[2 * d:2 * d + 1, :]
            f_r = _log_sigmoid(gc[2 * d + 1:2 * d + 2, :])
            mask = (col <= row) if d == 0 else (col >= row)
            b_c = jnp.sum(jnp.where(mask, f_r, 0.0), axis=1, keepdims=True)
            b_r = cumsum_row(f_r, tris[d])
            logw = jnp.where(mask, b_c - b_r + i_r, -jnp.inf)
            m = ms[d][:, 0:1]
            g = b_c + m
            mt = jnp.maximum(g, jnp.max(logw, axis=1, keepdims=True))
            s = _dot_nt(qb, kb) * jnp.exp(logw - mt)
            inter = jnp.exp(g - mt)
            cm = cs[d][...]
            nv = ns[d][...]
            num = (jnp.dot(s.astype(BF16), vb, preferred_element_type=F32)
                   + inter * jnp.dot(qb, cm.astype(BF16), preferred_element_type=F32))
            den = jnp.sum(s, axis=1, keepdims=True) + inter * jnp.sum(qf * nv, axis=1, keepdims=True)
            hs[d][rows, :] = num / jnp.maximum(jnp.abs(den), jnp.exp(-mt))
            bl = jnp.sum(f_r, axis=1, keepdims=True)
            wlog_r = bl - b_r + i_r
            wlog_c = bl - b_c + i_c
            m_new = jnp.maximum(bl + m, jnp.max(wlog_r, axis=1, keepdims=True))
            dec = jnp.exp(bl + m - m_new)
            kw = jnp.exp(wlog_c - m_new) * kb.astype(F32)
            cs[d][...] = dec * cm + jnp.dot(kw.T.astype(BF16), vb, preferred_element_type=F32)
            ns[d][...] = dec * nv + jnp.sum(kw, axis=0, keepdims=True)
            ms[d][...] = jnp.broadcast_to(m_new, (1, LANE))
        return 0

    lax.fori_loop(0, nc, step, 0, unroll=2)
    hsum = hs0[...] + hs1[...]
    hn = hsum * lax.rsqrt(jnp.mean(hsum * hsum, axis=-1, keepdims=True) + EPS) * gain_ref[...]
    y_ref[...] = (jax.nn.sigmoid(o_ref[...].astype(F32)) * hn).astype(BF16)
    for d in (0, 1):
        cout_ref[d] = cs[d][...]
        nout_ref[d] = ns[d][...]
        mout_ref[d] = ms[d][...]


def mlstm_mixer(proj, gates_r, gates_c, bias_r, bias_c, gain, row_block_off, nseq, seq_len, state):
    has_state = state is not None
    kern = functools.partial(_mlstm_kernel, seq_len=seq_len, has_state=has_state)
    col = lambda cb: pl.BlockSpec((seq_len, LANE), lambda b, h: (row_block_off + b, cb + h))
    c_spec = pl.BlockSpec((None, 2, None, ML_DK, ML_DK), lambda b, h: (b, 0, h, 0, 0))
    v_spec = pl.BlockSpec((None, 2, None, 1, LANE), lambda b, h: (b, 0, h, 0, 0))
    in_specs = [col(CB_MLQ), col(CB_MLK), col(CB_MLV), col(CB_MLO),
                pl.BlockSpec((None, seq_len, 4), lambda b, h: (h, row_block_off + b, 0)),
                pl.BlockSpec((None, 4, seq_len), lambda b, h: (h, 0, row_block_off + b)),
                pl.BlockSpec((None, 1, 4), lambda b, h: (h, 0, 0)),
                pl.BlockSpec((None, 4, 1), lambda b, h: (h, 0, 0)),
                pl.BlockSpec((1, LANE), lambda b, h: (0, h))]
    args = [proj, proj, proj, proj, gates_r, gates_c, bias_r, bias_c, gain]
    if has_state:
        in_specs += [c_spec, v_spec, v_spec]
        args += list(state)
    return pl.pallas_call(
        kern,
        grid=(nseq, ML_H),
        in_specs=in_specs,
        out_specs=[pl.BlockSpec((seq_len, LANE), lambda b, h: (b, h)), c_spec, v_spec, v_spec],
        out_shape=[jax.ShapeDtypeStruct((nseq * seq_len, MIX_W), BF16),
                   jax.ShapeDtypeStruct((nseq, 2, ML_H, ML_DK, ML_DK), F32),
                   jax.ShapeDtypeStruct((nseq, 2, ML_H, 1, LANE), F32),
                   jax.ShapeDtypeStruct((nseq, 2, ML_H, 1, LANE), F32)],
        scratch_shapes=([pltpu.VMEM((seq_len, LANE), F32)] * 2 + [pltpu.VMEM((ML_DK, ML_DK), F32)] * 2
                        + [pltpu.VMEM((1, LANE), F32)] * 4),
        compiler_params=_cp(("parallel", "parallel")),
        name=f"mlstm_{seq_len}",
    )(*args)


def _softplus(x):
    return jnp.maximum(x, 0.0) + jnp.log1p(jnp.exp(-jnp.abs(x)))


def _gelu_tanh(x):
    return 0.5 * x * (1.0 + jnp.tanh(math.sqrt(2.0 / math.pi) * (x + 0.044715 * (x * x * x))))


def _lru_kernel(*refs, seq_len, has_state):
    if has_state:
        (x_ref, y_ref, cw_ref, cb_ref, wa_ref, ba_ref, wx_ref, bx_ref, lam_ref, h0_ref,
         out_ref, hl_ref, a_s, b_s, hsum, hsum_b) = refs
    else:
        (x_ref, y_ref, cw_ref, cb_ref, wa_ref, ba_ref, wx_ref, bx_ref, lam_ref,
         out_ref, hl_ref, a_s, b_s, hsum, hsum_b) = refs
    xf = x_ref[...].astype(F32)
    row = lax.broadcasted_iota(I32, xf.shape, 0)
    w = cw_ref[...]
    x = (w[0:1] * _shift_rows(xf, 2, row) + w[1:2] * _shift_rows(xf, 1, row) + w[2:3] * xf
         + w[3:4] * _shift_rows(xf, -1, row) + cb_ref[...])
    xb = x.astype(BF16)
    for d in (0, 1):
        r = jax.nn.sigmoid(jnp.dot(xb, wa_ref[d].astype(BF16), preferred_element_type=F32) + ba_ref[d])
        i = jax.nn.sigmoid(jnp.dot(xb, wx_ref[d].astype(BF16), preferred_element_type=F32) + bx_ref[d])
        log_a = (-LRU_C) * r * _softplus(-lam_ref[d])
        a_s[d] = jnp.exp(log_a)
        b_s[d] = jnp.sqrt(jnp.maximum(1.0 - jnp.exp(2.0 * log_a), 0.0)) * (i * x)

    nb = seq_len // 8
    sub = lax.broadcasted_iota(I32, (8, LANE), 0)

    def block_scan(a, b, reverse):
        for s in (1, 2, 4):
            sh = 8 - s if reverse else s
            ok = (sub < 8 - s) if reverse else (sub >= s)
            b = jnp.where(ok, a * pltpu.roll(b, sh, 0) + b, b)
            a = jnp.where(ok, a * pltpu.roll(a, sh, 0), a)
        return a, b

    def both(i, carry):
        cf, cb = carry
        rows_f = pl.ds(pl.multiple_of(i * 8, 8), 8)
        rows_b = pl.ds(pl.multiple_of((nb - 1 - i) * 8, 8), 8)
        a, b = block_scan(a_s[0, rows_f, :], b_s[0, rows_f, :], False)
        hf = a * cf + b
        hsum[rows_f, :] = hf
        a, b = block_scan(a_s[1, rows_b, :], b_s[1, rows_b, :], True)
        hb = a * cb + b
        hsum_b[rows_b, :] = hb
        return hf[7:8, :], hb[0:1, :]

    zero = jnp.zeros((1, LANE), F32)
    init = (h0_ref[0], h0_ref[1]) if has_state else (zero, zero)
    hf, hb = lax.fori_loop(0, nb, both, init, unroll=4)
    out_ref[...] = ((hsum[...] + hsum_b[...]) * _gelu_tanh(y_ref[...].astype(F32))).astype(BF16)
    hl_ref[0] = hf
    hl_ref[1] = hb


def rglru_mixer(proj, conv_w, conv_b, wa, ba, wx, bx, lam, row_block_off, nseq, seq_len, h0):
    has_state = h0 is not None
    kern = functools.partial(_lru_kernel, seq_len=seq_len, has_state=has_state)
    col = lambda cb: pl.BlockSpec((seq_len, LANE), lambda b, h: (row_block_off + b, cb + h))
    w_spec = pl.BlockSpec((2, None, LANE, LANE), lambda b, h: (0, h, 0, 0))
    v_spec = pl.BlockSpec((2, 1, LANE), lambda b, h: (0, 0, h))
    s_spec = pl.BlockSpec((None, 2, 1, LANE), lambda b, h: (b, 0, 0, h))
    in_specs = [col(CB_LRX), col(CB_LRY),
                pl.BlockSpec((4, LANE), lambda b, h: (0, h)),
                pl.BlockSpec((1, LANE), lambda b, h: (0, h)),
                w_spec, v_spec, w_spec, v_spec, v_spec]
    args = [proj, proj, conv_w, conv_b, wa, ba, wx, bx, lam]
    if has_state:
        in_specs.append(s_spec)
        args.append(h0)
    return pl.pallas_call(
        kern,
        grid=(nseq, LRU_H),
        in_specs=in_specs,
        out_specs=[pl.BlockSpec((seq_len, LANE), lambda b, h: (b, h)), s_spec],
        out_shape=[jax.ShapeDtypeStruct((nseq * seq_len, MIX_W), BF16),
                   jax.ShapeDtypeStruct((nseq, 2, 1, MIX_W), F32)],
        scratch_shapes=[pltpu.VMEM((2, seq_len, LANE), F32), pltpu.VMEM((2, seq_len, LANE), F32),
                        pltpu.VMEM((seq_len, LANE), F32), pltpu.VMEM((seq_len, LANE), F32)],
        compiler_params=_cp(("parallel", "parallel")),
        name=f"rglru_{seq_len}",
    )(*args)


@functools.lru_cache(maxsize=None)
def _rope_tables():
    t = np.arange(DEC_SEQ)
    pos_row, pos_col = (t // GRID_W).astype(np.float64), (t % GRID_W).astype(np.float64)
    half = DA_DK // 2
    inv = ROPE_BASE ** (-np.arange(0, half, 2, dtype=np.float64) / half)
    lane = np.arange(LANE)
    sub = lane % DA_DK
    pos = np.where((sub < half)[None, :], pos_row[:, None], pos_col[:, None])
    ang = pos * inv[sub % (half // 2)][None, :]
    first = (sub % half) < (half // 2)
    cos = np.cos(ang)
    sin = np.where(first[None, :], -np.sin(ang), np.sin(ang))
    cos_all = np.concatenate([np.ones((T_CTX, LANE))] + [cos] * DEC_BATCH, axis=0)
    sin_all = np.concatenate([np.zeros((T_CTX, LANE))] + [sin] * DEC_BATCH, axis=0)
    pm = np.kron(np.eye(2), np.full((DA_DK, DA_DK), 1.0 / DA_DK))
    return np.asarray(cos_all, np.float32), np.asarray(sin_all, np.float32), np.asarray(pm, np.float32)


def _qkprep_kernel(q_ref, k_ref, cos_ref, sin_ref, qg_ref, kg_ref, pm_ref, qo_ref, ko_ref, kn_ref):
    pmb = pm_ref[...].astype(BF16)
    cos = cos_ref[...]
    sin = sin_ref[...]
    lane = lax.broadcasted_iota(I32, cos.shape, 1)
    first = (lane % (DA_DK // 2)) < (DA_DK // 4)

    def norm(x, g):
        hi, lo = _split_bf16(x * x)
        ms = jnp.dot(hi, pmb, preferred_element_type=F32) + jnp.dot(lo, pmb, preferred_element_type=F32)
        return x * lax.rsqrt(ms + EPS) * g

    def rope(x):
        partner = jnp.where(first, pltpu.roll(x, LANE - DA_DK // 4, 1), pltpu.roll(x, DA_DK // 4, 1))
        return x * cos + partner * sin

    q = norm(q_ref[...].astype(F32), qg_ref[...])
    k = norm(k_ref[...].astype(F32), kg_ref[...])
    kn_ref[...] = k
    qo_ref[...] = (rope(q) * (DA_DK ** -0.5)).astype(BF16)
    ko_ref[...] = rope(k).astype(BF16)


def qk_prepare(proj, q_gain, k_gain):
    cos, sin, pm = (jnp.asarray(t) for t in _rope_tables())
    tm = 512
    blk = lambda cb: pl.BlockSpec((tm, LANE), lambda i, h: (i, cb + h))
    tab = pl.BlockSpec((tm, LANE), lambda i, h: (i, 0))
    one = pl.BlockSpec((1, LANE), lambda i, h: (0, 0))
    out = pl.BlockSpec((tm, LANE), lambda i, h: (i, h))
    return pl.pallas_call(
        _qkprep_kernel,
        grid=(T_ALL // tm, DA_H),
        in_specs=[blk(CB_DAQ), blk(CB_DAK), tab, tab, one, one, pl.BlockSpec((LANE, LANE), lambda i, h: (0, 0))],
        out_specs=[out, out, out],
        out_shape=[jax.ShapeDtypeStruct((T_ALL, MIX_W), BF16), jax.ShapeDtypeStruct((T_ALL, MIX_W), BF16),
                   jax.ShapeDtypeStruct((T_ALL, MIX_W), F32)],
        compiler_params=_cp(("parallel", "parallel")),
        name="qk_prepare",
    )(proj, proj, cos, sin, q_gain, k_gain, pm)


def _attn_kernel(*refs, has_cache, lam_init, seq_len):
    if has_cache:
        q_ref, k_ref, v_ref, kc_ref, vc_ref, lp_ref, sg_ref, o_ref, ka_ref, va_ref = refs
    else:
        q_ref, k_ref, v_ref, lp_ref, sg_ref, o_ref, ka_ref, va_ref = refs

    @pl.when(pl.program_id(2) == 0)
    def _():
        ka_ref[0:seq_len, :] = k_ref[...]
        va_ref[0:seq_len, 0:LANE] = v_ref[...]
        if has_cache:
            ka_ref[seq_len:, :] = kc_ref[...].astype(BF16)
            va_ref[seq_len:, 0:LANE] = vc_ref[...].astype(BF16)
        va_ref[:, LANE:] = jnp.ones((va_ref.shape[0], LANE), BF16)

    q = q_ref[...]
    lane = lax.broadcasted_iota(I32, q.shape, 1)
    zero = jnp.zeros_like(q)
    lp = lp_ref[...]
    lam = (jnp.exp(jnp.sum(lp[0:1] * lp[1:2], axis=1, keepdims=True))
           - jnp.exp(jnp.sum(lp[2:3] * lp[3:4], axis=1, keepdims=True)) + lam_init)

    def softmax_half(qc):
        s = _dot_nt(qc, ka_ref[...])
        p = jnp.exp(s - jnp.max(s, axis=-1, keepdims=True)).astype(BF16)
        r = jnp.dot(p, va_ref[...], preferred_element_type=F32)
        return r[:, :LANE] / r[:, LANE:]

    o = softmax_half(jnp.where(lane < DA_DK, q, zero)) - lam * softmax_half(jnp.where(lane >= DA_DK, q, zero))
    o = o * lax.rsqrt(jnp.mean(o * o, axis=-1, keepdims=True) + EPS) * sg_ref[...]
    o_ref[...] = (o * (1.0 - lam_init)).astype(BF16)


def diff_attention(q_rot, k_rot, proj, lam_p, sub_gain, lam_init, row_block_off, nseq, seq_len, cache, layer):
    has_cache = cache is not None
    tq = 256
    kern = functools.partial(_attn_kernel, has_cache=has_cache, lam_init=lam_init, seq_len=seq_len)
    nq = seq_len // tq
    n_keys = seq_len + (cache[0].shape[2] if has_cache else 0)
    kv = lambda cb: pl.BlockSpec((seq_len, LANE), lambda b, h, i: (row_block_off + b, cb + h))
    in_specs = [pl.BlockSpec((tq, LANE), lambda b, h, i: ((row_block_off + b) * nq + i, h)), kv(0), kv(CB_DAV)]
    args = [q_rot, k_rot, proj]
    if has_cache:
        past = cache[0].shape[2]
        cspec = pl.BlockSpec((None, None, past, LANE), lambda b, h, i: (b, layer, 0, h))
        in_specs += [cspec, cspec]
        args += list(cache)
    in_specs += [pl.BlockSpec((4, DA_DK), lambda b, h, i: (0, 0)), pl.BlockSpec((1, LANE), lambda b, h, i: (0, 0))]
    args += [lam_p, sub_gain]
    return pl.pallas_call(
        kern,
        grid=(nseq, DA_H, nq),
        in_specs=in_specs,
        out_specs=pl.BlockSpec((tq, LANE), lambda b, h, i: (b * nq + i, h)),
        out_shape=jax.ShapeDtypeStruct((nseq * seq_len, MIX_W), BF16),
        scratch_shapes=[pltpu.VMEM((n_keys, LANE), BF16), pltpu.VMEM((n_keys, 2 * LANE), BF16)],
        compiler_params=_cp(("parallel", "parallel", "arbitrary")),
        name=f"diff_attention_{seq_len}",
    )(*args)


def _merge_kernel(h_ref, y0_ref, y1_ref, y2_ref, y3_ref, wg0_ref, wg1_ref, wg2_ref, wg3_ref, wb_ref, o_ref):
    h = h_ref[...]
    acc = None
    for bi, (y_ref, wg_ref) in enumerate(((y0_ref, wg0_ref), (y1_ref, wg1_ref), (y2_ref, wg2_ref), (y3_ref, wg3_ref))):
        gate = jax.nn.sigmoid(jnp.dot(h, wg_ref[...], preferred_element_type=F32))
        term = gate * jnp.dot(y_ref[...], wb_ref[bi], preferred_element_type=F32)
        acc = term if acc is None else acc + term
    o_ref[...] = acc.astype(BF16)


def branch_merge(h, ys, w_gate, w_branch):
    tm, tn = 1024, 256
    nj = D_MODEL // tn
    yspec = pl.BlockSpec((tm, MIX_W), lambda i, j: (i, 0))
    gspec = lambda bi: pl.BlockSpec((D_MODEL, tn), lambda i, j: (0, bi * nj + j))
    return pl.pallas_call(
        _merge_kernel,
        grid=(T_ALL // tm, nj),
        in_specs=[pl.BlockSpec((tm, D_MODEL), lambda i, j: (i, 0)), yspec, yspec, yspec, yspec,
                  gspec(0), gspec(1), gspec(2), gspec(3),
                  pl.BlockSpec((N_BRANCH, MIX_W, tn), lambda i, j: (0, 0, j))],
        out_specs=pl.BlockSpec((tm, tn), lambda i, j: (i, j)),
        out_shape=jax.ShapeDtypeStruct((T_ALL, D_MODEL), BF16),
        compiler_params=_cp(("parallel", "parallel")),
        name="branch_merge",
    )(h, *ys, w_gate, w_gate, w_gate, w_gate, w_branch)


def _outproj_kernel(m_ref, x_ref, g1_ref, w_ref, n2_ref, sh2_ref, sc2_ref, rt_ref, rl_ref, x1_ref, h2_ref, lg_ref):
    half = m_ref.shape[0] // 2
    for r in (0, half):
        rows = slice(r, r + half)
        y = jnp.dot(m_ref[rows, :], w_ref[...], preferred_element_type=F32)
        x1 = x_ref[rows, :] + g1_ref[...] * y
        x1_ref[rows, :] = x1
        h2 = _norm_mod(x1, n2_ref[...], sc2_ref[...], sh2_ref[...])
        h2_ref[rows, :] = _pack_bf16_pairs(h2)
        hi, lo = _split_bf16(h2)
        lg_ref[rows, :] = (jnp.dot(hi, rt_ref[...], preferred_element_type=F32)
                           + jnp.dot(lo, rt_ref[...], preferred_element_type=F32)
                           + jnp.dot(hi, rl_ref[...], preferred_element_type=F32))


def out_projection(merged, x, mod4, layer, w_out, norm2_g, router):
    tm = 256
    router_hi = router.astype(BF16)
    router_lo = (router - router_hi.astype(F32)).astype(BF16)
    return pl.pallas_call(
        _outproj_kernel,
        grid=(T_ALL // tm,),
        in_specs=[pl.BlockSpec((tm, D_MODEL), lambda i: (i, 0)),
                  pl.BlockSpec((tm, D_MODEL), lambda i: (i, 0)),
                  _mod_spec(layer, 2, tm, 1),
                  pl.BlockSpec((D_MODEL, D_MODEL), lambda i: (0, 0)),
                  pl.BlockSpec((None, 1, D_MODEL), lambda i: (layer, 0, 0)),
                  _mod_spec(layer, 3, tm, 1),
                  _mod_spec(layer, 4, tm, 1),
                  pl.BlockSpec((D_MODEL, LANE), lambda i: (0, 0)),
                  pl.BlockSpec((D_MODEL, LANE), lambda i: (0, 0))],
        out_specs=[pl.BlockSpec((tm, D_MODEL), lambda i: (i, 0)),
                   pl.BlockSpec((tm, D_MODEL // 2), lambda i: (i, 0)),
                   pl.BlockSpec((tm, LANE), lambda i: (i, 0))],
        out_shape=[jax.ShapeDtypeStruct((T_ALL, D_MODEL), F32),
                   jax.ShapeDtypeStruct((T_ALL, D_MODEL // 2), I32),
                   jax.ShapeDtypeStruct((T_ALL, LANE), F32)],
        compiler_params=_cp(("parallel",)),
        name="out_projection",
    )(merged, x, mod4, w_out, norm2_g, mod4, mod4, router_hi, router_lo)


MOE_TM = 256
N_ASSIGN = T_ALL * TOP_K
MOE_BLOCKS = (N_ASSIGN + N_EXPERTS * (MOE_TM - 1)) // MOE_TM + 1
ROUTER_TT = 512
GROUP_SIZE = N_EXPERTS // N_GROUPS


def _router_kernel(lg_ref, bias_ref, tri_ref, eidx_ref, w_ref, rank_ref, cnt_ref, carry_ref):
    i = pl.program_id(0)

    @pl.when(i == 0)
    def _():
        carry_ref[...] = jnp.zeros_like(carry_ref)

    tt = lg_ref.shape[0]
    shape3 = (N_GROUPS, GROUP_SIZE, tt)
    neg = -jnp.inf
    logits_t = lg_ref[...].T[:N_EXPERTS]
    scores = jax.nn.sigmoid(logits_t).reshape(shape3)
    sel = scores + bias_ref[...].reshape(N_GROUPS, GROUP_SIZE, 1)
    gi = lax.broadcasted_iota(I32, shape3, 0)
    ji = lax.broadcasted_iota(I32, shape3, 1)
    ei = gi * GROUP_SIZE + ji
    m1 = jnp.max(sel, axis=1, keepdims=True)
    first = jnp.min(jnp.where(sel == m1, ji, GROUP_SIZE), axis=1, keepdims=True)
    m2 = jnp.max(jnp.where(ji == first, neg, sel), axis=1, keepdims=True)
    cur = m1 + m2
    g1 = lax.broadcasted_iota(I32, cur.shape, 0)
    gmask = jnp.zeros(cur.shape, jnp.bool_)
    for _ in range(TOPK_GROUPS):
        mx = jnp.max(cur, axis=0, keepdims=True)
        idx = jnp.min(jnp.where(cur == mx, g1, N_GROUPS), axis=0, keepdims=True)
        hit = g1 == idx
        gmask = jnp.logical_or(gmask, hit)
        cur = jnp.where(hit, neg, cur)
    masked = jnp.where(gmask, sel, neg)

    def all_max(a):
        return jnp.max(jnp.max(a, axis=1, keepdims=True), axis=0, keepdims=True)

    def all_min(a):
        return jnp.min(jnp.min(a, axis=1, keepdims=True), axis=0, keepdims=True)

    def all_sum(a):
        return jnp.sum(jnp.sum(a, axis=1, keepdims=True), axis=0, keepdims=True)

    hits, idxs, ws = [], [], []
    for _ in range(TOP_K):
        mx = all_max(masked)
        idx = all_min(jnp.where(masked == mx, ei, N_EXPERTS))
        hit = ei == idx
        hits.append(hit)
        idxs.append(idx)
        ws.append(all_sum(jnp.where(hit, scores, 0.0)))
        masked = jnp.where(hit, neg, masked)
    wsum = ws[0]
    for wk in ws[1:]:
        wsum = wsum + wk
    onehot = jnp.zeros(shape3, F32)
    for hit in hits:
        onehot = onehot + hit.astype(F32)
    oh2 = onehot.reshape(N_EXPERTS, tt)
    before = jnp.dot(oh2.astype(BF16), tri_ref[...], preferred_element_type=F32) + carry_ref[:, 0:1]
    before3 = before.reshape(shape3)
    for k in range(TOP_K):
        eidx_ref[k:k + 1, :] = idxs[k].reshape(1, tt)
        w_ref[k:k + 1, :] = (ws[k] / wsum * ROUTED_SCALE).reshape(1, tt)
        rank_ref[k:k + 1, :] = all_sum(jnp.where(hits[k], before3, 0.0)).reshape(1, tt).astype(I32)
    eidx_ref[TOP_K:, :] = jnp.zeros((8 - TOP_K, tt), I32)
    w_ref[TOP_K:, :] = jnp.zeros((8 - TOP_K, tt), F32)
    rank_ref[TOP_K:, :] = jnp.zeros((8 - TOP_K, tt), I32)
    carry_ref[...] = carry_ref[...] + jnp.sum(oh2, axis=1, keepdims=True)
    cnt_ref[...] = carry_ref[...]


def moe_route(logits_t, bias):
    tt = ROUTER_TT
    tri = jnp.asarray(np.triu(np.ones((tt, tt), np.float32), 1), BF16)
    row8 = pl.BlockSpec((8, tt), lambda i: (0, i))
    return pl.pallas_call(
        _router_kernel,
        grid=(T_ALL // tt,),
        in_specs=[pl.BlockSpec((tt, LANE), lambda i: (i, 0)),
                  pl.BlockSpec((N_EXPERTS, 1), lambda i: (0, 0)),
                  pl.BlockSpec((tt, tt), lambda i: (0, 0))],
        out_specs=[row8, row8, row8, pl.BlockSpec((N_EXPERTS, LANE), lambda i: (0, 0))],
        out_shape=[jax.ShapeDtypeStruct((8, T_ALL), I32), jax.ShapeDtypeStruct((8, T_ALL), F32),
                   jax.ShapeDtypeStruct((8, T_ALL), I32), jax.ShapeDtypeStruct((N_EXPERTS, LANE), F32)],
        scratch_shapes=[pltpu.VMEM((N_EXPERTS, LANE), F32)],
        compiler_params=_cp(("arbitrary",)),
        name="moe_route",
    )(logits_t, bias, tri)


N_SLOTS = MOE_BLOCKS * MOE_TM
D_PACK = D_MODEL // 2


def _pack_bf16_pairs(x):
    n = x.shape[1] // 2
    hi = pltpu.bitcast(x[:, :n].astype(BF16).astype(F32), I32)
    lo = pltpu.bitcast(x[:, n:].astype(BF16).astype(F32), I32)
    return hi | lax.shift_right_logical(lo, 16)


def _unpack_bf16_pairs(u):
    hi = pltpu.bitcast(u & jnp.int32(-65536), F32)
    lo = pltpu.bitcast(lax.shift_left(u, 16), F32)
    return jnp.concatenate([hi, lo], axis=1)


def _dispatch_kernel(slots_ref, h2p_ref, xs_hbm, sem, *, tm):
    i = pl.program_id(0)

    def body(r, _):
        for k in range(TOP_K):
            s = slots_ref[(i * tm + r) * TOP_K + k]
            pltpu.make_async_copy(h2p_ref.at[pl.ds(r, 1), :], xs_hbm.at[pl.ds(s, 1), :], sem).start(priority=k % 2)
        return 0

    lax.fori_loop(0, tm, body, 0)
    for k in range(TOP_K):
        pltpu.make_async_copy(h2p_ref, xs_hbm.at[pl.ds(0, tm), :], sem).wait()


def moe_dispatch(slots, h2p):
    tm = 256
    return pl.pallas_call(
        functools.partial(_dispatch_kernel, tm=tm),
        grid_spec=pltpu.PrefetchScalarGridSpec(
            num_scalar_prefetch=1,
            grid=(T_ALL // tm,),
            in_specs=[pl.BlockSpec((tm, D_PACK), lambda i, s: (i, 0))],
            out_specs=pl.BlockSpec(memory_space=pl.ANY),
            scratch_shapes=[pltpu.SemaphoreType.DMA(())]),
        out_shape=jax.ShapeDtypeStruct((N_SLOTS, D_PACK), I32),
        compiler_params=_cp(("arbitrary",), unchecked_dma=True),
        name="moe_dispatch",
    )(slots, h2p)


def _gmm_kernel(be_ref, bv_ref, first_ref, par_ref, nxt_ref, x_ref, wg_hbm, wu_hbm, wd_hbm, y_ref,
                wgf, wuf, wdf, wgb, wub, wdb, wsem, *, layer):
    i = pl.program_id(0)

    def fetch(e, slot):
        return (pltpu.make_async_copy(wg_hbm.at[layer, e], wgf.at[slot], wsem.at[slot]),
                pltpu.make_async_copy(wu_hbm.at[layer, e], wuf.at[slot], wsem.at[slot]),
                pltpu.make_async_copy(wd_hbm.at[layer, e], wdf.at[slot], wsem.at[slot]))

    @pl.when(i == 0)
    def _():
        for cp in fetch(be_ref[0], 0):
            cp.start()

    @pl.when(first_ref[i] == 1)
    def _():
        slot = par_ref[i]
        for cp in fetch(be_ref[i], slot):
            cp.wait()
        nxt = nxt_ref[i]

        @pl.when(nxt >= 0)
        def _():
            for cp in fetch(nxt, 1 - slot):
                cp.start()

        step = 512
        for r in range(0, D_MODEL, step):
            wgb[r:r + step, :] = wgf[slot, r:r + step, :].astype(BF16)
            wub[r:r + step, :] = wuf[slot, r:r + step, :].astype(BF16)
        for r in range(0, D_EXPERT, 128):
            wdb[r:r + 128, :] = wdf[slot, r:r + 128, :].astype(BF16)

    nv = bv_ref[i]

    @pl.when(nv == 0)
    def _():
        y_ref[...] = jnp.zeros_like(y_ref)

    @pl.when(nv > 0)
    def _():
        row = lax.broadcasted_iota(I32, (x_ref.shape[0], 1), 0)
        xin = jnp.where(row < nv, x_ref[...], 0)
        x = _unpack_bf16_pairs(xin).astype(BF16)
        g = jnp.dot(x, wgb[...], preferred_element_type=F32)
        u = jnp.dot(x, wub[...], preferred_element_type=F32)
        a = (g * jax.nn.sigmoid(g) * u).astype(BF16)
        y_ref[...] = _pack_bf16_pairs(jnp.dot(a, wdb[...], preferred_element_type=F32))


def moe_experts(block_e, block_valid, block_first, block_par, block_next, xs, w_gate, w_up, w_down, layer):
    tm = MOE_TM
    row = pl.BlockSpec((tm, D_PACK), lambda i, *_: (i, 0))
    anyspace = pl.BlockSpec(memory_space=pl.ANY)
    return pl.pallas_call(
        functools.partial(_gmm_kernel, layer=layer),
        grid_spec=pltpu.PrefetchScalarGridSpec(
            num_scalar_prefetch=5,
            grid=(MOE_BLOCKS,),
            in_specs=[row, anyspace, anyspace, anyspace],
            out_specs=row,
            scratch_shapes=[pltpu.VMEM((2, D_MODEL, D_EXPERT), F32), pltpu.VMEM((2, D_MODEL, D_EXPERT), F32),
                            pltpu.VMEM((2, D_EXPERT, D_MODEL), F32),
                            pltpu.VMEM((D_MODEL, D_EXPERT), BF16), pltpu.VMEM((D_MODEL, D_EXPERT), BF16),
                            pltpu.VMEM((D_EXPERT, D_MODEL), BF16), pltpu.SemaphoreType.DMA((2,))]),
        out_shape=jax.ShapeDtypeStruct((N_SLOTS, D_PACK), I32),
        compiler_params=_cp(("arbitrary",)),
        name="moe_experts",
    )(block_e, block_valid, block_first, block_par, block_next, xs, w_gate, w_up, w_down)


def _combine_kernel(slots_ref, x1_ref, h2p_ref, w_ref, g2_ref, sg_ref, su_ref, sd_ref, ys_hbm, out_ref, buf, sem,
                    *, tm, nsteps):
    i = pl.program_id(0)

    def issue(step, slot):
        def body(r, _):
            for k in range(TOP_K):
                s = slots_ref[(step * tm + r) * TOP_K + k]
                pltpu.make_async_copy(ys_hbm.at[pl.ds(s, 1), :], buf.at[slot, pl.ds(k * tm + r, 1), :],
                                      sem.at[slot]).start(priority=k % 2)
            return 0

        lax.fori_loop(0, tm, body, 0)

    @pl.when(i == 0)
    def _():
        issue(0, 0)

    @pl.when(i + 1 < nsteps)
    def _():
        issue(i + 1, (i + 1) % 2)

    slot = i % 2
    pltpu.make_async_copy(ys_hbm.at[pl.ds(0, TOP_K * tm), :], buf.at[slot], sem.at[slot]).wait()
    hb = _unpack_bf16_pairs(h2p_ref[...]).astype(BF16)
    g = jnp.dot(hb, sg_ref[...], preferred_element_type=F32)
    u = jnp.dot(hb, su_ref[...], preferred_element_type=F32)
    acc = jnp.dot((g * jax.nn.sigmoid(g) * u).astype(BF16), sd_ref[...], preferred_element_type=F32)
    w = w_ref[...]
    for k in range(TOP_K):
        acc = acc + w[:, k:k + 1] * _unpack_bf16_pairs(buf[slot, k * tm:(k + 1) * tm, :])
    out_ref[...] = x1_ref[...] + g2_ref[...] * acc


def moe_combine(slots, x1, h2p, w_tok, mod4, layer, sg, su, sd, ys):
    tm = 256
    nsteps = T_ALL // tm
    kern = functools.partial(_combine_kernel, tm=tm, nsteps=nsteps)
    row = lambda n: pl.BlockSpec((tm, n), lambda i, s: (i, 0))
    return pl.pallas_call(
        kern,
        grid_spec=pltpu.PrefetchScalarGridSpec(
            num_scalar_prefetch=1,
            grid=(nsteps,),
            in_specs=[row(D_MODEL), row(D_PACK), row(8),
                      pl.BlockSpec((None, None, 1, D_MODEL),
                                   lambda i, s: (layer, _mod_row(i, tm) * 6 + 5, 0, 0)),
                      pl.BlockSpec((D_MODEL, D_EXPERT), lambda i, s: (0, 0)),
                      pl.BlockSpec((D_MODEL, D_EXPERT), lambda i, s: (0, 0)),
                      pl.BlockSpec((D_EXPERT, D_MODEL), lambda i, s: (0, 0)),
                      pl.BlockSpec(memory_space=pl.ANY)],
            out_specs=row(D_MODEL),
            scratch_shapes=[pltpu.VMEM((2, TOP_K * tm, D_PACK), I32), pltpu.SemaphoreType.DMA((2,))]),
        out_shape=jax.ShapeDtypeStruct((T_ALL, D_MODEL), F32),
        compiler_params=_cp(("arbitrary",), unchecked_dma=True),
        name="moe_combine",
    )(slots, x1, h2p, w_tok, mod4, sg, su, sd, ys)


def moe_ffn(x1, h2p, logits_t, mod4, layer, moe_bias, w_gate, w_up, w_down, sg, su, sd):
    eidx, w_t, rank, cnt = moe_route(logits_t, moe_bias)
    counts = cnt[:, 0].astype(I32)
    padded = (counts + MOE_TM - 1) // MOE_TM * MOE_TM
    pad_end = jnp.cumsum(padded)
    pad_start = pad_end - padded
    experts = jnp.arange(N_EXPERTS, dtype=I32)
    base = jnp.sum(jnp.where(eidx[:TOP_K, :, None] == experts, pad_start, 0), axis=-1)
    slots = (base + rank[:TOP_K]).T.reshape(-1)
    starts = jnp.arange(MOE_BLOCKS, dtype=I32) * MOE_TM
    block_e = jnp.minimum(jnp.sum((pad_end[None, :] <= starts[:, None]).astype(I32), axis=1), N_EXPERTS - 1)
    mine = block_e[:, None] == experts
    left = jnp.sum(jnp.where(mine, counts + pad_start, 0), axis=1) - starts
    block_valid = jnp.clip(left, 0, MOE_TM).astype(I32)
    nonempty = counts > 0
    pick = lambda v: jnp.sum(jnp.where(mine, v, 0), axis=1)
    block_first = jnp.logical_and(starts == pick(pad_start), block_valid > 0).astype(I32)
    block_par = pick(jnp.cumsum(nonempty.astype(I32)) - 1) % 2
    later = lax.cummin(jnp.where(nonempty, experts, N_EXPERTS), axis=0, reverse=True)
    nxt_e = jnp.concatenate([later[1:], jnp.full((1,), N_EXPERTS, I32)])
    block_next = pick(jnp.where(nxt_e >= N_EXPERTS, -1, nxt_e))
    xs = moe_dispatch(slots, h2p)
    ys = moe_experts(block_e, block_valid, block_first, block_par, block_next, xs, w_gate, w_up, w_down, layer)
    return moe_combine(slots, x1, h2p, w_t.T, mod4, layer, sg, su, sd, ys)


def _ml_gate_layouts(gp, gate_bias):
    nt = gp.shape[0]
    g16 = gp[:, :4 * ML_H].reshape(nt, 2, 2, ML_H)
    gr = jnp.transpose(g16, (3, 0, 1, 2)).reshape(ML_H, nt, 4)
    gc = jnp.transpose(gr, (0, 2, 1))
    b = jnp.transpose(gate_bias, (2, 0, 1)).reshape(ML_H, 4)
    return gr, gc, b[:, None, :], b[:, :, None]


def kernel(x_prompt, x_sample, c, cache_k, cache_v, state_mlstm_C, state_mlstm_n, state_mlstm_m, state_rglru,
           c_ctx, w_mod, b_mod, norm1, norm2, w_in, hy_short, hy_w1, hy_b1, hy_w2, hy_b2, hy_w3, hy_freq,
           hy_decay, hy_skip, ml_gate_bias, ml_out_norm, lru_conv_w, lru_conv_b, lru_wa, lru_ba, lru_wx, lru_bx,
           lru_lambda, da_q_norm, da_k_norm, da_lambda, da_sub_norm, w_branch, w_out, moe_router, moe_bias,
           moe_w_gate, moe_w_up, moe_w_down, sh_w_gate, sh_w_up, sh_w_down):
    x = jnp.concatenate([x_prompt.reshape(T_CTX, D_MODEL), x_sample.reshape(T_DEN, D_MODEL)], axis=0)
    cvecs = jnp.concatenate([c_ctx[None], c, jnp.zeros((N_MOD_ROWS - 1 - DEC_BATCH, D_MODEL), F32)], axis=0)
    mod4 = modulation_all(cvecs, w_mod, b_mod).reshape(DEPTH, N_MOD_ROWS * 6, 1, D_MODEL)
    norm1_3 = norm1.reshape(DEPTH, 1, D_MODEL)
    norm2_3 = norm2.reshape(DEPTH, 1, D_MODEL)
    cache_k4 = cache_k.reshape(DEC_BATCH, DEPTH, -1, MIX_W)
    cache_v4 = cache_v.reshape(DEC_BATCH, DEPTH, -1, MIX_W)
    den_off = T_CTX // DEC_SEQ
    new_k, new_v, new_c, new_n, new_m, new_h = [], [], [], [], [], []
    for l in range(DEPTH):
        lam_init = 0.8 - 0.6 * math.exp(-0.3 * l)
        wl = w_in[l]
        n_ml = 4 * ML_H
        w_main = jnp.concatenate([wl[:, :7 * MIX_W], wl[:, 7 * MIX_W + n_ml:12 * MIX_W + n_ml]], axis=1).astype(BF16)
        w_g16 = jnp.pad(wl[:, 7 * MIX_W:7 * MIX_W + n_ml], ((0, 0), (0, LANE - n_ml))).astype(BF16)
        w_gate = wl[:, 12 * MIX_W + n_ml:].astype(BF16)
        proj, h, gp = in_projection(x, norm1_3, mod4, l, w_main, w_g16)

        w1p = jnp.pad(hy_w1[l], ((0, LANE - HY_PE), (0, 0)))
        ws3 = jnp.transpose(hy_short[l].reshape(3, 3, MIX_W), (1, 0, 2))
        y_hy = []
        for off, nseq, sl in ((0, BATCH, SEQ), (den_off, DEC_BATCH, DEC_SEQ)):
            kfilt = hyena_filter(sl, w1p, hy_b1[l][None], hy_w2[l], hy_b2[l][None], hy_freq[l][None], hy_w3[l],
                                 hy_decay[l][None])
            if sl <= 512:
                y_hy.append(hyena_conv_direct(proj, off, nseq, sl, ws3, hyena_filter_fft_direct(sl, kfilt), hy_skip[l]))
            else:
                y_hy.append(hyena_conv(proj, off, nseq, sl, ws3, hyena_filter_fft(sl, kfilt), hy_skip[l]))

        gr, gc, b_r, b_c = _ml_gate_layouts(gp, ml_gate_bias[l])
        gain = ml_out_norm[l][None]
        y_ml0, c_new, n_new, m_new = mlstm_mixer(proj, gr, gc, b_r, b_c, gain, 0, BATCH, SEQ, None)
        st = (state_mlstm_C[:, l], state_mlstm_n[:, l][:, :, :, None, :],
              jnp.broadcast_to(state_mlstm_m[:, l][:, :, :, None, None], (DEC_BATCH, 2, ML_H, 1, LANE)))
        y_ml1 = mlstm_mixer(proj, gr, gc, b_r, b_c, gain, den_off, DEC_BATCH, DEC_SEQ, st)[0]

        lru_args = (lru_conv_w[l], lru_conv_b[l][None], lru_wa[l], lru_ba[l][:, None, :], lru_wx[l],
                    lru_bx[l][:, None, :], lru_lambda[l][:, None, :])
        y_lr0, h_new = rglru_mixer(proj, *lru_args, 0, BATCH, SEQ, None)
        y_lr1 = rglru_mixer(proj, *lru_args, den_off, DEC_BATCH, DEC_SEQ, state_rglru[:, l][:, :, None, :])[0]

        q_rot, k_rot, k_norm = qk_prepare(proj, jnp.tile(da_q_norm[l], 2)[None], jnp.tile(da_k_norm[l], 2)[None])
        sub_gain = da_sub_norm[l][None]
        y_da0 = diff_attention(q_rot, k_rot, proj, da_lambda[l], sub_gain, lam_init, 0, BATCH, SEQ, None, l)
        y_da1 = diff_attention(q_rot, k_rot, proj, da_lambda[l], sub_gain, lam_init, den_off, DEC_BATCH, DEC_SEQ,
                               (cache_k4, cache_v4), l)

        ys = [jnp.concatenate(p, axis=0) for p in (y_hy, (y_ml0, y_ml1), (y_lr0, y_lr1), (y_da0, y_da1))]
        merged = branch_merge(h, ys, w_gate, w_branch[l].astype(BF16))
        x1, h2, logits_t = out_projection(merged, x, mod4, l, w_out[l].astype(BF16), norm2_3,
                                          jnp.pad(moe_router[l], ((0, 0), (0, LANE - N_EXPERTS))))
        x = moe_ffn(x1, h2, logits_t, mod4, l, moe_bias[l][:, None], moe_w_gate, moe_w_up, moe_w_down,
                    sh_w_gate[l].astype(BF16), sh_w_up[l].astype(BF16), sh_w_down[l].astype(BF16))

        new_k.append(k_norm[:T_CTX].reshape(BATCH, SEQ, DA_H, 2, DA_DK))
        new_v.append(proj[:T_CTX, CB_DAV * LANE:(CB_DAV + DA_H) * LANE].astype(F32).reshape(BATCH, SEQ, DA_H, -1))
        new_c.append(c_new)
        new_n.append(n_new[:, :, :, 0, :])
        new_m.append(m_new[:, :, :, 0, 0])
        new_h.append(h_new[:, :, 0, :])
    y_prompt = x[:T_CTX].reshape(BATCH, SEQ, D_MODEL)
    y_sample = x[T_CTX:].reshape(DEC_BATCH, DEC_SEQ, D_MODEL)
    stack = lambda parts: jnp.stack(parts, axis=1)
    return (y_prompt, y_sample, stack(new_k), stack(new_v), stack(new_c), stack(new_n), stack(new_m), stack(new_h))
```
